```python
import jax
import jax.numpy as jnp
from jax import lax
import numpy as np

D_MODEL = 1024
BATCH = 2
SEQ = 16384
DEPTH = 2

GRID_W = 64
CTX_LEN = 256
EPS = 1e-6

N_Q_HEADS = 8
N_KV_HEADS = 2
GQA_GROUP = N_Q_HEADS // N_KV_HEADS
HEAD_DIM = 64
WINDOW = 128
ATTN_BLOCK = 128
ROPE_BASE = 10000.0
ATT_Q_W = N_Q_HEADS * HEAD_DIM
ATT_KV_W = N_KV_HEADS * HEAD_DIM
SGU_HEADS = 4
SGU_HEAD_DIM = 64
SGU_W = SGU_HEADS * SGU_HEAD_DIM
SGU_CHUNK = 128
POOL_WINDOWS = (2, 4, 8, 16)
POOL_GROUPS = 4
POOL_GROUP_DIM = 64
POOL_CH = POOL_GROUPS * POOL_GROUP_DIM
CONV_CH = 256
CONV_WIDTH = 3

MIX_WIDTH = ATT_Q_W + SGU_W + POOL_CH + CONV_CH
IN_COLS = ATT_Q_W + 2 * ATT_KV_W + 2 * SGU_W + POOL_CH + 3 * CONV_CH
_Q_END = ATT_Q_W
_K_END = _Q_END + ATT_KV_W
_V_END = _K_END + ATT_KV_W
_SU_END = _V_END + SGU_W
_SV_END = _SU_END + SGU_W
_POOL_END = _SV_END + POOL_CH
_CB_END = _POOL_END + CONV_CH
_CC_END = _CB_END + CONV_CH
SPLIT_POINTS = (_Q_END, _K_END, _V_END, _SU_END, _SV_END, _POOL_END, _CB_END, _CC_END)

N_EXPERTS = 32
TOP_K = 4
D_FF_EXPERT = 1024
SWIGLU_LIMIT = 7.0
SWIGLU_ALPHA = 1.702
MOE_BLOCK = 128

kernel_name = 'hybrid_parallel_group_dit_moe'


def rms_norm(x, g):
    xf = x.astype(jnp.float32)
    y = xf * lax.rsqrt(jnp.mean(xf * xf, axis=-1, keepdims=True) + EPS)
    return (y * g.astype(jnp.float32)).astype(x.dtype)


def layer_norm(x, g, b):
    xf = x.astype(jnp.float32)
    mu = jnp.mean(xf, axis=-1, keepdims=True)
    var = jnp.mean(jnp.square(xf - mu), axis=-1, keepdims=True)
    y = (xf - mu) * lax.rsqrt(var + EPS) * g.astype(jnp.float32) + b.astype(jnp.float32)
    return y.astype(x.dtype)


def modulate(h, shift, scale):
    return h * (1 + scale) + shift


def heads(t, n):
    return t.reshape(t.shape[:-1] + (n, t.shape[-1] // n))


def axial_rope_tables(n_tokens):
    rows = n_tokens // GRID_W
    row = jnp.repeat(jnp.arange(rows), GRID_W).astype(jnp.float32)
    col = jnp.tile(jnp.arange(GRID_W), rows).astype(jnp.float32)
    n_freq = HEAD_DIM // 4
    inv = ROPE_BASE ** (-jnp.arange(n_freq, dtype=jnp.float32) / n_freq)
    ang = jnp.concatenate([row[:, None] * inv, col[:, None] * inv], axis=-1)
    return jnp.cos(ang), jnp.sin(ang)


def apply_rope(x, cos, sin):
    xf = x.astype(jnp.float32)
    half = HEAD_DIM // 2
    x1, x2 = xf[..., :half], xf[..., half:]
    cs, sn = cos[None, :, None, :], sin[None, :, None, :]
    return jnp.concatenate([x1 * cs - x2 * sn, x1 * sn + x2 * cs], axis=-1).astype(x.dtype)


def window_attention(q, k, v, kc, vc, sink):
    B, S = q.shape[0], q.shape[1]
    L = kc.shape[1]
    nb = S // ATTN_BLOCK
    scale = HEAD_DIM ** -0.5
    qb = q.reshape(B, nb, ATTN_BLOCK, N_KV_HEADS, GQA_GROUP, HEAD_DIM)

    def band(t):
        tp = jnp.pad(t, ((0, 0), (ATTN_BLOCK, ATTN_BLOCK), (0, 0), (0, 0)))
        blocks = tp.reshape(B, nb + 2, ATTN_BLOCK, N_KV_HEADS, HEAD_DIM)
        return jnp.concatenate([blocks[:, :-2], blocks[:, 1:-1], blocks[:, 2:]], axis=2)

    kb, vb = band(k), band(v)
    blk = jnp.arange(nb)[:, None, None]
    qpos = blk * ATTN_BLOCK + jnp.arange(ATTN_BLOCK)[None, :, None]
    kpos = (blk - 1) * ATTN_BLOCK + jnp.arange(3 * ATTN_BLOCK)[None, None, :]
    valid = (jnp.abs(qpos - kpos) <= WINDOW) & (kpos >= 0) & (kpos < S)
    s_band = jnp.einsum('bnqkgd,bnskd->bnkgqs', qb, kb).astype(jnp.float32) * scale
    s_band = jnp.where(valid[None, :, None, None], s_band, -jnp.inf)
    s_ctx = jnp.einsum('bnqkgd,blkd->bnkgql', qb, kc).astype(jnp.float32) * scale
    s_sink = jnp.broadcast_to(sink.astype(jnp.float32).reshape(N_KV_HEADS, GQA_GROUP)[None, None, :, :, None, None],
                              s_band.shape[:-1] + (1,))
    probs = jax.nn.softmax(jnp.concatenate([s_band, s_ctx, s_sink], axis=-1), axis=-1)
    nband = 3 * ATTN_BLOCK
    p_band = probs[..., :nband].astype(v.dtype)
    p_ctx = probs[..., nband:nband + L].astype(vc.dtype)
    o = (jnp.einsum('bnkgqs,bnskd->bnqkgd', p_band, vb)
         + jnp.einsum('bnkgql,blkd->bnqkgd', p_ctx, vc))
    return o.reshape(B, S, ATT_Q_W)


def context_attention(qc, kc, vc, sink):
    B, L = qc.shape[0], qc.shape[1]
    scale = HEAD_DIM ** -0.5
    qg = qc.reshape(B, L, N_KV_HEADS, GQA_GROUP, HEAD_DIM)
    s = jnp.einsum('bqkgd,blkd->bkgql', qg, kc).astype(jnp.float32) * scale
    s_sink = jnp.broadcast_to(sink.astype(jnp.float32).reshape(N_KV_HEADS, GQA_GROUP)[None, :, :, None, None],
                              s.shape[:-1] + (1,))
    probs = jax.nn.softmax(jnp.concatenate([s, s_sink], axis=-1), axis=-1)[..., :L]
    o = jnp.einsum('bkgql,blkd->bqkgd', probs.astype(vc.dtype), vc)
    return o.reshape(B, L, ATT_Q_W)


def chunk_sgu(u, v, ws, bs, ln_g, ln_b):
    B, N, _ = v.shape
    nc = N // SGU_CHUNK
    vn = layer_norm(v, ln_g, ln_b).reshape(B, nc, SGU_CHUNK, SGU_HEADS, SGU_HEAD_DIM)
    s = jnp.einsum('hpr,bcrhd->bcphd', ws, vn) + bs.T[None, None, :, :, None]
    return u * s.reshape(B, N, SGU_W)


def multiscale_pool(xp, pool_w, pool_scale):
    B, N, _ = xp.shape
    xf = xp.astype(jnp.float32)
    cs = jnp.concatenate([jnp.zeros((B, 1, POOL_CH), jnp.float32), jnp.cumsum(xf, axis=1)], axis=1)
    t = jnp.arange(N)
    pooled = []
    for gi, w in enumerate(POOL_WINDOWS):
        lo = jnp.clip(t - w // 2, 0, N)
        hi = jnp.clip(t + w - w // 2, 0, N)
        csg = cs[..., gi * POOL_GROUP_DIM:(gi + 1) * POOL_GROUP_DIM]
        cnt = (hi - lo).astype(jnp.float32)
        pooled.append((csg[:, hi] - csg[:, lo]) / cnt[None, :, None])
    d = (jnp.concatenate(pooled, axis=-1) - xf).astype(xp.dtype)
    d = d.reshape(B, N, POOL_GROUPS, POOL_GROUP_DIM)
    y = jnp.einsum('bngc,gcd->bngd', d, pool_w).reshape(B, N, POOL_CH)
    return y * pool_scale


def short_conv(cb, cc, cx, conv_w):
    y = cc * cx
    rhs = conv_w[:, None, :].astype(y.dtype)
    z = lax.conv_general_dilated(y, rhs, window_strides=(1,),
                                 padding=((CONV_WIDTH // 2, CONV_WIDTH // 2),),
                                 dimension_numbers=('NWC', 'WIO', 'NWC'),
                                 feature_group_count=CONV_CH)
    return cb * z


def gated_mixers(su, sv, xp, cb, cc, cx, sgu_ws, sgu_b, sgu_ln_g, sgu_ln_b, pool_w, pool_scale, conv_w):
    y_b = chunk_sgu(jax.nn.gelu(su, approximate=False), jax.nn.gelu(sv, approximate=False),
                    sgu_ws, sgu_b, sgu_ln_g, sgu_ln_b)
    y_c = multiscale_pool(xp, pool_w, pool_scale)
    y_d = short_conv(cb, cc, cx, conv_w)
    return y_b, y_c, y_d


def moe(h, router_w, router_b, w1, b1, w2, b2):
    T, D = h.shape
    logits = (h @ router_w).astype(jnp.float32) + router_b.astype(jnp.float32)
    top_v, top_i = lax.top_k(logits, TOP_K)
    gates = jax.nn.softmax(top_v, axis=-1)
    flat_e = top_i.reshape(-1)
    flat_t = jnp.broadcast_to(jnp.arange(T)[:, None], (T, TOP_K)).reshape(-1)
    flat_g = gates.reshape(-1)
    order = jnp.argsort(flat_e)
    se, st, sg = flat_e[order], flat_t[order], flat_g[order]
    counts = jnp.bincount(flat_e, length=N_EXPERTS)
    starts = jnp.cumsum(counts) - counts
    padded = (counts + MOE_BLOCK - 1) // MOE_BLOCK * MOE_BLOCK
    pends = jnp.cumsum(padded)
    pstarts = pends - padded
    dest = pstarts[se] + (jnp.arange(T * TOP_K) - starts[se])
    n_blocks = -(-(T * TOP_K) // MOE_BLOCK) + N_EXPERTS
    n_slots = n_blocks * MOE_BLOCK
    slot_tok = jnp.full((n_slots,), T, jnp.int32).at[dest].set(st.astype(jnp.int32))
    slot_gate = jnp.zeros((n_slots,), h.dtype).at[dest].set(sg.astype(h.dtype))
    block_e = jnp.minimum(jnp.searchsorted(pends, jnp.arange(n_blocks) * MOE_BLOCK, side='right'), N_EXPERTS - 1)
    h_pad = jnp.concatenate([h, jnp.zeros((1, D), h.dtype)], axis=0)
    xs = h_pad[slot_tok].reshape(n_blocks, MOE_BLOCK, D)

    def expert_block(args):
        xb, e = args
        z = xb @ w1[e] + b1[e]
        g = jnp.minimum(z[:, ::2], SWIGLU_LIMIT)
        lin = jnp.clip(z[:, 1::2], -SWIGLU_LIMIT, SWIGLU_LIMIT)
        a = g * jax.nn.sigmoid(SWIGLU_ALPHA * g) * (lin + 1)
        return a @ w2[e] + b2[e]

    ys = lax.map(expert_block, (xs, block_e)).reshape(n_slots, D)
    out = jnp.zeros((T + 1, D), h.dtype).at[slot_tok].add(ys * slot_gate[:, None])
    return out[:T]


def setup_inputs(seed: int = 0) -> dict:
    key = jax.random.key(seed)
    ks = jax.random.split(key, 26)
    f32 = jnp.float32
    D = D_MODEL
    L = DEPTH

    def nrm(k, shape, s):
        return jax.random.normal(k, shape, f32) * s

    return {
        'x': nrm(ks[0], (BATCH, SEQ, D), 1.0),
        'c': nrm(ks[1], (BATCH, D), 1.0),
        'ctx': nrm(ks[2], (BATCH, CTX_LEN, D), 1.0),
        'c_ctx': nrm(ks[3], (D,), 1.0),
        'norm1_g': 1.0 + nrm(ks[4], (L, D), 0.02),
        'norm2_g': 1.0 + nrm(ks[5], (L, D), 0.02),
        'ada_w': nrm(ks[6], (L, D, 6 * D), 0.5 * D ** -0.5),
        'ada_b': nrm(ks[7], (L, 6 * D), 0.02),
        'w_in': nrm(ks[8], (L, D, IN_COLS), D ** -0.5),
        'attn_sink': nrm(ks[9], (L, N_Q_HEADS), 0.5),
        'sgu_ws': nrm(ks[10], (L, SGU_HEADS, SGU_CHUNK, SGU_CHUNK), 0.5 * SGU_CHUNK ** -0.5),
        'sgu_b': 1.0 + nrm(ks[11], (L, SGU_HEADS, SGU_CHUNK), 0.02),
        'sgu_ln_g': 1.0 + nrm(ks[12], (L, SGU_W), 0.02),
        'sgu_ln_b': nrm(ks[13], (L, SGU_W), 0.02),
        'pool_w': nrm(ks[14], (L, POOL_GROUPS, POOL_GROUP_DIM, POOL_GROUP_DIM), POOL_GROUP_DIM ** -0.5),
        'pool_scale': 1.0 + nrm(ks[15], (L, POOL_CH), 0.02),
        'conv_w': nrm(ks[16], (L, CONV_WIDTH, CONV_CH), CONV_WIDTH ** -0.5),
        'w_out': nrm(ks[17], (L, MIX_WIDTH, D), MIX_WIDTH ** -0.5),
        'router_w': nrm(ks[18], (L, D, N_EXPERTS), D ** -0.5),
        'router_b': nrm(ks[19], (L, N_EXPERTS), 0.01),
        'exp_w1': nrm(ks[20], (L, N_EXPERTS, D, 2 * D_FF_EXPERT), D ** -0.5),
        'exp_b1': nrm(ks[21], (L, N_EXPERTS, 2 * D_FF_EXPERT), 0.02),
        'exp_w2': nrm(ks[22], (L, N_EXPERTS, D_FF_EXPERT, D), D_FF_EXPERT ** -0.5),
        'exp_b2': nrm(ks[23], (L, N_EXPERTS, D), 0.02),
        'final_g': 1.0 + nrm(ks[24], (D,), 0.02),
    }


def reference(x, c, ctx, c_ctx, norm1_g, norm2_g, ada_w, ada_b, w_in, attn_sink, sgu_ws, sgu_b,
              sgu_ln_g, sgu_ln_b, pool_w, pool_scale, conv_w, w_out, router_w, router_b,
              exp_w1, exp_b1, exp_w2, exp_b2, final_g):
    B, S, D = x.shape
    L = ctx.shape[1]
    cos, sin = axial_rope_tables(S)
    xl, xc = x, ctx
    silu_c = jax.nn.silu(c)
    silu_cc = jax.nn.silu(c_ctx)
    for l in range(DEPTH):
        last = l == DEPTH - 1
        mod_l = (silu_c @ ada_w[l] + ada_b[l])[:, None, :]
        mod_c = (silu_cc @ ada_w[l] + ada_b[l])[None, None, :]
        sh1l, sc1l, g1l, sh2l, sc2l, g2l = jnp.split(mod_l, 6, axis=-1)
        sh1c, sc1c, g1c, sh2c, sc2c, g2c = jnp.split(mod_c, 6, axis=-1)
        mix_params = (sgu_ws[l], sgu_b[l], sgu_ln_g[l], sgu_ln_b[l], pool_w[l], pool_scale[l], conv_w[l])

        hc = modulate(rms_norm(xc, norm1_g[l]), sh1c, sc1c)
        if last:
            kv_c = hc @ w_in[l][:, _Q_END:_V_END]
            kc = heads(kv_c[..., :ATT_KV_W], N_KV_HEADS)
            vc = heads(kv_c[..., ATT_KV_W:], N_KV_HEADS)
        else:
            pc = jnp.split(hc @ w_in[l], SPLIT_POINTS, axis=-1)
            kc = heads(pc[1], N_KV_HEADS)
            vc = heads(pc[2], N_KV_HEADS)
            attn_c = context_attention(heads(pc[0], N_Q_HEADS), kc, vc, attn_sink[l])
            yb_c, yc_c, yd_c = gated_mixers(*pc[3:], *mix_params)
            mix_c = jnp.concatenate([attn_c, yb_c, yc_c, yd_c], axis=-1) @ w_out[l]
            xc_mid = xc + g1c * mix_c

        hl = modulate(rms_norm(xl, norm1_g[l]), sh1l, sc1l)
        pl = jnp.split(hl @ w_in[l], SPLIT_POINTS, axis=-1)
        ql = apply_rope(heads(pl[0], N_Q_HEADS), cos, sin)
        kl = apply_rope(heads(pl[1], N_KV_HEADS), cos, sin)
        vl = heads(pl[2], N_KV_HEADS)
        attn_l = window_attention(ql, kl, vl, kc, vc, attn_sink[l])
        yb_l, yc_l, yd_l = gated_mixers(*pl[3:], *mix_params)
        mix_l = jnp.concatenate([attn_l, yb_l, yc_l, yd_l], axis=-1) @ w_out[l]
        xl = xl + g1l * mix_l

        hl2 = modulate(rms_norm(xl, norm2_g[l]), sh2l, sc2l).reshape(B * S, D)
        moe_args = (router_w[l], router_b[l], exp_w1[l], exp_b1[l], exp_w2[l], exp_b2[l])
        if last:
            xl = xl + g2l * moe(hl2, *moe_args).reshape(B, S, D)
        else:
            hc2 = modulate(rms_norm(xc_mid, norm2_g[l]), sh2c, sc2c).reshape(B * L, D)
            y = moe(jnp.concatenate([hl2, hc2], axis=0), *moe_args)
            xl = xl + g2l * y[:B * S].reshape(B, S, D)
            xc = xc_mid + g2c * y[B * S:].reshape(B, L, D)
    return rms_norm(xl, final_g)
```

```python
import functools

import jax
import jax.numpy as jnp
from jax import lax
from jax.experimental import pallas as pl
from jax.experimental.pallas import tpu as pltpu

f32 = jnp.float32
bf16 = jnp.bfloat16
i32 = jnp.int32

GRID_W = 64
EPS = 1e-6
N_Q_HEADS = 8
HEAD_DIM = 64
WINDOW = 128
ROPE_BASE = 10000.0
ATT_Q_W = 512
ATT_KV_W = 128
SGU_HEADS = 4
SGU_W = 256
SGU_CHUNK = 128
POOL_CH = 256
CONV_CH = 256
MIX_WIDTH = 1280
TOP_K = 4
SWIGLU_LIMIT = 7.0
SWIGLU_ALPHA = 1.702
SQRT_HALF = 0.7071067811865476

LANES = 128
SUBLANES = 8
VMEM_LIMIT_BYTES = 56 * 1024 * 1024

INPROJ_TOKENS = 512
ATTN_TOKENS = 128
MIXER_TOKENS = 256
DISPATCH_TOKENS = 512
COMBINE_TOKENS = 256
EXPERT_ROWS = 512
HALO = 8


def _params(*sem):
    return pltpu.CompilerParams(dimension_semantics=sem, vmem_limit_bytes=VMEM_LIMIT_BYTES)


def _full(a):
    nd = a.ndim
    return pl.BlockSpec(a.shape, lambda *_: (0,) * nd)


def _gelu(x):
    return 0.5 * x * (1.0 + lax.erf(x * SQRT_HALF))


def _ada_kernel(c_ref, w_ref, b_ref, o_ref):
    c = c_ref[...]
    s = c * jax.nn.sigmoid(c)
    o_ref[...] = jnp.dot(s, w_ref[...], precision=lax.Precision.HIGHEST, preferred_element_type=f32) + b_ref[...]


def _ada(cvec, ada_w, ada_b):
    depth, d, n = ada_w.shape
    tn = 1536
    return pl.pallas_call(
        _ada_kernel,
        grid=(depth, n // tn),
        in_specs=[pl.BlockSpec(cvec.shape, lambda l, j: (0, 0)),
                  pl.BlockSpec((None, d, tn), lambda l, j: (l, 0, j)),
                  pl.BlockSpec((None, 1, tn), lambda l, j: (l, 0, j))],
        out_specs=pl.BlockSpec((None, cvec.shape[0], tn), lambda l, j: (l, 0, j)),
        out_shape=jax.ShapeDtypeStruct((depth, cvec.shape[0], n), f32),
        compiler_params=_params("arbitrary", "arbitrary"),
        name="ada_mod",
    )(cvec, ada_w, ada_b.reshape(depth, 1, n))


def _inproj_kernel(x_ref, sh_ref, sc_ref, g_ref, w_ref, cos_ref, sin_ref, lng_ref, lnb_ref,
                   q_ref, kv_ref, vn_ref, mixf_ref):
    x = x_ref[...]
    y = x * lax.rsqrt(jnp.mean(x * x, axis=-1, keepdims=True) + EPS) * g_ref[...]
    h = y * (1.0 + sc_ref[...]) + sh_ref[...]
    p = jnp.dot(h.astype(bf16), w_ref[...], preferred_element_type=f32)

    cos = cos_ref[...]
    sin = sin_ref[...]
    lane = lax.broadcasted_iota(i32, cos.shape, 1)
    first_half = (lane & (HEAD_DIM - 1)) < HEAD_DIM // 2

    def rope(t):
        partner = jnp.where(first_half, pltpu.roll(t, LANES - HEAD_DIM // 2, 1), pltpu.roll(t, HEAD_DIM // 2, 1))
        return t * cos + partner * sin

    scale = HEAD_DIM ** -0.5
    for m in range(ATT_Q_W // LANES):
        q_ref[:, m * LANES:(m + 1) * LANES] = (rope(p[:, m * LANES:(m + 1) * LANES]) * scale).astype(bf16)
    k = rope(p[:, 512:640])
    v = p[:, 640:768]
    kv_ref[:, 0:128] = k.astype(bf16)
    kv_ref[:, 128:256] = pltpu.roll(k, HEAD_DIM, 1).astype(bf16)
    kv_ref[:, 256:384] = v.astype(bf16)
    kv_ref[:, 384:512] = pltpu.roll(v, HEAD_DIM, 1).astype(bf16)

    u = _gelu(p[:, 768:1024])
    gv = _gelu(p[:, 1024:1280])
    mu = jnp.mean(gv, axis=-1, keepdims=True)
    var = jnp.mean(jnp.square(gv - mu), axis=-1, keepdims=True)
    vn_ref[...] = ((gv - mu) * lax.rsqrt(var + EPS) * lng_ref[...] + lnb_ref[...]).astype(bf16)

    mixf_ref[:, 0:256] = u
    mixf_ref[:, 256:512] = p[:, 1280:1536]
    mixf_ref[:, 512:768] = p[:, 1536:1792]
    mixf_ref[:, 768:1024] = p[:, 1792:2048] * p[:, 2048:2304]


def _inproj(x2, n_seq, shift, scale, g, w_bf, cos_t, sin_t, ln_g, ln_b):
    t, d = x2.shape
    bt = min(INPROJ_TOKENS, n_seq)
    tps = n_seq // bt
    ncol = w_bf.shape[1]
    row = lambda i: (i, 0)
    return pl.pallas_call(
        _inproj_kernel,
        grid=(t // bt,),
        in_specs=[pl.BlockSpec((bt, d), row),
                  pl.BlockSpec((None, 1, d), lambda i: (i // tps, 0, 0)),
                  pl.BlockSpec((None, 1, d), lambda i: (i // tps, 0, 0)),
                  _full(g),
                  pl.BlockSpec((d, ncol), lambda i: (0, 0)),
                  pl.BlockSpec((bt, LANES), lambda i: (i % tps, 0)),
                  pl.BlockSpec((bt, LANES), lambda i: (i % tps, 0)),
                  _full(ln_g), _full(ln_b)],
        out_specs=[pl.BlockSpec((bt, ATT_Q_W), row), pl.BlockSpec((bt, 512), row),
                   pl.BlockSpec((bt, SGU_W), row), pl.BlockSpec((bt, 1024), row)],
        out_shape=[jax.ShapeDtypeStruct((t, ATT_Q_W), bf16), jax.ShapeDtypeStruct((t, 512), bf16),
                   jax.ShapeDtypeStruct((t, SGU_W), bf16), jax.ShapeDtypeStruct((t, 1024), f32)],
        compiler_params=_params("arbitrary"),
        name="inproj",
    )(x2, shift, scale, g, w_bf, cos_t, sin_t, ln_g, ln_b)


def _attn_body(sink_ref, q_ref, kv, bias, o_ref):
    k_nat, k_swp, v_nat, v_swp = (kv[:, i * LANES:(i + 1) * LANES] for i in range(4))
    lane = lax.broadcasted_iota(i32, (q_ref.shape[0], LANES), 1)
    low = lane < HEAD_DIM
    for m in range(ATT_Q_W // LANES):
        qc = q_ref[:, m * LANES:(m + 1) * LANES]
        outs = []
        for half in range(2):
            h = 2 * m + half
            kvh = h // (N_Q_HEADS // 2)
            qz = jnp.where(low if half == 0 else jnp.logical_not(low), qc, jnp.zeros_like(qc))
            kh = k_nat if kvh == half else k_swp
            vh = v_nat if kvh == half else v_swp
            s = lax.dot_general(qz, kh, (((1,), (1,)), ((), ())), preferred_element_type=f32)
            if bias is not None:
                s = s + bias
            sk = sink_ref[h]
            mx = jnp.maximum(jnp.max(s, axis=1, keepdims=True), sk)
            e = jnp.exp(s - mx)
            den = jnp.sum(e, axis=1, keepdims=True) + jnp.exp(sk - mx)
            pv = jnp.dot(e.astype(bf16), vh, preferred_element_type=f32)
            outs.append(pv / den)
        o_ref[:, m * LANES:(m + 1) * LANES] = jnp.where(low, outs[0], outs[1]).astype(bf16)


def _window_attn_kernel(sink_ref, q_ref, kvp_ref, kvc_ref, kvn_ref, kvx_ref, o_ref, *, nb):
    j = pl.program_id(1)
    kv = jnp.concatenate([kvp_ref[...], kvc_ref[...], kvn_ref[...], kvx_ref[...]], axis=0)
    nband = 3 * ATTN_TOKENS
    r = lax.broadcasted_iota(i32, (ATTN_TOKENS, nband), 0)
    c = lax.broadcasted_iota(i32, (ATTN_TOKENS, nband), 1)
    dlt = c - r
    lo = jnp.where(j == 0, ATTN_TOKENS, 0)
    hi = jnp.where(j == nb - 1, 2 * ATTN_TOKENS, nband)
    valid = (dlt >= 0) & (dlt <= 2 * WINDOW) & (c >= lo) & (c < hi)
    bias = jnp.concatenate([jnp.where(valid, 0.0, -jnp.inf).astype(f32),
                            jnp.zeros((ATTN_TOKENS, kvx_ref.shape[0]), f32)], axis=1)
    _attn_body(sink_ref, q_ref, kv, bias, o_ref)


def _ctx_attn_kernel(sink_ref, q_ref, kvx_ref, o_ref):
    _attn_body(sink_ref, q_ref, kvx_ref[...], None, o_ref)


def _window_attn(q, kv, kv_ctx, sink, n_batch, n_seq, n_ctx):
    t = q.shape[0]
    nb = n_seq // ATTN_TOKENS
    blk = (ATTN_TOKENS, 512)
    return pl.pallas_call(
        functools.partial(_window_attn_kernel, nb=nb),
        grid=(n_batch, nb),
        in_specs=[pl.BlockSpec(memory_space=pltpu.SMEM),
                  pl.BlockSpec(blk, lambda b, j: (b * nb + j, 0)),
                  pl.BlockSpec(blk, lambda b, j: (b * nb + jnp.maximum(j - 1, 0), 0)),
                  pl.BlockSpec(blk, lambda b, j: (b * nb + j, 0)),
                  pl.BlockSpec(blk, lambda b, j: (b * nb + jnp.minimum(j + 1, nb - 1), 0)),
                  pl.BlockSpec((n_ctx, 512), lambda b, j: (b, 0))],
        out_specs=pl.BlockSpec(blk, lambda b, j: (b * nb + j, 0)),
        out_shape=jax.ShapeDtypeStruct((t, ATT_Q_W), bf16),
        compiler_params=_params("arbitrary", "arbitrary"),
        name="window_attn",
    )(sink, q, kv, kv, kv, kv_ctx)


def _ctx_attn(q, kv_ctx, sink, n_batch, n_ctx):
    nb = n_ctx // ATTN_TOKENS
    blk = (ATTN_TOKENS, 512)
    return pl.pallas_call(
        _ctx_attn_kernel,
        grid=(n_batch, nb),
        in_specs=[pl.BlockSpec(memory_space=pltpu.SMEM),
                  pl.BlockSpec(blk, lambda b, j: (b * nb + j, 0)),
                  pl.BlockSpec((n_ctx, 512), lambda b, j: (b, 0))],
        out_specs=pl.BlockSpec(blk, lambda b, j: (b * nb + j, 0)),
        out_shape=jax.ShapeDtypeStruct(q.shape, bf16),
        compiler_params=_params("arbitrary", "arbitrary"),
        name="ctx_attn",
    )(sink, q, kv_ctx)


def _mixer_kernel(attn_ref, vn_ref, mf_ref, mfp_ref, mfn_ref, x_ref, g1_ref, sh2_ref, sc2_ref, n2g_ref,
                  ws_ref, sb_ref, pw_ref, ps_ref, cw_ref, wo_ref, rw_ref, rb_ref, cin_ref,
                  xmid_ref, h2_ref, ti_ref, gt_ref, rk_ref, cout_ref, cnt_ref, *, n_seq, bt):
    i = pl.program_id(0)
    tps = n_seq // bt
    si = i % tps
    first = si == 0
    last = si == tps - 1
    n_ext = bt + 2 * HALO

    @pl.when(i == 0)
    def _():
        cnt_ref[...] = cin_ref[...]

    mf = mf_ref[...]
    u = mf[:, 0:256]

    def extended(lo, hi):
        prev = jnp.where(first, 0.0, mfp_ref[:, lo:hi])
        nxt = jnp.where(last, 0.0, mfn_ref[:, lo:hi])
        return jnp.concatenate([prev, mf[:, lo:hi], nxt], axis=0)

    def shifted(a, s):
        return pltpu.roll(a, s % n_ext, 0)

    xe = extended(256, 512)
    a1 = shifted(xe, 1) + xe
    a2 = shifted(a1, 1) + shifted(a1, -1)
    a3 = shifted(a2, 2) + shifted(a2, -2)
    a4 = shifted(a3, 4) + shifted(a3, -4)
    lane = lax.broadcasted_iota(i32, (bt, POOL_CH), 1)
    grp = lane >> 6
    sl = slice(HALO, HALO + bt)
    wsum = jnp.where(grp == 0, a1[sl], jnp.where(grp == 1, a2[sl], jnp.where(grp == 2, a3[sl], a4[sl])))
    pos = lax.broadcasted_iota(i32, (bt, POOL_CH), 0) + si * bt
    halfw = jnp.left_shift(1, grp)
    cnt = jnp.minimum(pos + halfw, n_seq) - jnp.maximum(pos - halfw, 0)
    dpool = wsum / cnt.astype(f32) - mf[:, 256:512]
    yc = jnp.dot(dpool.astype(bf16), pw_ref[...], preferred_element_type=f32) * ps_ref[...]

    ye = extended(768, 1024)
    cw = cw_ref[...]
    z = shifted(ye, 1) * cw[0:1, :] + ye * cw[1:2, :] + shifted(ye, -1) * cw[2:3, :]
    yd = mf[:, 512:768] * z[sl]

    hgrp = lax.broadcasted_iota(i32, (SGU_CHUNK, SGU_W), 1) >> 6
    ybs = []
    for cidx in range(bt // SGU_CHUNK):
        rows = slice(cidx * SGU_CHUNK, (cidx + 1) * SGU_CHUNK)
        vn_c = vn_ref[rows, :]
        s = jnp.zeros((SGU_CHUNK, SGU_W), f32)
        for hh in range(SGU_HEADS):
            sh = jnp.dot(ws_ref[hh], vn_c, preferred_element_type=f32)
            s = jnp.where(hgrp == hh, sh, s)
        ybs.append(u[rows, :] * (s + sb_ref[...]))
    yb = jnp.concatenate(ybs, axis=0)

    mix = jnp.concatenate([attn_ref[...], yb.astype(bf16), yc.astype(bf16), yd.astype(bf16)], axis=1)
    mo = jnp.dot(mix, wo_ref[...], preferred_element_type=f32)
    xm = x_ref[...] + g1_ref[...] * mo
    xmid_ref[...] = xm

    y = xm * lax.rsqrt(jnp.mean(xm * xm, axis=-1, keepdims=True) + EPS) * n2g_ref[...]
    h2 = y * (1.0 + sc2_ref[...]) + sh2_ref[...]
    h2_ref[...] = h2

    lt = lax.dot_general(rw_ref[...], h2, (((1,), (1,)), ((), ())),
                         precision=lax.Precision.HIGHEST, preferred_element_type=f32) + rb_ref[...]
    n_exp = lt.shape[0]
    eidx = lax.broadcasted_iota(i32, lt.shape, 0)
    work = lt
    idxs, vals = [], []
    for _ in range(TOP_K):
        m = jnp.max(work, axis=0, keepdims=True)
        idx = jnp.min(jnp.where(work == m, eidx, n_exp), axis=0, keepdims=True)
        idxs.append(idx)
        vals.append(m)
        work = jnp.where(eidx == idx, -jnp.inf, work)
    exps = [jnp.exp(v - vals[0]) for v in vals]
    den = exps[0] + exps[1] + exps[2] + exps[3]
    onehot = jnp.zeros(lt.shape, f32)
    for kk in range(TOP_K):
        ti_ref[kk:kk + 1, :] = idxs[kk]
        gt_ref[kk:kk + 1, :] = exps[kk] / den
        onehot = onehot + (eidx == idxs[kk]).astype(f32)
    tri = (lax.broadcasted_iota(i32, (bt, bt), 0) < lax.broadcasted_iota(i32, (bt, bt), 1)).astype(bf16)
    base = jnp.dot(onehot.astype(bf16), tri, preferred_element_type=f32) + cnt_ref[...]
    for kk in range(TOP_K):
        rk_ref[kk:kk + 1, :] = jnp.sum(jnp.where(eidx == idxs[kk], base, 0.0), axis=0, keepdims=True).astype(i32)
    cnt_ref[...] = cnt_ref[...] + jnp.sum(onehot, axis=1, keepdims=True)
    cout_ref[...] = cnt_ref[...]


def _mixer(attn, vn, mixf, x2, n_seq, g1, sh2, sc2, n2g, ws_bf, sgu_bias, pool_bd, pool_scale, conv_w,
           wo_bf, rw_t, rb, cnt_in):
    t, d = x2.shape
    bt = min(MIXER_TOKENS, n_seq)
    tps = n_seq // bt
    hb = bt // HALO
    n_halo = t // HALO
    n_exp = rw_t.shape[0]
    row = lambda i: (i, 0)
    per_batch = pl.BlockSpec((None, 1, d), lambda i: (i // tps, 0, 0))
    col = lambda i: (0, i)
    return pl.pallas_call(
        functools.partial(_mixer_kernel, n_seq=n_seq, bt=bt),
        grid=(t // bt,),
        in_specs=[pl.BlockSpec((bt, ATT_Q_W), row), pl.BlockSpec((bt, SGU_W), row), pl.BlockSpec((bt, 1024), row),
                  pl.BlockSpec((HALO, 1024), lambda i: (jnp.maximum(i * hb - 1, 0), 0)),
                  pl.BlockSpec((HALO, 1024), lambda i: (jnp.minimum((i + 1) * hb, n_halo - 1), 0)),
                  pl.BlockSpec((bt, d), row), per_batch, per_batch, per_batch, _full(n2g),
                  _full(ws_bf), _full(sgu_bias), _full(pool_bd), _full(pool_scale), _full(conv_w),
                  _full(wo_bf), _full(rw_t), _full(rb), _full(cnt_in)],
        out_specs=[pl.BlockSpec((bt, d), row), pl.BlockSpec((bt, d), row),
                   pl.BlockSpec((TOP_K, bt), col), pl.BlockSpec((TOP_K, bt), col), pl.BlockSpec((TOP_K, bt), col),
                   pl.BlockSpec((n_exp, 1), lambda i: (0, 0))],
        out_shape=[jax.ShapeDtypeStruct((t, d), f32), jax.ShapeDtypeStruct((t, d), f32),
                   jax.ShapeDtypeStruct((TOP_K, t), i32), jax.ShapeDtypeStruct((TOP_K, t), f32),
                   jax.ShapeDtypeStruct((TOP_K, t), i32), jax.ShapeDtypeStruct((n_exp, 1), f32)],
        scratch_shapes=[pltpu.VMEM((n_exp, 1), f32)],
        compiler_params=_params("arbitrary"),
        name="mixer_router",
    )(attn, vn, mixf, mixf, mixf, x2, g1, sh2, sc2, n2g, ws_bf, sgu_bias, pool_bd, pool_scale, conv_w,
      wo_bf, rw_t, rb, cnt_in)


def _dest_kernel(ti_ref, rk_ref, ps_ref, o_ref):
    ti = ti_ref[...]
    acc = rk_ref[...]
    for e in range(ps_ref.shape[0]):
        acc = acc + jnp.where(ti == e, ps_ref[e], 0)
    o_ref[...] = acc


def _dest(top_i, rank, pstarts):
    k, t = top_i.shape
    bt = next(b for b in (2048, 1024, 512, 256, 128) if t % b == 0)
    col = lambda i: (0, i)
    return pl.pallas_call(
        _dest_kernel,
        grid=(t // bt,),
        in_specs=[pl.BlockSpec((k, bt), col), pl.BlockSpec((k, bt), col), pl.BlockSpec(memory_space=pltpu.SMEM)],
        out_specs=pl.BlockSpec((k, bt), col),
        out_shape=jax.ShapeDtypeStruct((k, t), i32),
        compiler_params=_params("arbitrary"),
        name="slot_index",
    )(top_i, rank, pstarts)


def _row_copy(src, s, dst, d, sem):
    return pltpu.make_async_copy(src.at[pl.ds(s, 1)], dst.at[pl.ds(d, 1)], sem)


def _dispatch_kernel(dest_ref, h_ref, xs_in_ref, xs_ref, sem, *, bt):
    del xs_in_ref
    base = pl.program_id(0) * bt

    def start(t, carry):
        for kk in range(TOP_K):
            _row_copy(h_ref, base + t, xs_ref, dest_ref[kk, t], sem).start()
        return carry

    lax.fori_loop(0, bt, start, 0)

    def wait(t, carry):
        for kk in range(TOP_K):
            _row_copy(h_ref, 0, xs_ref, 0, sem).wait()
        return carry

    lax.fori_loop(0, bt, wait, 0)


def _dispatch(dest, h2, xs):
    t, d = h2.shape
    bt = min(DISPATCH_TOKENS, t)
    return pl.pallas_call(
        functools.partial(_dispatch_kernel, bt=bt),
        grid=(t // bt,),
        in_specs=[pl.BlockSpec((TOP_K, bt), lambda i: (0, i), memory_space=pltpu.SMEM),
                  pl.BlockSpec(memory_space=pl.ANY), pl.BlockSpec(memory_space=pl.ANY)],
        out_specs=pl.BlockSpec(memory_space=pl.ANY),
        out_shape=jax.ShapeDtypeStruct(xs.shape, xs.dtype),
        scratch_shapes=[pltpu.SemaphoreType.DMA(())],
        input_output_aliases={2: 0},
        compiler_params=_params("arbitrary"),
        name="dispatch",
    )(dest, h2, xs)


def _expert_kernel(be_ref, nu_ref, x_ref, w1g_ref, w1l_ref, b1g_ref, b1l_ref, w2_ref, b2_ref, y_ref):
    del be_ref
    i = pl.program_id(0)

    @pl.when(i < nu_ref[0])
    def _():
        x = x_ref[...].astype(bf16)
        zg = jnp.dot(x, w1g_ref[...], preferred_element_type=f32) + b1g_ref[...]
        zl = jnp.dot(x, w1l_ref[...], preferred_element_type=f32) + b1l_ref[...]
        g = jnp.minimum(zg, SWIGLU_LIMIT)
        lin = jnp.clip(zl, -SWIGLU_LIMIT, SWIGLU_LIMIT)
        a = g * jax.nn.sigmoid(SWIGLU_ALPHA * g) * (lin + 1.0)
        y_ref[...] = jnp.dot(a.astype(bf16), w2_ref[...], preferred_element_type=f32) + b2_ref[...]

    @pl.when(i >= nu_ref[0])
    def _():
        y_ref[...] = jnp.zeros_like(y_ref)


def _experts(block_e, n_used, xs, w1g, w1l, b1g, b1l, w2, b2):
    n_slots, d = xs.shape
    n_exp, _, f = w1g.shape
    bm = EXPERT_ROWS
    xrow = lambda i, be, nu: (jnp.minimum(i, nu[0] - 1), 0)
    wsel = lambda i, be, nu: (be[i], 0, 0)
    grid_spec = pltpu.PrefetchScalarGridSpec(
        num_scalar_prefetch=2,
        grid=(n_slots // bm,),
        in_specs=[pl.BlockSpec((bm, d), xrow),
                  pl.BlockSpec((None, d, f), wsel), pl.BlockSpec((None, d, f), wsel),
                  pl.BlockSpec((None, 1, f), wsel), pl.BlockSpec((None, 1, f), wsel),
                  pl.BlockSpec((None, f, d), wsel), pl.BlockSpec((None, 1, d), wsel)],
        out_specs=pl.BlockSpec((bm, d), lambda i, be, nu: (i, 0)),
    )
    return pl.pallas_call(
        _expert_kernel,
        grid_spec=grid_spec,
        out_shape=jax.ShapeDtypeStruct((n_slots, d), f32),
        compiler_params=_params("arbitrary"),
        name="experts",
    )(block_e, n_used, xs, w1g, w1l, b1g, b1l, w2, b2)


def _combine_kernel(dest_ref, gate_ref, ys_ref, x_ref, g2_ref, fg_ref, o_ref, buf, sem, *, bt, final):
    def start(t, carry):
        for kk in range(TOP_K):
            _row_copy(ys_ref, dest_ref[kk, t], buf.at[kk], t, sem).start()
        return carry

    lax.fori_loop(0, bt, start, 0)

    def wait(t, carry):
        for kk in range(TOP_K):
            _row_copy(ys_ref, 0, buf.at[kk], 0, sem).wait()
        return carry

    lax.fori_loop(0, bt, wait, 0)

    gate = gate_ref[...]
    acc = buf[0] * gate[:, 0:1]
    for kk in range(1, TOP_K):
        acc = acc + buf[kk] * gate[:, kk:kk + 1]
    xo = x_ref[...] + g2_ref[...] * acc
    if final:
        xo = xo * lax.rsqrt(jnp.mean(xo * xo, axis=-1, keepdims=True) + EPS) * fg_ref[...]
    o_ref[...] = xo


def _combine(dest, gates, ys, x_mid, n_seq, g2, final_g, final):
    t, d = x_mid.shape
    bt = min(COMBINE_TOKENS, n_seq)
    tps = n_seq // bt
    row = lambda i: (i, 0)
    return pl.pallas_call(
        functools.partial(_combine_kernel, bt=bt, final=final),
        grid=(t // bt,),
        in_specs=[pl.BlockSpec((TOP_K, bt), lambda i: (0, i), memory_space=pltpu.SMEM),
                  pl.BlockSpec((bt, TOP_K), row),
                  pl.BlockSpec(memory_space=pl.ANY),
                  pl.BlockSpec((bt, d), row),
                  pl.BlockSpec((None, 1, d), lambda i: (i // tps, 0, 0)),
                  _full(final_g)],
        out_specs=pl.BlockSpec((bt, d), row),
        out_shape=jax.ShapeDtypeStruct((t, d), f32),
        scratch_shapes=[pltpu.VMEM((TOP_K, bt, d), f32), pltpu.SemaphoreType.DMA(())],
        compiler_params=_params("arbitrary"),
        name="combine",
    )(dest, gates, ys, x_mid, g2, final_g)


def _rope_tables(n_tokens):
    rows = n_tokens // GRID_W
    row = jnp.repeat(jnp.arange(rows), GRID_W).astype(f32)
    col = jnp.tile(jnp.arange(GRID_W), rows).astype(f32)
    n_freq = HEAD_DIM // 4
    inv = ROPE_BASE ** (-jnp.arange(n_freq, dtype=f32) / n_freq)
    ang = jnp.concatenate([row[:, None] * inv, col[:, None] * inv], axis=-1)
    cos, sin = jnp.cos(ang), jnp.sin(ang)
    return jnp.tile(cos, (1, 4)), jnp.concatenate([-sin, sin, -sin, sin], axis=-1)


def kernel(x, c, ctx, c_ctx, norm1_g, norm2_g, ada_w, ada_b, w_in, attn_sink, sgu_ws, sgu_b, sgu_ln_g, sgu_ln_b,
           pool_w, pool_scale, conv_w, w_out, router_w, router_b, exp_w1, exp_b1, exp_w2, exp_b2, final_g):
    n_batch, n_seq, d = x.shape
    n_ctx = ctx.shape[1]
    depth = ada_w.shape[0]
    n_exp = router_w.shape[2]
    d_ff = exp_w2.shape[2]
    t_lat, t_ctx = n_batch * n_seq, n_batch * n_ctx
    bm = EXPERT_ROWS

    cvec = jnp.concatenate([c, c_ctx[None, :], jnp.zeros((SUBLANES - n_batch - 1, d), f32)], axis=0)
    mods = _ada(cvec, ada_w, ada_b)

    cos_l, sin_l = _rope_tables(n_seq)
    cos_c, sin_c = jnp.ones((n_ctx, LANES), f32), jnp.zeros((n_ctx, LANES), f32)

    xl = x.reshape(t_lat, d)
    xc = ctx.reshape(t_ctx, d)
    row2 = lambda a: a.reshape(1, -1)

    for l in range(depth):
        last = l == depth - 1
        ml = mods[l, :n_batch].reshape(n_batch, 6, 1, d)
        mc = jnp.broadcast_to(mods[l, n_batch].reshape(1, 6, 1, d), (n_batch, 6, 1, d))
        sh1l, sc1l, g1l, sh2l, sc2l, g2l = (ml[:, i] for i in range(6))
        sh1c, sc1c, g1c, sh2c, sc2c, g2c = (mc[:, i] for i in range(6))

        w_in_bf = w_in[l].astype(bf16)
        wo_bf = w_out[l].astype(bf16)
        ws_bf = sgu_ws[l].astype(bf16)
        sgu_bias = jnp.repeat(sgu_b[l].T, SGU_W // SGU_HEADS, axis=1)
        pool_bd = jax.scipy.linalg.block_diag(*[pool_w[l, g] for g in range(pool_w.shape[1])]).astype(bf16)
        n1g, n2g = row2(norm1_g[l]), row2(norm2_g[l])
        lng, lnb, psc = row2(sgu_ln_g[l]), row2(sgu_ln_b[l]), row2(pool_scale[l])
        rw_t = router_w[l].T
        rb = router_b[l].reshape(n_exp, 1)
        sink = attn_sink[l]
        mix_w = (ws_bf, sgu_bias, pool_bd, psc, conv_w[l], wo_bf, rw_t, rb)

        qc, kvc, vnc, mfc = _inproj(xc, n_ctx, sh1c, sc1c, n1g, w_in_bf, cos_c, sin_c, lng, lnb)
        ql, kvl, vnl, mfl = _inproj(xl, n_seq, sh1l, sc1l, n1g, w_in_bf, cos_l, sin_l, lng, lnb)
        attn_l = _window_attn(ql, kvl, kvc, sink, n_batch, n_seq, n_ctx)
        cnt0 = jnp.zeros((n_exp, 1), f32)
        xmid_l, h2_l, ti, gt, rk, cnt = _mixer(attn_l, vnl, mfl, xl, n_seq, g1l, sh2l, sc2l, n2g, *mix_w, cnt0)
        if not last:
            attn_c = _ctx_attn(qc, kvc, sink, n_batch, n_ctx)
            xmid_c, h2_c, ti_c, gt_c, rk_c, cnt = _mixer(attn_c, vnc, mfc, xc, n_ctx, g1c, sh2c, sc2c, n2g,
                                                         *mix_w, cnt)
            ti = jnp.concatenate([ti, ti_c], axis=1)
            gt = jnp.concatenate([gt, gt_c], axis=1)
            rk = jnp.concatenate([rk, rk_c], axis=1)
        t_all = ti.shape[1]

        counts = cnt[:, 0].astype(i32)
        padded = (counts + bm - 1) // bm * bm
        pends = jnp.cumsum(padded)
        pstarts = pends - padded
        n_blocks = -(-(t_all * TOP_K) // bm) + n_exp
        block_e = jnp.minimum(jnp.searchsorted(pends, jnp.arange(n_blocks, dtype=i32) * bm, side='right'),
                              n_exp - 1).astype(i32)
        n_used = (pends[-1:] // bm).astype(i32)
        dest = _dest(ti, rk, pstarts)
        gates = gt.T

        xs = jnp.zeros((n_blocks * bm, d), f32)
        xs = _dispatch(dest[:, :t_lat], h2_l, xs)
        if not last:
            xs = _dispatch(dest[:, t_lat:], h2_c, xs)

        w1 = exp_w1[l]
        w1g, w1l = w1[:, :, 0::2].astype(bf16), w1[:, :, 1::2].astype(bf16)
        b1 = exp_b1[l]
        b1g, b1l = b1[:, None, 0::2], b1[:, None, 1::2]
        ys = _experts(block_e, n_used, xs, w1g, w1l, b1g, b1l, exp_w2[l].astype(bf16), exp_b2[l][:, None, :])

        fg = row2(final_g)
        xl = _combine(dest[:, :t_lat], gates[:t_lat], ys, xmid_l, n_seq, g2l, fg, last)
        if not last:
            xc = _combine(dest[:, t_lat:], gates[t_lat:], ys, xmid_c, n_ctx, g2c, fg, False)

    return xl.reshape(n_batch, n_seq, d)
```

```python
import functools

import jax
import jax.numpy as jnp
from jax import lax
from jax.experimental import pallas as pl
from jax.experimental.pallas import tpu as pltpu

f32 = jnp.float32
bf16 = jnp.bfloat16
i32 = jnp.int32

GRID_W = 64
EPS = 1e-6
N_Q_HEADS = 8
HEAD_DIM = 64
WINDOW = 128
ROPE_BASE = 10000.0
ATT_Q_W = 512
ATT_KV_W = 128
SGU_HEADS = 4
SGU_W = 256
SGU_CHUNK = 128
POOL_CH = 256
CONV_CH = 256
MIX_WIDTH = 1280
TOP_K = 4
SWIGLU_LIMIT = 7.0
SWIGLU_ALPHA = 1.702
SQRT_HALF = 0.7071067811865476

LANES = 128
SUBLANES = 8
VMEM_LIMIT_BYTES = 56 * 1024 * 1024

INPROJ_TOKENS = 512
ATTN_TOKENS = 128
MIXER_TOKENS = 256
DISPATCH_TOKENS = 512
COMBINE_TOKENS = 256
EXPERT_ROWS = 512
HALO = 8


def _params(*sem):
    return pltpu.CompilerParams(dimension_semantics=sem, vmem_limit_bytes=VMEM_LIMIT_BYTES)


def _full(a):
    nd = a.ndim
    return pl.BlockSpec(a.shape, lambda *_: (0,) * nd)


def _gelu(x):
    return 0.5 * x * (1.0 + lax.erf(x * SQRT_HALF))


def _ada_kernel(c_ref, w_ref, b_ref, o_ref):
    c = c_ref[...]
    s = c * jax.nn.sigmoid(c)
    o_ref[...] = jnp.dot(s, w_ref[...], precision=lax.Precision.HIGHEST, preferred_element_type=f32) + b_ref[...]


def _ada(cvec, ada_w, ada_b):
    depth, d, n = ada_w.shape
    tn = 1536
    return pl.pallas_call(
        _ada_kernel,
        grid=(depth, n // tn),
        in_specs=[pl.BlockSpec(cvec.shape, lambda l, j: (0, 0)),
                  pl.BlockSpec((None, d, tn), lambda l, j: (l, 0, j)),
                  pl.BlockSpec((None, 1, tn), lambda l, j: (l, 0, j))],
        out_specs=pl.BlockSpec((None, cvec.shape[0], tn), lambda l, j: (l, 0, j)),
        out_shape=jax.ShapeDtypeStruct((depth, cvec.shape[0], n), f32),
        compiler_params=_params("arbitrary", "arbitrary"),
        name="ada_mod",
    )(cvec, ada_w, ada_b.reshape(depth, 1, n))


def _inproj_kernel(x_ref, sh_ref, sc_ref, g_ref, w_ref, cos_ref, sin_ref, lng_ref, lnb_ref,
                   q_ref, kv_ref, vn_ref, mixf_ref):
    x = x_ref[...]
    y = x * lax.rsqrt(jnp.mean(x * x, axis=-1, keepdims=True) + EPS) * g_ref[...]
    h = y * (1.0 + sc_ref[...]) + sh_ref[...]
    p = jnp.dot(h.astype(bf16), w_ref[...], preferred_element_type=f32)

    cos = cos_ref[...]
    sin = sin_ref[...]
    lane = lax.broadcasted_iota(i32, cos.shape, 1)
    first_half = (lane & (HEAD_DIM - 1)) < HEAD_DIM // 2

    def rope(t):
        partner = jnp.where(first_half, pltpu.roll(t, LANES - HEAD_DIM // 2, 1), pltpu.roll(t, HEAD_DIM // 2, 1))
        return t * cos + partner * sin

    scale = HEAD_DIM ** -0.5
    for m in range(ATT_Q_W // LANES):
        q_ref[:, m * LANES:(m + 1) * LANES] = (rope(p[:, m * LANES:(m + 1) * LANES]) * scale).astype(bf16)
    k = rope(p[:, 512:640])
    v = p[:, 640:768]
    kv_ref[:, 0:128] = k.astype(bf16)
    kv_ref[:, 128:256] = pltpu.roll(k, HEAD_DIM, 1).astype(bf16)
    kv_ref[:, 256:384] = v.astype(bf16)
    kv_ref[:, 384:512] = pltpu.roll(v, HEAD_DIM, 1).astype(bf16)

    u = _gelu(p[:, 768:1024])
    gv = _gelu(p[:, 1024:1280])
    mu = jnp.mean(gv, axis=-1, keepdims=True)
    var = jnp.mean(jnp.square(gv - mu), axis=-1, keepdims=True)
    vn_ref[...] = ((gv - mu) * lax.rsqrt(var + EPS) * lng_ref[...] + lnb_ref[...]).astype(bf16)

    mixf_ref[:, 0:256] = u
    mixf_ref[:, 256:512] = p[:, 1280:1536]
    mixf_ref[:, 512:768] = p[:, 1536:1792]
    mixf_ref[:, 768:1024] = p[:, 1792:2048] * p[:, 2048:2304]


def _inproj(x2, n_seq, shift, scale, g, w_bf, cos_t, sin_t, ln_g, ln_b):
    t, d = x2.shape
    bt = min(INPROJ_TOKENS, n_seq)
    tps = n_seq // bt
    ncol = w_bf.shape[1]
    row = lambda i: (i, 0)
    return pl.pallas_call(
        _inproj_kernel,
        grid=(t // bt,),
        in_specs=[pl.BlockSpec((bt, d), row),
                  pl.BlockSpec((None, 1, d), lambda i: (i // tps, 0, 0)),
                  pl.BlockSpec((None, 1, d), lambda i: (i // tps, 0, 0)),
                  _full(g),
                  pl.BlockSpec((d, ncol), lambda i: (0, 0)),
                  pl.BlockSpec((bt, LANES), lambda i: (i % tps, 0)),
                  pl.BlockSpec((bt, LANES), lambda i: (i % tps, 0)),
                  _full(ln_g), _full(ln_b)],
        out_specs=[pl.BlockSpec((bt, ATT_Q_W), row), pl.BlockSpec((bt, 512), row),
                   pl.BlockSpec((bt, SGU_W), row), pl.BlockSpec((bt, 1024), row)],
        out_shape=[jax.ShapeDtypeStruct((t, ATT_Q_W), bf16), jax.ShapeDtypeStruct((t, 512), bf16),
                   jax.ShapeDtypeStruct((t, SGU_W), bf16), jax.ShapeDtypeStruct((t, 1024), f32)],
        compiler_params=_params("arbitrary"),
        name="inproj",
    )(x2, shift, scale, g, w_bf, cos_t, sin_t, ln_g, ln_b)


def _attn_body(sink_ref, q_ref, kv, bias, o_ref):
    k_nat, k_swp, v_nat, v_swp = (kv[:, i * LANES:(i + 1) * LANES] for i in range(4))
    lane = lax.broadcasted_iota(i32, (q_ref.shape[0], LANES), 1)
    low = lane < HEAD_DIM
    for m in range(ATT_Q_W // LANES):
        qc = q_ref[:, m * LANES:(m + 1) * LANES]
        outs = []
        for half in range(2):
            h = 2 * m + half
            kvh = h // (N_Q_HEADS // 2)
            qz = jnp.where(low if half == 0 else jnp.logical_not(low), qc, jnp.zeros_like(qc))
            kh = k_nat if kvh == half else k_swp
            vh = v_nat if kvh == half else v_swp
            s = lax.dot_general(qz, kh, (((1,), (1,)), ((), ())), preferred_element_type=f32)
            if bias is not None:
                s = s + bias
            sk = sink_ref[h]
            mx = jnp.maximum(jnp.max(s, axis=1, keepdims=True), sk)
            e = jnp.exp(s - mx)
            den = jnp.sum(e, axis=1, keepdims=True) + jnp.exp(sk - mx)
            pv = jnp.dot(e.astype(bf16), vh, preferred_element_type=f32)
            outs.append(pv / den)
        o_ref[:, m * LANES:(m + 1) * LANES] = jnp.where(low, outs[0], outs[1]).astype(bf16)


def _window_attn_kernel(sink_ref, q_ref, kvp_ref, kvc_ref, kvn_ref, kvx_ref, o_ref, *, nb):
    j = pl.program_id(1)
    kv = jnp.concatenate([kvp_ref[...], kvc_ref[...], kvn_ref[...], kvx_ref[...]], axis=0)
    nband = 3 * ATTN_TOKENS
    r = lax.broadcasted_iota(i32, (ATTN_TOKENS, nband), 0)
    c = lax.broadcasted_iota(i32, (ATTN_TOKENS, nband), 1)
    dlt = c - r
    lo = jnp.where(j == 0, ATTN_TOKENS, 0)
    hi = jnp.where(j == nb - 1, 2 * ATTN_TOKENS, nband)
    valid = (dlt >= 0) & (dlt <= 2 * WINDOW) & (c >= lo) & (c < hi)
    bias = jnp.concatenate([jnp.where(valid, 0.0, -jnp.inf).astype(f32),
                            jnp.zeros((ATTN_TOKENS, kvx_ref.shape[0]), f32)], axis=1)
    _attn_body(sink_ref, q_ref, kv, bias, o_ref)


def _ctx_attn_kernel(sink_ref, q_ref, kvx_ref, o_ref):
    _attn_body(sink_ref, q_ref, kvx_ref[...], None, o_ref)


def _window_attn(q, kv, kv_ctx, sink, n_batch, n_seq, n_ctx):
    t = q.shape[0]
    nb = n_seq // ATTN_TOKENS
    blk = (ATTN_TOKENS, 512)
    return pl.pallas_call(
        functools.partial(_window_attn_kernel, nb=nb),
        grid=(n_batch, nb),
        in_specs=[pl.BlockSpec(memory_space=pltpu.SMEM),
                  pl.BlockSpec(blk, lambda b, j: (b * nb + j, 0)),
                  pl.BlockSpec(blk, lambda b, j: (b * nb + jnp.maximum(j - 1, 0), 0)),
                  pl.BlockSpec(blk, lambda b, j: (b * nb + j, 0)),
                  pl.BlockSpec(blk, lambda b, j: (b * nb + jnp.minimum(j + 1, nb - 1), 0)),
                  pl.BlockSpec((n_ctx, 512), lambda b, j: (b, 0))],
        out_specs=pl.BlockSpec(blk, lambda b, j: (b * nb + j, 0)),
        out_shape=jax.ShapeDtypeStruct((t, ATT_Q_W), bf16),
        compiler_params=_params("arbitrary", "arbitrary"),
        name="window_attn",
    )(sink, q, kv, kv, kv, kv_ctx)


def _ctx_attn(q, kv_ctx, sink, n_batch, n_ctx):
    nb = n_ctx // ATTN_TOKENS
    blk = (ATTN_TOKENS, 512)
    return pl.pallas_call(
        _ctx_attn_kernel,
        grid=(n_batch, nb),
        in_specs=[pl.BlockSpec(memory_space=pltpu.SMEM),
                  pl.BlockSpec(blk, lambda b, j: (b * nb + j, 0)),
                  pl.BlockSpec((n_ctx, 512), lambda b, j: (b, 0))],
        out_specs=pl.BlockSpec(blk, lambda b, j: (b * nb + j, 0)),
        out_shape=jax.ShapeDtypeStruct(q.shape, bf16),
        compiler_params=_params("arbitrary", "arbitrary"),
        name="ctx_attn",
    )(sink, q, kv_ctx)


def _mixer_kernel(attn_ref, vn_ref, mf_ref, mfp_ref, mfn_ref, x_ref, g1_ref, sh2_ref, sc2_ref, n2g_ref,
                  ws_ref, sb_ref, pw_ref, ps_ref, cw_ref, wo_ref, rw_ref, rb_ref, cin_ref,
                  xmid_ref, h2_ref, ti_ref, gt_ref, rk_ref, cout_ref, cnt_ref, *, n_seq, bt):
    i = pl.program_id(0)
    tps = n_seq // bt
    si = i % tps
    first = si == 0
    last = si == tps - 1
    n_ext = bt + 2 * HALO

    @pl.when(i == 0)
    def _():
        cnt_ref[...] = cin_ref[...]

    mf = mf_ref[...]
    u = mf[:, 0:256]

    def extended(lo, hi):
        prev = jnp.where(first, 0.0, mfp_ref[:, lo:hi])
        nxt = jnp.where(last, 0.0, mfn_ref[:, lo:hi])
        return jnp.concatenate([prev, mf[:, lo:hi], nxt], axis=0)

    def shifted(a, s):
        return pltpu.roll(a, s % n_ext, 0)

    xe = extended(256, 512)
    a1 = shifted(xe, 1) + xe
    a2 = shifted(a1, 1) + shifted(a1, -1)
    a3 = shifted(a2, 2) + shifted(a2, -2)
    a4 = shifted(a3, 4) + shifted(a3, -4)
    lane = lax.broadcasted_iota(i32, (bt, POOL_CH), 1)
    grp = lane >> 6
    sl = slice(HALO, HALO + bt)
    wsum = jnp.where(grp == 0, a1[sl], jnp.where(grp == 1, a2[sl], jnp.where(grp == 2, a3[sl], a4[sl])))
    pos = lax.broadcasted_iota(i32, (bt, POOL_CH), 0) + si * bt
    halfw = jnp.left_shift(1, grp)
    cnt = jnp.minimum(pos + halfw, n_seq) - jnp.maximum(pos - halfw, 0)
    dpool = wsum / cnt.astype(f32) - mf[:, 256:512]
    yc = jnp.dot(dpool.astype(bf16), pw_ref[...], preferred_element_type=f32) * ps_ref[...]

    ye = extended(768, 1024)
    cw = cw_ref[...]
    z = shifted(ye, 1) * cw[0:1, :] + ye * cw[1:2, :] + shifted(ye, -1) * cw[2:3, :]
    yd = mf[:, 512:768] * z[sl]

    hgrp = lax.broadcasted_iota(i32, (SGU_CHUNK, SGU_W), 1) >> 6
    ybs = []
    for cidx in range(bt // SGU_CHUNK):
        rows = slice(cidx * SGU_CHUNK, (cidx + 1) * SGU_CHUNK)
        vn_c = vn_ref[rows, :]
        s = jnp.zeros((SGU_CHUNK, SGU_W), f32)
        for hh in range(SGU_HEADS):
            sh = jnp.dot(ws_ref[hh], vn_c, preferred_element_type=f32)
            s = jnp.where(hgrp == hh, sh, s)
        ybs.append(u[rows, :] * (s + sb_ref[...]))
    yb = jnp.concatenate(ybs, axis=0)

    mix = jnp.concatenate([attn_ref[...], yb.astype(bf16), yc.astype(bf16), yd.astype(bf16)], axis=1)
    mo = jnp.dot(mix, wo_ref[...], preferred_element_type=f32)
    xm = x_ref[...] + g1_ref[...] * mo
    xmid_ref[...] = xm

    y = xm * lax.rsqrt(jnp.mean(xm * xm, axis=-1, keepdims=True) + EPS) * n2g_ref[...]
    h2 = y * (1.0 + sc2_ref[...]) + sh2_ref[...]
    h2_ref[...] = h2

    lt = lax.dot_general(rw_ref[...], h2, (((1,), (1,)), ((), ())),
                         precision=lax.Precision.HIGHEST, preferred_element_type=f32) + rb_ref[...]
    n_exp = lt.shape[0]
    eidx = lax.broadcasted_iota(i32, lt.shape, 0)
    work = lt
    idxs, vals = [], []
    for _ in range(TOP_K):
        m = jnp.max(work, axis=0, keepdims=True)
        idx = jnp.min(jnp.where(work == m, eidx, n_exp), axis=0, keepdims=True)
        idxs.append(idx)
        vals.append(m)
        work = jnp.where(eidx == idx, -jnp.inf, work)
    exps = [jnp.exp(v - vals[0]) for v in vals]
    den = exps[0] + exps[1] + exps[2] + exps[3]
    onehot = jnp.zeros(lt.shape, f32)
    for kk in range(TOP_K):
        ti_ref[kk:kk + 1, :] = idxs[kk]
        gt_ref[kk:kk + 1, :] = exps[kk] / den
        onehot = onehot + (eidx == idxs[kk]).astype(f32)
    tri = (lax.broadcasted_iota(i32, (bt, bt), 0) < lax.broadcasted_iota(i32, (bt, bt), 1)).astype(bf16)
    base = jnp.dot(onehot.astype(bf16), tri, preferred_element_type=f32) + cnt_ref[...]
    for kk in range(TOP_K):
        rk_ref[kk:kk + 1, :] = jnp.sum(jnp.where(eidx == idxs[kk], base, 0.0), axis=0, keepdims=True).astype(i32)
    cnt_ref[...] = cnt_ref[...] + jnp.sum(onehot, axis=1, keepdims=True)
    cout_ref[...] = cnt_ref[...]


def _mixer(attn, vn, mixf, x2, n_seq, g1, sh2, sc2, n2g, ws_bf, sgu_bias, pool_bd, pool_scale, conv_w,
           wo_bf, rw_t, rb, cnt_in):
    t, d = x2.shape
    bt = min(MIXER_TOKENS, n_seq)
    tps = n_seq // bt
    hb = bt // HALO
    n_halo = t // HALO
    n_exp = rw_t.shape[0]
    row = lambda i: (i, 0)
    per_batch = pl.BlockSpec((None, 1, d), lambda i: (i // tps, 0, 0))
    col = lambda i: (0, i)
    return pl.pallas_call(
        functools.partial(_mixer_kernel, n_seq=n_seq, bt=bt),
        grid=(t // bt,),
        in_specs=[pl.BlockSpec((bt, ATT_Q_W), row), pl.BlockSpec((bt, SGU_W), row), pl.BlockSpec((bt, 1024), row),
                  pl.BlockSpec((HALO, 1024), lambda i: (jnp.maximum(i * hb - 1, 0), 0)),
                  pl.BlockSpec((HALO, 1024), lambda i: (jnp.minimum((i + 1) * hb, n_halo - 1), 0)),
                  pl.BlockSpec((bt, d), row), per_batch, per_batch, per_batch, _full(n2g),
                  _full(ws_bf), _full(sgu_bias), _full(pool_bd), _full(pool_scale), _full(conv_w),
                  _full(wo_bf), _full(rw_t), _full(rb), _full(cnt_in)],
        out_specs=[pl.BlockSpec((bt, d), row), pl.BlockSpec((bt, d), row),
                   pl.BlockSpec((TOP_K, bt), col), pl.BlockSpec((TOP_K, bt), col), pl.BlockSpec((TOP_K, bt), col),
                   pl.BlockSpec((n_exp, 1), lambda i: (0, 0))],
        out_shape=[jax.ShapeDtypeStruct((t, d), f32), jax.ShapeDtypeStruct((t, d), f32),
                   jax.ShapeDtypeStruct((TOP_K, t), i32), jax.ShapeDtypeStruct((TOP_K, t), f32),
                   jax.ShapeDtypeStruct((TOP_K, t), i32), jax.ShapeDtypeStruct((n_exp, 1), f32)],
        scratch_shapes=[pltpu.VMEM((n_exp, 1), f32)],
        compiler_params=_params("arbitrary"),
        name="mixer_router",
    )(attn, vn, mixf, mixf, mixf, x2, g1, sh2, sc2, n2g, ws_bf, sgu_bias, pool_bd, pool_scale, conv_w,
      wo_bf, rw_t, rb, cnt_in)


def _dest_kernel(ti_ref, rk_ref, ps_ref, o_ref):
    ti = ti_ref[...]
    acc = rk_ref[...]
    for e in range(ps_ref.shape[0]):
        acc = acc + jnp.where(ti == e, ps_ref[e], 0)
    o_ref[...] = acc


def _dest(top_i, rank, pstarts):
    k, t = top_i.shape
    bt = next(b for b in (2048, 1024, 512, 256, 128) if t % b == 0)
    col = lambda i: (0, i)
    return pl.pallas_call(
        _dest_kernel,
        grid=(t // bt,),
        in_specs=[pl.BlockSpec((k, bt), col), pl.BlockSpec((k, bt), col), pl.BlockSpec(memory_space=pltpu.SMEM)],
        out_specs=pl.BlockSpec((k, bt), col),
        out_shape=jax.ShapeDtypeStruct((k, t), i32),
        compiler_params=_params("arbitrary"),
        name="slot_index",
    )(top_i, rank, pstarts)


def _row_copy(src, s, dst, d, sem):
    return pltpu.make_async_copy(src.at[pl.ds(s, 1)], dst.at[pl.ds(d, 1)], sem)


def _dispatch_kernel(dest_ref, h_ref, xs_in_ref, xs_ref, sem, *, bt):
    del xs_in_ref

    def start(t, carry):
        for kk in range(TOP_K):
            _row_copy(h_ref, t, xs_ref, dest_ref[kk, t], sem).start()
        return carry

    lax.fori_loop(0, bt, start, 0)

    def wait(t, carry):
        for kk in range(TOP_K):
            _row_copy(h_ref, 0, xs_ref, 0, sem).wait()
        return carry

    lax.fori_loop(0, bt, wait, 0)


def _dispatch(dest, h2, xs):
    t, d = h2.shape
    bt = min(DISPATCH_TOKENS, t)
    return pl.pallas_call(
        functools.partial(_dispatch_kernel, bt=bt),
        grid=(t // bt,),
        in_specs=[pl.BlockSpec((TOP_K, bt), lambda i: (0, i), memory_space=pltpu.SMEM),
                  pl.BlockSpec((bt, d), lambda i: (i, 0)), pl.BlockSpec(memory_space=pl.ANY)],
        out_specs=pl.BlockSpec(memory_space=pl.ANY),
        out_shape=jax.ShapeDtypeStruct(xs.shape, xs.dtype),
        scratch_shapes=[pltpu.SemaphoreType.DMA(())],
        input_output_aliases={2: 0},
        compiler_params=_params("arbitrary"),
        name="dispatch",
    )(dest, h2, xs)


def _split_w1_kernel(w_ref, g_ref, l_ref, scr):
    f = g_ref.shape[1]
    for s in range(w_ref.shape[0] // LANES):
        rows = slice(s * LANES, (s + 1) * LANES)
        scr[...] = w_ref[rows, :].T
        g_ref[rows, :] = scr[pl.ds(0, f, stride=2), :].T.astype(bf16)
        l_ref[rows, :] = scr[pl.ds(1, f, stride=2), :].T.astype(bf16)


def _split_w1(w1, layer):
    _, n_exp, d, f2 = w1.shape
    f = f2 // 2
    rows = 2 * LANES
    blk = pl.BlockSpec((None, rows, f), lambda e, j: (e, j, 0))
    return pl.pallas_call(
        _split_w1_kernel,
        grid=(n_exp, d // rows),
        in_specs=[pl.BlockSpec((None, None, rows, f2), lambda e, j: (layer, e, j, 0))],
        out_specs=[blk, blk],
        out_shape=[jax.ShapeDtypeStruct((n_exp, d, f), bf16)] * 2,
        scratch_shapes=[pltpu.VMEM((f2, LANES), f32)],
        compiler_params=_params("arbitrary", "arbitrary"),
        name="split_w1",
    )(w1)


def _expert_kernel(be_ref, nu_ref, x_ref, w1g_ref, w1l_ref, b1g_ref, b1l_ref, w2_ref, b2_ref, y_ref):
    del be_ref
    i = pl.program_id(0)

    @pl.when(i < nu_ref[0])
    def _():
        x = x_ref[...].astype(bf16)
        zg = jnp.dot(x, w1g_ref[...], preferred_element_type=f32) + b1g_ref[...]
        zl = jnp.dot(x, w1l_ref[...], preferred_element_type=f32) + b1l_ref[...]
        g = jnp.minimum(zg, SWIGLU_LIMIT)
        lin = jnp.clip(zl, -SWIGLU_LIMIT, SWIGLU_LIMIT)
        a = g * jax.nn.sigmoid(SWIGLU_ALPHA * g) * (lin + 1.0)
        y_ref[...] = jnp.dot(a.astype(bf16), w2_ref[...], preferred_element_type=f32) + b2_ref[...]

    @pl.when(i >= nu_ref[0])
    def _():
        y_ref[...] = jnp.zeros_like(y_ref)


def _experts(block_e, n_used, xs, w1g, w1l, b1g, b1l, w2, b2):
    n_slots, d = xs.shape
    n_exp, _, f = w1g.shape
    bm = EXPERT_ROWS
    xrow = lambda i, be, nu: (jnp.minimum(i, nu[0] - 1), 0)
    wsel = lambda i, be, nu: (be[i], 0, 0)
    grid_spec = pltpu.PrefetchScalarGridSpec(
        num_scalar_prefetch=2,
        grid=(n_slots // bm,),
        in_specs=[pl.BlockSpec((bm, d), xrow),
                  pl.BlockSpec((None, d, f), wsel), pl.BlockSpec((None, d, f), wsel),
                  pl.BlockSpec((None, 1, f), wsel), pl.BlockSpec((None, 1, f), wsel),
                  pl.BlockSpec((None, f, d), wsel), pl.BlockSpec((None, 1, d), wsel)],
        out_specs=pl.BlockSpec((bm, d), lambda i, be, nu: (i, 0)),
    )
    return pl.pallas_call(
        _expert_kernel,
        grid_spec=grid_spec,
        out_shape=jax.ShapeDtypeStruct((n_slots, d), f32),
        compiler_params=_params("arbitrary"),
        name="experts",
    )(block_e, n_used, xs, w1g, w1l, b1g, b1l, w2, b2)


def _combine_kernel(dest_ref, gate_ref, ys_ref, x_ref, g2_ref, fg_ref, o_ref, buf, sem, *, bt, final):
    def start(t, carry):
        for kk in range(TOP_K):
            _row_copy(ys_ref, dest_ref[kk, t], buf.at[kk], t, sem).start()
        return carry

    lax.fori_loop(0, bt, start, 0)

    def wait(t, carry):
        for kk in range(TOP_K):
            _row_copy(ys_ref, 0, buf.at[kk], 0, sem).wait()
        return carry

    lax.fori_loop(0, bt, wait, 0)

    gate = gate_ref[...]
    acc = buf[0] * gate[:, 0:1]
    for kk in range(1, TOP_K):
        acc = acc + buf[kk] * gate[:, kk:kk + 1]
    xo = x_ref[...] + g2_ref[...] * acc
    if final:
        xo = xo * lax.rsqrt(jnp.mean(xo * xo, axis=-1, keepdims=True) + EPS) * fg_ref[...]
    o_ref[...] = xo


def _combine(dest, gates, ys, x_mid, n_seq, g2, final_g, final):
    t, d = x_mid.shape
    bt = min(COMBINE_TOKENS, n_seq)
    tps = n_seq // bt
    row = lambda i: (i, 0)
    return pl.pallas_call(
        functools.partial(_combine_kernel, bt=bt, final=final),
        grid=(t // bt,),
        in_specs=[pl.BlockSpec((TOP_K, bt), lambda i: (0, i), memory_space=pltpu.SMEM),
                  pl.BlockSpec((bt, TOP_K), row),
                  pl.BlockSpec(memory_space=pl.ANY),
                  pl.BlockSpec((bt, d), row),
                  pl.BlockSpec((None, 1, d), lambda i: (i // tps, 0, 0)),
                  _full(final_g)],
        out_specs=pl.BlockSpec((bt, d), row),
        out_shape=jax.ShapeDtypeStruct((t, d), f32),
        scratch_shapes=[pltpu.VMEM((TOP_K, bt, d), f32), pltpu.SemaphoreType.DMA(())],
        compiler_params=_params("arbitrary"),
        name="combine",
    )(dest, gates, ys, x_mid, g2, final_g)


def _rope_tables(n_tokens):
    rows = n_tokens // GRID_W
    row = jnp.repeat(jnp.arange(rows), GRID_W).astype(f32)
    col = jnp.tile(jnp.arange(GRID_W), rows).astype(f32)
    n_freq = HEAD_DIM // 4
    inv = ROPE_BASE ** (-jnp.arange(n_freq, dtype=f32) / n_freq)
    ang = jnp.concatenate([row[:, None] * inv, col[:, None] * inv], axis=-1)
    cos, sin = jnp.cos(ang), jnp.sin(ang)
    return jnp.tile(cos, (1, 4)), jnp.concatenate([-sin, sin, -sin, sin], axis=-1)


def kernel(x, c, ctx, c_ctx, norm1_g, norm2_g, ada_w, ada_b, w_in, attn_sink, sgu_ws, sgu_b, sgu_ln_g, sgu_ln_b,
           pool_w, pool_scale, conv_w, w_out, router_w, router_b, exp_w1, exp_b1, exp_w2, exp_b2, final_g):
    n_batch, n_seq, d = x.shape
    n_ctx = ctx.shape[1]
    depth = ada_w.shape[0]
    n_exp = router_w.shape[2]
    d_ff = exp_w2.shape[2]
    t_lat, t_ctx = n_batch * n_seq, n_batch * n_ctx
    bm = EXPERT_ROWS

    cvec = jnp.concatenate([c, c_ctx[None, :], jnp.zeros((SUBLANES - n_batch - 1, d), f32)], axis=0)
    mods = _ada(cvec, ada_w, ada_b)

    cos_l, sin_l = _rope_tables(n_seq)
    cos_c, sin_c = jnp.ones((n_ctx, LANES), f32), jnp.zeros((n_ctx, LANES), f32)

    xl = x.reshape(t_lat, d)
    xc = ctx.reshape(t_ctx, d)
    row2 = lambda a: a.reshape(1, -1)

    for l in range(depth):
        last = l == depth - 1
        ml = mods[l, :n_batch].reshape(n_batch, 6, 1, d)
        mc = jnp.broadcast_to(mods[l, n_batch].reshape(1, 6, 1, d), (n_batch, 6, 1, d))
        sh1l, sc1l, g1l, sh2l, sc2l, g2l = (ml[:, i] for i in range(6))
        sh1c, sc1c, g1c, sh2c, sc2c, g2c = (mc[:, i] for i in range(6))

        w_in_bf = w_in[l].astype(bf16)
        wo_bf = w_out[l].astype(bf16)
        ws_bf = sgu_ws[l].astype(bf16)
        sgu_bias = jnp.repeat(sgu_b[l].T, SGU_W // SGU_HEADS, axis=1)
        pool_bd = jax.scipy.linalg.block_diag(*[pool_w[l, g] for g in range(pool_w.shape[1])]).astype(bf16)
        n1g, n2g = row2(norm1_g[l]), row2(norm2_g[l])
        lng, lnb, psc = row2(sgu_ln_g[l]), row2(sgu_ln_b[l]), row2(pool_scale[l])
        rw_t = router_w[l].T
        rb = router_b[l].reshape(n_exp, 1)
        sink = attn_sink[l]
        mix_w = (ws_bf, sgu_bias, pool_bd, psc, conv_w[l], wo_bf, rw_t, rb)

        qc, kvc, vnc, mfc = _inproj(xc, n_ctx, sh1c, sc1c, n1g, w_in_bf, cos_c, sin_c, lng, lnb)
        ql, kvl, vnl, mfl = _inproj(xl, n_seq, sh1l, sc1l, n1g, w_in_bf, cos_l, sin_l, lng, lnb)
        attn_l = _window_attn(ql, kvl, kvc, sink, n_batch, n_seq, n_ctx)
        cnt0 = jnp.zeros((n_exp, 1), f32)
        xmid_l, h2_l, ti, gt, rk, cnt = _mixer(attn_l, vnl, mfl, xl, n_seq, g1l, sh2l, sc2l, n2g, *mix_w, cnt0)
        if not last:
            attn_c = _ctx_attn(qc, kvc, sink, n_batch, n_ctx)
            xmid_c, h2_c, ti_c, gt_c, rk_c, cnt = _mixer(attn_c, vnc, mfc, xc, n_ctx, g1c, sh2c, sc2c, n2g,
                                                         *mix_w, cnt)
            ti = jnp.concatenate([ti, ti_c], axis=1)
            gt = jnp.concatenate([gt, gt_c], axis=1)
            rk = jnp.concatenate([rk, rk_c], axis=1)
        t_all = ti.shape[1]

        counts = cnt[:, 0].astype(i32)
        padded = (counts + bm - 1) // bm * bm
        pends = jnp.cumsum(padded)
        pstarts = pends - padded
        n_blocks = -(-(t_all * TOP_K) // bm) + n_exp
        starts = jnp.arange(n_blocks, dtype=i32) * bm
        block_e = jnp.minimum(jnp.sum((pends[None, :] <= starts[:, None]).astype(i32), axis=1), n_exp - 1)
        n_used = (pends[-1:] // bm).astype(i32)
        dest = _dest(ti, rk, pstarts)
        gates = gt.T

        xs = jnp.zeros((n_blocks * bm, d), f32)
        xs = _dispatch(dest[:, :t_lat], h2_l, xs)
        if not last:
            xs = _dispatch(dest[:, t_lat:], h2_c, xs)

        w1g, w1l = _split_w1(exp_w1, l)
        b1 = exp_b1[l]
        b1g, b1l = b1[:, None, 0::2], b1[:, None, 1::2]
        ys = _experts(block_e, n_used, xs, w1g, w1l, b1g, b1l, exp_w2[l].astype(bf16), exp_b2[l][:, None, :])

        fg = row2(final_g)
        xl = _combine(dest[:, :t_lat], gates[:t_lat], ys, xmid_l, n_seq, g2l, fg, last)
        if not last:
            xc = _combine(dest[:, t_lat:], gates[t_lat:], ys, xmid_c, n_ctx, g2c, fg, False)

    return xl.reshape(n_batch, n_seq, d)
```

```python
import functools

import jax
import jax.numpy as jnp
from jax import lax
from jax.experimental import pallas as pl
from jax.experimental.pallas import tpu as pltpu

f32 = jnp.float32
bf16 = jnp.bfloat16
i32 = jnp.int32

GRID_W = 64
EPS = 1e-6
N_Q_HEADS = 8
HEAD_DIM = 64
WINDOW = 128
ROPE_BASE = 10000.0
ATT_Q_W = 512
ATT_KV_W = 128
SGU_HEADS = 4
SGU_W = 256
SGU_CHUNK = 128
POOL_CH = 256
CONV_CH = 256
MIX_WIDTH = 1280
TOP_K = 4
SWIGLU_LIMIT = 7.0
SWIGLU_ALPHA = 1.702
SQRT_HALF = 0.7071067811865476

LANES = 128
SUBLANES = 8
VMEM_LIMIT_BYTES = 56 * 1024 * 1024

INPROJ_TOKENS = 512
ATTN_TOKENS = 128
MIXER_TOKENS = 256
DISPATCH_TOKENS = 512
COMBINE_TOKENS = 256
EXPERT_ROWS = 512
HALO = 8
DMA_UNROLL = 4


def _params(*sem):
    return pltpu.CompilerParams(dimension_semantics=sem, vmem_limit_bytes=VMEM_LIMIT_BYTES)


def _full(a):
    nd = a.ndim
    return pl.BlockSpec(a.shape, lambda *_: (0,) * nd)


def _store_token_tiles(ref, val):
    n = val.shape[0]
    for s in range(SUBLANES):
        ref[pl.ds(s, n, stride=SUBLANES), :] = val[:, s * LANES:(s + 1) * LANES]


def _load_token_tiles(ref, n):
    return jnp.concatenate([ref[pl.ds(s, n, stride=SUBLANES), :] for s in range(SUBLANES)], axis=1)


def _gelu(x):
    return 0.5 * x * (1.0 + lax.erf(x * SQRT_HALF))


def _ada_kernel(c_ref, w_ref, b_ref, o_ref):
    c = c_ref[...]
    s = c * jax.nn.sigmoid(c)
    o_ref[...] = jnp.dot(s, w_ref[...], precision=lax.Precision.HIGHEST, preferred_element_type=f32) + b_ref[...]


def _ada(cvec, ada_w, ada_b):
    depth, d, n = ada_w.shape
    tn = 1536
    return pl.pallas_call(
        _ada_kernel,
        grid=(depth, n // tn),
        in_specs=[pl.BlockSpec(cvec.shape, lambda l, j: (0, 0)),
                  pl.BlockSpec((None, d, tn), lambda l, j: (l, 0, j)),
                  pl.BlockSpec((None, 1, tn), lambda l, j: (l, 0, j))],
        out_specs=pl.BlockSpec((None, cvec.shape[0], tn), lambda l, j: (l, 0, j)),
        out_shape=jax.ShapeDtypeStruct((depth, cvec.shape[0], n), f32),
        compiler_params=_params("arbitrary", "arbitrary"),
        name="ada_mod",
    )(cvec, ada_w, ada_b.reshape(depth, 1, n))


def _inproj_kernel(x_ref, sh_ref, sc_ref, g_ref, w_ref, cos_ref, sin_ref, lng_ref, lnb_ref,
                   q_ref, kv_ref, vn_ref, mixf_ref):
    x = x_ref[...]
    y = x * lax.rsqrt(jnp.mean(x * x, axis=-1, keepdims=True) + EPS) * g_ref[...]
    h = y * (1.0 + sc_ref[...]) + sh_ref[...]
    p = jnp.dot(h.astype(bf16), w_ref[...], preferred_element_type=f32)

    cos = cos_ref[...]
    sin = sin_ref[...]
    lane = lax.broadcasted_iota(i32, cos.shape, 1)
    first_half = (lane & (HEAD_DIM - 1)) < HEAD_DIM // 2

    def rope(t):
        partner = jnp.where(first_half, pltpu.roll(t, LANES - HEAD_DIM // 2, 1), pltpu.roll(t, HEAD_DIM // 2, 1))
        return t * cos + partner * sin

    scale = HEAD_DIM ** -0.5
    for m in range(ATT_Q_W // LANES):
        q_ref[:, m * LANES:(m + 1) * LANES] = (rope(p[:, m * LANES:(m + 1) * LANES]) * scale).astype(bf16)
    k = rope(p[:, 512:640])
    v = p[:, 640:768]
    kv_ref[:, 0:128] = k.astype(bf16)
    kv_ref[:, 128:256] = pltpu.roll(k, HEAD_DIM, 1).astype(bf16)
    kv_ref[:, 256:384] = v.astype(bf16)
    kv_ref[:, 384:512] = pltpu.roll(v, HEAD_DIM, 1).astype(bf16)

    u = _gelu(p[:, 768:1024])
    gv = _gelu(p[:, 1024:1280])
    mu = jnp.mean(gv, axis=-1, keepdims=True)
    var = jnp.mean(jnp.square(gv - mu), axis=-1, keepdims=True)
    vn_ref[...] = ((gv - mu) * lax.rsqrt(var + EPS) * lng_ref[...] + lnb_ref[...]).astype(bf16)

    mixf_ref[:, 0:256] = u
    mixf_ref[:, 256:512] = p[:, 1280:1536]
    mixf_ref[:, 512:768] = p[:, 1536:1792]
    mixf_ref[:, 768:1024] = p[:, 1792:2048] * p[:, 2048:2304]


def _inproj(x2, n_seq, shift, scale, g, w_bf, cos_t, sin_t, ln_g, ln_b):
    t, d = x2.shape
    bt = min(INPROJ_TOKENS, n_seq)
    tps = n_seq // bt
    ncol = w_bf.shape[1]
    row = lambda i: (i, 0)
    return pl.pallas_call(
        _inproj_kernel,
        grid=(t // bt,),
        in_specs=[pl.BlockSpec((bt, d), row),
                  pl.BlockSpec((None, 1, d), lambda i: (i // tps, 0, 0)),
                  pl.BlockSpec((None, 1, d), lambda i: (i // tps, 0, 0)),
                  _full(g),
                  pl.BlockSpec((d, ncol), lambda i: (0, 0)),
                  pl.BlockSpec((bt, LANES), lambda i: (i % tps, 0)),
                  pl.BlockSpec((bt, LANES), lambda i: (i % tps, 0)),
                  _full(ln_g), _full(ln_b)],
        out_specs=[pl.BlockSpec((bt, ATT_Q_W), row), pl.BlockSpec((bt, 512), row),
                   pl.BlockSpec((bt, SGU_W), row), pl.BlockSpec((bt, 1024), row)],
        out_shape=[jax.ShapeDtypeStruct((t, ATT_Q_W), bf16), jax.ShapeDtypeStruct((t, 512), bf16),
                   jax.ShapeDtypeStruct((t, SGU_W), bf16), jax.ShapeDtypeStruct((t, 1024), f32)],
        compiler_params=_params("arbitrary"),
        name="inproj",
    )(x2, shift, scale, g, w_bf, cos_t, sin_t, ln_g, ln_b)


def _attn_body(sink_ref, q_ref, kv, bias, o_ref):
    k_nat, k_swp, v_nat, v_swp = (kv[:, i * LANES:(i + 1) * LANES] for i in range(4))
    lane = lax.broadcasted_iota(i32, (q_ref.shape[0], LANES), 1)
    low = lane < HEAD_DIM
    for m in range(ATT_Q_W // LANES):
        qc = q_ref[:, m * LANES:(m + 1) * LANES]
        outs = []
        for half in range(2):
            h = 2 * m + half
            kvh = h // (N_Q_HEADS // 2)
            qz = jnp.where(low if half == 0 else jnp.logical_not(low), qc, jnp.zeros_like(qc))
            kh = k_nat if kvh == half else k_swp
            vh = v_nat if kvh == half else v_swp
            s = lax.dot_general(qz, kh, (((1,), (1,)), ((), ())), preferred_element_type=f32)
            if bias is not None:
                s = s + bias
            sk = sink_ref[h]
            mx = jnp.maximum(jnp.max(s, axis=1, keepdims=True), sk)
            e = jnp.exp(s - mx)
            den = jnp.sum(e, axis=1, keepdims=True) + jnp.exp(sk - mx)
            pv = jnp.dot(e.astype(bf16), vh, preferred_element_type=f32)
            outs.append(pv / den)
        o_ref[:, m * LANES:(m + 1) * LANES] = jnp.where(low, outs[0], outs[1]).astype(bf16)


def _window_attn_kernel(sink_ref, q_ref, kvp_ref, kvc_ref, kvn_ref, kvx_ref, o_ref, *, nb):
    j = pl.program_id(1)
    kv = jnp.concatenate([kvp_ref[...], kvc_ref[...], kvn_ref[...], kvx_ref[...]], axis=0)
    nband = 3 * ATTN_TOKENS
    r = lax.broadcasted_iota(i32, (ATTN_TOKENS, nband), 0)
    c = lax.broadcasted_iota(i32, (ATTN_TOKENS, nband), 1)
    dlt = c - r
    lo = jnp.where(j == 0, ATTN_TOKENS, 0)
    hi = jnp.where(j == nb - 1, 2 * ATTN_TOKENS, nband)
    valid = (dlt >= 0) & (dlt <= 2 * WINDOW) & (c >= lo) & (c < hi)
    bias = jnp.concatenate([jnp.where(valid, 0.0, -jnp.inf).astype(f32),
                            jnp.zeros((ATTN_TOKENS, kvx_ref.shape[0]), f32)], axis=1)
    _attn_body(sink_ref, q_ref, kv, bias, o_ref)


def _ctx_attn_kernel(sink_ref, q_ref, kvx_ref, o_ref):
    _attn_body(sink_ref, q_ref, kvx_ref[...], None, o_ref)


def _window_attn(q, kv, kv_ctx, sink, n_batch, n_seq, n_ctx):
    t = q.shape[0]
    nb = n_seq // ATTN_TOKENS
    blk = (ATTN_TOKENS, 512)
    return pl.pallas_call(
        functools.partial(_window_attn_kernel, nb=nb),
        grid=(n_batch, nb),
        in_specs=[pl.BlockSpec(memory_space=pltpu.SMEM),
                  pl.BlockSpec(blk, lambda b, j: (b * nb + j, 0)),
                  pl.BlockSpec(blk, lambda b, j: (b * nb + jnp.maximum(j - 1, 0), 0)),
                  pl.BlockSpec(blk, lambda b, j: (b * nb + j, 0)),
                  pl.BlockSpec(blk, lambda b, j: (b * nb + jnp.minimum(j + 1, nb - 1), 0)),
                  pl.BlockSpec((n_ctx, 512), lambda b, j: (b, 0))],
        out_specs=pl.BlockSpec(blk, lambda b, j: (b * nb + j, 0)),
        out_shape=jax.ShapeDtypeStruct((t, ATT_Q_W), bf16),
        compiler_params=_params("arbitrary", "arbitrary"),
        name="window_attn",
    )(sink, q, kv, kv, kv, kv_ctx)


def _ctx_attn(q, kv_ctx, sink, n_batch, n_ctx):
    nb = n_ctx // ATTN_TOKENS
    blk = (ATTN_TOKENS, 512)
    return pl.pallas_call(
        _ctx_attn_kernel,
        grid=(n_batch, nb),
        in_specs=[pl.BlockSpec(memory_space=pltpu.SMEM),
                  pl.BlockSpec(blk, lambda b, j: (b * nb + j, 0)),
                  pl.BlockSpec((n_ctx, 512), lambda b, j: (b, 0))],
        out_specs=pl.BlockSpec(blk, lambda b, j: (b * nb + j, 0)),
        out_shape=jax.ShapeDtypeStruct(q.shape, bf16),
        compiler_params=_params("arbitrary", "arbitrary"),
        name="ctx_attn",
    )(sink, q, kv_ctx)


def _mixer_kernel(attn_ref, vn_ref, mf_ref, mfp_ref, mfn_ref, x_ref, g1_ref, sh2_ref, sc2_ref, n2g_ref,
                  ws_ref, sb_ref, pw_ref, ps_ref, cw_ref, wo_ref, rw_ref, rb_ref, cin_ref,
                  xmid_ref, h2_ref, ti_ref, gt_ref, rk_ref, cout_ref, cnt_ref, *, n_seq, bt):
    i = pl.program_id(0)
    tps = n_seq // bt
    si = i % tps
    first = si == 0
    last = si == tps - 1
    n_ext = bt + 2 * HALO

    @pl.when(i == 0)
    def _():
        cnt_ref[...] = cin_ref[...]

    mf = mf_ref[...]
    u = mf[:, 0:256]

    def extended(lo, hi):
        prev = jnp.where(first, 0.0, mfp_ref[:, lo:hi])
        nxt = jnp.where(last, 0.0, mfn_ref[:, lo:hi])
        return jnp.concatenate([prev, mf[:, lo:hi], nxt], axis=0)

    def shifted(a, s):
        return pltpu.roll(a, s % n_ext, 0)

    xe = extended(256, 512)
    a1 = shifted(xe, 1) + xe
    a2 = shifted(a1, 1) + shifted(a1, -1)
    a3 = shifted(a2, 2) + shifted(a2, -2)
    a4 = shifted(a3, 4) + shifted(a3, -4)
    lane = lax.broadcasted_iota(i32, (bt, POOL_CH), 1)
    grp = lane >> 6
    sl = slice(HALO, HALO + bt)
    wsum = jnp.where(grp == 0, a1[sl], jnp.where(grp == 1, a2[sl], jnp.where(grp == 2, a3[sl], a4[sl])))
    pos = lax.broadcasted_iota(i32, (bt, POOL_CH), 0) + si * bt
    halfw = jnp.left_shift(1, grp)
    cnt = jnp.minimum(pos + halfw, n_seq) - jnp.maximum(pos - halfw, 0)
    dpool = wsum / cnt.astype(f32) - mf[:, 256:512]
    yc = jnp.dot(dpool.astype(bf16), pw_ref[...], preferred_element_type=f32) * ps_ref[...]

    ye = extended(768, 1024)
    cw = cw_ref[...]
    z = shifted(ye, 1) * cw[0:1, :] + ye * cw[1:2, :] + shifted(ye, -1) * cw[2:3, :]
    yd = mf[:, 512:768] * z[sl]

    hgrp = lax.broadcasted_iota(i32, (SGU_CHUNK, SGU_W), 1) >> 6
    ybs = []
    for cidx in range(bt // SGU_CHUNK):
        rows = slice(cidx * SGU_CHUNK, (cidx + 1) * SGU_CHUNK)
        vn_c = vn_ref[rows, :]
        s = jnp.zeros((SGU_CHUNK, SGU_W), f32)
        for hh in range(SGU_HEADS):
            sh = jnp.dot(ws_ref[hh], vn_c, preferred_element_type=f32)
            s = jnp.where(hgrp == hh, sh, s)
        ybs.append(u[rows, :] * (s + sb_ref[...]))
    yb = jnp.concatenate(ybs, axis=0)

    mix = jnp.concatenate([attn_ref[...], yb.astype(bf16), yc.astype(bf16), yd.astype(bf16)], axis=1)
    mo = jnp.dot(mix, wo_ref[...], preferred_element_type=f32)
    xm = x_ref[...] + g1_ref[...] * mo
    xmid_ref[...] = xm

    y = xm * lax.rsqrt(jnp.mean(xm * xm, axis=-1, keepdims=True) + EPS) * n2g_ref[...]
    h2 = y * (1.0 + sc2_ref[...]) + sh2_ref[...]
    _store_token_tiles(h2_ref, h2)

    lt = lax.dot_general(rw_ref[...], h2, (((1,), (1,)), ((), ())),
                         precision=lax.Precision.HIGHEST, preferred_element_type=f32) + rb_ref[...]
    n_exp = lt.shape[0]
    eidx = lax.broadcasted_iota(i32, lt.shape, 0)
    work = lt
    idxs, vals = [], []
    for _ in range(TOP_K):
        m = jnp.max(work, axis=0, keepdims=True)
        idx = jnp.min(jnp.where(work == m, eidx, n_exp), axis=0, keepdims=True)
        idxs.append(idx)
        vals.append(m)
        work = jnp.where(eidx == idx, -jnp.inf, work)
    exps = [jnp.exp(v - vals[0]) for v in vals]
    den = exps[0] + exps[1] + exps[2] + exps[3]
    onehot = jnp.zeros(lt.shape, f32)
    for kk in range(TOP_K):
        ti_ref[kk:kk + 1, :] = idxs[kk]
        gt_ref[kk:kk + 1, :] = exps[kk] / den
        onehot = onehot + (eidx == idxs[kk]).astype(f32)
    tri = (lax.broadcasted_iota(i32, (bt, bt), 0) < lax.broadcasted_iota(i32, (bt, bt), 1)).astype(bf16)
    base = jnp.dot(onehot.astype(bf16), tri, preferred_element_type=f32) + cnt_ref[...]
    for kk in range(TOP_K):
        rk_ref[kk:kk + 1, :] = jnp.sum(jnp.where(eidx == idxs[kk], base, 0.0), axis=0, keepdims=True).astype(i32)
    cnt_ref[...] = cnt_ref[...] + jnp.sum(onehot, axis=1, keepdims=True)
    cout_ref[...] = cnt_ref[...]


def _mixer(attn, vn, mixf, x2, n_seq, g1, sh2, sc2, n2g, ws_bf, sgu_bias, pool_bd, pool_scale, conv_w,
           wo_bf, rw_t, rb, cnt_in):
    t, d = x2.shape
    bt = min(MIXER_TOKENS, n_seq)
    tps = n_seq // bt
    hb = bt // HALO
    n_halo = t // HALO
    n_exp = rw_t.shape[0]
    row = lambda i: (i, 0)
    per_batch = pl.BlockSpec((None, 1, d), lambda i: (i // tps, 0, 0))
    col = lambda i: (0, i)
    return pl.pallas_call(
        functools.partial(_mixer_kernel, n_seq=n_seq, bt=bt),
        grid=(t // bt,),
        in_specs=[pl.BlockSpec((bt, ATT_Q_W), row), pl.BlockSpec((bt, SGU_W), row), pl.BlockSpec((bt, 1024), row),
                  pl.BlockSpec((HALO, 1024), lambda i: (jnp.maximum(i * hb - 1, 0), 0)),
                  pl.BlockSpec((HALO, 1024), lambda i: (jnp.minimum((i + 1) * hb, n_halo - 1), 0)),
                  pl.BlockSpec((bt, d), row), per_batch, per_batch, per_batch, _full(n2g),
                  _full(ws_bf), _full(sgu_bias), _full(pool_bd), _full(pool_scale), _full(conv_w),
                  _full(wo_bf), _full(rw_t), _full(rb), _full(cnt_in)],
        out_specs=[pl.BlockSpec((bt, d), row), pl.BlockSpec((bt * SUBLANES, LANES), row),
                   pl.BlockSpec((TOP_K, bt), col), pl.BlockSpec((TOP_K, bt), col), pl.BlockSpec((TOP_K, bt), col),
                   pl.BlockSpec((n_exp, 1), lambda i: (0, 0))],
        out_shape=[jax.ShapeDtypeStruct((t, d), f32), jax.ShapeDtypeStruct((t * SUBLANES, LANES), f32),
                   jax.ShapeDtypeStruct((TOP_K, t), i32), jax.ShapeDtypeStruct((TOP_K, t), f32),
                   jax.ShapeDtypeStruct((TOP_K, t), i32), jax.ShapeDtypeStruct((n_exp, 1), f32)],
        scratch_shapes=[pltpu.VMEM((n_exp, 1), f32)],
        compiler_params=_params("arbitrary"),
        name="mixer_router",
    )(attn, vn, mixf, mixf, mixf, x2, g1, sh2, sc2, n2g, ws_bf, sgu_bias, pool_bd, pool_scale, conv_w,
      wo_bf, rw_t, rb, cnt_in)


def _dest_kernel(ti_ref, rk_ref, ps_ref, o_ref):
    ti = ti_ref[...]
    acc = rk_ref[...]
    for e in range(ps_ref.shape[0]):
        acc = acc + jnp.where(ti == e, ps_ref[e], 0)
    o_ref[...] = acc


def _dest(top_i, rank, pstarts):
    k, t = top_i.shape
    bt = next(b for b in (2048, 1024, 512, 256, 128) if t % b == 0)
    col = lambda i: (0, i)
    return pl.pallas_call(
        _dest_kernel,
        grid=(t // bt,),
        in_specs=[pl.BlockSpec((k, bt), col), pl.BlockSpec((k, bt), col), pl.BlockSpec(memory_space=pltpu.SMEM)],
        out_specs=pl.BlockSpec((k, bt), col),
        out_shape=jax.ShapeDtypeStruct((k, t), i32),
        compiler_params=_params("arbitrary"),
        name="slot_index",
    )(top_i, rank, pstarts)


def _tile_copy(src, s, dst, d, sem):
    return pltpu.make_async_copy(src.at[pl.ds(pl.multiple_of(s * SUBLANES, SUBLANES), SUBLANES)],
                                 dst.at[pl.ds(pl.multiple_of(d * SUBLANES, SUBLANES), SUBLANES)], sem)


def _wait_tiles(ref, n, sem):
    pltpu.make_async_copy(ref.at[pl.ds(0, n * SUBLANES)], ref.at[pl.ds(0, n * SUBLANES)], sem).wait()


def _chunk_major(dest, bt):
    k, t = dest.shape
    return dest.reshape(k, t // bt, bt).transpose(1, 0, 2).reshape(-1)


def _dispatch_kernel(dest_ref, pad_ref, h_ref, *rest, bt, fill_pad):
    if fill_pad:
        xs_ref, zero_ref, sem = rest
    else:
        _, xs_ref, sem = rest

    if fill_pad:
        @pl.when(pl.program_id(0) == 0)
        def _():
            zero_ref[...] = jnp.zeros_like(zero_ref)
            n_exp = pad_ref.shape[0] // 2

            def per_expert(e, carry):
                first, count = pad_ref[e], pad_ref[n_exp + e]

                def issue(r, c):
                    _tile_copy(zero_ref, 0, xs_ref, first + r, sem).start()
                    return c

                def drain(r, c):
                    _tile_copy(zero_ref, 0, xs_ref, 0, sem).wait()
                    return c

                lax.fori_loop(0, count, issue, 0)
                lax.fori_loop(0, count, drain, 0)
                return carry

            lax.fori_loop(0, n_exp, per_expert, 0)

    def start(t, carry):
        for kk in range(TOP_K):
            _tile_copy(h_ref, t, xs_ref, dest_ref[kk * bt + t], sem).start(priority=kk % 2)
        return carry

    lax.fori_loop(0, bt, start, 0, unroll=DMA_UNROLL)
    for _ in range(TOP_K):
        _wait_tiles(h_ref, bt, sem)


def _dispatch(dest, pad, h2, xs, n_slots):
    t = h2.shape[0] // SUBLANES
    bt = min(DISPATCH_TOKENS, t)
    fill_pad = xs is None
    in_specs = [pl.BlockSpec((TOP_K * bt,), lambda i: (i,), memory_space=pltpu.SMEM),
                pl.BlockSpec(memory_space=pltpu.SMEM),
                pl.BlockSpec((bt * SUBLANES, LANES), lambda i: (i, 0))]
    args = [_chunk_major(dest, bt), pad, h2]
    scratch = [pltpu.SemaphoreType.DMA(())]
    if fill_pad:
        scratch = [pltpu.VMEM((SUBLANES, LANES), f32)] + scratch
    else:
        in_specs.append(pl.BlockSpec(memory_space=pl.ANY))
        args.append(xs)
    return pl.pallas_call(
        functools.partial(_dispatch_kernel, bt=bt, fill_pad=fill_pad),
        grid=(t // bt,),
        in_specs=in_specs,
        out_specs=pl.BlockSpec(memory_space=pl.ANY),
        out_shape=jax.ShapeDtypeStruct((n_slots * SUBLANES, LANES), f32),
        scratch_shapes=scratch,
        input_output_aliases={} if fill_pad else {3: 0},
        compiler_params=_params("arbitrary"),
        name="dispatch",
    )(*args)


def _split_w1_kernel(w_ref, g_ref, l_ref, scr):
    f = g_ref.shape[1]
    for s in range(w_ref.shape[0] // LANES):
        rows = slice(s * LANES, (s + 1) * LANES)
        scr[...] = w_ref[rows, :].T
        g_ref[rows, :] = scr[pl.ds(0, f, stride=2), :].T.astype(bf16)
        l_ref[rows, :] = scr[pl.ds(1, f, stride=2), :].T.astype(bf16)


def _split_w1(w1, layer):
    _, n_exp, d, f2 = w1.shape
    f = f2 // 2
    rows = 2 * LANES
    blk = pl.BlockSpec((None, rows, f), lambda e, j: (e, j, 0))
    return pl.pallas_call(
        _split_w1_kernel,
        grid=(n_exp, d // rows),
        in_specs=[pl.BlockSpec((None, None, rows, f2), lambda e, j: (layer, e, j, 0))],
        out_specs=[blk, blk],
        out_shape=[jax.ShapeDtypeStruct((n_exp, d, f), bf16)] * 2,
        scratch_shapes=[pltpu.VMEM((f2, LANES), f32)],
        compiler_params=_params("arbitrary", "arbitrary"),
        name="split_w1",
    )(w1)


def _expert_kernel(be_ref, nu_ref, x_ref, w1g_ref, w1l_ref, b1g_ref, b1l_ref, w2_ref, b2_ref, y_ref):
    del be_ref
    i = pl.program_id(0)

    @pl.when(i < nu_ref[0])
    def _():
        bm = x_ref.shape[0] // SUBLANES
        x = _load_token_tiles(x_ref, bm).astype(bf16)
        zg = jnp.dot(x, w1g_ref[...], preferred_element_type=f32) + b1g_ref[...]
        zl = jnp.dot(x, w1l_ref[...], preferred_element_type=f32) + b1l_ref[...]
        g = jnp.minimum(zg, SWIGLU_LIMIT)
        lin = jnp.clip(zl, -SWIGLU_LIMIT, SWIGLU_LIMIT)
        a = g * jax.nn.sigmoid(SWIGLU_ALPHA * g) * (lin + 1.0)
        y = jnp.dot(a.astype(bf16), w2_ref[...], preferred_element_type=f32) + b2_ref[...]
        _store_token_tiles(y_ref, y)

    @pl.when(i >= nu_ref[0])
    def _():
        y_ref[...] = jnp.zeros_like(y_ref)


def _experts(block_e, n_used, xs, w1g, w1l, b1g, b1l, w2, b2):
    n_exp, d, f = w1g.shape
    bm = EXPERT_ROWS
    blk = (bm * SUBLANES, LANES)
    xrow = lambda i, be, nu: (jnp.minimum(i, nu[0] - 1), 0)
    wsel = lambda i, be, nu: (be[i], 0, 0)
    grid_spec = pltpu.PrefetchScalarGridSpec(
        num_scalar_prefetch=2,
        grid=(xs.shape[0] // blk[0],),
        in_specs=[pl.BlockSpec(blk, xrow),
                  pl.BlockSpec((None, d, f), wsel), pl.BlockSpec((None, d, f), wsel),
                  pl.BlockSpec((None, 1, f), wsel), pl.BlockSpec((None, 1, f), wsel),
                  pl.BlockSpec((None, f, d), wsel), pl.BlockSpec((None, 1, d), wsel)],
        out_specs=pl.BlockSpec(blk, lambda i, be, nu: (i, 0)),
    )
    return pl.pallas_call(
        _expert_kernel,
        grid_spec=grid_spec,
        out_shape=jax.ShapeDtypeStruct(xs.shape, f32),
        compiler_params=_params("arbitrary"),
        name="experts",
    )(block_e, n_used, xs, w1g, w1l, b1g, b1l, w2, b2)


def _combine_kernel(dest_ref, gate_ref, ys_ref, x_ref, g2_ref, fg_ref, o_ref, buf, sem, *, bt, final):
    def start(t, carry):
        for kk in range(TOP_K):
            _tile_copy(ys_ref, dest_ref[kk * bt + t], buf.at[kk], t, sem).start(priority=kk % 2)
        return carry

    lax.fori_loop(0, bt, start, 0, unroll=DMA_UNROLL)
    for kk in range(TOP_K):
        _wait_tiles(buf.at[kk], bt, sem)

    gate = gate_ref[...]
    gate_b = [jnp.broadcast_to(gate[:, kk:kk + 1], (bt, LANES)) for kk in range(TOP_K)]
    chunks = []
    for s in range(SUBLANES):
        lanes = slice(s * LANES, (s + 1) * LANES)
        acc = gate_b[0] * buf.at[0][pl.ds(s, bt, stride=SUBLANES), :]
        for kk in range(1, TOP_K):
            acc = acc + gate_b[kk] * buf.at[kk][pl.ds(s, bt, stride=SUBLANES), :]
        chunks.append(x_ref[:, lanes] + g2_ref[:, lanes] * acc)
    xo = jnp.concatenate(chunks, axis=1)
    if final:
        xo = xo * lax.rsqrt(jnp.mean(xo * xo, axis=-1, keepdims=True) + EPS) * fg_ref[...]
    o_ref[...] = xo


def _combine(dest, gates, ys, x_mid, n_seq, g2, final_g, final):
    t, d = x_mid.shape
    bt = min(COMBINE_TOKENS, n_seq)
    tps = n_seq // bt
    row = lambda i: (i, 0)
    return pl.pallas_call(
        functools.partial(_combine_kernel, bt=bt, final=final),
        grid=(t // bt,),
        in_specs=[pl.BlockSpec((TOP_K * bt,), lambda i: (i,), memory_space=pltpu.SMEM),
                  pl.BlockSpec((bt, TOP_K), row),
                  pl.BlockSpec(memory_space=pl.ANY),
                  pl.BlockSpec((bt, d), row),
                  pl.BlockSpec((None, 1, d), lambda i: (i // tps, 0, 0)),
                  _full(final_g)],
        out_specs=pl.BlockSpec((bt, d), row),
        out_shape=jax.ShapeDtypeStruct((t, d), f32),
        scratch_shapes=[pltpu.VMEM((TOP_K, bt * SUBLANES, LANES), f32), pltpu.SemaphoreType.DMA(())],
        compiler_params=_params("arbitrary"),
        name="combine",
    )(_chunk_major(dest, bt), gates, ys, x_mid, g2, final_g)


def _rope_tables(n_tokens):
    rows = n_tokens // GRID_W
    row = jnp.repeat(jnp.arange(rows), GRID_W).astype(f32)
    col = jnp.tile(jnp.arange(GRID_W), rows).astype(f32)
    n_freq = HEAD_DIM // 4
    inv = ROPE_BASE ** (-jnp.arange(n_freq, dtype=f32) / n_freq)
    ang = jnp.concatenate([row[:, None] * inv, col[:, None] * inv], axis=-1)
    cos, sin = jnp.cos(ang), jnp.sin(ang)
    return jnp.tile(cos, (1, 4)), jnp.concatenate([-sin, sin, -sin, sin], axis=-1)


def kernel(x, c, ctx, c_ctx, norm1_g, norm2_g, ada_w, ada_b, w_in, attn_sink, sgu_ws, sgu_b, sgu_ln_g, sgu_ln_b,
           pool_w, pool_scale, conv_w, w_out, router_w, router_b, exp_w1, exp_b1, exp_w2, exp_b2, final_g):
    n_batch, n_seq, d = x.shape
    n_ctx = ctx.shape[1]
    depth = ada_w.shape[0]
    n_exp = router_w.shape[2]
    assert d == SUBLANES * LANES, "token rows are moved as single (8, 128) tiles"
    t_lat, t_ctx = n_batch * n_seq, n_batch * n_ctx
    bm = EXPERT_ROWS

    cvec = jnp.concatenate([c, c_ctx[None, :], jnp.zeros((SUBLANES - n_batch - 1, d), f32)], axis=0)
    mods = _ada(cvec, ada_w, ada_b)

    cos_l, sin_l = _rope_tables(n_seq)
    cos_c, sin_c = jnp.ones((n_ctx, LANES), f32), jnp.zeros((n_ctx, LANES), f32)

    xl = x.reshape(t_lat, d)
    xc = ctx.reshape(t_ctx, d)
    row2 = lambda a: a.reshape(1, -1)

    for l in range(depth):
        last = l == depth - 1
        ml = mods[l, :n_batch].reshape(n_batch, 6, 1, d)
        mc = jnp.broadcast_to(mods[l, n_batch].reshape(1, 6, 1, d), (n_batch, 6, 1, d))
        sh1l, sc1l, g1l, sh2l, sc2l, g2l = (ml[:, i] for i in range(6))
        sh1c, sc1c, g1c, sh2c, sc2c, g2c = (mc[:, i] for i in range(6))

        w_in_bf = w_in[l].astype(bf16)
        wo_bf = w_out[l].astype(bf16)
        ws_bf = sgu_ws[l].astype(bf16)
        sgu_bias = jnp.repeat(sgu_b[l].T, SGU_W // SGU_HEADS, axis=1)
        pool_bd = jax.scipy.linalg.block_diag(*[pool_w[l, g] for g in range(pool_w.shape[1])]).astype(bf16)
        n1g, n2g = row2(norm1_g[l]), row2(norm2_g[l])
        lng, lnb, psc = row2(sgu_ln_g[l]), row2(sgu_ln_b[l]), row2(pool_scale[l])
        rw_t = router_w[l].T
        rb = router_b[l].reshape(n_exp, 1)
        sink = attn_sink[l]
        mix_w = (ws_bf, sgu_bias, pool_bd, psc, conv_w[l], wo_bf, rw_t, rb)

        qc, kvc, vnc, mfc = _inproj(xc, n_ctx, sh1c, sc1c, n1g, w_in_bf, cos_c, sin_c, lng, lnb)
        ql, kvl, vnl, mfl = _inproj(xl, n_seq, sh1l, sc1l, n1g, w_in_bf, cos_l, sin_l, lng, lnb)
        attn_l = _window_attn(ql, kvl, kvc, sink, n_batch, n_seq, n_ctx)
        cnt0 = jnp.zeros((n_exp, 1), f32)
        xmid_l, h2_l, ti, gt, rk, cnt = _mixer(attn_l, vnl, mfl, xl, n_seq, g1l, sh2l, sc2l, n2g, *mix_w, cnt0)
        if not last:
            attn_c = _ctx_attn(qc, kvc, sink, n_batch, n_ctx)
            xmid_c, h2_c, ti_c, gt_c, rk_c, cnt = _mixer(attn_c, vnc, mfc, xc, n_ctx, g1c, sh2c, sc2c, n2g,
                                                         *mix_w, cnt)
            ti = jnp.concatenate([ti, ti_c], axis=1)
            gt = jnp.concatenate([gt, gt_c], axis=1)
            rk = jnp.concatenate([rk, rk_c], axis=1)
        t_all = ti.shape[1]

        counts = cnt[:, 0].astype(i32)
        padded = (counts + bm - 1) // bm * bm
        pends = jnp.cumsum(padded)
        pstarts = pends - padded
        n_blocks = -(-(t_all * TOP_K) // bm) + n_exp
        starts = jnp.arange(n_blocks, dtype=i32) * bm
        block_e = jnp.minimum(jnp.sum((pends[None, :] <= starts[:, None]).astype(i32), axis=1), n_exp - 1)
        n_used = (pends[-1:] // bm).astype(i32)
        dest = _dest(ti, rk, pstarts)
        gates = gt.T
        pad = jnp.concatenate([pstarts + counts, padded - counts]).astype(i32)

        xs = _dispatch(dest[:, :t_lat], pad, h2_l, None, n_blocks * bm)
        if not last:
            xs = _dispatch(dest[:, t_lat:], pad, h2_c, xs, n_blocks * bm)

        w1g, w1l = _split_w1(exp_w1, l)
        b1 = exp_b1[l]
        b1g, b1l = b1[:, None, 0::2], b1[:, None, 1::2]
        ys = _experts(block_e, n_used, xs, w1g, w1l, b1g, b1l, exp_w2[l].astype(bf16), exp_b2[l][:, None, :])

        fg = row2(final_g)
        xl = _combine(dest[:, :t_lat], gates[:t_lat], ys, xmid_l, n_seq, g2l, fg, last)
        if not last:
            xc = _combine(dest[:, t_lat:], gates[t_lat:], ys, xmid_c, n_ctx, g2c, fg, False)

    return xl.reshape(n_batch, n_seq, d)
```

```python
import functools

import jax
import jax.numpy as jnp
from jax import lax
from jax.experimental import pallas as pl
from jax.experimental.pallas import tpu as pltpu

f32 = jnp.float32
bf16 = jnp.bfloat16
i32 = jnp.int32

GRID_W = 64
EPS = 1e-6
N_Q_HEADS = 8
HEAD_DIM = 64
WINDOW = 128
ROPE_BASE = 10000.0
ATT_Q_W = 512
ATT_KV_W = 128
SGU_HEADS = 4
SGU_W = 256
SGU_CHUNK = 128
POOL_CH = 256
CONV_CH = 256
MIX_WIDTH = 1280
TOP_K = 4
SWIGLU_LIMIT = 7.0
SWIGLU_ALPHA = 1.702
SQRT_HALF = 0.7071067811865476

LANES = 128
SUBLANES = 8
VMEM_LIMIT_BYTES = 56 * 1024 * 1024

INPROJ_TOKENS = 512
ATTN_TOKENS = 128
MIXER_TOKENS = 512
DISPATCH_TOKENS = 512
COMBINE_TOKENS = 256
EXPERT_ROWS = 512
HALO = 8
DMA_UNROLL = 4


def _params(*sem):
    return pltpu.CompilerParams(dimension_semantics=sem, vmem_limit_bytes=VMEM_LIMIT_BYTES)


def _full(a):
    nd = a.ndim
    return pl.BlockSpec(a.shape, lambda *_: (0,) * nd)


def _store_token_tiles(ref, val):
    n = val.shape[0]
    for s in range(SUBLANES):
        ref[pl.ds(s, n, stride=SUBLANES), :] = val[:, s * LANES:(s + 1) * LANES]


def _load_token_tiles(ref, n):
    return jnp.concatenate([ref[pl.ds(s, n, stride=SUBLANES), :] for s in range(SUBLANES)], axis=1)


def _gelu(x):
    return 0.5 * x * (1.0 + lax.erf(x * SQRT_HALF))


def _ada_kernel(c_ref, w_ref, b_ref, o_ref):
    c = c_ref[...]
    s = c * jax.nn.sigmoid(c)
    o_ref[...] = jnp.dot(s, w_ref[...], precision=lax.Precision.HIGHEST, preferred_element_type=f32) + b_ref[...]


def _ada(cvec, ada_w, ada_b):
    depth, d, n = ada_w.shape
    tn = 1536
    return pl.pallas_call(
        _ada_kernel,
        grid=(depth, n // tn),
        in_specs=[pl.BlockSpec(cvec.shape, lambda l, j: (0, 0)),
                  pl.BlockSpec((None, d, tn), lambda l, j: (l, 0, j)),
                  pl.BlockSpec((None, 1, tn), lambda l, j: (l, 0, j))],
        out_specs=pl.BlockSpec((None, cvec.shape[0], tn), lambda l, j: (l, 0, j)),
        out_shape=jax.ShapeDtypeStruct((depth, cvec.shape[0], n), f32),
        compiler_params=_params("arbitrary", "arbitrary"),
        name="ada_mod",
    )(cvec, ada_w, ada_b.reshape(depth, 1, n))


def _inproj_kernel(x_ref, sh_ref, sc_ref, g_ref, w_ref, cos_ref, sin_ref, lng_ref, lnb_ref,
                   q_ref, kv_ref, vn_ref, mixf_ref):
    x = x_ref[...]
    y = x * lax.rsqrt(jnp.mean(x * x, axis=-1, keepdims=True) + EPS) * g_ref[...]
    h = y * (1.0 + sc_ref[...]) + sh_ref[...]
    p = jnp.dot(h.astype(bf16), w_ref[...], preferred_element_type=f32)

    cos = cos_ref[...]
    sin = sin_ref[...]
    lane = lax.broadcasted_iota(i32, cos.shape, 1)
    first_half = (lane & (HEAD_DIM - 1)) < HEAD_DIM // 2

    def rope(t):
        partner = jnp.where(first_half, pltpu.roll(t, LANES - HEAD_DIM // 2, 1), pltpu.roll(t, HEAD_DIM // 2, 1))
        return t * cos + partner * sin

    scale = HEAD_DIM ** -0.5
    for m in range(ATT_Q_W // LANES):
        q_ref[:, m * LANES:(m + 1) * LANES] = (rope(p[:, m * LANES:(m + 1) * LANES]) * scale).astype(bf16)
    k = rope(p[:, 512:640])
    v = p[:, 640:768]
    kv_ref[:, 0:128] = k.astype(bf16)
    kv_ref[:, 128:256] = pltpu.roll(k, HEAD_DIM, 1).astype(bf16)
    kv_ref[:, 256:384] = v.astype(bf16)
    kv_ref[:, 384:512] = pltpu.roll(v, HEAD_DIM, 1).astype(bf16)

    u = _gelu(p[:, 768:1024])
    gv = _gelu(p[:, 1024:1280])
    mu = jnp.mean(gv, axis=-1, keepdims=True)
    var = jnp.mean(jnp.square(gv - mu), axis=-1, keepdims=True)
    vn_ref[...] = ((gv - mu) * lax.rsqrt(var + EPS) * lng_ref[...] + lnb_ref[...]).astype(bf16)

    mixf_ref[:, 0:256] = u
    mixf_ref[:, 256:512] = p[:, 1280:1536]
    mixf_ref[:, 512:768] = p[:, 1536:1792]
    mixf_ref[:, 768:1024] = p[:, 1792:2048] * p[:, 2048:2304]


def _inproj(x2, n_seq, shift, scale, g, w_bf, cos_t, sin_t, ln_g, ln_b):
    t, d = x2.shape
    bt = min(INPROJ_TOKENS, n_seq)
    tps = n_seq // bt
    ncol = w_bf.shape[1]
    row = lambda i: (i, 0)
    return pl.pallas_call(
        _inproj_kernel,
        grid=(t // bt,),
        in_specs=[pl.BlockSpec((bt, d), row),
                  pl.BlockSpec((None, 1, d), lambda i: (i // tps, 0, 0)),
                  pl.BlockSpec((None, 1, d), lambda i: (i // tps, 0, 0)),
                  _full(g),
                  pl.BlockSpec((d, ncol), lambda i: (0, 0)),
                  pl.BlockSpec((bt, LANES), lambda i: (i % tps, 0)),
                  pl.BlockSpec((bt, LANES), lambda i: (i % tps, 0)),
                  _full(ln_g), _full(ln_b)],
        out_specs=[pl.BlockSpec((bt, ATT_Q_W), row), pl.BlockSpec((bt, 512), row),
                   pl.BlockSpec((bt, SGU_W), row), pl.BlockSpec((bt, 1024), row)],
        out_shape=[jax.ShapeDtypeStruct((t, ATT_Q_W), bf16), jax.ShapeDtypeStruct((t, 512), bf16),
                   jax.ShapeDtypeStruct((t, SGU_W), bf16), jax.ShapeDtypeStruct((t, 1024), f32)],
        compiler_params=_params("arbitrary"),
        name="inproj",
    )(x2, shift, scale, g, w_bf, cos_t, sin_t, ln_g, ln_b)


def _attn_block(sink_ref, q, kv, bias, o_ref, row0):
    k_nat, k_swp, v_nat, v_swp = (kv[:, i * LANES:(i + 1) * LANES] for i in range(4))
    nq = q.shape[0]
    low = lax.broadcasted_iota(i32, (nq, LANES), 1) < HEAD_DIM
    top = lax.broadcasted_iota(i32, (2 * nq, 1), 0) < nq
    zero = jnp.zeros((nq, LANES), q.dtype)
    for kvh in range(2):
        chunks = [q[:, (2 * kvh + i) * LANES:(2 * kvh + i + 1) * LANES] for i in range(2)]
        outs = []
        for half in range(2):
            keep = low if half == 0 else jnp.logical_not(low)
            qz = jnp.concatenate([jnp.where(keep, c, zero) for c in chunks], axis=0)
            kh = k_nat if kvh == half else k_swp
            vh = v_nat if kvh == half else v_swp
            s = lax.dot_general(qz, kh, (((1,), (1,)), ((), ())), preferred_element_type=f32)
            if bias is not None:
                s = s + bias
            h0 = 4 * kvh + half
            sk = jnp.where(top, sink_ref[h0], sink_ref[h0 + 2])
            mx = jnp.maximum(jnp.max(s, axis=1, keepdims=True), sk)
            e = jnp.exp(s - mx)
            den = jnp.sum(e, axis=1, keepdims=True) + jnp.exp(sk - mx)
            outs.append(jnp.dot(e.astype(bf16), vh, preferred_element_type=f32) / den)
        for i in range(2):
            rows = slice(i * nq, (i + 1) * nq)
            m = 2 * kvh + i
            o_ref[row0:row0 + nq, m * LANES:(m + 1) * LANES] = jnp.where(low, outs[0][rows], outs[1][rows]).astype(bf16)


def _window_attn_kernel(sink_ref, q_ref, kvp_ref, kvm_ref, kvn_ref, kvx_ref, o_ref, *, nb):
    jj = pl.program_id(1)
    nq = ATTN_TOKENS
    nband = 3 * nq
    kvm = kvm_ref[...]
    kvx = kvx_ref[...]
    r = lax.broadcasted_iota(i32, (nq, nband), 0)
    c = lax.broadcasted_iota(i32, (nq, nband), 1)
    dlt = c - r
    in_window = (dlt >= 0) & (dlt <= 2 * WINDOW)
    ctx_zeros = jnp.zeros((nq, kvx.shape[0]), f32)
    for i, kv_band in enumerate((jnp.concatenate([kvp_ref[...], kvm], axis=0),
                                 jnp.concatenate([kvm, kvn_ref[...]], axis=0))):
        j = 2 * jj + i
        lo = jnp.where(j == 0, nq, 0)
        hi = jnp.where(j == nb - 1, 2 * nq, nband)
        valid = in_window & (c >= lo) & (c < hi)
        bias = jnp.concatenate([jnp.where(valid, 0.0, -jnp.inf).astype(f32), ctx_zeros], axis=1)
        _attn_block(sink_ref, q_ref[i * nq:(i + 1) * nq, :], jnp.concatenate([kv_band, kvx], axis=0),
                    jnp.concatenate([bias, bias], axis=0), o_ref, i * nq)


def _ctx_attn_kernel(sink_ref, q_ref, kvx_ref, o_ref):
    _attn_block(sink_ref, q_ref[...], kvx_ref[...], None, o_ref, 0)


def _window_attn(q, kv, kv_ctx, sink, n_batch, n_seq, n_ctx):
    t = q.shape[0]
    nb = n_seq // ATTN_TOKENS
    assert nb % 2 == 0
    nb2 = nb // 2
    one, two = (ATTN_TOKENS, 512), (2 * ATTN_TOKENS, 512)
    return pl.pallas_call(
        functools.partial(_window_attn_kernel, nb=nb),
        grid=(n_batch, nb2),
        in_specs=[pl.BlockSpec(memory_space=pltpu.SMEM),
                  pl.BlockSpec(two, lambda b, j: (b * nb2 + j, 0)),
                  pl.BlockSpec(one, lambda b, j: (b * nb + jnp.maximum(2 * j - 1, 0), 0)),
                  pl.BlockSpec(two, lambda b, j: (b * nb2 + j, 0)),
                  pl.BlockSpec(one, lambda b, j: (b * nb + jnp.minimum(2 * j + 2, nb - 1), 0)),
                  pl.BlockSpec((n_ctx, 512), lambda b, j: (b, 0))],
        out_specs=pl.BlockSpec(two, lambda b, j: (b * nb2 + j, 0)),
        out_shape=jax.ShapeDtypeStruct((t, ATT_Q_W), bf16),
        compiler_params=_params("arbitrary", "arbitrary"),
        name="window_attn",
    )(sink, q, kv, kv, kv, kv_ctx)


def _ctx_attn(q, kv_ctx, sink, n_batch, n_ctx):
    nb = n_ctx // ATTN_TOKENS
    blk = (ATTN_TOKENS, 512)
    return pl.pallas_call(
        _ctx_attn_kernel,
        grid=(n_batch, nb),
        in_specs=[pl.BlockSpec(memory_space=pltpu.SMEM),
                  pl.BlockSpec(blk, lambda b, j: (b * nb + j, 0)),
                  pl.BlockSpec((n_ctx, 512), lambda b, j: (b, 0))],
        out_specs=pl.BlockSpec(blk, lambda b, j: (b * nb + j, 0)),
        out_shape=jax.ShapeDtypeStruct(q.shape, bf16),
        compiler_params=_params("arbitrary", "arbitrary"),
        name="ctx_attn",
    )(sink, q, kv_ctx)


def _mixer_kernel(attn_ref, vn_ref, mf_ref, mfp_ref, mfn_ref, x_ref, g1_ref, sh2_ref, sc2_ref, n2g_ref,
                  ws_ref, sb_ref, pw_ref, ps_ref, cw_ref, wo_ref, rw_ref, rb_ref, cin_ref,
                  xmid_ref, h2_ref, ti_ref, gt_ref, rk_ref, cout_ref, cnt_ref, *, n_seq, bt):
    i = pl.program_id(0)
    tps = n_seq // bt
    si = i % tps
    first = si == 0
    last = si == tps - 1
    n_ext = bt + 2 * HALO

    @pl.when(i == 0)
    def _():
        cnt_ref[...] = cin_ref[...]

    mf = mf_ref[...]
    u = mf[:, 0:256]

    def extended(lo, hi):
        prev = jnp.where(first, 0.0, mfp_ref[:, lo:hi])
        nxt = jnp.where(last, 0.0, mfn_ref[:, lo:hi])
        return jnp.concatenate([prev, mf[:, lo:hi], nxt], axis=0)

    def shifted(a, s):
        return pltpu.roll(a, s % n_ext, 0)

    xe = extended(256, 512)
    a1 = shifted(xe, 1) + xe
    a2 = shifted(a1, 1) + shifted(a1, -1)
    a3 = shifted(a2, 2) + shifted(a2, -2)
    a4 = shifted(a3, 4) + shifted(a3, -4)
    lane = lax.broadcasted_iota(i32, (bt, POOL_CH), 1)
    grp = lane >> 6
    sl = slice(HALO, HALO + bt)
    wsum = jnp.where(grp == 0, a1[sl], jnp.where(grp == 1, a2[sl], jnp.where(grp == 2, a3[sl], a4[sl])))
    pos = lax.broadcasted_iota(i32, (bt, POOL_CH), 0) + si * bt
    halfw = jnp.left_shift(1, grp)
    cnt = jnp.minimum(pos + halfw, n_seq) - jnp.maximum(pos - halfw, 0)
    dpool = wsum / cnt.astype(f32) - mf[:, 256:512]
    yc = jnp.dot(dpool.astype(bf16), pw_ref[...], preferred_element_type=f32) * ps_ref[...]

    ye = extended(768, 1024)
    cw = cw_ref[...]
    z = shifted(ye, 1) * cw[0:1, :] + ye * cw[1:2, :] + shifted(ye, -1) * cw[2:3, :]
    yd = mf[:, 512:768] * z[sl]

    hgrp = lax.broadcasted_iota(i32, (SGU_CHUNK, SGU_W), 1) >> 6
    ybs = []
    for cidx in range(bt // SGU_CHUNK):
        rows = slice(cidx * SGU_CHUNK, (cidx + 1) * SGU_CHUNK)
        vn_c = vn_ref[rows, :]
        s = jnp.zeros((SGU_CHUNK, SGU_W), f32)
        for hh in range(SGU_HEADS):
            sh = jnp.dot(ws_ref[hh], vn_c, preferred_element_type=f32)
            s = jnp.where(hgrp == hh, sh, s)
        ybs.append(u[rows, :] * (s + sb_ref[...]))
    yb = jnp.concatenate(ybs, axis=0)

    mix = jnp.concatenate([attn_ref[...], yb.astype(bf16), yc.astype(bf16), yd.astype(bf16)], axis=1)
    mo = jnp.dot(mix, wo_ref[...], preferred_element_type=f32)
    xm = x_ref[...] + g1_ref[...] * mo
    xmid_ref[...] = xm

    y = xm * lax.rsqrt(jnp.mean(xm * xm, axis=-1, keepdims=True) + EPS) * n2g_ref[...]
    h2 = y * (1.0 + sc2_ref[...]) + sh2_ref[...]
    _store_token_tiles(h2_ref, h2)

    lt = lax.dot_general(rw_ref[...], h2, (((1,), (1,)), ((), ())),
                         precision=lax.Precision.HIGHEST, preferred_element_type=f32) + rb_ref[...]
    n_exp = lt.shape[0]
    eidx = lax.broadcasted_iota(i32, lt.shape, 0)
    work = lt
    idxs, vals = [], []
    for _ in range(TOP_K):
        m = jnp.max(work, axis=0, keepdims=True)
        idx = jnp.min(jnp.where(work == m, eidx, n_exp), axis=0, keepdims=True)
        idxs.append(idx)
        vals.append(m)
        work = jnp.where(eidx == idx, -jnp.inf, work)
    exps = [jnp.exp(v - vals[0]) for v in vals]
    den = exps[0] + exps[1] + exps[2] + exps[3]
    onehot = jnp.zeros(lt.shape, f32)
    for kk in range(TOP_K):
        ti_ref[kk:kk + 1, :] = idxs[kk]
        gt_ref[kk:kk + 1, :] = exps[kk] / den
        onehot = onehot + (eidx == idxs[kk]).astype(f32)
    tri = (lax.broadcasted_iota(i32, (bt, bt), 0) < lax.broadcasted_iota(i32, (bt, bt), 1)).astype(bf16)
    base = jnp.dot(onehot.astype(bf16), tri, preferred_element_type=f32) + cnt_ref[...]
    for kk in range(TOP_K):
        rk_ref[kk:kk + 1, :] = jnp.sum(jnp.where(eidx == idxs[kk], base, 0.0), axis=0, keepdims=True).astype(i32)
    cnt_ref[...] = cnt_ref[...] + jnp.sum(onehot, axis=1, keepdims=True)
    cout_ref[...] = cnt_ref[...]


def _mixer(attn, vn, mixf, x2, n_seq, g1, sh2, sc2, n2g, ws_bf, sgu_bias, pool_bd, pool_scale, conv_w,
           wo_bf, rw_t, rb, cnt_in):
    t, d = x2.shape
    bt = min(MIXER_TOKENS, n_seq)
    tps = n_seq // bt
    hb = bt // HALO
    n_halo = t // HALO
    n_exp = rw_t.shape[0]
    row = lambda i: (i, 0)
    per_batch = pl.BlockSpec((None, 1, d), lambda i: (i // tps, 0, 0))
    col = lambda i: (0, i)
    return pl.pallas_call(
        functools.partial(_mixer_kernel, n_seq=n_seq, bt=bt),
        grid=(t // bt,),
        in_specs=[pl.BlockSpec((bt, ATT_Q_W), row), pl.BlockSpec((bt, SGU_W), row), pl.BlockSpec((bt, 1024), row),
                  pl.BlockSpec((HALO, 1024), lambda i: (jnp.maximum(i * hb - 1, 0), 0)),
                  pl.BlockSpec((HALO, 1024), lambda i: (jnp.minimum((i + 1) * hb, n_halo - 1), 0)),
                  pl.BlockSpec((bt, d), row), per_batch, per_batch, per_batch, _full(n2g),
                  _full(ws_bf), _full(sgu_bias), _full(pool_bd), _full(pool_scale), _full(conv_w),
                  _full(wo_bf), _full(rw_t), _full(rb), _full(cnt_in)],
        out_specs=[pl.BlockSpec((bt, d), row), pl.BlockSpec((bt * SUBLANES, LANES), row),
                   pl.BlockSpec((TOP_K, bt), col), pl.BlockSpec((TOP_K, bt), col), pl.BlockSpec((TOP_K, bt), col),
                   pl.BlockSpec((n_exp, 1), lambda i: (0, 0))],
        out_shape=[jax.ShapeDtypeStruct((t, d), f32), jax.ShapeDtypeStruct((t * SUBLANES, LANES), f32),
                   jax.ShapeDtypeStruct((TOP_K, t), i32), jax.ShapeDtypeStruct((TOP_K, t), f32),
                   jax.ShapeDtypeStruct((TOP_K, t), i32), jax.ShapeDtypeStruct((n_exp, 1), f32)],
        scratch_shapes=[pltpu.VMEM((n_exp, 1), f32)],
        compiler_params=_params("arbitrary"),
        name="mixer_router",
    )(attn, vn, mixf, mixf, mixf, x2, g1, sh2, sc2, n2g, ws_bf, sgu_bias, pool_bd, pool_scale, conv_w,
      wo_bf, rw_t, rb, cnt_in)


def _dest_kernel(ti_ref, rk_ref, ps_ref, o_ref):
    ti = ti_ref[...]
    acc = rk_ref[...]
    for e in range(ps_ref.shape[0]):
        acc = acc + jnp.where(ti == e, ps_ref[e], 0)
    o_ref[...] = acc


def _dest(top_i, rank, pstarts):
    k, t = top_i.shape
    bt = next(b for b in (2048, 1024, 512, 256, 128) if t % b == 0)
    col = lambda i: (0, i)
    return pl.pallas_call(
        _dest_kernel,
        grid=(t // bt,),
        in_specs=[pl.BlockSpec((k, bt), col), pl.BlockSpec((k, bt), col), pl.BlockSpec(memory_space=pltpu.SMEM)],
        out_specs=pl.BlockSpec((k, bt), col),
        out_shape=jax.ShapeDtypeStruct((k, t), i32),
        compiler_params=_params("arbitrary"),
        name="slot_index",
    )(top_i, rank, pstarts)


def _tile_copy(src, s, dst, d, sem):
    return pltpu.make_async_copy(src.at[pl.ds(pl.multiple_of(s * SUBLANES, SUBLANES), SUBLANES)],
                                 dst.at[pl.ds(pl.multiple_of(d * SUBLANES, SUBLANES), SUBLANES)], sem)


def _wait_tiles(ref, n, sem):
    pltpu.make_async_copy(ref.at[pl.ds(0, n * SUBLANES)], ref.at[pl.ds(0, n * SUBLANES)], sem).wait()


def _chunk_major(dest, bt):
    k, t = dest.shape
    return dest.reshape(k, t // bt, bt).transpose(1, 0, 2).reshape(-1)


def _dispatch_kernel(dest_ref, pad_ref, h_ref, *rest, bt, fill_pad):
    if fill_pad:
        xs_ref, zero_ref, sem = rest
    else:
        _, xs_ref, sem = rest

    if fill_pad:
        @pl.when(pl.program_id(0) == 0)
        def _():
            zero_ref[...] = jnp.zeros_like(zero_ref)
            n_exp = pad_ref.shape[0] // 2

            def per_expert(e, carry):
                first, count = pad_ref[e], pad_ref[n_exp + e]

                def issue(r, c):
                    _tile_copy(zero_ref, 0, xs_ref, first + r, sem).start()
                    return c

                def drain(r, c):
                    _tile_copy(zero_ref, 0, xs_ref, 0, sem).wait()
                    return c

                lax.fori_loop(0, count, issue, 0)
                lax.fori_loop(0, count, drain, 0)
                return carry

            lax.fori_loop(0, n_exp, per_expert, 0)

    def start(t, carry):
        for kk in range(TOP_K):
            _tile_copy(h_ref, t, xs_ref, dest_ref[kk * bt + t], sem).start(priority=kk % 2)
        return carry

    lax.fori_loop(0, bt, start, 0, unroll=DMA_UNROLL)
    for _ in range(TOP_K):
        _wait_tiles(h_ref, bt, sem)


def _dispatch(dest, pad, h2, xs, n_slots):
    t = h2.shape[0] // SUBLANES
    bt = min(DISPATCH_TOKENS, t)
    fill_pad = xs is None
    in_specs = [pl.BlockSpec((TOP_K * bt,), lambda i: (i,), memory_space=pltpu.SMEM),
                pl.BlockSpec(memory_space=pltpu.SMEM),
                pl.BlockSpec((bt * SUBLANES, LANES), lambda i: (i, 0))]
    args = [_chunk_major(dest, bt), pad, h2]
    scratch = [pltpu.SemaphoreType.DMA(())]
    if fill_pad:
        scratch = [pltpu.VMEM((SUBLANES, LANES), f32)] + scratch
    else:
        in_specs.append(pl.BlockSpec(memory_space=pl.ANY))
        args.append(xs)
    return pl.pallas_call(
        functools.partial(_dispatch_kernel, bt=bt, fill_pad=fill_pad),
        grid=(t // bt,),
        in_specs=in_specs,
        out_specs=pl.BlockSpec(memory_space=pl.ANY),
        out_shape=jax.ShapeDtypeStruct((n_slots * SUBLANES, LANES), f32),
        scratch_shapes=scratch,
        input_output_aliases={} if fill_pad else {3: 0},
        compiler_params=_params("arbitrary"),
        name="dispatch",
    )(*args)


def _split_w1_kernel(w_ref, g_ref, l_ref, scr):
    f = g_ref.shape[1]
    for s in range(w_ref.shape[0] // LANES):
        rows = slice(s * LANES, (s + 1) * LANES)
        scr[...] = w_ref[rows, :].T
        g_ref[rows, :] = scr[pl.ds(0, f, stride=2), :].T.astype(bf16)
        l_ref[rows, :] = scr[pl.ds(1, f, stride=2), :].T.astype(bf16)


def _split_w1(w1, layer):
    _, n_exp, d, f2 = w1.shape
    f = f2 // 2
    rows = 2 * LANES
    blk = pl.BlockSpec((None, rows, f), lambda e, j: (e, j, 0))
    return pl.pallas_call(
        _split_w1_kernel,
        grid=(n_exp, d // rows),
        in_specs=[pl.BlockSpec((None, None, rows, f2), lambda e, j: (layer, e, j, 0))],
        out_specs=[blk, blk],
        out_shape=[jax.ShapeDtypeStruct((n_exp, d, f), bf16)] * 2,
        scratch_shapes=[pltpu.VMEM((f2, LANES), f32)],
        compiler_params=_params("arbitrary", "arbitrary"),
        name="split_w1",
    )(w1)


def _expert_kernel(be_ref, nu_ref, x_ref, w1g_ref, w1l_ref, b1g_ref, b1l_ref, w2_ref, b2_ref, y_ref):
    del be_ref
    i = pl.program_id(0)

    @pl.when(i < nu_ref[0])
    def _():
        bm = x_ref.shape[0] // SUBLANES
        x = _load_token_tiles(x_ref, bm).astype(bf16)
        zg = jnp.dot(x, w1g_ref[...], preferred_element_type=f32) + b1g_ref[...]
        zl = jnp.dot(x, w1l_ref[...], preferred_element_type=f32) + b1l_ref[...]
        g = jnp.minimum(zg, SWIGLU_LIMIT)
        lin = jnp.clip(zl, -SWIGLU_LIMIT, SWIGLU_LIMIT)
        a = g * jax.nn.sigmoid(SWIGLU_ALPHA * g) * (lin + 1.0)
        y = jnp.dot(a.astype(bf16), w2_ref[...], preferred_element_type=f32) + b2_ref[...]
        _store_token_tiles(y_ref, y)

    @pl.when(i >= nu_ref[0])
    def _():
        y_ref[...] = jnp.zeros_like(y_ref)


def _experts(block_e, n_used, xs, w1g, w1l, b1g, b1l, w2, b2):
    n_exp, d, f = w1g.shape
    bm = EXPERT_ROWS
    blk = (bm * SUBLANES, LANES)
    xrow = lambda i, be, nu: (jnp.minimum(i, nu[0] - 1), 0)
    wsel = lambda i, be, nu: (be[i], 0, 0)
    grid_spec = pltpu.PrefetchScalarGridSpec(
        num_scalar_prefetch=2,
        grid=(xs.shape[0] // blk[0],),
        in_specs=[pl.BlockSpec(blk, xrow),
                  pl.BlockSpec((None, d, f), wsel), pl.BlockSpec((None, d, f), wsel),
                  pl.BlockSpec((None, 1, f), wsel), pl.BlockSpec((None, 1, f), wsel),
                  pl.BlockSpec((None, f, d), wsel), pl.BlockSpec((None, 1, d), wsel)],
        out_specs=pl.BlockSpec(blk, lambda i, be, nu: (i, 0)),
    )
    return pl.pallas_call(
        _expert_kernel,
        grid_spec=grid_spec,
        out_shape=jax.ShapeDtypeStruct(xs.shape, f32),
        compiler_params=_params("arbitrary"),
        name="experts",
    )(block_e, n_used, xs, w1g, w1l, b1g, b1l, w2, b2)


def _combine_kernel(dest_ref, gate_ref, ys_ref, x_ref, g2_ref, fg_ref, o_ref, buf, sem, *, bt, final):
    def start(t, carry):
        for kk in range(TOP_K):
            _tile_copy(ys_ref, dest_ref[kk * bt + t], buf.at[kk], t, sem).start(priority=kk % 2)
        return carry

    lax.fori_loop(0, bt, start, 0, unroll=DMA_UNROLL)
    for kk in range(TOP_K):
        _wait_tiles(buf.at[kk], bt, sem)

    gate = gate_ref[...]
    gate_b = [jnp.broadcast_to(gate[:, kk:kk + 1], (bt, LANES)) for kk in range(TOP_K)]
    chunks = []
    for s in range(SUBLANES):
        lanes = slice(s * LANES, (s + 1) * LANES)
        acc = gate_b[0] * buf.at[0][pl.ds(s, bt, stride=SUBLANES), :]
        for kk in range(1, TOP_K):
            acc = acc + gate_b[kk] * buf.at[kk][pl.ds(s, bt, stride=SUBLANES), :]
        chunks.append(x_ref[:, lanes] + g2_ref[:, lanes] * acc)
    xo = jnp.concatenate(chunks, axis=1)
    if final:
        xo = xo * lax.rsqrt(jnp.mean(xo * xo, axis=-1, keepdims=True) + EPS) * fg_ref[...]
    o_ref[...] = xo


def _combine(dest, gates, ys, x_mid, n_seq, g2, final_g, final):
    t, d = x_mid.shape
    bt = min(COMBINE_TOKENS, n_seq)
    tps = n_seq // bt
    row = lambda i: (i, 0)
    return pl.pallas_call(
        functools.partial(_combine_kernel, bt=bt, final=final),
        grid=(t // bt,),
        in_specs=[pl.BlockSpec((TOP_K * bt,), lambda i: (i,), memory_space=pltpu.SMEM),
                  pl.BlockSpec((bt, TOP_K), row),
                  pl.BlockSpec(memory_space=pl.ANY),
                  pl.BlockSpec((bt, d), row),
                  pl.BlockSpec((None, 1, d), lambda i: (i // tps, 0, 0)),
                  _full(final_g)],
        out_specs=pl.BlockSpec((bt, d), row),
        out_shape=jax.ShapeDtypeStruct((t, d), f32),
        scratch_shapes=[pltpu.VMEM((TOP_K, bt * SUBLANES, LANES), f32), pltpu.SemaphoreType.DMA(())],
        compiler_params=_params("arbitrary"),
        name="combine",
    )(_chunk_major(dest, bt), gates, ys, x_mid, g2, final_g)


def _rope_tables(n_tokens):
    rows = n_tokens // GRID_W
    row = jnp.repeat(jnp.arange(rows), GRID_W).astype(f32)
    col = jnp.tile(jnp.arange(GRID_W), rows).astype(f32)
    n_freq = HEAD_DIM // 4
    inv = ROPE_BASE ** (-jnp.arange(n_freq, dtype=f32) / n_freq)
    ang = jnp.concatenate([row[:, None] * inv, col[:, None] * inv], axis=-1)
    cos, sin = jnp.cos(ang), jnp.sin(ang)
    return jnp.tile(cos, (1, 4)), jnp.concatenate([-sin, sin, -sin, sin], axis=-1)


def kernel(x, c, ctx, c_ctx, norm1_g, norm2_g, ada_w, ada_b, w_in, attn_sink, sgu_ws, sgu_b, sgu_ln_g, sgu_ln_b,
           pool_w, pool_scale, conv_w, w_out, router_w, router_b, exp_w1, exp_b1, exp_w2, exp_b2, final_g):
    n_batch, n_seq, d = x.shape
    n_ctx = ctx.shape[1]
    depth = ada_w.shape[0]
    n_exp = router_w.shape[2]
    assert d == SUBLANES * LANES, "token rows are moved as single (8, 128) tiles"
    t_lat, t_ctx = n_batch * n_seq, n_batch * n_ctx
    bm = EXPERT_ROWS

    cvec = jnp.concatenate([c, c_ctx[None, :], jnp.zeros((SUBLANES - n_batch - 1, d), f32)], axis=0)
    mods = _ada(cvec, ada_w, ada_b)

    cos_l, sin_l = _rope_tables(n_seq)
    cos_c, sin_c = jnp.ones((n_ctx, LANES), f32), jnp.zeros((n_ctx, LANES), f32)

    xl = x.reshape(t_lat, d)
    xc = ctx.reshape(t_ctx, d)
    row2 = lambda a: a.reshape(1, -1)

    for l in range(depth):
        last = l == depth - 1
        ml = mods[l, :n_batch].reshape(n_batch, 6, 1, d)
        mc = jnp.broadcast_to(mods[l, n_batch].reshape(1, 6, 1, d), (n_batch, 6, 1, d))
        sh1l, sc1l, g1l, sh2l, sc2l, g2l = (ml[:, i] for i in range(6))
        sh1c, sc1c, g1c, sh2c, sc2c, g2c = (mc[:, i] for i in range(6))

        w_in_bf = w_in[l].astype(bf16)
        wo_bf = w_out[l].astype(bf16)
        ws_bf = sgu_ws[l].astype(bf16)
        sgu_bias = jnp.repeat(sgu_b[l].T, SGU_W // SGU_HEADS, axis=1)
        pool_bd = jax.scipy.linalg.block_diag(*[pool_w[l, g] for g in range(pool_w.shape[1])]).astype(bf16)
        n1g, n2g = row2(norm1_g[l]), row2(norm2_g[l])
        lng, lnb, psc = row2(sgu_ln_g[l]), row2(sgu_ln_b[l]), row2(pool_scale[l])
        rw_t = router_w[l].T
        rb = router_b[l].reshape(n_exp, 1)
        sink = attn_sink[l]
        mix_w = (ws_bf, sgu_bias, pool_bd, psc, conv_w[l], wo_bf, rw_t, rb)

        qc, kvc, vnc, mfc = _inproj(xc, n_ctx, sh1c, sc1c, n1g, w_in_bf, cos_c, sin_c, lng, lnb)
        ql, kvl, vnl, mfl = _inproj(xl, n_seq, sh1l, sc1l, n1g, w_in_bf, cos_l, sin_l, lng, lnb)
        attn_l = _window_attn(ql, kvl, kvc, sink, n_batch, n_seq, n_ctx)
        cnt0 = jnp.zeros((n_exp, 1), f32)
        xmid_l, h2_l, ti, gt, rk, cnt = _mixer(attn_l, vnl, mfl, xl, n_seq, g1l, sh2l, sc2l, n2g, *mix_w, cnt0)
        if not last:
            attn_c = _ctx_attn(qc, kvc, sink, n_batch, n_ctx)
            xmid_c, h2_c, ti_c, gt_c, rk_c, cnt = _mixer(attn_c, vnc, mfc, xc, n_ctx, g1c, sh2c, sc2c, n2g,
                                                         *mix_w, cnt)
            ti = jnp.concatenate([ti, ti_c], axis=1)
            gt = jnp.concatenate([gt, gt_c], axis=1)
            rk = jnp.concatenate([rk, rk_c], axis=1)
        t_all = ti.shape[1]

        counts = cnt[:, 0].astype(i32)
        padded = (counts + bm - 1) // bm * bm
        pends = jnp.cumsum(padded)
        pstarts = pends - padded
        n_blocks = -(-(t_all * TOP_K) // bm) + n_exp
        starts = jnp.arange(n_blocks, dtype=i32) * bm
        block_e = jnp.minimum(jnp.sum((pends[None, :] <= starts[:, None]).astype(i32), axis=1), n_exp - 1)
        n_used = (pends[-1:] // bm).astype(i32)
        dest = _dest(ti, rk, pstarts)
        gates = gt.T
        pad = jnp.concatenate([pstarts + counts, padded - counts]).astype(i32)

        xs = _dispatch(dest[:, :t_lat], pad, h2_l, None, n_blocks * bm)
        if not last:
            xs = _dispatch(dest[:, t_lat:], pad, h2_c, xs, n_blocks * bm)

        w1g, w1l = _split_w1(exp_w1, l)
        b1 = exp_b1[l]
        b1g, b1l = b1[:, None, 0::2], b1[:, None, 1::2]
        ys = _experts(block_e, n_used, xs, w1g, w1l, b1g, b1l, exp_w2[l].astype(bf16), exp_b2[l][:, None, :])

        fg = row2(final_g)
        xl = _combine(dest[:, :t_lat], gates[:t_lat], ys, xmid_l, n_seq, g2l, fg, last)
        if not last:
            xc = _combine(dest[:, t_lat:], gates[t_lat:], ys, xmid_c, n_ctx, g2c, fg, False)

    return xl.reshape(n_batch, n_seq, d)
```

```python
import functools

import jax
import jax.numpy as jnp
from jax import lax
from jax.experimental import pallas as pl
from jax.experimental.pallas import tpu as pltpu

f32 = jnp.float32
bf16 = jnp.bfloat16
i32 = jnp.int32

GRID_W = 64
EPS = 1e-6
N_Q_HEADS = 8
HEAD_DIM = 64
WINDOW = 128
ROPE_BASE = 10000.0
ATT_Q_W = 512
ATT_KV_W = 128
SGU_HEADS = 4
SGU_W = 256
SGU_CHUNK = 128
POOL_CH = 256
CONV_CH = 256
MIX_WIDTH = 1280
TOP_K = 4
SWIGLU_LIMIT = 7.0
SWIGLU_ALPHA = 1.702
SQRT_HALF = 0.7071067811865476

LANES = 128
SUBLANES = 8
VMEM_LIMIT_BYTES = 56 * 1024 * 1024

INPROJ_TOKENS = 512
ATTN_TOKENS = 128
MIXER_TOKENS = 512
DISPATCH_TOKENS = 512
COMBINE_TOKENS = 256
EXPERT_ROWS = 512
HALO = 8
DMA_UNROLL = 4


def _params(*sem):
    return pltpu.CompilerParams(dimension_semantics=sem, vmem_limit_bytes=VMEM_LIMIT_BYTES)


def _full(a):
    nd = a.ndim
    return pl.BlockSpec(a.shape, lambda *_: (0,) * nd)


def _store_token_tiles(ref, val):
    n = val.shape[0]
    for s in range(SUBLANES):
        ref[pl.ds(s, n, stride=SUBLANES), :] = val[:, s * LANES:(s + 1) * LANES]


def _load_token_tiles(ref, n):
    return jnp.concatenate([ref[pl.ds(s, n, stride=SUBLANES), :] for s in range(SUBLANES)], axis=1)


def _gelu(x):
    return 0.5 * x * (1.0 + lax.erf(x * SQRT_HALF))


def _ada_kernel(c_ref, w_ref, b_ref, o_ref):
    c = c_ref[...]
    s = c * jax.nn.sigmoid(c)
    o_ref[...] = jnp.dot(s, w_ref[...], precision=lax.Precision.HIGHEST, preferred_element_type=f32) + b_ref[...]


def _ada(cvec, ada_w, ada_b):
    depth, d, n = ada_w.shape
    tn = 1536
    return pl.pallas_call(
        _ada_kernel,
        grid=(depth, n // tn),
        in_specs=[pl.BlockSpec(cvec.shape, lambda l, j: (0, 0)),
                  pl.BlockSpec((None, d, tn), lambda l, j: (l, 0, j)),
                  pl.BlockSpec((None, 1, tn), lambda l, j: (l, 0, j))],
        out_specs=pl.BlockSpec((None, cvec.shape[0], tn), lambda l, j: (l, 0, j)),
        out_shape=jax.ShapeDtypeStruct((depth, cvec.shape[0], n), f32),
        compiler_params=_params("arbitrary", "arbitrary"),
        name="ada_mod",
    )(cvec, ada_w, ada_b.reshape(depth, 1, n))


def _inproj_kernel(x_ref, sh_ref, sc_ref, g_ref, w_ref, cos_ref, sin_ref, lng_ref, lnb_ref,
                   q_ref, kv_ref, vn_ref, mixf_ref):
    x = x_ref[...]
    y = x * lax.rsqrt(jnp.mean(x * x, axis=-1, keepdims=True) + EPS) * g_ref[...]
    h = y * (1.0 + sc_ref[...]) + sh_ref[...]
    p = jnp.dot(h.astype(bf16), w_ref[...], preferred_element_type=f32)

    cos = cos_ref[...]
    sin = sin_ref[...]
    lane = lax.broadcasted_iota(i32, cos.shape, 1)
    first_half = (lane & (HEAD_DIM - 1)) < HEAD_DIM // 2

    def rope(t):
        partner = jnp.where(first_half, pltpu.roll(t, LANES - HEAD_DIM // 2, 1), pltpu.roll(t, HEAD_DIM // 2, 1))
        return t * cos + partner * sin

    scale = HEAD_DIM ** -0.5
    for m in range(ATT_Q_W // LANES):
        q_ref[:, m * LANES:(m + 1) * LANES] = (rope(p[:, m * LANES:(m + 1) * LANES]) * scale).astype(bf16)
    k = rope(p[:, 512:640])
    v = p[:, 640:768]
    kv_ref[:, 0:128] = k.astype(bf16)
    kv_ref[:, 128:256] = pltpu.roll(k, HEAD_DIM, 1).astype(bf16)
    kv_ref[:, 256:384] = v.astype(bf16)
    kv_ref[:, 384:512] = pltpu.roll(v, HEAD_DIM, 1).astype(bf16)

    u = _gelu(p[:, 768:1024])
    gv = _gelu(p[:, 1024:1280])
    mu = jnp.mean(gv, axis=-1, keepdims=True)
    var = jnp.mean(jnp.square(gv - mu), axis=-1, keepdims=True)
    vn_ref[...] = ((gv - mu) * lax.rsqrt(var + EPS) * lng_ref[...] + lnb_ref[...]).astype(bf16)

    mixf_ref[:, 0:256] = u
    mixf_ref[:, 256:512] = p[:, 1280:1536]
    mixf_ref[:, 512:768] = p[:, 1536:1792]
    mixf_ref[:, 768:1024] = p[:, 1792:2048] * p[:, 2048:2304]


def _inproj(x2, n_seq, shift, scale, g, w_bf, cos_t, sin_t, ln_g, ln_b):
    t, d = x2.shape
    bt = min(INPROJ_TOKENS, n_seq)
    tps = n_seq // bt
    ncol = w_bf.shape[1]
    row = lambda i: (i, 0)
    return pl.pallas_call(
        _inproj_kernel,
        grid=(t // bt,),
        in_specs=[pl.BlockSpec((bt, d), row),
                  pl.BlockSpec((None, 1, d), lambda i: (i // tps, 0, 0)),
                  pl.BlockSpec((None, 1, d), lambda i: (i // tps, 0, 0)),
                  _full(g),
                  pl.BlockSpec((d, ncol), lambda i: (0, 0)),
                  pl.BlockSpec((bt, LANES), lambda i: (i % tps, 0)),
                  pl.BlockSpec((bt, LANES), lambda i: (i % tps, 0)),
                  _full(ln_g), _full(ln_b)],
        out_specs=[pl.BlockSpec((bt, ATT_Q_W), row), pl.BlockSpec((bt, 512), row),
                   pl.BlockSpec((bt, SGU_W), row), pl.BlockSpec((bt, 1024), row)],
        out_shape=[jax.ShapeDtypeStruct((t, ATT_Q_W), bf16), jax.ShapeDtypeStruct((t, 512), bf16),
                   jax.ShapeDtypeStruct((t, SGU_W), bf16), jax.ShapeDtypeStruct((t, 1024), f32)],
        compiler_params=_params("arbitrary"),
        name="inproj",
    )(x2, shift, scale, g, w_bf, cos_t, sin_t, ln_g, ln_b)


def _attn_block(sink_ref, q, kv, bias, o_ref, row0):
    k_nat, k_swp, v_nat, v_swp = (kv[:, i * LANES:(i + 1) * LANES] for i in range(4))
    nq = q.shape[0]
    low = lax.broadcasted_iota(i32, (nq, LANES), 1) < HEAD_DIM
    top = lax.broadcasted_iota(i32, (2 * nq, 1), 0) < nq
    zero = jnp.zeros((nq, LANES), q.dtype)
    for kvh in range(2):
        chunks = [q[:, (2 * kvh + i) * LANES:(2 * kvh + i + 1) * LANES] for i in range(2)]
        outs = []
        for half in range(2):
            keep = low if half == 0 else jnp.logical_not(low)
            qz = jnp.concatenate([jnp.where(keep, c, zero) for c in chunks], axis=0)
            kh = k_nat if kvh == half else k_swp
            vh = v_nat if kvh == half else v_swp
            s = lax.dot_general(qz, kh, (((1,), (1,)), ((), ())), preferred_element_type=f32)
            if bias is not None:
                s = s + bias
            h0 = 4 * kvh + half
            sk = jnp.where(top, sink_ref[h0], sink_ref[h0 + 2])
            mx = jnp.maximum(jnp.max(s, axis=1, keepdims=True), sk)
            e = jnp.exp(s - mx)
            den = jnp.sum(e, axis=1, keepdims=True) + jnp.exp(sk - mx)
            outs.append(jnp.dot(e.astype(bf16), vh, preferred_element_type=f32) / den)
        for i in range(2):
            rows = slice(i * nq, (i + 1) * nq)
            m = 2 * kvh + i
            o_ref[row0:row0 + nq, m * LANES:(m + 1) * LANES] = jnp.where(low, outs[0][rows], outs[1][rows]).astype(bf16)


def _window_attn_kernel(sink_ref, q_ref, kvp_ref, kvm_ref, kvn_ref, kvx_ref, o_ref, *, nb):
    jj = pl.program_id(1)
    nq = ATTN_TOKENS
    nband = 3 * nq
    kvm = kvm_ref[...]
    kvx = kvx_ref[...]
    r = lax.broadcasted_iota(i32, (nq, nband), 0)
    c = lax.broadcasted_iota(i32, (nq, nband), 1)
    dlt = c - r
    in_window = (dlt >= 0) & (dlt <= 2 * WINDOW)
    ctx_zeros = jnp.zeros((nq, kvx.shape[0]), f32)
    for i, kv_band in enumerate((jnp.concatenate([kvp_ref[...], kvm], axis=0),
                                 jnp.concatenate([kvm, kvn_ref[...]], axis=0))):
        j = 2 * jj + i
        lo = jnp.where(j == 0, nq, 0)
        hi = jnp.where(j == nb - 1, 2 * nq, nband)
        valid = in_window & (c >= lo) & (c < hi)
        bias = jnp.concatenate([jnp.where(valid, 0.0, -jnp.inf).astype(f32), ctx_zeros], axis=1)
        _attn_block(sink_ref, q_ref[i * nq:(i + 1) * nq, :], jnp.concatenate([kv_band, kvx], axis=0),
                    jnp.concatenate([bias, bias], axis=0), o_ref, i * nq)


def _ctx_attn_kernel(sink_ref, q_ref, kvx_ref, o_ref):
    _attn_block(sink_ref, q_ref[...], kvx_ref[...], None, o_ref, 0)


def _window_attn(q, kv, kv_ctx, sink, n_batch, n_seq, n_ctx):
    t = q.shape[0]
    nb = n_seq // ATTN_TOKENS
    assert nb % 2 == 0
    nb2 = nb // 2
    one, two = (ATTN_TOKENS, 512), (2 * ATTN_TOKENS, 512)
    return pl.pallas_call(
        functools.partial(_window_attn_kernel, nb=nb),
        grid=(n_batch, nb2),
        in_specs=[pl.BlockSpec(memory_space=pltpu.SMEM),
                  pl.BlockSpec(two, lambda b, j: (b * nb2 + j, 0)),
                  pl.BlockSpec(one, lambda b, j: (b * nb + jnp.maximum(2 * j - 1, 0), 0)),
                  pl.BlockSpec(two, lambda b, j: (b * nb2 + j, 0)),
                  pl.BlockSpec(one, lambda b, j: (b * nb + jnp.minimum(2 * j + 2, nb - 1), 0)),
                  pl.BlockSpec((n_ctx, 512), lambda b, j: (b, 0))],
        out_specs=pl.BlockSpec(two, lambda b, j: (b * nb2 + j, 0)),
        out_shape=jax.ShapeDtypeStruct((t, ATT_Q_W), bf16),
        compiler_params=_params("arbitrary", "arbitrary"),
        name="window_attn",
    )(sink, q, kv, kv, kv, kv_ctx)


def _ctx_attn(q, kv_ctx, sink, n_batch, n_ctx):
    nb = n_ctx // ATTN_TOKENS
    blk = (ATTN_TOKENS, 512)
    return pl.pallas_call(
        _ctx_attn_kernel,
        grid=(n_batch, nb),
        in_specs=[pl.BlockSpec(memory_space=pltpu.SMEM),
                  pl.BlockSpec(blk, lambda b, j: (b * nb + j, 0)),
                  pl.BlockSpec((n_ctx, 512), lambda b, j: (b, 0))],
        out_specs=pl.BlockSpec(blk, lambda b, j: (b * nb + j, 0)),
        out_shape=jax.ShapeDtypeStruct(q.shape, bf16),
        compiler_params=_params("arbitrary", "arbitrary"),
        name="ctx_attn",
    )(sink, q, kv_ctx)


def _mixer_kernel(attn_ref, vn_ref, mf_ref, mfp_ref, mfn_ref, x_ref, g1_ref, sh2_ref, sc2_ref, n2g_ref,
                  ws_ref, sb_ref, pw_ref, ps_ref, cw_ref, wo_ref, rw_ref, rb_ref, cin_ref,
                  xmid_ref, h2_ref, ti_ref, gt_ref, rk_ref, cout_ref, cnt_ref, *, n_seq, bt):
    i = pl.program_id(0)
    tps = n_seq // bt
    si = i % tps
    first = si == 0
    last = si == tps - 1
    n_ext = bt + 2 * HALO

    @pl.when(i == 0)
    def _():
        cnt_ref[...] = cin_ref[...]

    mf = mf_ref[...]
    u = mf[:, 0:256]

    def extended(lo, hi):
        prev = jnp.where(first, 0.0, mfp_ref[:, lo:hi])
        nxt = jnp.where(last, 0.0, mfn_ref[:, lo:hi])
        return jnp.concatenate([prev, mf[:, lo:hi], nxt], axis=0)

    def shifted(a, s):
        return pltpu.roll(a, s % n_ext, 0)

    xe = extended(256, 512)
    a1 = shifted(xe, 1) + xe
    a2 = shifted(a1, 1) + shifted(a1, -1)
    a3 = shifted(a2, 2) + shifted(a2, -2)
    a4 = shifted(a3, 4) + shifted(a3, -4)
    lane = lax.broadcasted_iota(i32, (bt, POOL_CH), 1)
    grp = lane >> 6
    sl = slice(HALO, HALO + bt)
    wsum = jnp.where(grp == 0, a1[sl], jnp.where(grp == 1, a2[sl], jnp.where(grp == 2, a3[sl], a4[sl])))
    pos = lax.broadcasted_iota(i32, (bt, POOL_CH), 0) + si * bt
    halfw = jnp.left_shift(1, grp)
    cnt = jnp.minimum(pos + halfw, n_seq) - jnp.maximum(pos - halfw, 0)
    dpool = wsum / cnt.astype(f32) - mf[:, 256:512]
    yc = jnp.dot(dpool.astype(bf16), pw_ref[...], preferred_element_type=f32) * ps_ref[...]

    ye = extended(768, 1024)
    cw = cw_ref[...]
    z = shifted(ye, 1) * cw[0:1, :] + ye * cw[1:2, :] + shifted(ye, -1) * cw[2:3, :]
    yd = mf[:, 512:768] * z[sl]

    hgrp = lax.broadcasted_iota(i32, (SGU_CHUNK, SGU_W), 1) >> 6
    ybs = []
    for cidx in range(bt // SGU_CHUNK):
        rows = slice(cidx * SGU_CHUNK, (cidx + 1) * SGU_CHUNK)
        vn_c = vn_ref[rows, :]
        s = jnp.zeros((SGU_CHUNK, SGU_W), f32)
        for hh in range(SGU_HEADS):
            sh = jnp.dot(ws_ref[hh], vn_c, preferred_element_type=f32)
            s = jnp.where(hgrp == hh, sh, s)
        ybs.append(u[rows, :] * (s + sb_ref[...]))
    yb = jnp.concatenate(ybs, axis=0)

    mix = jnp.concatenate([attn_ref[...], yb.astype(bf16), yc.astype(bf16), yd.astype(bf16)], axis=1)
    mo = jnp.dot(mix, wo_ref[...], preferred_element_type=f32)
    xm = x_ref[...] + g1_ref[...] * mo
    xmid_ref[...] = xm

    y = xm * lax.rsqrt(jnp.mean(xm * xm, axis=-1, keepdims=True) + EPS) * n2g_ref[...]
    h2 = y * (1.0 + sc2_ref[...]) + sh2_ref[...]
    _store_token_tiles(h2_ref, h2)

    lt = lax.dot_general(rw_ref[...], h2, (((1,), (1,)), ((), ())),
                         precision=lax.Precision.HIGHEST, preferred_element_type=f32) + rb_ref[...]
    n_exp = lt.shape[0]
    eidx = lax.broadcasted_iota(i32, lt.shape, 0)
    work = lt
    idxs, vals = [], []
    for _ in range(TOP_K):
        m = jnp.max(work, axis=0, keepdims=True)
        idx = jnp.min(jnp.where(work == m, eidx, n_exp), axis=0, keepdims=True)
        idxs.append(idx)
        vals.append(m)
        work = jnp.where(eidx == idx, -jnp.inf, work)
    exps = [jnp.exp(v - vals[0]) for v in vals]
    den = exps[0] + exps[1] + exps[2] + exps[3]
    onehot = jnp.zeros(lt.shape, f32)
    for kk in range(TOP_K):
        ti_ref[kk:kk + 1, :] = idxs[kk]
        gt_ref[kk:kk + 1, :] = exps[kk] / den
        onehot = onehot + (eidx == idxs[kk]).astype(f32)
    tri = (lax.broadcasted_iota(i32, (bt, bt), 0) < lax.broadcasted_iota(i32, (bt, bt), 1)).astype(bf16)
    base = jnp.dot(onehot.astype(bf16), tri, preferred_element_type=f32) + cnt_ref[...]
    for kk in range(TOP_K):
        rk_ref[kk:kk + 1, :] = jnp.sum(jnp.where(eidx == idxs[kk], base, 0.0), axis=0, keepdims=True).astype(i32)
    cnt_ref[...] = cnt_ref[...] + jnp.sum(onehot, axis=1, keepdims=True)
    cout_ref[...] = cnt_ref[...]


def _mixer(attn, vn, mixf, x2, n_seq, g1, sh2, sc2, n2g, ws_bf, sgu_bias, pool_bd, pool_scale, conv_w,
           wo_bf, rw_t, rb, cnt_in):
    t, d = x2.shape
    bt = min(MIXER_TOKENS, n_seq)
    tps = n_seq // bt
    hb = bt // HALO
    n_halo = t // HALO
    n_exp = rw_t.shape[0]
    row = lambda i: (i, 0)
    per_batch = pl.BlockSpec((None, 1, d), lambda i: (i // tps, 0, 0))
    col = lambda i: (0, i)
    return pl.pallas_call(
        functools.partial(_mixer_kernel, n_seq=n_seq, bt=bt),
        grid=(t // bt,),
        in_specs=[pl.BlockSpec((bt, ATT_Q_W), row), pl.BlockSpec((bt, SGU_W), row), pl.BlockSpec((bt, 1024), row),
                  pl.BlockSpec((HALO, 1024), lambda i: (jnp.maximum(i * hb - 1, 0), 0)),
                  pl.BlockSpec((HALO, 1024), lambda i: (jnp.minimum((i + 1) * hb, n_halo - 1), 0)),
                  pl.BlockSpec((bt, d), row), per_batch, per_batch, per_batch, _full(n2g),
                  _full(ws_bf), _full(sgu_bias), _full(pool_bd), _full(pool_scale), _full(conv_w),
                  _full(wo_bf), _full(rw_t), _full(rb), _full(cnt_in)],
        out_specs=[pl.BlockSpec((bt, d), row), pl.BlockSpec((bt * SUBLANES, LANES), row),
                   pl.BlockSpec((TOP_K, bt), col), pl.BlockSpec((TOP_K, bt), col), pl.BlockSpec((TOP_K, bt), col),
                   pl.BlockSpec((n_exp, 1), lambda i: (0, 0))],
        out_shape=[jax.ShapeDtypeStruct((t, d), f32), jax.ShapeDtypeStruct((t * SUBLANES, LANES), f32),
                   jax.ShapeDtypeStruct((TOP_K, t), i32), jax.ShapeDtypeStruct((TOP_K, t), f32),
                   jax.ShapeDtypeStruct((TOP_K, t), i32), jax.ShapeDtypeStruct((n_exp, 1), f32)],
        scratch_shapes=[pltpu.VMEM((n_exp, 1), f32)],
        compiler_params=_params("arbitrary"),
        name="mixer_router",
    )(attn, vn, mixf, mixf, mixf, x2, g1, sh2, sc2, n2g, ws_bf, sgu_bias, pool_bd, pool_scale, conv_w,
      wo_bf, rw_t, rb, cnt_in)


def _dest_kernel(ti_ref, rk_ref, ps_ref, o_ref):
    ti = ti_ref[...]
    acc = rk_ref[...]
    for e in range(ps_ref.shape[0]):
        acc = acc + jnp.where(ti == e, ps_ref[e], 0)
    o_ref[...] = acc


def _dest(top_i, rank, pstarts):
    k, t = top_i.shape
    bt = next(b for b in (2048, 1024, 512, 256, 128) if t % b == 0)
    col = lambda i: (0, i)
    return pl.pallas_call(
        _dest_kernel,
        grid=(t // bt,),
        in_specs=[pl.BlockSpec((k, bt), col), pl.BlockSpec((k, bt), col), pl.BlockSpec(memory_space=pltpu.SMEM)],
        out_specs=pl.BlockSpec((k, bt), col),
        out_shape=jax.ShapeDtypeStruct((k, t), i32),
        compiler_params=_params("arbitrary"),
        name="slot_index",
    )(top_i, rank, pstarts)


def _tile_copy(src, s, dst, d, sem):
    return pltpu.make_async_copy(src.at[pl.ds(pl.multiple_of(s * SUBLANES, SUBLANES), SUBLANES)],
                                 dst.at[pl.ds(pl.multiple_of(d * SUBLANES, SUBLANES), SUBLANES)], sem)


def _wait_tiles(ref, n, sem):
    pltpu.make_async_copy(ref.at[pl.ds(0, n * SUBLANES)], ref.at[pl.ds(0, n * SUBLANES)], sem).wait()


def _chunk_major(dest, bt):
    k, t = dest.shape
    return dest.reshape(k, t // bt, bt).transpose(1, 0, 2).reshape(-1)


def _dispatch_kernel(dest_ref, pad_ref, h_ref, *rest, bt, fill_pad):
    if fill_pad:
        xs_ref, zero_ref, sem = rest
    else:
        _, xs_ref, sem = rest

    if fill_pad:
        @pl.when(pl.program_id(0) == 0)
        def _():
            zero_ref[...] = jnp.zeros_like(zero_ref)
            n_exp = pad_ref.shape[0] // 2

            def per_expert(e, carry):
                first, count = pad_ref[e], pad_ref[n_exp + e]

                def issue(r, c):
                    _tile_copy(zero_ref, 0, xs_ref, first + r, sem).start()
                    return c

                def drain(r, c):
                    _tile_copy(zero_ref, 0, xs_ref, 0, sem).wait()
                    return c

                lax.fori_loop(0, count, issue, 0)
                lax.fori_loop(0, count, drain, 0)
                return carry

            lax.fori_loop(0, n_exp, per_expert, 0)

    def start(t, carry):
        for kk in range(TOP_K):
            _tile_copy(h_ref, t, xs_ref, dest_ref[kk * bt + t], sem).start(priority=kk % 2)
        return carry

    lax.fori_loop(0, bt, start, 0, unroll=DMA_UNROLL)
    for _ in range(TOP_K):
        _wait_tiles(h_ref, bt, sem)


def _dispatch(dest, pad, h2, xs, n_slots):
    t = h2.shape[0] // SUBLANES
    bt = min(DISPATCH_TOKENS, t)
    fill_pad = xs is None
    in_specs = [pl.BlockSpec((TOP_K * bt,), lambda i: (i,), memory_space=pltpu.SMEM),
                pl.BlockSpec(memory_space=pltpu.SMEM),
                pl.BlockSpec((bt * SUBLANES, LANES), lambda i: (i, 0))]
    args = [_chunk_major(dest, bt), pad, h2]
    scratch = [pltpu.SemaphoreType.DMA(())]
    if fill_pad:
        scratch = [pltpu.VMEM((SUBLANES, LANES), f32)] + scratch
    else:
        in_specs.append(pl.BlockSpec(memory_space=pl.ANY))
        args.append(xs)
    return pl.pallas_call(
        functools.partial(_dispatch_kernel, bt=bt, fill_pad=fill_pad),
        grid=(t // bt,),
        in_specs=in_specs,
        out_specs=pl.BlockSpec(memory_space=pl.ANY),
        out_shape=jax.ShapeDtypeStruct((n_slots * SUBLANES, LANES), f32),
        scratch_shapes=scratch,
        input_output_aliases={} if fill_pad else {3: 0},
        compiler_params=_params("arbitrary"),
        name="dispatch",
    )(*args)


def _split_w1_kernel(w_ref, g_ref, l_ref, scr):
    f = g_ref.shape[1]
    for s in range(w_ref.shape[0] // LANES):
        rows = slice(s * LANES, (s + 1) * LANES)
        scr[...] = w_ref[rows, :].T
        g_ref[rows, :] = scr[pl.ds(0, f, stride=2), :].T.astype(bf16)
        l_ref[rows, :] = scr[pl.ds(1, f, stride=2), :].T.astype(bf16)


def _split_w1(w1, layer):
    _, n_exp, d, f2 = w1.shape
    f = f2 // 2
    rows = 2 * LANES
    blk = pl.BlockSpec((None, rows, f), lambda e, j: (e, j, 0))
    return pl.pallas_call(
        _split_w1_kernel,
        grid=(n_exp, d // rows),
        in_specs=[pl.BlockSpec((None, None, rows, f2), lambda e, j: (layer, e, j, 0))],
        out_specs=[blk, blk],
        out_shape=[jax.ShapeDtypeStruct((n_exp, d, f), bf16)] * 2,
        scratch_shapes=[pltpu.VMEM((f2, LANES), f32)],
        compiler_params=_params("arbitrary", "arbitrary"),
        name="split_w1",
    )(w1)


def _expert_kernel(be_ref, nu_ref, x_ref, w1g_ref, w1l_ref, b1g_ref, b1l_ref, w2f_ref, b2_ref, y_ref, w2_ref):
    i = pl.program_id(0)
    active = i < nu_ref[0]
    new_expert = jnp.logical_or(i == 0, be_ref[i] != be_ref[jnp.maximum(i - 1, 0)])

    @pl.when(jnp.logical_and(active, new_expert))
    def _():
        w2_ref[...] = w2f_ref[...].astype(bf16)

    @pl.when(active)
    def _():
        bm = x_ref.shape[0] // SUBLANES
        x = _load_token_tiles(x_ref, bm).astype(bf16)
        zg = jnp.dot(x, w1g_ref[...], preferred_element_type=f32) + b1g_ref[...]
        zl = jnp.dot(x, w1l_ref[...], preferred_element_type=f32) + b1l_ref[...]
        g = jnp.minimum(zg, SWIGLU_LIMIT)
        lin = jnp.clip(zl, -SWIGLU_LIMIT, SWIGLU_LIMIT)
        a = g * jax.nn.sigmoid(SWIGLU_ALPHA * g) * (lin + 1.0)
        y = jnp.dot(a.astype(bf16), w2_ref[...], preferred_element_type=f32) + b2_ref[...]
        _store_token_tiles(y_ref, y)

    @pl.when(i >= nu_ref[0])
    def _():
        y_ref[...] = jnp.zeros_like(y_ref)


def _experts(block_e, n_used, xs, w1g, w1l, b1g, b1l, w2_all, layer, b2):
    n_exp, d, f = w1g.shape
    bm = EXPERT_ROWS
    blk = (bm * SUBLANES, LANES)
    xrow = lambda i, be, nu: (jnp.minimum(i, nu[0] - 1), 0)
    wsel = lambda i, be, nu: (be[i], 0, 0)
    grid_spec = pltpu.PrefetchScalarGridSpec(
        num_scalar_prefetch=2,
        grid=(xs.shape[0] // blk[0],),
        in_specs=[pl.BlockSpec(blk, xrow),
                  pl.BlockSpec((None, d, f), wsel), pl.BlockSpec((None, d, f), wsel),
                  pl.BlockSpec((None, 1, f), wsel), pl.BlockSpec((None, 1, f), wsel),
                  pl.BlockSpec((None, None, f, d), lambda i, be, nu: (layer, be[i], 0, 0)),
                  pl.BlockSpec((None, 1, d), wsel)],
        out_specs=pl.BlockSpec(blk, lambda i, be, nu: (i, 0)),
        scratch_shapes=[pltpu.VMEM((f, d), bf16)],
    )
    return pl.pallas_call(
        _expert_kernel,
        grid_spec=grid_spec,
        out_shape=jax.ShapeDtypeStruct(xs.shape, f32),
        compiler_params=_params("arbitrary"),
        name="experts",
    )(block_e, n_used, xs, w1g, w1l, b1g, b1l, w2_all, b2)


def _combine_kernel(dest_ref, dnext_ref, gate_ref, ys_ref, x_ref, g2_ref, fg_ref, o_ref, buf, sem, *, bt, final):
    i = pl.program_id(0)
    n = pl.num_programs(0)
    slot = i % 2

    def gather(table_ref, slot_):
        def start(t, carry):
            for kk in range(TOP_K):
                _tile_copy(ys_ref, table_ref[kk * bt + t], buf.at[slot_, kk], t, sem.at[slot_]).start(priority=kk % 2)
            return carry

        lax.fori_loop(0, bt, start, 0, unroll=DMA_UNROLL)

    @pl.when(i == 0)
    def _():
        gather(dest_ref, 0)

    @pl.when(i + 1 < n)
    def _():
        gather(dnext_ref, 1 - slot)

    for kk in range(TOP_K):
        _wait_tiles(buf.at[slot, kk], bt, sem.at[slot])

    gate = gate_ref[...]
    gate_b = [jnp.broadcast_to(gate[:, kk:kk + 1], (bt, LANES)) for kk in range(TOP_K)]
    chunks = []
    for s in range(SUBLANES):
        lanes = slice(s * LANES, (s + 1) * LANES)
        acc = gate_b[0] * buf.at[slot, 0][pl.ds(s, bt, stride=SUBLANES), :]
        for kk in range(1, TOP_K):
            acc = acc + gate_b[kk] * buf.at[slot, kk][pl.ds(s, bt, stride=SUBLANES), :]
        chunks.append(x_ref[:, lanes] + g2_ref[:, lanes] * acc)
    xo = jnp.concatenate(chunks, axis=1)
    if final:
        xo = xo * lax.rsqrt(jnp.mean(xo * xo, axis=-1, keepdims=True) + EPS) * fg_ref[...]
    o_ref[...] = xo


def _combine(dest, gates, ys, x_mid, n_seq, g2, final_g, final):
    t, d = x_mid.shape
    bt = min(COMBINE_TOKENS, n_seq)
    tps = n_seq // bt
    row = lambda i: (i, 0)
    n_steps = t // bt
    table = _chunk_major(dest, bt)
    return pl.pallas_call(
        functools.partial(_combine_kernel, bt=bt, final=final),
        grid=(n_steps,),
        in_specs=[pl.BlockSpec((TOP_K * bt,), lambda i: (i,), memory_space=pltpu.SMEM),
                  pl.BlockSpec((TOP_K * bt,), lambda i: (jnp.minimum(i + 1, n_steps - 1),), memory_space=pltpu.SMEM),
                  pl.BlockSpec((bt, TOP_K), row),
                  pl.BlockSpec(memory_space=pl.ANY),
                  pl.BlockSpec((bt, d), row),
                  pl.BlockSpec((None, 1, d), lambda i: (i // tps, 0, 0)),
                  _full(final_g)],
        out_specs=pl.BlockSpec((bt, d), row),
        out_shape=jax.ShapeDtypeStruct((t, d), f32),
        scratch_shapes=[pltpu.VMEM((2, TOP_K, bt * SUBLANES, LANES), f32), pltpu.SemaphoreType.DMA((2,))],
        compiler_params=_params("arbitrary"),
        name="combine",
    )(table, table, gates, ys, x_mid, g2, final_g)


def _rope_tables(n_tokens):
    rows = n_tokens // GRID_W
    row = jnp.repeat(jnp.arange(rows), GRID_W).astype(f32)
    col = jnp.tile(jnp.arange(GRID_W), rows).astype(f32)
    n_freq = HEAD_DIM // 4
    inv = ROPE_BASE ** (-jnp.arange(n_freq, dtype=f32) / n_freq)
    ang = jnp.concatenate([row[:, None] * inv, col[:, None] * inv], axis=-1)
    cos, sin = jnp.cos(ang), jnp.sin(ang)
    return jnp.tile(cos, (1, 4)), jnp.concatenate([-sin, sin, -sin, sin], axis=-1)


def kernel(x, c, ctx, c_ctx, norm1_g, norm2_g, ada_w, ada_b, w_in, attn_sink, sgu_ws, sgu_b, sgu_ln_g, sgu_ln_b,
           pool_w, pool_scale, conv_w, w_out, router_w, router_b, exp_w1, exp_b1, exp_w2, exp_b2, final_g):
    n_batch, n_seq, d = x.shape
    n_ctx = ctx.shape[1]
    depth = ada_w.shape[0]
    n_exp = router_w.shape[2]
    assert d == SUBLANES * LANES, "token rows are moved as single (8, 128) tiles"
    t_lat, t_ctx = n_batch * n_seq, n_batch * n_ctx
    bm = EXPERT_ROWS

    cvec = jnp.concatenate([c, c_ctx[None, :], jnp.zeros((SUBLANES - n_batch - 1, d), f32)], axis=0)
    mods = _ada(cvec, ada_w, ada_b)

    cos_l, sin_l = _rope_tables(n_seq)
    cos_c, sin_c = jnp.ones((n_ctx, LANES), f32), jnp.zeros((n_ctx, LANES), f32)

    xl = x.reshape(t_lat, d)
    xc = ctx.reshape(t_ctx, d)
    row2 = lambda a: a.reshape(1, -1)

    for l in range(depth):
        last = l == depth - 1
        ml = mods[l, :n_batch].reshape(n_batch, 6, 1, d)
        mc = jnp.broadcast_to(mods[l, n_batch].reshape(1, 6, 1, d), (n_batch, 6, 1, d))
        sh1l, sc1l, g1l, sh2l, sc2l, g2l = (ml[:, i] for i in range(6))
        sh1c, sc1c, g1c, sh2c, sc2c, g2c = (mc[:, i] for i in range(6))

        w_in_bf = w_in[l].astype(bf16)
        wo_bf = w_out[l].astype(bf16)
        ws_bf = sgu_ws[l].astype(bf16)
        sgu_bias = jnp.repeat(sgu_b[l].T, SGU_W // SGU_HEADS, axis=1)
        pool_bd = jax.scipy.linalg.block_diag(*[pool_w[l, g] for g in range(pool_w.shape[1])]).astype(bf16)
        n1g, n2g = row2(norm1_g[l]), row2(norm2_g[l])
        lng, lnb, psc = row2(sgu_ln_g[l]), row2(sgu_ln_b[l]), row2(pool_scale[l])
        rw_t = router_w[l].T
        rb = router_b[l].reshape(n_exp, 1)
        sink = attn_sink[l]
        mix_w = (ws_bf, sgu_bias, pool_bd, psc, conv_w[l], wo_bf, rw_t, rb)

        qc, kvc, vnc, mfc = _inproj(xc, n_ctx, sh1c, sc1c, n1g, w_in_bf, cos_c, sin_c, lng, lnb)
        ql, kvl, vnl, mfl = _inproj(xl, n_seq, sh1l, sc1l, n1g, w_in_bf, cos_l, sin_l, lng, lnb)
        attn_l = _window_attn(ql, kvl, kvc, sink, n_batch, n_seq, n_ctx)
        cnt0 = jnp.zeros((n_exp, 1), f32)
        xmid_l, h2_l, ti, gt, rk, cnt = _mixer(attn_l, vnl, mfl, xl, n_seq, g1l, sh2l, sc2l, n2g, *mix_w, cnt0)
        if not last:
            attn_c = _ctx_attn(qc, kvc, sink, n_batch, n_ctx)
            xmid_c, h2_c, ti_c, gt_c, rk_c, cnt = _mixer(attn_c, vnc, mfc, xc, n_ctx, g1c, sh2c, sc2c, n2g,
                                                         *mix_w, cnt)
            ti = jnp.concatenate([ti, ti_c], axis=1)
            gt = jnp.concatenate([gt, gt_c], axis=1)
            rk = jnp.concatenate([rk, rk_c], axis=1)
        t_all = ti.shape[1]

        counts = cnt[:, 0].astype(i32)
        padded = (counts + bm - 1) // bm * bm
        pends = jnp.cumsum(padded)
        pstarts = pends - padded
        n_blocks = -(-(t_all * TOP_K) // bm) + n_exp
        starts = jnp.arange(n_blocks, dtype=i32) * bm
        block_e = jnp.minimum(jnp.sum((pends[None, :] <= starts[:, None]).astype(i32), axis=1), n_exp - 1)
        n_used = (pends[-1:] // bm).astype(i32)
        dest = _dest(ti, rk, pstarts)
        gates = gt.T
        pad = jnp.concatenate([pstarts + counts, padded - counts]).astype(i32)

        xs = _dispatch(dest[:, :t_lat], pad, h2_l, None, n_blocks * bm)
        if not last:
            xs = _dispatch(dest[:, t_lat:], pad, h2_c, xs, n_blocks * bm)

        w1g, w1l = _split_w1(exp_w1, l)
        b1 = exp_b1[l]
        b1g, b1l = b1[:, None, 0::2], b1[:, None, 1::2]
        ys = _experts(block_e, n_used, xs, w1g, w1l, b1g, b1l, exp_w2, l, exp_b2[l][:, None, :])

        fg = row2(final_g)
        xl = _combine(dest[:, :t_lat], gates[:t_lat], ys, xmid_l, n_seq, g2l, fg, last)
        if not last:
            xc = _combine(dest[:, t_lat:], gates[t_lat:], ys, xmid_c, n_ctx, g2c, fg, False)

    return xl.reshape(n_batch, n_seq, d)
```

```python
import functools

import jax
import jax.numpy as jnp
from jax import lax
from jax.experimental import pallas as pl
from jax.experimental.pallas import tpu as pltpu

f32 = jnp.float32
bf16 = jnp.bfloat16
i32 = jnp.int32

GRID_W = 64
EPS = 1e-6
N_Q_HEADS = 8
HEAD_DIM = 64
WINDOW = 128
ROPE_BASE = 10000.0
ATT_Q_W = 512
ATT_KV_W = 128
SGU_HEADS = 4
SGU_W = 256
SGU_CHUNK = 128
POOL_CH = 256
CONV_CH = 256
MIX_WIDTH = 1280
TOP_K = 4
SWIGLU_LIMIT = 7.0
SWIGLU_ALPHA = 1.702
SQRT_HALF = 0.7071067811865476

LANES = 128
SUBLANES = 8
VMEM_LIMIT_BYTES = 56 * 1024 * 1024

INPROJ_TOKENS = 512
ATTN_TOKENS = 128
MIXER_TOKENS = 512
DISPATCH_TOKENS = 512
COMBINE_TOKENS = 256
EXPERT_ROWS = 512
HALO = 8
DMA_UNROLL = 4


def _params(*sem):
    return pltpu.CompilerParams(dimension_semantics=sem, vmem_limit_bytes=VMEM_LIMIT_BYTES)


def _full(a):
    nd = a.ndim
    return pl.BlockSpec(a.shape, lambda *_: (0,) * nd)


def _store_token_tiles(ref, val):
    n = val.shape[0]
    for s in range(SUBLANES):
        ref[pl.ds(s, n, stride=SUBLANES), :] = val[:, s * LANES:(s + 1) * LANES]


def _load_token_tiles(ref, n):
    return jnp.concatenate([ref[pl.ds(s, n, stride=SUBLANES), :] for s in range(SUBLANES)], axis=1)


def _gelu(x):
    return 0.5 * x * (1.0 + lax.erf(x * SQRT_HALF))


def _ada_kernel(c_ref, w_ref, b_ref, o_ref):
    c = c_ref[...]
    s = c * jax.nn.sigmoid(c)
    o_ref[...] = jnp.dot(s, w_ref[...], precision=lax.Precision.HIGHEST, preferred_element_type=f32) + b_ref[...]


def _ada(cvec, ada_w, ada_b):
    depth, d, n = ada_w.shape
    tn = 1536
    return pl.pallas_call(
        _ada_kernel,
        grid=(depth, n // tn),
        in_specs=[pl.BlockSpec(cvec.shape, lambda l, j: (0, 0)),
                  pl.BlockSpec((None, d, tn), lambda l, j: (l, 0, j)),
                  pl.BlockSpec((None, 1, tn), lambda l, j: (l, 0, j))],
        out_specs=pl.BlockSpec((None, cvec.shape[0], tn), lambda l, j: (l, 0, j)),
        out_shape=jax.ShapeDtypeStruct((depth, cvec.shape[0], n), f32),
        compiler_params=_params("arbitrary", "arbitrary"),
        name="ada_mod",
    )(cvec, ada_w, ada_b.reshape(depth, 1, n))


def _inproj_kernel(x_ref, sh_ref, sc_ref, g_ref, w_ref, cos_ref, sin_ref, lng_ref, lnb_ref,
                   q_ref, kv_ref, vn_ref, mixf_ref):
    x = x_ref[...]
    y = x * lax.rsqrt(jnp.mean(x * x, axis=-1, keepdims=True) + EPS) * g_ref[...]
    h = y * (1.0 + sc_ref[...]) + sh_ref[...]
    p = jnp.dot(h.astype(bf16), w_ref[...], preferred_element_type=f32)

    cos = cos_ref[...]
    sin = sin_ref[...]
    lane = lax.broadcasted_iota(i32, cos.shape, 1)
    first_half = (lane & (HEAD_DIM - 1)) < HEAD_DIM // 2

    def rope(t):
        partner = jnp.where(first_half, pltpu.roll(t, LANES - HEAD_DIM // 2, 1), pltpu.roll(t, HEAD_DIM // 2, 1))
        return t * cos + partner * sin

    scale = HEAD_DIM ** -0.5
    for m in range(ATT_Q_W // LANES):
        q_ref[:, m * LANES:(m + 1) * LANES] = (rope(p[:, m * LANES:(m + 1) * LANES]) * scale).astype(bf16)
    k = rope(p[:, 512:640])
    v = p[:, 640:768]
    kv_ref[:, 0:128] = k.astype(bf16)
    kv_ref[:, 128:256] = pltpu.roll(k, HEAD_DIM, 1).astype(bf16)
    kv_ref[:, 256:384] = v.astype(bf16)
    kv_ref[:, 384:512] = pltpu.roll(v, HEAD_DIM, 1).astype(bf16)

    u = _gelu(p[:, 768:1024])
    gv = _gelu(p[:, 1024:1280])
    mu = jnp.mean(gv, axis=-1, keepdims=True)
    var = jnp.mean(jnp.square(gv - mu), axis=-1, keepdims=True)
    vn_ref[...] = ((gv - mu) * lax.rsqrt(var + EPS) * lng_ref[...] + lnb_ref[...]).astype(bf16)

    mixf_ref[:, 0:256] = u
    mixf_ref[:, 256:512] = p[:, 1280:1536]
    mixf_ref[:, 512:768] = p[:, 1536:1792]
    mixf_ref[:, 768:1024] = p[:, 1792:2048] * p[:, 2048:2304]


def _inproj(x2, n_seq, shift, scale, g, w_bf, cos_t, sin_t, ln_g, ln_b):
    t, d = x2.shape
    bt = min(INPROJ_TOKENS, n_seq)
    tps = n_seq // bt
    ncol = w_bf.shape[1]
    row = lambda i: (i, 0)
    return pl.pallas_call(
        _inproj_kernel,
        grid=(t // bt,),
        in_specs=[pl.BlockSpec((bt, d), row),
                  pl.BlockSpec((None, 1, d), lambda i: (i // tps, 0, 0)),
                  pl.BlockSpec((None, 1, d), lambda i: (i // tps, 0, 0)),
                  _full(g),
                  pl.BlockSpec((d, ncol), lambda i: (0, 0)),
                  pl.BlockSpec((bt, LANES), lambda i: (i % tps, 0)),
                  pl.BlockSpec((bt, LANES), lambda i: (i % tps, 0)),
                  _full(ln_g), _full(ln_b)],
        out_specs=[pl.BlockSpec((bt, ATT_Q_W), row), pl.BlockSpec((bt, 512), row),
                   pl.BlockSpec((bt, SGU_W), row), pl.BlockSpec((bt, 1024), row)],
        out_shape=[jax.ShapeDtypeStruct((t, ATT_Q_W), bf16), jax.ShapeDtypeStruct((t, 512), bf16),
                   jax.ShapeDtypeStruct((t, SGU_W), bf16), jax.ShapeDtypeStruct((t, 1024), f32)],
        compiler_params=_params("arbitrary"),
        name="inproj",
    )(x2, shift, scale, g, w_bf, cos_t, sin_t, ln_g, ln_b)


def _attn_block(sink_ref, q, kv, bias, o_ref, row0):
    k_nat, k_swp, v_nat, v_swp = (kv[:, i * LANES:(i + 1) * LANES] for i in range(4))
    nq = q.shape[0]
    low = lax.broadcasted_iota(i32, (nq, LANES), 1) < HEAD_DIM
    top = lax.broadcasted_iota(i32, (2 * nq, 1), 0) < nq
    zero = jnp.zeros((nq, LANES), q.dtype)
    for kvh in range(2):
        chunks = [q[:, (2 * kvh + i) * LANES:(2 * kvh + i + 1) * LANES] for i in range(2)]
        outs = []
        for half in range(2):
            keep = low if half == 0 else jnp.logical_not(low)
            qz = jnp.concatenate([jnp.where(keep, c, zero) for c in chunks], axis=0)
            kh = k_nat if kvh == half else k_swp
            vh = v_nat if kvh == half else v_swp
            s = lax.dot_general(qz, kh, (((1,), (1,)), ((), ())), preferred_element_type=f32)
            if bias is not None:
                s = s + bias
            h0 = 4 * kvh + half
            sk = jnp.where(top, sink_ref[h0], sink_ref[h0 + 2])
            mx = jnp.maximum(jnp.max(s, axis=1, keepdims=True), sk)
            e = jnp.exp(s - mx)
            den = jnp.sum(e, axis=1, keepdims=True) + jnp.exp(sk - mx)
            outs.append(jnp.dot(e.astype(bf16), vh, preferred_element_type=f32) / den)
        for i in range(2):
            rows = slice(i * nq, (i + 1) * nq)
            m = 2 * kvh + i
            o_ref[row0:row0 + nq, m * LANES:(m + 1) * LANES] = jnp.where(low, outs[0][rows], outs[1][rows]).astype(bf16)


def _window_attn_kernel(sink_ref, q_ref, kvp_ref, kvm_ref, kvn_ref, kvx_ref, o_ref, *, nb):
    jj = pl.program_id(1)
    nq = ATTN_TOKENS
    nband = 3 * nq
    kvm = kvm_ref[...]
    kvx = kvx_ref[...]
    r = lax.broadcasted_iota(i32, (nq, nband), 0)
    c = lax.broadcasted_iota(i32, (nq, nband), 1)
    dlt = c - r
    in_window = (dlt >= 0) & (dlt <= 2 * WINDOW)
    ctx_zeros = jnp.zeros((nq, kvx.shape[0]), f32)
    for i, kv_band in enumerate((jnp.concatenate([kvp_ref[...], kvm], axis=0),
                                 jnp.concatenate([kvm, kvn_ref[...]], axis=0))):
        j = 2 * jj + i
        lo = jnp.where(j == 0, nq, 0)
        hi = jnp.where(j == nb - 1, 2 * nq, nband)
        valid = in_window & (c >= lo) & (c < hi)
        bias = jnp.concatenate([jnp.where(valid, 0.0, -jnp.inf).astype(f32), ctx_zeros], axis=1)
        _attn_block(sink_ref, q_ref[i * nq:(i + 1) * nq, :], jnp.concatenate([kv_band, kvx], axis=0),
                    jnp.concatenate([bias, bias], axis=0), o_ref, i * nq)


def _ctx_attn_kernel(sink_ref, q_ref, kvx_ref, o_ref):
    _attn_block(sink_ref, q_ref[...], kvx_ref[...], None, o_ref, 0)


def _window_attn(q, kv, kv_ctx, sink, n_batch, n_seq, n_ctx):
    t = q.shape[0]
    nb = n_seq // ATTN_TOKENS
    assert nb % 2 == 0
    nb2 = nb // 2
    one, two = (ATTN_TOKENS, 512), (2 * ATTN_TOKENS, 512)
    return pl.pallas_call(
        functools.partial(_window_attn_kernel, nb=nb),
        grid=(n_batch, nb2),
        in_specs=[pl.BlockSpec(memory_space=pltpu.SMEM),
                  pl.BlockSpec(two, lambda b, j: (b * nb2 + j, 0)),
                  pl.BlockSpec(one, lambda b, j: (b * nb + jnp.maximum(2 * j - 1, 0), 0)),
                  pl.BlockSpec(two, lambda b, j: (b * nb2 + j, 0)),
                  pl.BlockSpec(one, lambda b, j: (b * nb + jnp.minimum(2 * j + 2, nb - 1), 0)),
                  pl.BlockSpec((n_ctx, 512), lambda b, j: (b, 0))],
        out_specs=pl.BlockSpec(two, lambda b, j: (b * nb2 + j, 0)),
        out_shape=jax.ShapeDtypeStruct((t, ATT_Q_W), bf16),
        compiler_params=_params("arbitrary", "arbitrary"),
        name="window_attn",
    )(sink, q, kv, kv, kv, kv_ctx)


def _ctx_attn(q, kv_ctx, sink, n_batch, n_ctx):
    nb = n_ctx // ATTN_TOKENS
    blk = (ATTN_TOKENS, 512)
    return pl.pallas_call(
        _ctx_attn_kernel,
        grid=(n_batch, nb),
        in_specs=[pl.BlockSpec(memory_space=pltpu.SMEM),
                  pl.BlockSpec(blk, lambda b, j: (b * nb + j, 0)),
                  pl.BlockSpec((n_ctx, 512), lambda b, j: (b, 0))],
        out_specs=pl.BlockSpec(blk, lambda b, j: (b * nb + j, 0)),
        out_shape=jax.ShapeDtypeStruct(q.shape, bf16),
        compiler_params=_params("arbitrary", "arbitrary"),
        name="ctx_attn",
    )(sink, q, kv_ctx)


def _mixer_kernel(attn_ref, vn_ref, mf_ref, mfp_ref, mfn_ref, x_ref, g1_ref, sh2_ref, sc2_ref, n2g_ref,
                  ws_ref, sb_ref, pw_ref, ps_ref, cw_ref, wo_ref, rw_ref, rb_ref, cin_ref,
                  xmid_ref, h2_ref, ti_ref, gt_ref, rk_ref, cout_ref, cnt_ref, *, n_seq, bt):
    i = pl.program_id(0)
    tps = n_seq // bt
    si = i % tps
    first = si == 0
    last = si == tps - 1
    n_ext = bt + 2 * HALO

    @pl.when(i == 0)
    def _():
        cnt_ref[...] = cin_ref[...]

    mf = mf_ref[...]
    u = mf[:, 0:256]

    def extended(lo, hi):
        prev = jnp.where(first, 0.0, mfp_ref[:, lo:hi])
        nxt = jnp.where(last, 0.0, mfn_ref[:, lo:hi])
        return jnp.concatenate([prev, mf[:, lo:hi], nxt], axis=0)

    def shifted(a, s):
        return pltpu.roll(a, s % n_ext, 0)

    xe = extended(256, 512)
    a1 = shifted(xe, 1) + xe
    a2 = shifted(a1, 1) + shifted(a1, -1)
    a3 = shifted(a2, 2) + shifted(a2, -2)
    a4 = shifted(a3, 4) + shifted(a3, -4)
    lane = lax.broadcasted_iota(i32, (bt, POOL_CH), 1)
    grp = lane >> 6
    sl = slice(HALO, HALO + bt)
    wsum = jnp.where(grp == 0, a1[sl], jnp.where(grp == 1, a2[sl], jnp.where(grp == 2, a3[sl], a4[sl])))
    pos = lax.broadcasted_iota(i32, (bt, POOL_CH), 0) + si * bt
    halfw = jnp.left_shift(1, grp)
    cnt = jnp.minimum(pos + halfw, n_seq) - jnp.maximum(pos - halfw, 0)
    dpool = wsum / cnt.astype(f32) - mf[:, 256:512]
    yc = jnp.dot(dpool.astype(bf16), pw_ref[...], preferred_element_type=f32) * ps_ref[...]

    ye = extended(768, 1024)
    cw = cw_ref[...]
    z = shifted(ye, 1) * cw[0:1, :] + ye * cw[1:2, :] + shifted(ye, -1) * cw[2:3, :]
    yd = mf[:, 512:768] * z[sl]

    hgrp = lax.broadcasted_iota(i32, (SGU_CHUNK, SGU_W), 1) >> 6
    ybs = []
    for cidx in range(bt // SGU_CHUNK):
        rows = slice(cidx * SGU_CHUNK, (cidx + 1) * SGU_CHUNK)
        vn_c = vn_ref[rows, :]
        s = jnp.zeros((SGU_CHUNK, SGU_W), f32)
        for hh in range(SGU_HEADS):
            sh = jnp.dot(ws_ref[hh], vn_c, preferred_element_type=f32)
            s = jnp.where(hgrp == hh, sh, s)
        ybs.append(u[rows, :] * (s + sb_ref[...]))
    yb = jnp.concatenate(ybs, axis=0)

    mix = jnp.concatenate([attn_ref[...], yb.astype(bf16), yc.astype(bf16), yd.astype(bf16)], axis=1)
    mo = jnp.dot(mix, wo_ref[...], preferred_element_type=f32)
    xm = x_ref[...] + g1_ref[...] * mo
    xmid_ref[...] = xm

    y = xm * lax.rsqrt(jnp.mean(xm * xm, axis=-1, keepdims=True) + EPS) * n2g_ref[...]
    h2 = y * (1.0 + sc2_ref[...]) + sh2_ref[...]
    _store_token_tiles(h2_ref, h2)

    lt = lax.dot_general(rw_ref[...], h2, (((1,), (1,)), ((), ())),
                         precision=lax.Precision.HIGHEST, preferred_element_type=f32) + rb_ref[...]
    n_exp = lt.shape[0]
    eidx = lax.broadcasted_iota(i32, lt.shape, 0)
    work = lt
    idxs, vals = [], []
    for _ in range(TOP_K):
        m = jnp.max(work, axis=0, keepdims=True)
        idx = jnp.min(jnp.where(work == m, eidx, n_exp), axis=0, keepdims=True)
        idxs.append(idx)
        vals.append(m)
        work = jnp.where(eidx == idx, -jnp.inf, work)
    exps = [jnp.exp(v - vals[0]) for v in vals]
    den = exps[0] + exps[1] + exps[2] + exps[3]
    onehot = jnp.zeros(lt.shape, f32)
    for kk in range(TOP_K):
        ti_ref[kk:kk + 1, :] = idxs[kk]
        gt_ref[kk:kk + 1, :] = exps[kk] / den
        onehot = onehot + (eidx == idxs[kk]).astype(f32)
    tri = (lax.broadcasted_iota(i32, (bt, bt), 0) < lax.broadcasted_iota(i32, (bt, bt), 1)).astype(bf16)
    base = jnp.dot(onehot.astype(bf16), tri, preferred_element_type=f32) + cnt_ref[...]
    for kk in range(TOP_K):
        rk_ref[kk:kk + 1, :] = jnp.sum(jnp.where(eidx == idxs[kk], base, 0.0), axis=0, keepdims=True).astype(i32)
    cnt_ref[...] = cnt_ref[...] + jnp.sum(onehot, axis=1, keepdims=True)
    cout_ref[...] = cnt_ref[...]


def _mixer(attn, vn, mixf, x2, n_seq, g1, sh2, sc2, n2g, ws_bf, sgu_bias, pool_bd, pool_scale, conv_w,
           wo_bf, rw_t, rb, cnt_in):
    t, d = x2.shape
    bt = min(MIXER_TOKENS, n_seq)
    tps = n_seq // bt
    hb = bt // HALO
    n_halo = t // HALO
    n_exp = rw_t.shape[0]
    row = lambda i: (i, 0)
    per_batch = pl.BlockSpec((None, 1, d), lambda i: (i // tps, 0, 0))
    col = lambda i: (0, i)
    return pl.pallas_call(
        functools.partial(_mixer_kernel, n_seq=n_seq, bt=bt),
        grid=(t // bt,),
        in_specs=[pl.BlockSpec((bt, ATT_Q_W), row), pl.BlockSpec((bt, SGU_W), row), pl.BlockSpec((bt, 1024), row),
                  pl.BlockSpec((HALO, 1024), lambda i: (jnp.maximum(i * hb - 1, 0), 0)),
                  pl.BlockSpec((HALO, 1024), lambda i: (jnp.minimum((i + 1) * hb, n_halo - 1), 0)),
                  pl.BlockSpec((bt, d), row), per_batch, per_batch, per_batch, _full(n2g),
                  _full(ws_bf), _full(sgu_bias), _full(pool_bd), _full(pool_scale), _full(conv_w),
                  _full(wo_bf), _full(rw_t), _full(rb), _full(cnt_in)],
        out_specs=[pl.BlockSpec((bt, d), row), pl.BlockSpec((bt * SUBLANES, LANES), row),
                   pl.BlockSpec((TOP_K, bt), col), pl.BlockSpec((TOP_K, bt), col), pl.BlockSpec((TOP_K, bt), col),
                   pl.BlockSpec((n_exp, 1), lambda i: (0, 0))],
        out_shape=[jax.ShapeDtypeStruct((t, d), f32), jax.ShapeDtypeStruct((t * SUBLANES, LANES), f32),
                   jax.ShapeDtypeStruct((TOP_K, t), i32), jax.ShapeDtypeStruct((TOP_K, t), f32),
                   jax.ShapeDtypeStruct((TOP_K, t), i32), jax.ShapeDtypeStruct((n_exp, 1), f32)],
        scratch_shapes=[pltpu.VMEM((n_exp, 1), f32)],
        compiler_params=_params("arbitrary"),
        name="mixer_router",
    )(attn, vn, mixf, mixf, mixf, x2, g1, sh2, sc2, n2g, ws_bf, sgu_bias, pool_bd, pool_scale, conv_w,
      wo_bf, rw_t, rb, cnt_in)


def _dest_kernel(ti_ref, rk_ref, ps_ref, o_ref):
    ti = ti_ref[...]
    acc = rk_ref[...]
    for e in range(ps_ref.shape[0]):
        acc = acc + jnp.where(ti == e, ps_ref[e], 0)
    o_ref[...] = acc


def _dest(top_i, rank, pstarts):
    k, t = top_i.shape
    bt = next(b for b in (2048, 1024, 512, 256, 128) if t % b == 0)
    col = lambda i: (0, i)
    return pl.pallas_call(
        _dest_kernel,
        grid=(t // bt,),
        in_specs=[pl.BlockSpec((k, bt), col), pl.BlockSpec((k, bt), col), pl.BlockSpec(memory_space=pltpu.SMEM)],
        out_specs=pl.BlockSpec((k, bt), col),
        out_shape=jax.ShapeDtypeStruct((k, t), i32),
        compiler_params=_params("arbitrary"),
        name="slot_index",
    )(top_i, rank, pstarts)


def _tile_copy(src, s, dst, d, sem):
    return pltpu.make_async_copy(src.at[pl.ds(pl.multiple_of(s * SUBLANES, SUBLANES), SUBLANES)],
                                 dst.at[pl.ds(pl.multiple_of(d * SUBLANES, SUBLANES), SUBLANES)], sem)


def _wait_tiles(ref, n, sem):
    pltpu.make_async_copy(ref.at[pl.ds(0, n * SUBLANES)], ref.at[pl.ds(0, n * SUBLANES)], sem).wait()


def _chunk_major(dest, bt):
    k, t = dest.shape
    return dest.reshape(k, t // bt, bt).transpose(1, 0, 2).reshape(-1)


def _split_slabs(w_ref, g_ref, l_ref, scr):
    f = g_ref.shape[1]
    for s in range(w_ref.shape[0] // LANES):
        rows = slice(s * LANES, (s + 1) * LANES)
        scr[...] = w_ref[rows, :].T
        g_ref[rows, :] = scr[pl.ds(0, f, stride=2), :].T.astype(bf16)
        l_ref[rows, :] = scr[pl.ds(1, f, stride=2), :].T.astype(bf16)


def _split_w1(w1, layer):
    _, n_exp, d, f2 = w1.shape
    f = f2 // 2
    rows = 2 * LANES
    blk = pl.BlockSpec((None, rows, f), lambda e, j: (e, j, 0))
    return pl.pallas_call(
        _split_slabs,
        grid=(n_exp, d // rows),
        in_specs=[pl.BlockSpec((None, None, rows, f2), lambda e, j: (layer, e, j, 0))],
        out_specs=[blk, blk],
        out_shape=[jax.ShapeDtypeStruct((n_exp, d, f), bf16)] * 2,
        scratch_shapes=[pltpu.VMEM((f2, LANES), f32)],
        compiler_params=_params("arbitrary", "arbitrary"),
        name="split_w1",
    )(w1)


class _SplitSide:
    def __init__(self, w1, layer, n_steps):
        _, n_exp, d, f2 = w1.shape
        fits = [r for r in (LANES, 2 * LANES, 4 * LANES, 8 * LANES) if d % r == 0 and n_exp * (d // r) <= n_steps]
        self.ok = bool(fits)
        if not self.ok:
            return
        rows = fits[0]
        per = d // rows
        self.w1 = w1
        self.n_side = n_exp * per
        last = self.n_side - 1
        self.in_spec = pl.BlockSpec((None, None, rows, f2),
                                    lambda i: (layer, jnp.minimum(i, last) // per, jnp.minimum(i, last) % per, 0))
        self.out_spec = pl.BlockSpec((None, rows, f2 // 2),
                                     lambda i: (jnp.minimum(i, last) // per, jnp.minimum(i, last) % per, 0))
        self.out_shape = jax.ShapeDtypeStruct((n_exp, d, f2 // 2), bf16)
        self.scratch = pltpu.VMEM((f2, LANES), f32)

    def run(self, w_ref, g_ref, l_ref, scr):
        @pl.when(pl.program_id(0) < self.n_side)
        def _():
            _split_slabs(w_ref, g_ref, l_ref, scr)


def _dispatch_kernel(dest_ref, pad_ref, h_ref, *rest, bt, fill_pad, side):
    rest = list(rest)
    if not fill_pad:
        rest.pop(0)
    w_ref = rest.pop(0) if side else None
    xs_ref = rest.pop(0)
    g_ref, l_ref = (rest.pop(0), rest.pop(0)) if side else (None, None)
    zero_ref = rest.pop(0) if fill_pad else None
    sem = rest.pop(0)

    if fill_pad:
        @pl.when(pl.program_id(0) == 0)
        def _():
            zero_ref[...] = jnp.zeros_like(zero_ref)
            n_exp = pad_ref.shape[0] // 2

            def per_expert(e, carry):
                first, count = pad_ref[e], pad_ref[n_exp + e]

                def issue(r, c):
                    _tile_copy(zero_ref, 0, xs_ref, first + r, sem).start()
                    return c

                def drain(r, c):
                    _tile_copy(zero_ref, 0, xs_ref, 0, sem).wait()
                    return c

                lax.fori_loop(0, count, issue, 0)
                lax.fori_loop(0, count, drain, 0)
                return carry

            lax.fori_loop(0, n_exp, per_expert, 0)

    def start(t, carry):
        for kk in range(TOP_K):
            _tile_copy(h_ref, t, xs_ref, dest_ref[kk * bt + t], sem).start(priority=kk % 2)
        return carry

    lax.fori_loop(0, bt, start, 0, unroll=DMA_UNROLL)
    if side:
        side.run(w_ref, g_ref, l_ref, rest.pop(0))
    for _ in range(TOP_K):
        _wait_tiles(h_ref, bt, sem)


def _dispatch(dest, pad, h2, xs, n_slots, w1_side=None):
    t = h2.shape[0] // SUBLANES
    bt = min(DISPATCH_TOKENS, t)
    n_steps = t // bt
    fill_pad = xs is None
    side = _SplitSide(*w1_side, n_steps) if w1_side else None
    side = side if side is not None and side.ok else None
    in_specs = [pl.BlockSpec((TOP_K * bt,), lambda i: (i,), memory_space=pltpu.SMEM),
                pl.BlockSpec(memory_space=pltpu.SMEM),
                pl.BlockSpec((bt * SUBLANES, LANES), lambda i: (i, 0))]
    args = [_chunk_major(dest, bt), pad, h2]
    out_specs = [pl.BlockSpec(memory_space=pl.ANY)]
    out_shape = [jax.ShapeDtypeStruct((n_slots * SUBLANES, LANES), f32)]
    scratch = [pltpu.SemaphoreType.DMA(())]
    if fill_pad:
        scratch = [pltpu.VMEM((SUBLANES, LANES), f32)] + scratch
    else:
        in_specs.append(pl.BlockSpec(memory_space=pl.ANY))
        args.append(xs)
    if side:
        in_specs.append(side.in_spec)
        args.append(side.w1)
        out_specs += [side.out_spec] * 2
        out_shape += [side.out_shape] * 2
        scratch.append(side.scratch)
    outs = pl.pallas_call(
        functools.partial(_dispatch_kernel, bt=bt, fill_pad=fill_pad, side=side),
        grid=(n_steps,),
        in_specs=in_specs,
        out_specs=out_specs,
        out_shape=out_shape,
        scratch_shapes=scratch,
        input_output_aliases={} if fill_pad else {3: 0},
        compiler_params=_params("arbitrary"),
        name="dispatch",
    )(*args)
    return outs[0], (tuple(outs[1:]) if side else None)


def _expert_kernel(be_ref, nu_ref, x_ref, w1g_ref, w1l_ref, b1g_ref, b1l_ref, w2f_ref, b2_ref, y_ref, w2_ref):
    i = pl.program_id(0)
    active = i < nu_ref[0]
    new_expert = jnp.logical_or(i == 0, be_ref[i] != be_ref[jnp.maximum(i - 1, 0)])

    @pl.when(jnp.logical_and(active, new_expert))
    def _():
        w2_ref[...] = w2f_ref[...].astype(bf16)

    @pl.when(active)
    def _():
        bm = x_ref.shape[0] // SUBLANES
        x = _load_token_tiles(x_ref, bm).astype(bf16)
        zg = jnp.dot(x, w1g_ref[...], preferred_element_type=f32) + b1g_ref[...]
        zl = jnp.dot(x, w1l_ref[...], preferred_element_type=f32) + b1l_ref[...]
        g = jnp.minimum(zg, SWIGLU_LIMIT)
        lin = jnp.clip(zl, -SWIGLU_LIMIT, SWIGLU_LIMIT)
        a = g * jax.nn.sigmoid(SWIGLU_ALPHA * g) * (lin + 1.0)
        y = jnp.dot(a.astype(bf16), w2_ref[...], preferred_element_type=f32) + b2_ref[...]
        _store_token_tiles(y_ref, y)

    @pl.when(i >= nu_ref[0])
    def _():
        y_ref[...] = jnp.zeros_like(y_ref)


def _experts(block_e, n_used, xs, w1g, w1l, b1g, b1l, w2_all, layer, b2):
    n_exp, d, f = w1g.shape
    bm = EXPERT_ROWS
    blk = (bm * SUBLANES, LANES)
    xrow = lambda i, be, nu: (jnp.minimum(i, nu[0] - 1), 0)
    wsel = lambda i, be, nu: (be[i], 0, 0)
    grid_spec = pltpu.PrefetchScalarGridSpec(
        num_scalar_prefetch=2,
        grid=(xs.shape[0] // blk[0],),
        in_specs=[pl.BlockSpec(blk, xrow),
                  pl.BlockSpec((None, d, f), wsel), pl.BlockSpec((None, d, f), wsel),
                  pl.BlockSpec((None, 1, f), wsel), pl.BlockSpec((None, 1, f), wsel),
                  pl.BlockSpec((None, None, f, d), lambda i, be, nu: (layer, be[i], 0, 0)),
                  pl.BlockSpec((None, 1, d), wsel)],
        out_specs=pl.BlockSpec(blk, lambda i, be, nu: (i, 0)),
        scratch_shapes=[pltpu.VMEM((f, d), bf16)],
    )
    return pl.pallas_call(
        _expert_kernel,
        grid_spec=grid_spec,
        out_shape=jax.ShapeDtypeStruct(xs.shape, f32),
        compiler_params=_params("arbitrary"),
        name="experts",
    )(block_e, n_used, xs, w1g, w1l, b1g, b1l, w2_all, b2)


def _combine_kernel(dest_ref, dnext_ref, gate_ref, ys_ref, x_ref, g2_ref, fg_ref, *rest, bt, final, side):
    rest = list(rest)
    w_ref = rest.pop(0) if side else None
    o_ref = rest.pop(0)
    g_ref, l_ref = (rest.pop(0), rest.pop(0)) if side else (None, None)
    buf, sem = rest.pop(0), rest.pop(0)
    i = pl.program_id(0)
    n = pl.num_programs(0)
    slot = i % 2

    def gather(table_ref, slot_):
        def start(t, carry):
            for kk in range(TOP_K):
                _tile_copy(ys_ref, table_ref[kk * bt + t], buf.at[slot_, kk], t, sem.at[slot_]).start(priority=kk % 2)
            return carry

        lax.fori_loop(0, bt, start, 0, unroll=DMA_UNROLL)

    @pl.when(i == 0)
    def _():
        gather(dest_ref, 0)

    @pl.when(i + 1 < n)
    def _():
        gather(dnext_ref, 1 - slot)

    if side:
        side.run(w_ref, g_ref, l_ref, rest.pop(0))

    for kk in range(TOP_K):
        _wait_tiles(buf.at[slot, kk], bt, sem.at[slot])

    gate = gate_ref[...]
    gate_b = [jnp.broadcast_to(gate[:, kk:kk + 1], (bt, LANES)) for kk in range(TOP_K)]
    chunks = []
    for s in range(SUBLANES):
        lanes = slice(s * LANES, (s + 1) * LANES)
        acc = gate_b[0] * buf.at[slot, 0][pl.ds(s, bt, stride=SUBLANES), :]
        for kk in range(1, TOP_K):
            acc = acc + gate_b[kk] * buf.at[slot, kk][pl.ds(s, bt, stride=SUBLANES), :]
        chunks.append(x_ref[:, lanes] + g2_ref[:, lanes] * acc)
    xo = jnp.concatenate(chunks, axis=1)
    if final:
        xo = xo * lax.rsqrt(jnp.mean(xo * xo, axis=-1, keepdims=True) + EPS) * fg_ref[...]
    o_ref[...] = xo


def _combine(dest, gates, ys, x_mid, n_seq, g2, final_g, final, w1_side=None):
    t, d = x_mid.shape
    bt = min(COMBINE_TOKENS, n_seq)
    tps = n_seq // bt
    row = lambda i: (i, 0)
    n_steps = t // bt
    table = _chunk_major(dest, bt)
    side = _SplitSide(*w1_side, n_steps) if w1_side else None
    side = side if side is not None and side.ok else None
    in_specs = [pl.BlockSpec((TOP_K * bt,), lambda i: (i,), memory_space=pltpu.SMEM),
                pl.BlockSpec((TOP_K * bt,), lambda i: (jnp.minimum(i + 1, n_steps - 1),), memory_space=pltpu.SMEM),
                pl.BlockSpec((bt, TOP_K), row),
                pl.BlockSpec(memory_space=pl.ANY),
                pl.BlockSpec((bt, d), row),
                pl.BlockSpec((None, 1, d), lambda i: (i // tps, 0, 0)),
                _full(final_g)]
    args = [table, table, gates, ys, x_mid, g2, final_g]
    out_specs = [pl.BlockSpec((bt, d), row)]
    out_shape = [jax.ShapeDtypeStruct((t, d), f32)]
    scratch = [pltpu.VMEM((2, TOP_K, bt * SUBLANES, LANES), f32), pltpu.SemaphoreType.DMA((2,))]
    if side:
        in_specs.append(side.in_spec)
        args.append(side.w1)
        out_specs += [side.out_spec] * 2
        out_shape += [side.out_shape] * 2
        scratch.append(side.scratch)
    outs = pl.pallas_call(
        functools.partial(_combine_kernel, bt=bt, final=final, side=side),
        grid=(n_steps,),
        in_specs=in_specs,
        out_specs=out_specs,
        out_shape=out_shape,
        scratch_shapes=scratch,
        compiler_params=_params("arbitrary"),
        name="combine",
    )(*args)
    return outs[0], (tuple(outs[1:]) if side else None)


def _rope_tables(n_tokens):
    rows = n_tokens // GRID_W
    row = jnp.repeat(jnp.arange(rows), GRID_W).astype(f32)
    col = jnp.tile(jnp.arange(GRID_W), rows).astype(f32)
    n_freq = HEAD_DIM // 4
    inv = ROPE_BASE ** (-jnp.arange(n_freq, dtype=f32) / n_freq)
    ang = jnp.concatenate([row[:, None] * inv, col[:, None] * inv], axis=-1)
    cos, sin = jnp.cos(ang), jnp.sin(ang)
    return jnp.tile(cos, (1, 4)), jnp.concatenate([-sin, sin, -sin, sin], axis=-1)


def kernel(x, c, ctx, c_ctx, norm1_g, norm2_g, ada_w, ada_b, w_in, attn_sink, sgu_ws, sgu_b, sgu_ln_g, sgu_ln_b,
           pool_w, pool_scale, conv_w, w_out, router_w, router_b, exp_w1, exp_b1, exp_w2, exp_b2, final_g):
    n_batch, n_seq, d = x.shape
    n_ctx = ctx.shape[1]
    depth = ada_w.shape[0]
    n_exp = router_w.shape[2]
    assert d == SUBLANES * LANES, "token rows are moved as single (8, 128) tiles"
    t_lat, t_ctx = n_batch * n_seq, n_batch * n_ctx
    bm = EXPERT_ROWS

    cvec = jnp.concatenate([c, c_ctx[None, :], jnp.zeros((SUBLANES - n_batch - 1, d), f32)], axis=0)
    mods = _ada(cvec, ada_w, ada_b)

    cos_l, sin_l = _rope_tables(n_seq)
    cos_c, sin_c = jnp.ones((n_ctx, LANES), f32), jnp.zeros((n_ctx, LANES), f32)

    xl = x.reshape(t_lat, d)
    xc = ctx.reshape(t_ctx, d)
    row2 = lambda a: a.reshape(1, -1)
    w1_split = {}

    for l in range(depth):
        last = l == depth - 1
        ml = mods[l, :n_batch].reshape(n_batch, 6, 1, d)
        mc = jnp.broadcast_to(mods[l, n_batch].reshape(1, 6, 1, d), (n_batch, 6, 1, d))
        sh1l, sc1l, g1l, sh2l, sc2l, g2l = (ml[:, i] for i in range(6))
        sh1c, sc1c, g1c, sh2c, sc2c, g2c = (mc[:, i] for i in range(6))

        w_in_bf = w_in[l].astype(bf16)
        wo_bf = w_out[l].astype(bf16)
        ws_bf = sgu_ws[l].astype(bf16)
        sgu_bias = jnp.repeat(sgu_b[l].T, SGU_W // SGU_HEADS, axis=1)
        pool_bd = jax.scipy.linalg.block_diag(*[pool_w[l, g] for g in range(pool_w.shape[1])]).astype(bf16)
        n1g, n2g = row2(norm1_g[l]), row2(norm2_g[l])
        lng, lnb, psc = row2(sgu_ln_g[l]), row2(sgu_ln_b[l]), row2(pool_scale[l])
        rw_t = router_w[l].T
        rb = router_b[l].reshape(n_exp, 1)
        sink = attn_sink[l]
        mix_w = (ws_bf, sgu_bias, pool_bd, psc, conv_w[l], wo_bf, rw_t, rb)

        qc, kvc, vnc, mfc = _inproj(xc, n_ctx, sh1c, sc1c, n1g, w_in_bf, cos_c, sin_c, lng, lnb)
        ql, kvl, vnl, mfl = _inproj(xl, n_seq, sh1l, sc1l, n1g, w_in_bf, cos_l, sin_l, lng, lnb)
        attn_l = _window_attn(ql, kvl, kvc, sink, n_batch, n_seq, n_ctx)
        cnt0 = jnp.zeros((n_exp, 1), f32)
        xmid_l, h2_l, ti, gt, rk, cnt = _mixer(attn_l, vnl, mfl, xl, n_seq, g1l, sh2l, sc2l, n2g, *mix_w, cnt0)
        if not last:
            attn_c = _ctx_attn(qc, kvc, sink, n_batch, n_ctx)
            xmid_c, h2_c, ti_c, gt_c, rk_c, cnt = _mixer(attn_c, vnc, mfc, xc, n_ctx, g1c, sh2c, sc2c, n2g,
                                                         *mix_w, cnt)
            ti = jnp.concatenate([ti, ti_c], axis=1)
            gt = jnp.concatenate([gt, gt_c], axis=1)
            rk = jnp.concatenate([rk, rk_c], axis=1)
        t_all = ti.shape[1]

        counts = cnt[:, 0].astype(i32)
        padded = (counts + bm - 1) // bm * bm
        pends = jnp.cumsum(padded)
        pstarts = pends - padded
        n_blocks = -(-(t_all * TOP_K) // bm) + n_exp
        starts = jnp.arange(n_blocks, dtype=i32) * bm
        block_e = jnp.minimum(jnp.sum((pends[None, :] <= starts[:, None]).astype(i32), axis=1), n_exp - 1)
        n_used = (pends[-1:] // bm).astype(i32)
        dest = _dest(ti, rk, pstarts)
        gates = gt.T
        pad = jnp.concatenate([pstarts + counts, padded - counts]).astype(i32)

        xs, split = _dispatch(dest[:, :t_lat], pad, h2_l, None, n_blocks * bm,
                              None if l in w1_split else (exp_w1, l))
        if l not in w1_split:
            w1_split[l] = split or _split_w1(exp_w1, l)
        if not last:
            xs, _ = _dispatch(dest[:, t_lat:], pad, h2_c, xs, n_blocks * bm)

        w1g, w1l = w1_split[l]
        b1 = exp_b1[l]
        b1g, b1l = b1[:, None, 0::2], b1[:, None, 1::2]
        ys = _experts(block_e, n_used, xs, w1g, w1l, b1g, b1l, exp_w2, l, exp_b2[l][:, None, :])

        fg = row2(final_g)
        xl, split = _combine(dest[:, :t_lat], gates[:t_lat], ys, xmid_l, n_seq, g2l, fg, last,
                             None if last else (exp_w1, l + 1))
        if split:
            w1_split[l + 1] = split
        if not last:
            xc, _ = _combine(dest[:, t_lat:], gates[t_lat:], ys, xmid_c, n_ctx, g2c, fg, False)

    return xl.reshape(n_batch, n_seq, d)
```

```python
import functools

import jax
import jax.numpy as jnp
from jax import lax
from jax.experimental import pallas as pl
from jax.experimental.pallas import tpu as pltpu

f32 = jnp.float32
bf16 = jnp.bfloat16
i32 = jnp.int32

GRID_W = 64
EPS = 1e-6
N_Q_HEADS = 8
HEAD_DIM = 64
WINDOW = 128
ROPE_BASE = 10000.0
ATT_Q_W = 512
ATT_KV_W = 128
SGU_HEADS = 4
SGU_W = 256
SGU_CHUNK = 128
POOL_CH = 256
CONV_CH = 256
MIX_WIDTH = 1280
TOP_K = 4
SWIGLU_LIMIT = 7.0
SWIGLU_ALPHA = 1.702
SQRT_HALF = 0.7071067811865476

LANES = 128
SUBLANES = 8
VMEM_LIMIT_BYTES = 56 * 1024 * 1024

INPROJ_TOKENS = 512
ATTN_TOKENS = 128
MIXER_TOKENS = 512
EXPERT_ROWS = 512
HALO = 8
ROW_UNROLL = 4


def _params(*sem):
    return pltpu.CompilerParams(dimension_semantics=sem, vmem_limit_bytes=VMEM_LIMIT_BYTES)


def _full(a):
    nd = a.ndim
    return pl.BlockSpec(a.shape, lambda *_: (0,) * nd)


def _store_token_tiles(ref, val):
    n = val.shape[0]
    for s in range(SUBLANES):
        ref[pl.ds(s, n, stride=SUBLANES), :] = val[:, s * LANES:(s + 1) * LANES]


def _load_token_tiles(ref, n):
    return jnp.concatenate([ref[pl.ds(s, n, stride=SUBLANES), :] for s in range(SUBLANES)], axis=1)


def _gelu(x):
    return 0.5 * x * (1.0 + lax.erf(x * SQRT_HALF))


def _ada_kernel(c_ref, w_ref, b_ref, o_ref):
    c = c_ref[...]
    s = c * jax.nn.sigmoid(c)
    o_ref[...] = jnp.dot(s, w_ref[...], precision=lax.Precision.HIGHEST, preferred_element_type=f32) + b_ref[...]


def _ada(cvec, ada_w, ada_b):
    depth, d, n = ada_w.shape
    tn = 1536
    return pl.pallas_call(
        _ada_kernel,
        grid=(depth, n // tn),
        in_specs=[pl.BlockSpec(cvec.shape, lambda l, j: (0, 0)),
                  pl.BlockSpec((None, d, tn), lambda l, j: (l, 0, j)),
                  pl.BlockSpec((None, 1, tn), lambda l, j: (l, 0, j))],
        out_specs=pl.BlockSpec((None, cvec.shape[0], tn), lambda l, j: (l, 0, j)),
        out_shape=jax.ShapeDtypeStruct((depth, cvec.shape[0], n), f32),
        compiler_params=_params("arbitrary", "arbitrary"),
        name="ada_mod",
    )(cvec, ada_w, ada_b.reshape(depth, 1, n))


def _inproj_kernel(x_ref, sh_ref, sc_ref, g_ref, w_ref, cos_ref, sin_ref, lng_ref, lnb_ref,
                   q_ref, kv_ref, vn_ref, mixf_ref):
    x = x_ref[...]
    y = x * lax.rsqrt(jnp.mean(x * x, axis=-1, keepdims=True) + EPS) * g_ref[...]
    h = y * (1.0 + sc_ref[...]) + sh_ref[...]
    p = jnp.dot(h.astype(bf16), w_ref[...], preferred_element_type=f32)

    cos = cos_ref[...]
    sin = sin_ref[...]
    lane = lax.broadcasted_iota(i32, cos.shape, 1)
    first_half = (lane & (HEAD_DIM - 1)) < HEAD_DIM // 2

    def rope(t):
        partner = jnp.where(first_half, pltpu.roll(t, LANES - HEAD_DIM // 2, 1), pltpu.roll(t, HEAD_DIM // 2, 1))
        return t * cos + partner * sin

    scale = HEAD_DIM ** -0.5
    for m in range(ATT_Q_W // LANES):
        q_ref[:, m * LANES:(m + 1) * LANES] = (rope(p[:, m * LANES:(m + 1) * LANES]) * scale).astype(bf16)
    k = rope(p[:, 512:640])
    v = p[:, 640:768]
    kv_ref[:, 0:128] = k.astype(bf16)
    kv_ref[:, 128:256] = pltpu.roll(k, HEAD_DIM, 1).astype(bf16)
    kv_ref[:, 256:384] = v.astype(bf16)
    kv_ref[:, 384:512] = pltpu.roll(v, HEAD_DIM, 1).astype(bf16)

    u = _gelu(p[:, 768:1024])
    gv = _gelu(p[:, 1024:1280])
    mu = jnp.mean(gv, axis=-1, keepdims=True)
    var = jnp.mean(jnp.square(gv - mu), axis=-1, keepdims=True)
    vn_ref[...] = ((gv - mu) * lax.rsqrt(var + EPS) * lng_ref[...] + lnb_ref[...]).astype(bf16)

    mixf_ref[:, 0:256] = u
    mixf_ref[:, 256:512] = p[:, 1280:1536]
    mixf_ref[:, 512:768] = p[:, 1536:1792]
    mixf_ref[:, 768:1024] = p[:, 1792:2048] * p[:, 2048:2304]


def _inproj(x2, n_seq, shift, scale, g, w_bf, cos_t, sin_t, ln_g, ln_b):
    t, d = x2.shape
    bt = min(INPROJ_TOKENS, n_seq)
    tps = n_seq // bt
    ncol = w_bf.shape[1]
    row = lambda i: (i, 0)
    return pl.pallas_call(
        _inproj_kernel,
        grid=(t // bt,),
        in_specs=[pl.BlockSpec((bt, d), row),
                  pl.BlockSpec((None, 1, d), lambda i: (i // tps, 0, 0)),
                  pl.BlockSpec((None, 1, d), lambda i: (i // tps, 0, 0)),
                  _full(g),
                  pl.BlockSpec((d, ncol), lambda i: (0, 0)),
                  pl.BlockSpec((bt, LANES), lambda i: (i % tps, 0)),
                  pl.BlockSpec((bt, LANES), lambda i: (i % tps, 0)),
                  _full(ln_g), _full(ln_b)],
        out_specs=[pl.BlockSpec((bt, ATT_Q_W), row), pl.BlockSpec((bt, 512), row),
                   pl.BlockSpec((bt, SGU_W), row), pl.BlockSpec((bt, 1024), row)],
        out_shape=[jax.ShapeDtypeStruct((t, ATT_Q_W), bf16), jax.ShapeDtypeStruct((t, 512), bf16),
                   jax.ShapeDtypeStruct((t, SGU_W), bf16), jax.ShapeDtypeStruct((t, 1024), f32)],
        compiler_params=_params("arbitrary"),
        name="inproj",
    )(x2, shift, scale, g, w_bf, cos_t, sin_t, ln_g, ln_b)


def _attn_block(sink_ref, q, kv, bias, o_ref, row0):
    k_nat, k_swp, v_nat, v_swp = (kv[:, i * LANES:(i + 1) * LANES] for i in range(4))
    nq = q.shape[0]
    low = lax.broadcasted_iota(i32, (nq, LANES), 1) < HEAD_DIM
    top = lax.broadcasted_iota(i32, (2 * nq, 1), 0) < nq
    zero = jnp.zeros((nq, LANES), q.dtype)
    for kvh in range(2):
        chunks = [q[:, (2 * kvh + i) * LANES:(2 * kvh + i + 1) * LANES] for i in range(2)]
        outs = []
        for half in range(2):
            keep = low if half == 0 else jnp.logical_not(low)
            qz = jnp.concatenate([jnp.where(keep, c, zero) for c in chunks], axis=0)
            kh = k_nat if kvh == half else k_swp
            vh = v_nat if kvh == half else v_swp
            s = lax.dot_general(qz, kh, (((1,), (1,)), ((), ())), preferred_element_type=f32)
            if bias is not None:
                s = s + bias
            h0 = 4 * kvh + half
            sk = jnp.where(top, sink_ref[h0], sink_ref[h0 + 2])
            mx = jnp.maximum(jnp.max(s, axis=1, keepdims=True), sk)
            e = jnp.exp(s - mx)
            den = jnp.sum(e, axis=1, keepdims=True) + jnp.exp(sk - mx)
            outs.append(jnp.dot(e.astype(bf16), vh, preferred_element_type=f32) / den)
        for i in range(2):
            rows = slice(i * nq, (i + 1) * nq)
            m = 2 * kvh + i
            o_ref[row0:row0 + nq, m * LANES:(m + 1) * LANES] = jnp.where(low, outs[0][rows], outs[1][rows]).astype(bf16)


def _window_attn_kernel(sink_ref, q_ref, kvp_ref, kvm_ref, kvn_ref, kvx_ref, o_ref, *, nb):
    jj = pl.program_id(1)
    nq = ATTN_TOKENS
    nband = 3 * nq
    kvm = kvm_ref[...]
    kvx = kvx_ref[...]
    r = lax.broadcasted_iota(i32, (nq, nband), 0)
    c = lax.broadcasted_iota(i32, (nq, nband), 1)
    dlt = c - r
    in_window = (dlt >= 0) & (dlt <= 2 * WINDOW)
    ctx_zeros = jnp.zeros((nq, kvx.shape[0]), f32)
    for i, kv_band in enumerate((jnp.concatenate([kvp_ref[...], kvm], axis=0),
                                 jnp.concatenate([kvm, kvn_ref[...]], axis=0))):
        j = 2 * jj + i
        lo = jnp.where(j == 0, nq, 0)
        hi = jnp.where(j == nb - 1, 2 * nq, nband)
        valid = in_window & (c >= lo) & (c < hi)
        bias = jnp.concatenate([jnp.where(valid, 0.0, -jnp.inf).astype(f32), ctx_zeros], axis=1)
        _attn_block(sink_ref, q_ref[i * nq:(i + 1) * nq, :], jnp.concatenate([kv_band, kvx], axis=0),
                    jnp.concatenate([bias, bias], axis=0), o_ref, i * nq)


def _ctx_attn_kernel(sink_ref, q_ref, kvx_ref, o_ref):
    _attn_block(sink_ref, q_ref[...], kvx_ref[...], None, o_ref, 0)


def _window_attn(q, kv, kv_ctx, sink, n_batch, n_seq, n_ctx):
    t = q.shape[0]
    nb = n_seq // ATTN_TOKENS
    assert nb % 2 == 0
    nb2 = nb // 2
    one, two = (ATTN_TOKENS, 512), (2 * ATTN_TOKENS, 512)
    return pl.pallas_call(
        functools.partial(_window_attn_kernel, nb=nb),
        grid=(n_batch, nb2),
        in_specs=[pl.BlockSpec(memory_space=pltpu.SMEM),
                  pl.BlockSpec(two, lambda b, j: (b * nb2 + j, 0)),
                  pl.BlockSpec(one, lambda b, j: (b * nb + jnp.maximum(2 * j - 1, 0), 0)),
                  pl.BlockSpec(two, lambda b, j: (b * nb2 + j, 0)),
                  pl.BlockSpec(one, lambda b, j: (b * nb + jnp.minimum(2 * j + 2, nb - 1), 0)),
                  pl.BlockSpec((n_ctx, 512), lambda b, j: (b, 0))],
        out_specs=pl.BlockSpec(two, lambda b, j: (b * nb2 + j, 0)),
        out_shape=jax.ShapeDtypeStruct((t, ATT_Q_W), bf16),
        compiler_params=_params("arbitrary", "arbitrary"),
        name="window_attn",
    )(sink, q, kv, kv, kv, kv_ctx)


def _ctx_attn(q, kv_ctx, sink, n_batch, n_ctx):
    nb = n_ctx // ATTN_TOKENS
    blk = (ATTN_TOKENS, 512)
    return pl.pallas_call(
        _ctx_attn_kernel,
        grid=(n_batch, nb),
        in_specs=[pl.BlockSpec(memory_space=pltpu.SMEM),
                  pl.BlockSpec(blk, lambda b, j: (b * nb + j, 0)),
                  pl.BlockSpec((n_ctx, 512), lambda b, j: (b, 0))],
        out_specs=pl.BlockSpec(blk, lambda b, j: (b * nb + j, 0)),
        out_shape=jax.ShapeDtypeStruct(q.shape, bf16),
        compiler_params=_params("arbitrary", "arbitrary"),
        name="ctx_attn",
    )(sink, q, kv_ctx)


def _mixer_kernel(attn_ref, vn_ref, mf_ref, mfp_ref, mfn_ref, x_ref, g1_ref, sh2_ref, sc2_ref, n2g_ref,
                  ws_ref, sb_ref, pw_ref, ps_ref, cw_ref, wo_ref, rw_ref, rb_ref,
                  xmid_ref, h2_ref, ti_ref, gt_ref, rk_ref, cnt_ref, *, n_seq, bt):
    i = pl.program_id(0)
    tps = n_seq // bt
    si = i % tps
    first = si == 0
    last = si == tps - 1
    n_ext = bt + 2 * HALO

    mf = mf_ref[...]
    u = mf[:, 0:256]

    def extended(lo, hi):
        prev = jnp.where(first, 0.0, mfp_ref[:, lo:hi])
        nxt = jnp.where(last, 0.0, mfn_ref[:, lo:hi])
        return jnp.concatenate([prev, mf[:, lo:hi], nxt], axis=0)

    def shifted(a, s):
        return pltpu.roll(a, s % n_ext, 0)

    xe = extended(256, 512)
    a1 = shifted(xe, 1) + xe
    a2 = shifted(a1, 1) + shifted(a1, -1)
    a3 = shifted(a2, 2) + shifted(a2, -2)
    a4 = shifted(a3, 4) + shifted(a3, -4)
    lane = lax.broadcasted_iota(i32, (bt, POOL_CH), 1)
    grp = lane >> 6
    sl = slice(HALO, HALO + bt)
    wsum = jnp.where(grp == 0, a1[sl], jnp.where(grp == 1, a2[sl], jnp.where(grp == 2, a3[sl], a4[sl])))
    pos = lax.broadcasted_iota(i32, (bt, POOL_CH), 0) + si * bt
    halfw = jnp.left_shift(1, grp)
    cnt = jnp.minimum(pos + halfw, n_seq) - jnp.maximum(pos - halfw, 0)
    dpool = wsum / cnt.astype(f32) - mf[:, 256:512]
    yc = jnp.dot(dpool.astype(bf16), pw_ref[...], preferred_element_type=f32) * ps_ref[...]

    ye = extended(768, 1024)
    cw = cw_ref[...]
    z = shifted(ye, 1) * cw[0:1, :] + ye * cw[1:2, :] + shifted(ye, -1) * cw[2:3, :]
    yd = mf[:, 512:768] * z[sl]

    hgrp = lax.broadcasted_iota(i32, (SGU_CHUNK, SGU_W), 1) >> 6
    ybs = []
    for cidx in range(bt // SGU_CHUNK):
        rows = slice(cidx * SGU_CHUNK, (cidx + 1) * SGU_CHUNK)
        vn_c = vn_ref[rows, :]
        s = jnp.zeros((SGU_CHUNK, SGU_W), f32)
        for hh in range(SGU_HEADS):
            sh = jnp.dot(ws_ref[hh], vn_c, preferred_element_type=f32)
            s = jnp.where(hgrp == hh, sh, s)
        ybs.append(u[rows, :] * (s + sb_ref[...]))
    yb = jnp.concatenate(ybs, axis=0)

    mix = jnp.concatenate([attn_ref[...], yb.astype(bf16), yc.astype(bf16), yd.astype(bf16)], axis=1)
    mo = jnp.dot(mix, wo_ref[...], preferred_element_type=f32)
    xm = x_ref[...] + g1_ref[...] * mo
    xmid_ref[...] = xm

    y = xm * lax.rsqrt(jnp.mean(xm * xm, axis=-1, keepdims=True) + EPS) * n2g_ref[...]
    h2 = y * (1.0 + sc2_ref[...]) + sh2_ref[...]
    _store_token_tiles(h2_ref, h2)

    lt = lax.dot_general(rw_ref[...], h2, (((1,), (1,)), ((), ())),
                         precision=lax.Precision.HIGHEST, preferred_element_type=f32) + rb_ref[...]
    n_exp = lt.shape[0]
    eidx = lax.broadcasted_iota(i32, lt.shape, 0)
    work = lt
    idxs, vals = [], []
    for _ in range(TOP_K):
        m = jnp.max(work, axis=0, keepdims=True)
        idx = jnp.min(jnp.where(work == m, eidx, n_exp), axis=0, keepdims=True)
        idxs.append(idx)
        vals.append(m)
        work = jnp.where(eidx == idx, -jnp.inf, work)
    exps = [jnp.exp(v - vals[0]) for v in vals]
    den = exps[0] + exps[1] + exps[2] + exps[3]
    onehot = jnp.zeros(lt.shape, f32)
    for kk in range(TOP_K):
        ti_ref[kk:kk + 1, :] = idxs[kk]
        gt_ref[kk:kk + 1, :] = exps[kk] / den
        onehot = onehot + (eidx == idxs[kk]).astype(f32)
    tri = (lax.broadcasted_iota(i32, (bt, bt), 0) < lax.broadcasted_iota(i32, (bt, bt), 1)).astype(bf16)
    before = jnp.dot(onehot.astype(bf16), tri, preferred_element_type=f32)
    for kk in range(TOP_K):
        rk_ref[kk:kk + 1, :] = jnp.sum(jnp.where(eidx == idxs[kk], before, 0.0), axis=0, keepdims=True).astype(i32)
    cnt_ref[...] = jnp.sum(onehot, axis=1, keepdims=True)


def _mixer(attn, vn, mixf, x2, n_seq, g1, sh2, sc2, n2g, ws_bf, sgu_bias, pool_bd, pool_scale, conv_w,
           wo_bf, rw_t, rb):
    t, d = x2.shape
    bt = min(MIXER_TOKENS, n_seq)
    tps = n_seq // bt
    hb = bt // HALO
    n_halo = t // HALO
    n_exp = rw_t.shape[0]
    row = lambda i: (i, 0)
    per_batch = pl.BlockSpec((None, 1, d), lambda i: (i // tps, 0, 0))
    col = lambda i: (0, i)
    return pl.pallas_call(
        functools.partial(_mixer_kernel, n_seq=n_seq, bt=bt),
        grid=(t // bt,),
        in_specs=[pl.BlockSpec((bt, ATT_Q_W), row), pl.BlockSpec((bt, SGU_W), row), pl.BlockSpec((bt, 1024), row),
                  pl.BlockSpec((HALO, 1024), lambda i: (jnp.maximum(i * hb - 1, 0), 0)),
                  pl.BlockSpec((HALO, 1024), lambda i: (jnp.minimum((i + 1) * hb, n_halo - 1), 0)),
                  pl.BlockSpec((bt, d), row), per_batch, per_batch, per_batch, _full(n2g),
                  _full(ws_bf), _full(sgu_bias), _full(pool_bd), _full(pool_scale), _full(conv_w),
                  _full(wo_bf), _full(rw_t), _full(rb)],
        out_specs=[pl.BlockSpec((bt, d), row), pl.BlockSpec((bt * SUBLANES, LANES), row),
                   pl.BlockSpec((TOP_K, bt), col), pl.BlockSpec((TOP_K, bt), col), pl.BlockSpec((TOP_K, bt), col),
                   pl.BlockSpec((None, n_exp, 1), lambda i: (i, 0, 0))],
        out_shape=[jax.ShapeDtypeStruct((t, d), f32), jax.ShapeDtypeStruct((t * SUBLANES, LANES), f32),
                   jax.ShapeDtypeStruct((TOP_K, t), i32), jax.ShapeDtypeStruct((TOP_K, t), f32),
                   jax.ShapeDtypeStruct((TOP_K, t), i32), jax.ShapeDtypeStruct((t // bt, n_exp, 1), f32)],
        compiler_params=_params("arbitrary"),
        name="mixer_router",
    )(attn, vn, mixf, mixf, mixf, x2, g1, sh2, sc2, n2g, ws_bf, sgu_bias, pool_bd, pool_scale, conv_w,
      wo_bf, rw_t, rb)


def _stage_index_kernel(ti_ref, rk_ref, off_ref, o_ref, *, w0, n_exp):
    base = (w0 + pl.program_id(0)) * n_exp
    ti = ti_ref[...]
    acc = rk_ref[...]
    for e in range(n_exp):
        acc = acc + jnp.where(ti == e, off_ref[base + e], 0)
    o_ref[...] = acc


def _stage_index(top_i, rank, run_off, bt, w0, n_exp):
    k, t = top_i.shape
    col = lambda i: (0, i)
    return pl.pallas_call(
        functools.partial(_stage_index_kernel, w0=w0, n_exp=n_exp),
        grid=(t // bt,),
        in_specs=[pl.BlockSpec((k, bt), col), pl.BlockSpec((k, bt), col), pl.BlockSpec(memory_space=pltpu.SMEM)],
        out_specs=pl.BlockSpec((k, bt), col),
        out_shape=jax.ShapeDtypeStruct((k, t), i32),
        compiler_params=_params("arbitrary"),
        name="stage_index",
    )(top_i, rank, run_off)


def _tile_rows(t, n=1):
    return pl.ds(pl.multiple_of(t * SUBLANES, SUBLANES), n * SUBLANES)


def _tile_copy(src, s, dst, d, sem):
    return pltpu.make_async_copy(src.at[_tile_rows(s)], dst.at[_tile_rows(d)], sem)


def _for_each_run_piece(runs_ref, w, n_exp, max_len, fn):
    bits = [1 << b for b in range(max_len.bit_length() - 1, -1, -1)]

    def per_expert(e, carry):
        base = (w * n_exp + e) * 3
        stage0, slot0, length = runs_ref[base], runs_ref[base + 1], runs_ref[base + 2]
        done = 0
        for bit in bits:
            take = length & bit

            @pl.when(take != 0)
            def _():
                fn(stage0 + done, slot0 + done, bit)

            done = done + take
        return carry

    lax.fori_loop(0, n_exp, per_expert, 0)


def _chunk_major(table, bt):
    k, t = table.shape
    return table.reshape(k, t // bt, bt).transpose(1, 0, 2).reshape(-1)


def _split_slabs(w_ref, g_ref, l_ref, scr):
    f = g_ref.shape[1]
    for s in range(w_ref.shape[0] // LANES):
        rows = slice(s * LANES, (s + 1) * LANES)
        scr[...] = w_ref[rows, :].T
        g_ref[rows, :] = scr[pl.ds(0, f, stride=2), :].T.astype(bf16)
        l_ref[rows, :] = scr[pl.ds(1, f, stride=2), :].T.astype(bf16)


def _split_w1(w1, layer):
    _, n_exp, d, f2 = w1.shape
    f = f2 // 2
    rows = 2 * LANES
    blk = pl.BlockSpec((None, rows, f), lambda e, j: (e, j, 0))
    return pl.pallas_call(
        _split_slabs,
        grid=(n_exp, d // rows),
        in_specs=[pl.BlockSpec((None, None, rows, f2), lambda e, j: (layer, e, j, 0))],
        out_specs=[blk, blk],
        out_shape=[jax.ShapeDtypeStruct((n_exp, d, f), bf16)] * 2,
        scratch_shapes=[pltpu.VMEM((f2, LANES), f32)],
        compiler_params=_params("arbitrary", "arbitrary"),
        name="split_w1",
    )(w1)


class _SplitSide:
    def __init__(self, w1, layer, n_steps):
        _, n_exp, d, f2 = w1.shape
        fits = [r for r in (LANES, 2 * LANES, 4 * LANES, 8 * LANES) if d % r == 0 and n_exp * (d // r) <= n_steps]
        self.ok = bool(fits)
        if not self.ok:
            return
        rows = fits[0]
        per = d // rows
        self.w1 = w1
        self.n_side = n_exp * per
        last = self.n_side - 1
        self.in_spec = pl.BlockSpec((None, None, rows, f2),
                                    lambda i: (layer, jnp.minimum(i, last) // per, jnp.minimum(i, last) % per, 0))
        self.out_spec = pl.BlockSpec((None, rows, f2 // 2),
                                     lambda i: (jnp.minimum(i, last) // per, jnp.minimum(i, last) % per, 0))
        self.out_shape = jax.ShapeDtypeStruct((n_exp, d, f2 // 2), bf16)
        self.scratch = pltpu.VMEM((f2, LANES), f32)

    def run(self, w_ref, g_ref, l_ref, scr):
        @pl.when(pl.program_id(0) < self.n_side)
        def _():
            _split_slabs(w_ref, g_ref, l_ref, scr)


def _dispatch_kernel(sidx_ref, runs_ref, pad_ref, h_ref, *rest, bt, w0, n_exp, fill_pad, side):
    rest = list(rest)
    if not fill_pad:
        rest.pop(0)
    w_ref = rest.pop(0) if side else None
    xs_ref = rest.pop(0)
    g_ref, l_ref = (rest.pop(0), rest.pop(0)) if side else (None, None)
    zero_ref = rest.pop(0) if fill_pad else None
    stage, sem = rest.pop(0), rest.pop(0)
    i = pl.program_id(0)
    slot = i % 2

    if fill_pad:
        @pl.when(i == 0)
        def _():
            zero_ref[...] = jnp.zeros_like(zero_ref)

            def per_expert(e, carry):
                first, count = pad_ref[e], pad_ref[n_exp + e]

                def issue(r, c):
                    _tile_copy(zero_ref, 0, xs_ref, first + r, sem.at[0]).start()
                    return c

                def drain(r, c):
                    _tile_copy(zero_ref, 0, xs_ref, 0, sem.at[0]).wait()
                    return c

                lax.fori_loop(0, count, issue, 0)
                lax.fori_loop(0, count, drain, 0)
                return carry

            lax.fori_loop(0, n_exp, per_expert, 0)

    def run_copy(buf):
        def make(stage_tile, slot_tile, n):
            return pltpu.make_async_copy(stage.at[buf, _tile_rows(stage_tile, n)], xs_ref.at[_tile_rows(slot_tile, n)],
                                         sem.at[buf])
        return make

    def place(t, carry):
        tile = h_ref[_tile_rows(t), :]
        for kk in range(TOP_K):
            stage[slot, _tile_rows(sidx_ref[kk * bt + t]), :] = tile
        return carry

    lax.fori_loop(0, bt, place, 0, unroll=ROW_UNROLL)
    _for_each_run_piece(runs_ref, w0 + i, n_exp, bt, lambda a, b, n: run_copy(slot)(a, b, n).start())
    if side:
        side.run(w_ref, g_ref, l_ref, rest.pop(0))

    @pl.when(i > 0)
    def _():
        _for_each_run_piece(runs_ref, w0 + i - 1, n_exp, bt, lambda a, b, n: run_copy(1 - slot)(a, b, n).wait())

    @pl.when(i == pl.num_programs(0) - 1)
    def _():
        _for_each_run_piece(runs_ref, w0 + i, n_exp, bt, lambda a, b, n: run_copy(slot)(a, b, n).wait())


def _dispatch(sidx, runs, pad, h2, xs, n_slots, bt, w0, n_exp, w1_side=None):
    t = h2.shape[0] // SUBLANES
    n_steps = t // bt
    fill_pad = xs is None
    side = _SplitSide(*w1_side, n_steps) if w1_side else None
    side = side if side is not None and side.ok else None
    smem = pl.BlockSpec(memory_space=pltpu.SMEM)
    in_specs = [pl.BlockSpec((TOP_K * bt,), lambda i: (i,), memory_space=pltpu.SMEM), smem, smem,
                pl.BlockSpec((bt * SUBLANES, LANES), lambda i: (i, 0))]
    args = [_chunk_major(sidx, bt), runs, pad, h2]
    out_specs = [pl.BlockSpec(memory_space=pl.ANY)]
    out_shape = [jax.ShapeDtypeStruct((n_slots * SUBLANES, LANES), f32)]
    scratch = [pltpu.VMEM((2, TOP_K * bt * SUBLANES, LANES), f32), pltpu.SemaphoreType.DMA((2,))]
    if fill_pad:
        scratch = [pltpu.VMEM((SUBLANES, LANES), f32)] + scratch
    else:
        in_specs.append(pl.BlockSpec(memory_space=pl.ANY))
        args.append(xs)
    if side:
        in_specs.append(side.in_spec)
        args.append(side.w1)
        out_specs += [side.out_spec] * 2
        out_shape += [side.out_shape] * 2
        scratch.append(side.scratch)
    outs = pl.pallas_call(
        functools.partial(_dispatch_kernel, bt=bt, w0=w0, n_exp=n_exp, fill_pad=fill_pad, side=side),
        grid=(n_steps,),
        in_specs=in_specs,
        out_specs=out_specs,
        out_shape=out_shape,
        scratch_shapes=scratch,
        input_output_aliases={} if fill_pad else {4: 0},
        compiler_params=_params("arbitrary"),
        name="dispatch",
    )(*args)
    return outs[0], (tuple(outs[1:]) if side else None)


def _expert_kernel(be_ref, nu_ref, x_ref, w1g_ref, w1l_ref, b1g_ref, b1l_ref, w2f_ref, b2_ref, y_ref, w2_ref):
    i = pl.program_id(0)
    active = i < nu_ref[0]
    new_expert = jnp.logical_or(i == 0, be_ref[i] != be_ref[jnp.maximum(i - 1, 0)])

    @pl.when(jnp.logical_and(active, new_expert))
    def _():
        w2_ref[...] = w2f_ref[...].astype(bf16)

    @pl.when(active)
    def _():
        bm = x_ref.shape[0] // SUBLANES
        x = _load_token_tiles(x_ref, bm).astype(bf16)
        zg = jnp.dot(x, w1g_ref[...], preferred_element_type=f32) + b1g_ref[...]
        zl = jnp.dot(x, w1l_ref[...], preferred_element_type=f32) + b1l_ref[...]
        g = jnp.minimum(zg, SWIGLU_LIMIT)
        lin = jnp.clip(zl, -SWIGLU_LIMIT, SWIGLU_LIMIT)
        a = g * jax.nn.sigmoid(SWIGLU_ALPHA * g) * (lin + 1.0)
        y = jnp.dot(a.astype(bf16), w2_ref[...], preferred_element_type=f32) + b2_ref[...]
        _store_token_tiles(y_ref, y)

    @pl.when(i >= nu_ref[0])
    def _():
        y_ref[...] = jnp.zeros_like(y_ref)


def _experts(block_e, n_used, xs, w1g, w1l, b1g, b1l, w2_all, layer, b2):
    n_exp, d, f = w1g.shape
    bm = EXPERT_ROWS
    blk = (bm * SUBLANES, LANES)
    xrow = lambda i, be, nu: (jnp.minimum(i, nu[0] - 1), 0)
    wsel = lambda i, be, nu: (be[i], 0, 0)
    grid_spec = pltpu.PrefetchScalarGridSpec(
        num_scalar_prefetch=2,
        grid=(xs.shape[0] // blk[0],),
        in_specs=[pl.BlockSpec(blk, xrow),
                  pl.BlockSpec((None, d, f), wsel), pl.BlockSpec((None, d, f), wsel),
                  pl.BlockSpec((None, 1, f), wsel), pl.BlockSpec((None, 1, f), wsel),
                  pl.BlockSpec((None, None, f, d), lambda i, be, nu: (layer, be[i], 0, 0)),
                  pl.BlockSpec((None, 1, d), wsel)],
        out_specs=pl.BlockSpec(blk, lambda i, be, nu: (i, 0)),
        scratch_shapes=[pltpu.VMEM((f, d), bf16)],
    )
    return pl.pallas_call(
        _expert_kernel,
        grid_spec=grid_spec,
        out_shape=jax.ShapeDtypeStruct(xs.shape, f32),
        compiler_params=_params("arbitrary"),
        name="experts",
    )(block_e, n_used, xs, w1g, w1l, b1g, b1l, w2_all, b2)


def _combine_kernel(sidx_ref, gate_ref, runs_ref, ys_ref, x_ref, g2_ref, fg_ref, *rest, bt, w0, n_exp, final, side):
    rest = list(rest)
    w_ref = rest.pop(0) if side else None
    o_ref = rest.pop(0)
    g_ref, l_ref = (rest.pop(0), rest.pop(0)) if side else (None, None)
    stage, acc_ref, sem = rest.pop(0), rest.pop(0), rest.pop(0)
    i = pl.program_id(0)
    slot = i % 2

    def run_copy(buf):
        def make(stage_tile, slot_tile, n):
            return pltpu.make_async_copy(ys_ref.at[_tile_rows(slot_tile, n)], stage.at[buf, _tile_rows(stage_tile, n)],
                                         sem.at[buf])
        return make

    def fetch(w, buf):
        _for_each_run_piece(runs_ref, w, n_exp, bt, lambda a, b, n: run_copy(buf)(a, b, n).start())

    @pl.when(i == 0)
    def _():
        fetch(w0, 0)

    @pl.when(i + 1 < pl.num_programs(0))
    def _():
        fetch(w0 + i + 1, 1 - slot)

    if side:
        side.run(w_ref, g_ref, l_ref, rest.pop(0))

    _for_each_run_piece(runs_ref, w0 + i, n_exp, bt, lambda a, b, n: run_copy(slot)(a, b, n).wait())

    def token(t, carry):
        acc = gate_ref[t] * stage[slot, _tile_rows(sidx_ref[t]), :]
        for kk in range(1, TOP_K):
            acc = acc + gate_ref[kk * bt + t] * stage[slot, _tile_rows(sidx_ref[kk * bt + t]), :]
        acc_ref[_tile_rows(t), :] = acc
        return carry

    lax.fori_loop(0, bt, token, 0, unroll=ROW_UNROLL)
    xo = x_ref[...] + g2_ref[...] * _load_token_tiles(acc_ref, bt)
    if final:
        xo = xo * lax.rsqrt(jnp.mean(xo * xo, axis=-1, keepdims=True) + EPS) * fg_ref[...]
    o_ref[...] = xo


def _combine(sidx, gates, runs, ys, x_mid, n_seq, g2, final_g, final, bt, w0, n_exp, w1_side=None):
    t, d = x_mid.shape
    tps = n_seq // bt
    row = lambda i: (i, 0)
    n_steps = t // bt
    side = _SplitSide(*w1_side, n_steps) if w1_side else None
    side = side if side is not None and side.ok else None
    per_step = pl.BlockSpec((TOP_K * bt,), lambda i: (i,), memory_space=pltpu.SMEM)
    in_specs = [per_step, per_step, pl.BlockSpec(memory_space=pltpu.SMEM),
                pl.BlockSpec(memory_space=pl.ANY),
                pl.BlockSpec((bt, d), row),
                pl.BlockSpec((None, 1, d), lambda i: (i // tps, 0, 0)),
                _full(final_g)]
    args = [_chunk_major(sidx, bt), _chunk_major(gates, bt), runs, ys, x_mid, g2, final_g]
    out_specs = [pl.BlockSpec((bt, d), row)]
    out_shape = [jax.ShapeDtypeStruct((t, d), f32)]
    scratch = [pltpu.VMEM((2, TOP_K * bt * SUBLANES, LANES), f32), pltpu.VMEM((bt * SUBLANES, LANES), f32),
               pltpu.SemaphoreType.DMA((2,))]
    if side:
        in_specs.append(side.in_spec)
        args.append(side.w1)
        out_specs += [side.out_spec] * 2
        out_shape += [side.out_shape] * 2
        scratch.append(side.scratch)
    outs = pl.pallas_call(
        functools.partial(_combine_kernel, bt=bt, w0=w0, n_exp=n_exp, final=final, side=side),
        grid=(n_steps,),
        in_specs=in_specs,
        out_specs=out_specs,
        out_shape=out_shape,
        scratch_shapes=scratch,
        compiler_params=_params("arbitrary"),
        name="combine",
    )(*args)
    return outs[0], (tuple(outs[1:]) if side else None)


def _rope_tables(n_tokens):
    rows = n_tokens // GRID_W
    row = jnp.repeat(jnp.arange(rows), GRID_W).astype(f32)
    col = jnp.tile(jnp.arange(GRID_W), rows).astype(f32)
    n_freq = HEAD_DIM // 4
    inv = ROPE_BASE ** (-jnp.arange(n_freq, dtype=f32) / n_freq)
    ang = jnp.concatenate([row[:, None] * inv, col[:, None] * inv], axis=-1)
    cos, sin = jnp.cos(ang), jnp.sin(ang)
    return jnp.tile(cos, (1, 4)), jnp.concatenate([-sin, sin, -sin, sin], axis=-1)


def kernel(x, c, ctx, c_ctx, norm1_g, norm2_g, ada_w, ada_b, w_in, attn_sink, sgu_ws, sgu_b, sgu_ln_g, sgu_ln_b,
           pool_w, pool_scale, conv_w, w_out, router_w, router_b, exp_w1, exp_b1, exp_w2, exp_b2, final_g):
    n_batch, n_seq, d = x.shape
    n_ctx = ctx.shape[1]
    depth = ada_w.shape[0]
    n_exp = router_w.shape[2]
    assert d == SUBLANES * LANES, "token rows are moved as single (8, 128) tiles"
    t_lat, t_ctx = n_batch * n_seq, n_batch * n_ctx
    bm = EXPERT_ROWS

    cvec = jnp.concatenate([c, c_ctx[None, :], jnp.zeros((SUBLANES - n_batch - 1, d), f32)], axis=0)
    mods = _ada(cvec, ada_w, ada_b)

    cos_l, sin_l = _rope_tables(n_seq)
    cos_c, sin_c = jnp.ones((n_ctx, LANES), f32), jnp.zeros((n_ctx, LANES), f32)

    xl = x.reshape(t_lat, d)
    xc = ctx.reshape(t_ctx, d)
    row2 = lambda a: a.reshape(1, -1)
    w1_split = {}

    for l in range(depth):
        last = l == depth - 1
        ml = mods[l, :n_batch].reshape(n_batch, 6, 1, d)
        mc = jnp.broadcast_to(mods[l, n_batch].reshape(1, 6, 1, d), (n_batch, 6, 1, d))
        sh1l, sc1l, g1l, sh2l, sc2l, g2l = (ml[:, i] for i in range(6))
        sh1c, sc1c, g1c, sh2c, sc2c, g2c = (mc[:, i] for i in range(6))

        w_in_bf = w_in[l].astype(bf16)
        wo_bf = w_out[l].astype(bf16)
        ws_bf = sgu_ws[l].astype(bf16)
        sgu_bias = jnp.repeat(sgu_b[l].T, SGU_W // SGU_HEADS, axis=1)
        pool_bd = jax.scipy.linalg.block_diag(*[pool_w[l, g] for g in range(pool_w.shape[1])]).astype(bf16)
        n1g, n2g = row2(norm1_g[l]), row2(norm2_g[l])
        lng, lnb, psc = row2(sgu_ln_g[l]), row2(sgu_ln_b[l]), row2(pool_scale[l])
        rw_t = router_w[l].T
        rb = router_b[l].reshape(n_exp, 1)
        sink = attn_sink[l]
        mix_w = (ws_bf, sgu_bias, pool_bd, psc, conv_w[l], wo_bf, rw_t, rb)

        qc, kvc, vnc, mfc = _inproj(xc, n_ctx, sh1c, sc1c, n1g, w_in_bf, cos_c, sin_c, lng, lnb)
        ql, kvl, vnl, mfl = _inproj(xl, n_seq, sh1l, sc1l, n1g, w_in_bf, cos_l, sin_l, lng, lnb)
        attn_l = _window_attn(ql, kvl, kvc, sink, n_batch, n_seq, n_ctx)
        bt_l, bt_c = min(MIXER_TOKENS, n_seq), min(MIXER_TOKENS, n_ctx)
        xmid_l, h2_l, ti_l, gt_l, rk_l, cnt = _mixer(attn_l, vnl, mfl, xl, n_seq, g1l, sh2l, sc2l, n2g, *mix_w)
        n_win_l = cnt.shape[0]
        t_all = t_lat
        if not last:
            attn_c = _ctx_attn(qc, kvc, sink, n_batch, n_ctx)
            xmid_c, h2_c, ti_c, gt_c, rk_c, cnt_c = _mixer(attn_c, vnc, mfc, xc, n_ctx, g1c, sh2c, sc2c, n2g, *mix_w)
            cnt = jnp.concatenate([cnt, cnt_c], axis=0)
            t_all = t_lat + t_ctx

        win_cnt = cnt[:, :, 0].astype(i32)
        counts = jnp.sum(win_cnt, axis=0)
        padded = (counts + bm - 1) // bm * bm
        pends = jnp.cumsum(padded)
        pstarts = pends - padded
        n_blocks = -(-(t_all * TOP_K) // bm) + n_exp
        starts = jnp.arange(n_blocks, dtype=i32) * bm
        block_e = jnp.minimum(jnp.sum((pends[None, :] <= starts[:, None]).astype(i32), axis=1), n_exp - 1)
        n_used = (pends[-1:] // bm).astype(i32)
        pad = jnp.concatenate([pstarts + counts, padded - counts]).astype(i32)
        run_stage = jnp.cumsum(win_cnt, axis=1) - win_cnt
        run_slot = pstarts[None, :] + jnp.cumsum(win_cnt, axis=0) - win_cnt
        runs = jnp.stack([run_stage, run_slot, win_cnt], axis=-1).reshape(-1)
        run_off = run_stage.reshape(-1)
        sidx_l = _stage_index(ti_l, rk_l, run_off, bt_l, 0, n_exp)
        if not last:
            sidx_c = _stage_index(ti_c, rk_c, run_off, bt_c, n_win_l, n_exp)

        xs, split = _dispatch(sidx_l, runs, pad, h2_l, None, n_blocks * bm, bt_l, 0, n_exp,
                              None if l in w1_split else (exp_w1, l))
        if l not in w1_split:
            w1_split[l] = split or _split_w1(exp_w1, l)
        if not last:
            xs, _ = _dispatch(sidx_c, runs, pad, h2_c, xs, n_blocks * bm, bt_c, n_win_l, n_exp)

        w1g, w1l = w1_split[l]
        b1 = exp_b1[l]
        b1g, b1l = b1[:, None, 0::2], b1[:, None, 1::2]
        ys = _experts(block_e, n_used, xs, w1g, w1l, b1g, b1l, exp_w2, l, exp_b2[l][:, None, :])

        fg = row2(final_g)
        xl, split = _combine(sidx_l, gt_l, runs, ys, xmid_l, n_seq, g2l, fg, last, bt_l, 0, n_exp,
                             None if last else (exp_w1, l + 1))
        if split:
            w1_split[l + 1] = split
        if not last:
            xc, _ = _combine(sidx_c, gt_c, runs, ys, xmid_c, n_ctx, g2c, fg, False, bt_c, n_win_l, n_exp)

    return xl.reshape(n_batch, n_seq, d)
```

```python
import functools

import jax
import jax.numpy as jnp
from jax import lax
from jax.experimental import pallas as pl
from jax.experimental.pallas import tpu as pltpu

f32 = jnp.float32
bf16 = jnp.bfloat16
i32 = jnp.int32

GRID_W = 64
EPS = 1e-6
N_Q_HEADS = 8
HEAD_DIM = 64
WINDOW = 128
ROPE_BASE = 10000.0
ATT_Q_W = 512
ATT_KV_W = 128
SGU_HEADS = 4
SGU_W = 256
SGU_CHUNK = 128
POOL_CH = 256
CONV_CH = 256
MIX_WIDTH = 1280
TOP_K = 4
SWIGLU_LIMIT = 7.0
SWIGLU_ALPHA = 1.702
SQRT_HALF = 0.7071067811865476

LANES = 128
SUBLANES = 8
VMEM_LIMIT_BYTES = 56 * 1024 * 1024

INPROJ_TOKENS = 512
ATTN_TOKENS = 128
MIXER_TOKENS = 512
EXPERT_ROWS = 512
HALO = 8
ROW_UNROLL = 4


def _params(*sem):
    return pltpu.CompilerParams(dimension_semantics=sem, vmem_limit_bytes=VMEM_LIMIT_BYTES)


def _full(a):
    nd = a.ndim
    return pl.BlockSpec(a.shape, lambda *_: (0,) * nd)


def _store_token_tiles(ref, val):
    n = val.shape[0]
    for s in range(SUBLANES):
        ref[pl.ds(s, n, stride=SUBLANES), :] = val[:, s * LANES:(s + 1) * LANES]


def _load_token_tiles(ref, n):
    return jnp.concatenate([ref[pl.ds(s, n, stride=SUBLANES), :] for s in range(SUBLANES)], axis=1)


def _gelu(x):
    return 0.5 * x * (1.0 + lax.erf(x * SQRT_HALF))


def _ada_kernel(c_ref, w_ref, b_ref, o_ref):
    c = c_ref[...]
    s = c * jax.nn.sigmoid(c)
    o_ref[...] = jnp.dot(s, w_ref[...], precision=lax.Precision.HIGHEST, preferred_element_type=f32) + b_ref[...]


def _ada(cvec, ada_w, ada_b):
    depth, d, n = ada_w.shape
    tn = 1536
    return pl.pallas_call(
        _ada_kernel,
        grid=(depth, n // tn),
        in_specs=[pl.BlockSpec(cvec.shape, lambda l, j: (0, 0)),
                  pl.BlockSpec((None, d, tn), lambda l, j: (l, 0, j)),
                  pl.BlockSpec((None, 1, tn), lambda l, j: (l, 0, j))],
        out_specs=pl.BlockSpec((None, cvec.shape[0], tn), lambda l, j: (l, 0, j)),
        out_shape=jax.ShapeDtypeStruct((depth, cvec.shape[0], n), f32),
        compiler_params=_params("arbitrary", "arbitrary"),
        name="ada_mod",
    )(cvec, ada_w, ada_b.reshape(depth, 1, n))


def _inproj_kernel(x_ref, sh_ref, sc_ref, g_ref, w_ref, cos_ref, sin_ref, lng_ref, lnb_ref,
                   q_ref, kv_ref, vn_ref, mixf_ref):
    x = x_ref[...]
    y = x * lax.rsqrt(jnp.mean(x * x, axis=-1, keepdims=True) + EPS) * g_ref[...]
    h = y * (1.0 + sc_ref[...]) + sh_ref[...]
    p = jnp.dot(h.astype(bf16), w_ref[...], preferred_element_type=f32)

    cos = cos_ref[...]
    sin = sin_ref[...]
    lane = lax.broadcasted_iota(i32, cos.shape, 1)
    first_half = (lane & (HEAD_DIM - 1)) < HEAD_DIM // 2

    def rope(t):
        partner = jnp.where(first_half, pltpu.roll(t, LANES - HEAD_DIM // 2, 1), pltpu.roll(t, HEAD_DIM // 2, 1))
        return t * cos + partner * sin

    scale = HEAD_DIM ** -0.5
    for m in range(ATT_Q_W // LANES):
        q_ref[:, m * LANES:(m + 1) * LANES] = (rope(p[:, m * LANES:(m + 1) * LANES]) * scale).astype(bf16)
    k = rope(p[:, 512:640])
    v = p[:, 640:768]
    kv_ref[:, 0:128] = k.astype(bf16)
    kv_ref[:, 128:256] = pltpu.roll(k, HEAD_DIM, 1).astype(bf16)
    kv_ref[:, 256:384] = v.astype(bf16)
    kv_ref[:, 384:512] = pltpu.roll(v, HEAD_DIM, 1).astype(bf16)

    u = _gelu(p[:, 768:1024])
    gv = _gelu(p[:, 1024:1280])
    mu = jnp.mean(gv, axis=-1, keepdims=True)
    var = jnp.mean(jnp.square(gv - mu), axis=-1, keepdims=True)
    vn_ref[...] = ((gv - mu) * lax.rsqrt(var + EPS) * lng_ref[...] + lnb_ref[...]).astype(bf16)

    mixf_ref[:, 0:256] = u
    mixf_ref[:, 256:512] = p[:, 1280:1536]
    mixf_ref[:, 512:768] = p[:, 1536:1792]
    mixf_ref[:, 768:1024] = p[:, 1792:2048] * p[:, 2048:2304]


def _inproj(x2, n_seq, shift, scale, g, w_bf, cos_t, sin_t, ln_g, ln_b):
    t, d = x2.shape
    bt = min(INPROJ_TOKENS, n_seq)
    tps = n_seq // bt
    ncol = w_bf.shape[1]
    row = lambda i: (i, 0)
    return pl.pallas_call(
        _inproj_kernel,
        grid=(t // bt,),
        in_specs=[pl.BlockSpec((bt, d), row),
                  pl.BlockSpec((None, 1, d), lambda i: (i // tps, 0, 0)),
                  pl.BlockSpec((None, 1, d), lambda i: (i // tps, 0, 0)),
                  _full(g),
                  pl.BlockSpec((d, ncol), lambda i: (0, 0)),
                  pl.BlockSpec((bt, LANES), lambda i: (i % tps, 0)),
                  pl.BlockSpec((bt, LANES), lambda i: (i % tps, 0)),
                  _full(ln_g), _full(ln_b)],
        out_specs=[pl.BlockSpec((bt, ATT_Q_W), row), pl.BlockSpec((bt, 512), row),
                   pl.BlockSpec((bt, SGU_W), row), pl.BlockSpec((bt, 1024), row)],
        out_shape=[jax.ShapeDtypeStruct((t, ATT_Q_W), bf16), jax.ShapeDtypeStruct((t, 512), bf16),
                   jax.ShapeDtypeStruct((t, SGU_W), bf16), jax.ShapeDtypeStruct((t, 1024), f32)],
        compiler_params=_params("arbitrary"),
        name="inproj",
    )(x2, shift, scale, g, w_bf, cos_t, sin_t, ln_g, ln_b)


def _attn_block(sink_ref, q, kv, bias, o_ref, row0):
    k_nat, k_swp, v_nat, v_swp = (kv[:, i * LANES:(i + 1) * LANES] for i in range(4))
    nq = q.shape[0]
    low = lax.broadcasted_iota(i32, (nq, LANES), 1) < HEAD_DIM
    top = lax.broadcasted_iota(i32, (2 * nq, 1), 0) < nq
    zero = jnp.zeros((nq, LANES), q.dtype)
    for kvh in range(2):
        chunks = [q[:, (2 * kvh + i) * LANES:(2 * kvh + i + 1) * LANES] for i in range(2)]
        outs = []
        for half in range(2):
            keep = low if half == 0 else jnp.logical_not(low)
            qz = jnp.concatenate([jnp.where(keep, c, zero) for c in chunks], axis=0)
            kh = k_nat if kvh == half else k_swp
            vh = v_nat if kvh == half else v_swp
            s = lax.dot_general(qz, kh, (((1,), (1,)), ((), ())), preferred_element_type=f32)
            if bias is not None:
                s = s + bias
            h0 = 4 * kvh + half
            sk = jnp.where(top, sink_ref[h0], sink_ref[h0 + 2])
            mx = jnp.maximum(jnp.max(s, axis=1, keepdims=True), sk)
            e = jnp.exp(s - mx)
            den = jnp.sum(e, axis=1, keepdims=True) + jnp.exp(sk - mx)
            outs.append(jnp.dot(e.astype(bf16), vh, preferred_element_type=f32) / den)
        for i in range(2):
            rows = slice(i * nq, (i + 1) * nq)
            m = 2 * kvh + i
            o_ref[row0:row0 + nq, m * LANES:(m + 1) * LANES] = jnp.where(low, outs[0][rows], outs[1][rows]).astype(bf16)


def _window_attn_kernel(sink_ref, q_ref, kvp_ref, kvm_ref, kvn_ref, kvx_ref, o_ref, *, nb):
    jj = pl.program_id(1)
    nq = ATTN_TOKENS
    nband = 3 * nq
    kvm = kvm_ref[...]
    kvx = kvx_ref[...]
    r = lax.broadcasted_iota(i32, (nq, nband), 0)
    c = lax.broadcasted_iota(i32, (nq, nband), 1)
    dlt = c - r
    in_window = (dlt >= 0) & (dlt <= 2 * WINDOW)
    ctx_zeros = jnp.zeros((nq, kvx.shape[0]), f32)
    for i, kv_band in enumerate((jnp.concatenate([kvp_ref[...], kvm], axis=0),
                                 jnp.concatenate([kvm, kvn_ref[...]], axis=0))):
        j = 2 * jj + i
        lo = jnp.where(j == 0, nq, 0)
        hi = jnp.where(j == nb - 1, 2 * nq, nband)
        valid = in_window & (c >= lo) & (c < hi)
        bias = jnp.concatenate([jnp.where(valid, 0.0, -jnp.inf).astype(f32), ctx_zeros], axis=1)
        _attn_block(sink_ref, q_ref[i * nq:(i + 1) * nq, :], jnp.concatenate([kv_band, kvx], axis=0),
                    jnp.concatenate([bias, bias], axis=0), o_ref, i * nq)


def _ctx_attn_kernel(sink_ref, q_ref, kvx_ref, o_ref):
    _attn_block(sink_ref, q_ref[...], kvx_ref[...], None, o_ref, 0)


def _window_attn(q, kv, kv_ctx, sink, n_batch, n_seq, n_ctx):
    t = q.shape[0]
    nb = n_seq // ATTN_TOKENS
    assert nb % 2 == 0
    nb2 = nb // 2
    one, two = (ATTN_TOKENS, 512), (2 * ATTN_TOKENS, 512)
    return pl.pallas_call(
        functools.partial(_window_attn_kernel, nb=nb),
        grid=(n_batch, nb2),
        in_specs=[pl.BlockSpec(memory_space=pltpu.SMEM),
                  pl.BlockSpec(two, lambda b, j: (b * nb2 + j, 0)),
                  pl.BlockSpec(one, lambda b, j: (b * nb + jnp.maximum(2 * j - 1, 0), 0)),
                  pl.BlockSpec(two, lambda b, j: (b * nb2 + j, 0)),
                  pl.BlockSpec(one, lambda b, j: (b * nb + jnp.minimum(2 * j + 2, nb - 1), 0)),
                  pl.BlockSpec((n_ctx, 512), lambda b, j: (b, 0))],
        out_specs=pl.BlockSpec(two, lambda b, j: (b * nb2 + j, 0)),
        out_shape=jax.ShapeDtypeStruct((t, ATT_Q_W), bf16),
        compiler_params=_params("arbitrary", "arbitrary"),
        name="window_attn",
    )(sink, q, kv, kv, kv, kv_ctx)


def _ctx_attn(q, kv_ctx, sink, n_batch, n_ctx):
    nb = n_ctx // ATTN_TOKENS
    blk = (ATTN_TOKENS, 512)
    return pl.pallas_call(
        _ctx_attn_kernel,
        grid=(n_batch, nb),
        in_specs=[pl.BlockSpec(memory_space=pltpu.SMEM),
                  pl.BlockSpec(blk, lambda b, j: (b * nb + j, 0)),
                  pl.BlockSpec((n_ctx, 512), lambda b, j: (b, 0))],
        out_specs=pl.BlockSpec(blk, lambda b, j: (b * nb + j, 0)),
        out_shape=jax.ShapeDtypeStruct(q.shape, bf16),
        compiler_params=_params("arbitrary", "arbitrary"),
        name="ctx_attn",
    )(sink, q, kv_ctx)


def _mixer_kernel(attn_ref, vn_ref, mf_ref, mfp_ref, mfn_ref, x_ref, g1_ref, sh2_ref, sc2_ref, n2g_ref,
                  ws_ref, sb_ref, pw_ref, ps_ref, cw_ref, wo_ref, rw_ref, rb_ref,
                  xmid_ref, h2_ref, ti_ref, gt_ref, rk_ref, cnt_ref, *, n_seq, bt):
    i = pl.program_id(0)
    tps = n_seq // bt
    si = i % tps
    first = si == 0
    last = si == tps - 1
    n_ext = bt + 2 * HALO

    mf = mf_ref[...]
    u = mf[:, 0:256]

    def extended(lo, hi):
        prev = jnp.where(first, 0.0, mfp_ref[:, lo:hi])
        nxt = jnp.where(last, 0.0, mfn_ref[:, lo:hi])
        return jnp.concatenate([prev, mf[:, lo:hi], nxt], axis=0)

    def shifted(a, s):
        return pltpu.roll(a, s % n_ext, 0)

    xe = extended(256, 512)
    a1 = shifted(xe, 1) + xe
    a2 = shifted(a1, 1) + shifted(a1, -1)
    a3 = shifted(a2, 2) + shifted(a2, -2)
    a4 = shifted(a3, 4) + shifted(a3, -4)
    lane = lax.broadcasted_iota(i32, (bt, POOL_CH), 1)
    grp = lane >> 6
    sl = slice(HALO, HALO + bt)
    wsum = jnp.where(grp == 0, a1[sl], jnp.where(grp == 1, a2[sl], jnp.where(grp == 2, a3[sl], a4[sl])))
    pos = lax.broadcasted_iota(i32, (bt, POOL_CH), 0) + si * bt
    halfw = jnp.left_shift(1, grp)
    cnt = jnp.minimum(pos + halfw, n_seq) - jnp.maximum(pos - halfw, 0)
    dpool = wsum / cnt.astype(f32) - mf[:, 256:512]
    yc = jnp.dot(dpool.astype(bf16), pw_ref[...], preferred_element_type=f32) * ps_ref[...]

    ye = extended(768, 1024)
    cw = cw_ref[...]
    z = shifted(ye, 1) * cw[0:1, :] + ye * cw[1:2, :] + shifted(ye, -1) * cw[2:3, :]
    yd = mf[:, 512:768] * z[sl]

    hgrp = lax.broadcasted_iota(i32, (SGU_CHUNK, SGU_W), 1) >> 6
    ybs = []
    for cidx in range(bt // SGU_CHUNK):
        rows = slice(cidx * SGU_CHUNK, (cidx + 1) * SGU_CHUNK)
        vn_c = vn_ref[rows, :]
        s = jnp.zeros((SGU_CHUNK, SGU_W), f32)
        for hh in range(SGU_HEADS):
            sh = jnp.dot(ws_ref[hh], vn_c, preferred_element_type=f32)
            s = jnp.where(hgrp == hh, sh, s)
        ybs.append(u[rows, :] * (s + sb_ref[...]))
    yb = jnp.concatenate(ybs, axis=0)

    mix = jnp.concatenate([attn_ref[...], yb.astype(bf16), yc.astype(bf16), yd.astype(bf16)], axis=1)
    mo = jnp.dot(mix, wo_ref[...], preferred_element_type=f32)
    xm = x_ref[...] + g1_ref[...] * mo
    xmid_ref[...] = xm

    y = xm * lax.rsqrt(jnp.mean(xm * xm, axis=-1, keepdims=True) + EPS) * n2g_ref[...]
    h2 = y * (1.0 + sc2_ref[...]) + sh2_ref[...]
    _store_token_tiles(h2_ref, h2)

    lt = lax.dot_general(rw_ref[...], h2, (((1,), (1,)), ((), ())),
                         precision=lax.Precision.HIGHEST, preferred_element_type=f32) + rb_ref[...]
    n_exp = lt.shape[0]
    eidx = lax.broadcasted_iota(i32, lt.shape, 0)
    work = lt
    idxs, vals = [], []
    for _ in range(TOP_K):
        m = jnp.max(work, axis=0, keepdims=True)
        idx = jnp.min(jnp.where(work == m, eidx, n_exp), axis=0, keepdims=True)
        idxs.append(idx)
        vals.append(m)
        work = jnp.where(eidx == idx, -jnp.inf, work)
    exps = [jnp.exp(v - vals[0]) for v in vals]
    den = exps[0] + exps[1] + exps[2] + exps[3]
    onehot = jnp.zeros(lt.shape, f32)
    for kk in range(TOP_K):
        ti_ref[kk:kk + 1, :] = idxs[kk]
        gt_ref[kk:kk + 1, :] = exps[kk] / den
        onehot = onehot + (eidx == idxs[kk]).astype(f32)
    tri = (lax.broadcasted_iota(i32, (bt, bt), 0) < lax.broadcasted_iota(i32, (bt, bt), 1)).astype(bf16)
    before = jnp.dot(onehot.astype(bf16), tri, preferred_element_type=f32)
    for kk in range(TOP_K):
        rk_ref[kk:kk + 1, :] = jnp.sum(jnp.where(eidx == idxs[kk], before, 0.0), axis=0, keepdims=True).astype(i32)
    cnt_ref[...] = jnp.sum(onehot, axis=1, keepdims=True)


def _mixer(attn, vn, mixf, x2, n_seq, g1, sh2, sc2, n2g, ws_bf, sgu_bias, pool_bd, pool_scale, conv_w,
           wo_bf, rw_t, rb):
    t, d = x2.shape
    bt = min(MIXER_TOKENS, n_seq)
    tps = n_seq // bt
    hb = bt // HALO
    n_halo = t // HALO
    n_exp = rw_t.shape[0]
    row = lambda i: (i, 0)
    per_batch = pl.BlockSpec((None, 1, d), lambda i: (i // tps, 0, 0))
    col = lambda i: (0, i)
    return pl.pallas_call(
        functools.partial(_mixer_kernel, n_seq=n_seq, bt=bt),
        grid=(t // bt,),
        in_specs=[pl.BlockSpec((bt, ATT_Q_W), row), pl.BlockSpec((bt, SGU_W), row), pl.BlockSpec((bt, 1024), row),
                  pl.BlockSpec((HALO, 1024), lambda i: (jnp.maximum(i * hb - 1, 0), 0)),
                  pl.BlockSpec((HALO, 1024), lambda i: (jnp.minimum((i + 1) * hb, n_halo - 1), 0)),
                  pl.BlockSpec((bt, d), row), per_batch, per_batch, per_batch, _full(n2g),
                  _full(ws_bf), _full(sgu_bias), _full(pool_bd), _full(pool_scale), _full(conv_w),
                  _full(wo_bf), _full(rw_t), _full(rb)],
        out_specs=[pl.BlockSpec((bt, d), row), pl.BlockSpec((bt * SUBLANES, LANES), row),
                   pl.BlockSpec((TOP_K, bt), col), pl.BlockSpec((TOP_K, bt), col), pl.BlockSpec((TOP_K, bt), col),
                   pl.BlockSpec((None, n_exp, 1), lambda i: (i, 0, 0))],
        out_shape=[jax.ShapeDtypeStruct((t, d), f32), jax.ShapeDtypeStruct((t * SUBLANES, LANES), f32),
                   jax.ShapeDtypeStruct((TOP_K, t), i32), jax.ShapeDtypeStruct((TOP_K, t), f32),
                   jax.ShapeDtypeStruct((TOP_K, t), i32), jax.ShapeDtypeStruct((t // bt, n_exp, 1), f32)],
        compiler_params=_params("arbitrary"),
        name="mixer_router",
    )(attn, vn, mixf, mixf, mixf, x2, g1, sh2, sc2, n2g, ws_bf, sgu_bias, pool_bd, pool_scale, conv_w,
      wo_bf, rw_t, rb)


def _stage_index_kernel(ti_ref, rk_ref, off_ref, o_ref, *, w0, n_exp):
    base = (w0 + pl.program_id(0)) * n_exp
    ti = ti_ref[...]
    acc = rk_ref[...]
    for e in range(n_exp):
        acc = acc + jnp.where(ti == e, off_ref[base + e], 0)
    o_ref[...] = acc * SUBLANES


def _stage_index(top_i, rank, run_off, bt, w0, n_exp):
    k, t = top_i.shape
    col = lambda i: (0, i)
    return pl.pallas_call(
        functools.partial(_stage_index_kernel, w0=w0, n_exp=n_exp),
        grid=(t // bt,),
        in_specs=[pl.BlockSpec((k, bt), col), pl.BlockSpec((k, bt), col), pl.BlockSpec(memory_space=pltpu.SMEM)],
        out_specs=pl.BlockSpec((k, bt), col),
        out_shape=jax.ShapeDtypeStruct((k, t), i32),
        compiler_params=_params("arbitrary"),
        name="stage_index",
    )(top_i, rank, run_off)


def _tile_rows(t, n=1):
    return pl.ds(pl.multiple_of(t * SUBLANES, SUBLANES), n * SUBLANES)


def _tile_copy(src, s, dst, d, sem):
    return pltpu.make_async_copy(src.at[_tile_rows(s)], dst.at[_tile_rows(d)], sem)


def _for_each_run_piece(runs_ref, w, n_exp, max_len, fn):
    bits = [1 << b for b in range(max_len.bit_length() - 1, -1, -1)]

    def per_expert(e, carry):
        base = (w * n_exp + e) * 3
        stage0, slot0, length = runs_ref[base], runs_ref[base + 1], runs_ref[base + 2]
        done = 0
        for bit in bits:
            take = length & bit

            @pl.when(take != 0)
            def _():
                fn(stage0 + done, slot0 + done, bit)

            done = done + take
        return carry

    lax.fori_loop(0, n_exp, per_expert, 0)


def _token_major(table):
    return table.T.reshape(-1)


def _per_buffer(slot, fn):
    for b in range(2):
        @pl.when(slot == b)
        def _():
            fn(b)


def _split_slabs(w_ref, g_ref, l_ref, scr):
    f = g_ref.shape[1]
    for s in range(w_ref.shape[0] // LANES):
        rows = slice(s * LANES, (s + 1) * LANES)
        scr[...] = w_ref[rows, :].T
        g_ref[rows, :] = scr[pl.ds(0, f, stride=2), :].T.astype(bf16)
        l_ref[rows, :] = scr[pl.ds(1, f, stride=2), :].T.astype(bf16)


def _split_w1(w1, layer):
    _, n_exp, d, f2 = w1.shape
    f = f2 // 2
    rows = 2 * LANES
    blk = pl.BlockSpec((None, rows, f), lambda e, j: (e, j, 0))
    return pl.pallas_call(
        _split_slabs,
        grid=(n_exp, d // rows),
        in_specs=[pl.BlockSpec((None, None, rows, f2), lambda e, j: (layer, e, j, 0))],
        out_specs=[blk, blk],
        out_shape=[jax.ShapeDtypeStruct((n_exp, d, f), bf16)] * 2,
        scratch_shapes=[pltpu.VMEM((f2, LANES), f32)],
        compiler_params=_params("arbitrary", "arbitrary"),
        name="split_w1",
    )(w1)


class _SplitSide:
    def __init__(self, w1, layer, n_steps):
        _, n_exp, d, f2 = w1.shape
        fits = [r for r in (LANES, 2 * LANES, 4 * LANES, 8 * LANES) if d % r == 0 and n_exp * (d // r) <= n_steps]
        self.ok = bool(fits)
        if not self.ok:
            return
        rows = fits[0]
        per = d // rows
        self.w1 = w1
        self.n_side = n_exp * per
        last = self.n_side - 1
        self.in_spec = pl.BlockSpec((None, None, rows, f2),
                                    lambda i: (layer, jnp.minimum(i, last) // per, jnp.minimum(i, last) % per, 0))
        self.out_spec = pl.BlockSpec((None, rows, f2 // 2),
                                     lambda i: (jnp.minimum(i, last) // per, jnp.minimum(i, last) % per, 0))
        self.out_shape = jax.ShapeDtypeStruct((n_exp, d, f2 // 2), bf16)
        self.scratch = pltpu.VMEM((f2, LANES), f32)

    def run(self, w_ref, g_ref, l_ref, scr):
        @pl.when(pl.program_id(0) < self.n_side)
        def _():
            _split_slabs(w_ref, g_ref, l_ref, scr)


def _dispatch_kernel(sidx_ref, runs_ref, pad_ref, h_ref, *rest, bt, w0, n_exp, fill_pad, side):
    rest = list(rest)
    if not fill_pad:
        rest.pop(0)
    w_ref = rest.pop(0) if side else None
    xs_ref = rest.pop(0)
    g_ref, l_ref = (rest.pop(0), rest.pop(0)) if side else (None, None)
    zero_ref = rest.pop(0) if fill_pad else None
    stage, sem = rest.pop(0), rest.pop(0)
    i = pl.program_id(0)
    slot = i % 2

    if fill_pad:
        @pl.when(i == 0)
        def _():
            zero_ref[...] = jnp.zeros_like(zero_ref)

            def per_expert(e, carry):
                first, count = pad_ref[e], pad_ref[n_exp + e]

                def issue(r, c):
                    _tile_copy(zero_ref, 0, xs_ref, first + r, sem.at[0]).start()
                    return c

                def drain(r, c):
                    _tile_copy(zero_ref, 0, xs_ref, 0, sem.at[0]).wait()
                    return c

                lax.fori_loop(0, count, issue, 0)
                lax.fori_loop(0, count, drain, 0)
                return carry

            lax.fori_loop(0, n_exp, per_expert, 0)

    def run_copy(buf):
        def make(stage_tile, slot_tile, n):
            return pltpu.make_async_copy(stage.at[buf, _tile_rows(stage_tile, n)], xs_ref.at[_tile_rows(slot_tile, n)],
                                         sem.at[buf])
        return make

    def place_all(b):
        def place(t, carry):
            tile = h_ref[_tile_rows(t), :]
            for kk in range(TOP_K):
                row = pl.multiple_of(sidx_ref[t * TOP_K + kk], SUBLANES)
                stage[b, pl.ds(row, SUBLANES), :] = tile
            return carry

        lax.fori_loop(0, bt, place, 0, unroll=ROW_UNROLL)

    _per_buffer(slot, place_all)
    _for_each_run_piece(runs_ref, w0 + i, n_exp, bt, lambda a, b, n: run_copy(slot)(a, b, n).start())
    if side:
        side.run(w_ref, g_ref, l_ref, rest.pop(0))

    @pl.when(i > 0)
    def _():
        _for_each_run_piece(runs_ref, w0 + i - 1, n_exp, bt, lambda a, b, n: run_copy(1 - slot)(a, b, n).wait())

    @pl.when(i == pl.num_programs(0) - 1)
    def _():
        _for_each_run_piece(runs_ref, w0 + i, n_exp, bt, lambda a, b, n: run_copy(slot)(a, b, n).wait())


def _dispatch(sidx, runs, pad, h2, xs, n_slots, bt, w0, n_exp, w1_side=None):
    t = h2.shape[0] // SUBLANES
    n_steps = t // bt
    fill_pad = xs is None
    side = _SplitSide(*w1_side, n_steps) if w1_side else None
    side = side if side is not None and side.ok else None
    smem = pl.BlockSpec(memory_space=pltpu.SMEM)
    in_specs = [pl.BlockSpec((TOP_K * bt,), lambda i: (i,), memory_space=pltpu.SMEM), smem, smem,
                pl.BlockSpec((bt * SUBLANES, LANES), lambda i: (i, 0))]
    args = [_token_major(sidx), runs, pad, h2]
    out_specs = [pl.BlockSpec(memory_space=pl.ANY)]
    out_shape = [jax.ShapeDtypeStruct((n_slots * SUBLANES, LANES), f32)]
    scratch = [pltpu.VMEM((2, TOP_K * bt * SUBLANES, LANES), f32), pltpu.SemaphoreType.DMA((2,))]
    if fill_pad:
        scratch = [pltpu.VMEM((SUBLANES, LANES), f32)] + scratch
    else:
        in_specs.append(pl.BlockSpec(memory_space=pl.ANY))
        args.append(xs)
    if side:
        in_specs.append(side.in_spec)
        args.append(side.w1)
        out_specs += [side.out_spec] * 2
        out_shape += [side.out_shape] * 2
        scratch.append(side.scratch)
    outs = pl.pallas_call(
        functools.partial(_dispatch_kernel, bt=bt, w0=w0, n_exp=n_exp, fill_pad=fill_pad, side=side),
        grid=(n_steps,),
        in_specs=in_specs,
        out_specs=out_specs,
        out_shape=out_shape,
        scratch_shapes=scratch,
        input_output_aliases={} if fill_pad else {4: 0},
        compiler_params=_params("arbitrary"),
        name="dispatch",
    )(*args)
    return outs[0], (tuple(outs[1:]) if side else None)


def _expert_kernel(be_ref, nu_ref, x_ref, w1g_ref, w1l_ref, b1g_ref, b1l_ref, w2f_ref, b2_ref, y_ref, w2_ref):
    i = pl.program_id(0)
    active = i < nu_ref[0]
    new_expert = jnp.logical_or(i == 0, be_ref[i] != be_ref[jnp.maximum(i - 1, 0)])

    @pl.when(jnp.logical_and(active, new_expert))
    def _():
        w2_ref[...] = w2f_ref[...].astype(bf16)

    @pl.when(active)
    def _():
        bm = x_ref.shape[0] // SUBLANES
        x = _load_token_tiles(x_ref, bm).astype(bf16)
        zg = jnp.dot(x, w1g_ref[...], preferred_element_type=f32) + b1g_ref[...]
        zl = jnp.dot(x, w1l_ref[...], preferred_element_type=f32) + b1l_ref[...]
        g = jnp.minimum(zg, SWIGLU_LIMIT)
        lin = jnp.clip(zl, -SWIGLU_LIMIT, SWIGLU_LIMIT)
        a = g * jax.nn.sigmoid(SWIGLU_ALPHA * g) * (lin + 1.0)
        y = jnp.dot(a.astype(bf16), w2_ref[...], preferred_element_type=f32) + b2_ref[...]
        _store_token_tiles(y_ref, y)

    @pl.when(i >= nu_ref[0])
    def _():
        y_ref[...] = jnp.zeros_like(y_ref)


def _experts(block_e, n_used, xs, w1g, w1l, b1g, b1l, w2_all, layer, b2):
    n_exp, d, f = w1g.shape
    bm = EXPERT_ROWS
    blk = (bm * SUBLANES, LANES)
    xrow = lambda i, be, nu: (jnp.minimum(i, nu[0] - 1), 0)
    wsel = lambda i, be, nu: (be[i], 0, 0)
    grid_spec = pltpu.PrefetchScalarGridSpec(
        num_scalar_prefetch=2,
        grid=(xs.shape[0] // blk[0],),
        in_specs=[pl.BlockSpec(blk, xrow),
                  pl.BlockSpec((None, d, f), wsel), pl.BlockSpec((None, d, f), wsel),
                  pl.BlockSpec((None, 1, f), wsel), pl.BlockSpec((None, 1, f), wsel),
                  pl.BlockSpec((None, None, f, d), lambda i, be, nu: (layer, be[i], 0, 0)),
                  pl.BlockSpec((None, 1, d), wsel)],
        out_specs=pl.BlockSpec(blk, lambda i, be, nu: (i, 0)),
        scratch_shapes=[pltpu.VMEM((f, d), bf16)],
    )
    return pl.pallas_call(
        _expert_kernel,
        grid_spec=grid_spec,
        out_shape=jax.ShapeDtypeStruct(xs.shape, f32),
        compiler_params=_params("arbitrary"),
        name="experts",
    )(block_e, n_used, xs, w1g, w1l, b1g, b1l, w2_all, b2)


def _combine_kernel(sidx_ref, gate_ref, runs_ref, ys_ref, x_ref, g2_ref, fg_ref, *rest, bt, w0, n_exp, final, side):
    rest = list(rest)
    w_ref = rest.pop(0) if side else None
    o_ref = rest.pop(0)
    g_ref, l_ref = (rest.pop(0), rest.pop(0)) if side else (None, None)
    stage, acc_ref, sem = rest.pop(0), rest.pop(0), rest.pop(0)
    i = pl.program_id(0)
    slot = i % 2

    def run_copy(buf):
        def make(stage_tile, slot_tile, n):
            return pltpu.make_async_copy(ys_ref.at[_tile_rows(slot_tile, n)], stage.at[buf, _tile_rows(stage_tile, n)],
                                         sem.at[buf])
        return make

    def fetch(w, buf):
        _for_each_run_piece(runs_ref, w, n_exp, bt, lambda a, b, n: run_copy(buf)(a, b, n).start())

    @pl.when(i == 0)
    def _():
        fetch(w0, 0)

    @pl.when(i + 1 < pl.num_programs(0))
    def _():
        fetch(w0 + i + 1, 1 - slot)

    if side:
        side.run(w_ref, g_ref, l_ref, rest.pop(0))

    _for_each_run_piece(runs_ref, w0 + i, n_exp, bt, lambda a, b, n: run_copy(slot)(a, b, n).wait())

    def sum_all(b):
        def token(t, carry):
            acc = None
            for kk in range(TOP_K):
                row = pl.multiple_of(sidx_ref[t * TOP_K + kk], SUBLANES)
                term = gate_ref[t * TOP_K + kk] * stage[b, pl.ds(row, SUBLANES), :]
                acc = term if acc is None else acc + term
            acc_ref[_tile_rows(t), :] = acc
            return carry

        lax.fori_loop(0, bt, token, 0, unroll=ROW_UNROLL)

    _per_buffer(slot, sum_all)
    xo = x_ref[...] + g2_ref[...] * _load_token_tiles(acc_ref, bt)
    if final:
        xo = xo * lax.rsqrt(jnp.mean(xo * xo, axis=-1, keepdims=True) + EPS) * fg_ref[...]
    o_ref[...] = xo


def _combine(sidx, gates, runs, ys, x_mid, n_seq, g2, final_g, final, bt, w0, n_exp, w1_side=None):
    t, d = x_mid.shape
    tps = n_seq // bt
    row = lambda i: (i, 0)
    n_steps = t // bt
    side = _SplitSide(*w1_side, n_steps) if w1_side else None
    side = side if side is not None and side.ok else None
    per_step = pl.BlockSpec((TOP_K * bt,), lambda i: (i,), memory_space=pltpu.SMEM)
    in_specs = [per_step, per_step, pl.BlockSpec(memory_space=pltpu.SMEM),
                pl.BlockSpec(memory_space=pl.ANY),
                pl.BlockSpec((bt, d), row),
                pl.BlockSpec((None, 1, d), lambda i: (i // tps, 0, 0)),
                _full(final_g)]
    args = [_token_major(sidx), _token_major(gates), runs, ys, x_mid, g2, final_g]
    out_specs = [pl.BlockSpec((bt, d), row)]
    out_shape = [jax.ShapeDtypeStruct((t, d), f32)]
    scratch = [pltpu.VMEM((2, TOP_K * bt * SUBLANES, LANES), f32), pltpu.VMEM((bt * SUBLANES, LANES), f32),
               pltpu.SemaphoreType.DMA((2,))]
    if side:
        in_specs.append(side.in_spec)
        args.append(side.w1)
        out_specs += [side.out_spec] * 2
        out_shape += [side.out_shape] * 2
        scratch.append(side.scratch)
    outs = pl.pallas_call(
        functools.partial(_combine_kernel, bt=bt, w0=w0, n_exp=n_exp, final=final, side=side),
        grid=(n_steps,),
        in_specs=in_specs,
        out_specs=out_specs,
        out_shape=out_shape,
        scratch_shapes=scratch,
        compiler_params=_params("arbitrary"),
        name="combine",
    )(*args)
    return outs[0], (tuple(outs[1:]) if side else None)


def _rope_tables(n_tokens):
    rows = n_tokens // GRID_W
    row = jnp.repeat(jnp.arange(rows), GRID_W).astype(f32)
    col = jnp.tile(jnp.arange(GRID_W), rows).astype(f32)
    n_freq = HEAD_DIM // 4
    inv = ROPE_BASE ** (-jnp.arange(n_freq, dtype=f32) / n_freq)
    ang = jnp.concatenate([row[:, None] * inv, col[:, None] * inv], axis=-1)
    cos, sin = jnp.cos(ang), jnp.sin(ang)
    return jnp.tile(cos, (1, 4)), jnp.concatenate([-sin, sin, -sin, sin], axis=-1)


def kernel(x, c, ctx, c_ctx, norm1_g, norm2_g, ada_w, ada_b, w_in, attn_sink, sgu_ws, sgu_b, sgu_ln_g, sgu_ln_b,
           pool_w, pool_scale, conv_w, w_out, router_w, router_b, exp_w1, exp_b1, exp_w2, exp_b2, final_g):
    n_batch, n_seq, d = x.shape
    n_ctx = ctx.shape[1]
    depth = ada_w.shape[0]
    n_exp = router_w.shape[2]
    assert d == SUBLANES * LANES, "token rows are moved as single (8, 128) tiles"
    t_lat, t_ctx = n_batch * n_seq, n_batch * n_ctx
    bm = EXPERT_ROWS

    cvec = jnp.concatenate([c, c_ctx[None, :], jnp.zeros((SUBLANES - n_batch - 1, d), f32)], axis=0)
    mods = _ada(cvec, ada_w, ada_b)

    cos_l, sin_l = _rope_tables(n_seq)
    cos_c, sin_c = jnp.ones((n_ctx, LANES), f32), jnp.zeros((n_ctx, LANES), f32)

    xl = x.reshape(t_lat, d)
    xc = ctx.reshape(t_ctx, d)
    row2 = lambda a: a.reshape(1, -1)
    w1_split = {}

    for l in range(depth):
        last = l == depth - 1
        ml = mods[l, :n_batch].reshape(n_batch, 6, 1, d)
        mc = jnp.broadcast_to(mods[l, n_batch].reshape(1, 6, 1, d), (n_batch, 6, 1, d))
        sh1l, sc1l, g1l, sh2l, sc2l, g2l = (ml[:, i] for i in range(6))
        sh1c, sc1c, g1c, sh2c, sc2c, g2c = (mc[:, i] for i in range(6))

        w_in_bf = w_in[l].astype(bf16)
        wo_bf = w_out[l].astype(bf16)
        ws_bf = sgu_ws[l].astype(bf16)
        sgu_bias = jnp.repeat(sgu_b[l].T, SGU_W // SGU_HEADS, axis=1)
        pool_bd = jax.scipy.linalg.block_diag(*[pool_w[l, g] for g in range(pool_w.shape[1])]).astype(bf16)
        n1g, n2g = row2(norm1_g[l]), row2(norm2_g[l])
        lng, lnb, psc = row2(sgu_ln_g[l]), row2(sgu_ln_b[l]), row2(pool_scale[l])
        rw_t = router_w[l].T
        rb = router_b[l].reshape(n_exp, 1)
        sink = attn_sink[l]
        mix_w = (ws_bf, sgu_bias, pool_bd, psc, conv_w[l], wo_bf, rw_t, rb)

        qc, kvc, vnc, mfc = _inproj(xc, n_ctx, sh1c, sc1c, n1g, w_in_bf, cos_c, sin_c, lng, lnb)
        ql, kvl, vnl, mfl = _inproj(xl, n_seq, sh1l, sc1l, n1g, w_in_bf, cos_l, sin_l, lng, lnb)
        attn_l = _window_attn(ql, kvl, kvc, sink, n_batch, n_seq, n_ctx)
        bt_l, bt_c = min(MIXER_TOKENS, n_seq), min(MIXER_TOKENS, n_ctx)
        xmid_l, h2_l, ti_l, gt_l, rk_l, cnt = _mixer(attn_l, vnl, mfl, xl, n_seq, g1l, sh2l, sc2l, n2g, *mix_w)
        n_win_l = cnt.shape[0]
        t_all = t_lat
        if not last:
            attn_c = _ctx_attn(qc, kvc, sink, n_batch, n_ctx)
            xmid_c, h2_c, ti_c, gt_c, rk_c, cnt_c = _mixer(attn_c, vnc, mfc, xc, n_ctx, g1c, sh2c, sc2c, n2g, *mix_w)
            cnt = jnp.concatenate([cnt, cnt_c], axis=0)
            t_all = t_lat + t_ctx

        win_cnt = cnt[:, :, 0].astype(i32)
        counts = jnp.sum(win_cnt, axis=0)
        padded = (counts + bm - 1) // bm * bm
        pends = jnp.cumsum(padded)
        pstarts = pends - padded
        n_blocks = -(-(t_all * TOP_K) // bm) + n_exp
        starts = jnp.arange(n_blocks, dtype=i32) * bm
        block_e = jnp.minimum(jnp.sum((pends[None, :] <= starts[:, None]).astype(i32), axis=1), n_exp - 1)
        n_used = (pends[-1:] // bm).astype(i32)
        pad = jnp.concatenate([pstarts + counts, padded - counts]).astype(i32)
        run_stage = jnp.cumsum(win_cnt, axis=1) - win_cnt
        run_slot = pstarts[None, :] + jnp.cumsum(win_cnt, axis=0) - win_cnt
        runs = jnp.stack([run_stage, run_slot, win_cnt], axis=-1).reshape(-1)
        run_off = run_stage.reshape(-1)
        sidx_l = _stage_index(ti_l, rk_l, run_off, bt_l, 0, n_exp)
        if not last:
            sidx_c = _stage_index(ti_c, rk_c, run_off, bt_c, n_win_l, n_exp)

        xs, split = _dispatch(sidx_l, runs, pad, h2_l, None, n_blocks * bm, bt_l, 0, n_exp,
                              None if l in w1_split else (exp_w1, l))
        if l not in w1_split:
            w1_split[l] = split or _split_w1(exp_w1, l)
        if not last:
            xs, _ = _dispatch(sidx_c, runs, pad, h2_c, xs, n_blocks * bm, bt_c, n_win_l, n_exp)

        w1g, w1l = w1_split[l]
        b1 = exp_b1[l]
        b1g, b1l = b1[:, None, 0::2], b1[:, None, 1::2]
        ys = _experts(block_e, n_used, xs, w1g, w1l, b1g, b1l, exp_w2, l, exp_b2[l][:, None, :])

        fg = row2(final_g)
        xl, split = _combine(sidx_l, gt_l, runs, ys, xmid_l, n_seq, g2l, fg, last, bt_l, 0, n_exp,
                             None if last else (exp_w1, l + 1))
        if split:
            w1_split[l + 1] = split
        if not last:
            xc, _ = _combine(sidx_c, gt_c, runs, ys, xmid_c, n_ctx, g2c, fg, False, bt_c, n_win_l, n_exp)

    return xl.reshape(n_batch, n_seq, d)
```

```python
import functools

import jax
import jax.numpy as jnp
from jax import lax
from jax.experimental import pallas as pl
from jax.experimental.pallas import tpu as pltpu

f32 = jnp.float32
bf16 = jnp.bfloat16
i32 = jnp.int32

GRID_W = 64
EPS = 1e-6
N_Q_HEADS = 8
HEAD_DIM = 64
WINDOW = 128
ROPE_BASE = 10000.0
ATT_Q_W = 512
ATT_KV_W = 128
SGU_HEADS = 4
SGU_W = 256
SGU_CHUNK = 128
POOL_CH = 256
CONV_CH = 256
MIX_WIDTH = 1280
TOP_K = 4
SWIGLU_LIMIT = 7.0
SWIGLU_ALPHA = 1.702
SQRT_HALF = 0.7071067811865476

LANES = 128
SUBLANES = 8
VMEM_LIMIT_BYTES = 56 * 1024 * 1024

INPROJ_TOKENS = 512
ATTN_TOKENS = 128
MIXER_TOKENS = 512
EXPERT_ROWS = 512
HALO = 8
ROW_UNROLL = 4


def _params(*sem):
    return pltpu.CompilerParams(dimension_semantics=sem, vmem_limit_bytes=VMEM_LIMIT_BYTES)


def _full(a):
    nd = a.ndim
    return pl.BlockSpec(a.shape, lambda *_: (0,) * nd)


def _store_token_tiles(ref, val):
    n = val.shape[0]
    for s in range(SUBLANES):
        ref[pl.ds(s, n, stride=SUBLANES), :] = val[:, s * LANES:(s + 1) * LANES]


def _load_token_tiles(ref, n):
    return jnp.concatenate([ref[pl.ds(s, n, stride=SUBLANES), :] for s in range(SUBLANES)], axis=1)


def _gelu(x):
    return 0.5 * x * (1.0 + lax.erf(x * SQRT_HALF))


def _ada_kernel(c_ref, w_ref, b_ref, o_ref):
    c = c_ref[...]
    s = c * jax.nn.sigmoid(c)
    o_ref[...] = jnp.dot(s, w_ref[...], precision=lax.Precision.HIGHEST, preferred_element_type=f32) + b_ref[...]


def _ada(cvec, ada_w, ada_b):
    depth, d, n = ada_w.shape
    tn = 1536
    return pl.pallas_call(
        _ada_kernel,
        grid=(depth, n // tn),
        in_specs=[pl.BlockSpec(cvec.shape, lambda l, j: (0, 0)),
                  pl.BlockSpec((None, d, tn), lambda l, j: (l, 0, j)),
                  pl.BlockSpec((None, 1, tn), lambda l, j: (l, 0, j))],
        out_specs=pl.BlockSpec((None, cvec.shape[0], tn), lambda l, j: (l, 0, j)),
        out_shape=jax.ShapeDtypeStruct((depth, cvec.shape[0], n), f32),
        compiler_params=_params("arbitrary", "arbitrary"),
        name="ada_mod",
    )(cvec, ada_w, ada_b.reshape(depth, 1, n))


def _inproj_kernel(x_ref, sh_ref, sc_ref, g_ref, w_ref, cos_ref, sin_ref, lng_ref, lnb_ref,
                   q_ref, kv_ref, vn_ref, mixf_ref):
    x = x_ref[...]
    y = x * lax.rsqrt(jnp.mean(x * x, axis=-1, keepdims=True) + EPS) * g_ref[...]
    h = y * (1.0 + sc_ref[...]) + sh_ref[...]
    p = jnp.dot(h.astype(bf16), w_ref[...], preferred_element_type=f32)

    cos = cos_ref[...]
    sin = sin_ref[...]
    lane = lax.broadcasted_iota(i32, cos.shape, 1)
    first_half = (lane & (HEAD_DIM - 1)) < HEAD_DIM // 2

    def rope(t):
        partner = jnp.where(first_half, pltpu.roll(t, LANES - HEAD_DIM // 2, 1), pltpu.roll(t, HEAD_DIM // 2, 1))
        return t * cos + partner * sin

    scale = HEAD_DIM ** -0.5
    for m in range(ATT_Q_W // LANES):
        q_ref[:, m * LANES:(m + 1) * LANES] = (rope(p[:, m * LANES:(m + 1) * LANES]) * scale).astype(bf16)
    k = rope(p[:, 512:640])
    v = p[:, 640:768]
    kv_ref[:, 0:128] = k.astype(bf16)
    kv_ref[:, 128:256] = pltpu.roll(k, HEAD_DIM, 1).astype(bf16)
    kv_ref[:, 256:384] = v.astype(bf16)
    kv_ref[:, 384:512] = pltpu.roll(v, HEAD_DIM, 1).astype(bf16)

    u = _gelu(p[:, 768:1024])
    gv = _gelu(p[:, 1024:1280])
    mu = jnp.mean(gv, axis=-1, keepdims=True)
    var = jnp.mean(jnp.square(gv - mu), axis=-1, keepdims=True)
    vn_ref[...] = ((gv - mu) * lax.rsqrt(var + EPS) * lng_ref[...] + lnb_ref[...]).astype(bf16)

    mixf_ref[:, 0:256] = u
    mixf_ref[:, 256:512] = p[:, 1280:1536]
    mixf_ref[:, 512:768] = p[:, 1536:1792]
    mixf_ref[:, 768:1024] = p[:, 1792:2048] * p[:, 2048:2304]


def _inproj(x2, n_seq, shift, scale, g, w_bf, cos_t, sin_t, ln_g, ln_b):
    t, d = x2.shape
    bt = min(INPROJ_TOKENS, n_seq)
    tps = n_seq // bt
    ncol = w_bf.shape[1]
    row = lambda i: (i, 0)
    return pl.pallas_call(
        _inproj_kernel,
        grid=(t // bt,),
        in_specs=[pl.BlockSpec((bt, d), row),
                  pl.BlockSpec((None, 1, d), lambda i: (i // tps, 0, 0)),
                  pl.BlockSpec((None, 1, d), lambda i: (i // tps, 0, 0)),
                  _full(g),
                  pl.BlockSpec((d, ncol), lambda i: (0, 0)),
                  pl.BlockSpec((bt, LANES), lambda i: (i % tps, 0)),
                  pl.BlockSpec((bt, LANES), lambda i: (i % tps, 0)),
                  _full(ln_g), _full(ln_b)],
        out_specs=[pl.BlockSpec((bt, ATT_Q_W), row), pl.BlockSpec((bt, 512), row),
                   pl.BlockSpec((bt, SGU_W), row), pl.BlockSpec((bt, 1024), row)],
        out_shape=[jax.ShapeDtypeStruct((t, ATT_Q_W), bf16), jax.ShapeDtypeStruct((t, 512), bf16),
                   jax.ShapeDtypeStruct((t, SGU_W), bf16), jax.ShapeDtypeStruct((t, 1024), f32)],
        compiler_params=_params("arbitrary"),
        name="inproj",
    )(x2, shift, scale, g, w_bf, cos_t, sin_t, ln_g, ln_b)


def _attn_block(sink_ref, q, kv, bias, o_ref, row0):
    k_nat, k_swp, v_nat, v_swp = (kv[:, i * LANES:(i + 1) * LANES] for i in range(4))
    nq = q.shape[0]
    low = lax.broadcasted_iota(i32, (nq, LANES), 1) < HEAD_DIM
    top = lax.broadcasted_iota(i32, (2 * nq, 1), 0) < nq
    zero = jnp.zeros((nq, LANES), q.dtype)
    for kvh in range(2):
        chunks = [q[:, (2 * kvh + i) * LANES:(2 * kvh + i + 1) * LANES] for i in range(2)]
        outs = []
        for half in range(2):
            keep = low if half == 0 else jnp.logical_not(low)
            qz = jnp.concatenate([jnp.where(keep, c, zero) for c in chunks], axis=0)
            kh = k_nat if kvh == half else k_swp
            vh = v_nat if kvh == half else v_swp
            s = lax.dot_general(qz, kh, (((1,), (1,)), ((), ())), preferred_element_type=f32)
            if bias is not None:
                s = s + bias
            h0 = 4 * kvh + half
            sk = jnp.where(top, sink_ref[h0], sink_ref[h0 + 2])
            mx = jnp.maximum(jnp.max(s, axis=1, keepdims=True), sk)
            e = jnp.exp(s - mx)
            den = jnp.sum(e, axis=1, keepdims=True) + jnp.exp(sk - mx)
            outs.append(jnp.dot(e.astype(bf16), vh, preferred_element_type=f32) / den)
        for i in range(2):
            rows = slice(i * nq, (i + 1) * nq)
            m = 2 * kvh + i
            o_ref[row0:row0 + nq, m * LANES:(m + 1) * LANES] = jnp.where(low, outs[0][rows], outs[1][rows]).astype(bf16)


def _window_attn_kernel(sink_ref, q_ref, kvp_ref, kvm_ref, kvn_ref, kvx_ref, o_ref, *, nb):
    jj = pl.program_id(1)
    nq = ATTN_TOKENS
    nband = 3 * nq
    kvm = kvm_ref[...]
    kvx = kvx_ref[...]
    r = lax.broadcasted_iota(i32, (nq, nband), 0)
    c = lax.broadcasted_iota(i32, (nq, nband), 1)
    dlt = c - r
    in_window = (dlt >= 0) & (dlt <= 2 * WINDOW)
    ctx_zeros = jnp.zeros((nq, kvx.shape[0]), f32)
    for i, kv_band in enumerate((jnp.concatenate([kvp_ref[...], kvm], axis=0),
                                 jnp.concatenate([kvm, kvn_ref[...]], axis=0))):
        j = 2 * jj + i
        lo = jnp.where(j == 0, nq, 0)
        hi = jnp.where(j == nb - 1, 2 * nq, nband)
        valid = in_window & (c >= lo) & (c < hi)
        bias = jnp.concatenate([jnp.where(valid, 0.0, -jnp.inf).astype(f32), ctx_zeros], axis=1)
        _attn_block(sink_ref, q_ref[i * nq:(i + 1) * nq, :], jnp.concatenate([kv_band, kvx], axis=0),
                    jnp.concatenate([bias, bias], axis=0), o_ref, i * nq)


def _ctx_attn_kernel(sink_ref, q_ref, kvx_ref, o_ref):
    _attn_block(sink_ref, q_ref[...], kvx_ref[...], None, o_ref, 0)


def _window_attn(q, kv, kv_ctx, sink, n_batch, n_seq, n_ctx):
    t = q.shape[0]
    nb = n_seq // ATTN_TOKENS
    assert nb % 2 == 0
    nb2 = nb // 2
    one, two = (ATTN_TOKENS, 512), (2 * ATTN_TOKENS, 512)
    return pl.pallas_call(
        functools.partial(_window_attn_kernel, nb=nb),
        grid=(n_batch, nb2),
        in_specs=[pl.BlockSpec(memory_space=pltpu.SMEM),
                  pl.BlockSpec(two, lambda b, j: (b * nb2 + j, 0)),
                  pl.BlockSpec(one, lambda b, j: (b * nb + jnp.maximum(2 * j - 1, 0), 0)),
                  pl.BlockSpec(two, lambda b, j: (b * nb2 + j, 0)),
                  pl.BlockSpec(one, lambda b, j: (b * nb + jnp.minimum(2 * j + 2, nb - 1), 0)),
                  pl.BlockSpec((n_ctx, 512), lambda b, j: (b, 0))],
        out_specs=pl.BlockSpec(two, lambda b, j: (b * nb2 + j, 0)),
        out_shape=jax.ShapeDtypeStruct((t, ATT_Q_W), bf16),
        compiler_params=_params("arbitrary", "arbitrary"),
        name="window_attn",
    )(sink, q, kv, kv, kv, kv_ctx)


def _ctx_attn(q, kv_ctx, sink, n_batch, n_ctx):
    nb = n_ctx // ATTN_TOKENS
    blk = (ATTN_TOKENS, 512)
    return pl.pallas_call(
        _ctx_attn_kernel,
        grid=(n_batch, nb),
        in_specs=[pl.BlockSpec(memory_space=pltpu.SMEM),
                  pl.BlockSpec(blk, lambda b, j: (b * nb + j, 0)),
                  pl.BlockSpec((n_ctx, 512), lambda b, j: (b, 0))],
        out_specs=pl.BlockSpec(blk, lambda b, j: (b * nb + j, 0)),
        out_shape=jax.ShapeDtypeStruct(q.shape, bf16),
        compiler_params=_params("arbitrary", "arbitrary"),
        name="ctx_attn",
    )(sink, q, kv_ctx)


def _mixer_kernel(attn_ref, vn_ref, mf_ref, mfp_ref, mfn_ref, x_ref, g1_ref, sh2_ref, sc2_ref, n2g_ref,
                  ws_ref, sb_ref, pw_ref, ps_ref, cw_ref, wo_ref, rw_ref, rb_ref,
                  xmid_ref, h2_ref, ti_ref, gt_ref, rk_ref, cnt_ref, *, n_seq, bt):
    i = pl.program_id(0)
    tps = n_seq // bt
    si = i % tps
    first = si == 0
    last = si == tps - 1
    n_ext = bt + 2 * HALO

    mf = mf_ref[...]
    u = mf[:, 0:256]

    def extended(lo, hi):
        prev = jnp.where(first, 0.0, mfp_ref[:, lo:hi])
        nxt = jnp.where(last, 0.0, mfn_ref[:, lo:hi])
        return jnp.concatenate([prev, mf[:, lo:hi], nxt], axis=0)

    def shifted(a, s):
        return pltpu.roll(a, s % n_ext, 0)

    xe = extended(256, 512)
    a1 = shifted(xe, 1) + xe
    a2 = shifted(a1, 1) + shifted(a1, -1)
    a3 = shifted(a2, 2) + shifted(a2, -2)
    a4 = shifted(a3, 4) + shifted(a3, -4)
    lane = lax.broadcasted_iota(i32, (bt, POOL_CH), 1)
    grp = lane >> 6
    sl = slice(HALO, HALO + bt)
    wsum = jnp.where(grp == 0, a1[sl], jnp.where(grp == 1, a2[sl], jnp.where(grp == 2, a3[sl], a4[sl])))
    pos = lax.broadcasted_iota(i32, (bt, POOL_CH), 0) + si * bt
    halfw = jnp.left_shift(1, grp)
    cnt = jnp.minimum(pos + halfw, n_seq) - jnp.maximum(pos - halfw, 0)
    dpool = wsum / cnt.astype(f32) - mf[:, 256:512]
    yc = jnp.dot(dpool.astype(bf16), pw_ref[...], preferred_element_type=f32) * ps_ref[...]

    ye = extended(768, 1024)
    cw = cw_ref[...]
    z = shifted(ye, 1) * cw[0:1, :] + ye * cw[1:2, :] + shifted(ye, -1) * cw[2:3, :]
    yd = mf[:, 512:768] * z[sl]

    hgrp = lax.broadcasted_iota(i32, (SGU_CHUNK, SGU_W), 1) >> 6
    ybs = []
    for cidx in range(bt // SGU_CHUNK):
        rows = slice(cidx * SGU_CHUNK, (cidx + 1) * SGU_CHUNK)
        vn_c = vn_ref[rows, :]
        s = jnp.zeros((SGU_CHUNK, SGU_W), f32)
        for hh in range(SGU_HEADS):
            sh = jnp.dot(ws_ref[hh], vn_c, preferred_element_type=f32)
            s = jnp.where(hgrp == hh, sh, s)
        ybs.append(u[rows, :] * (s + sb_ref[...]))
    yb = jnp.concatenate(ybs, axis=0)

    mix = jnp.concatenate([attn_ref[...], yb.astype(bf16), yc.astype(bf16), yd.astype(bf16)], axis=1)
    mo = jnp.dot(mix, wo_ref[...], preferred_element_type=f32)
    xm = x_ref[...] + g1_ref[...] * mo
    xmid_ref[...] = xm

    y = xm * lax.rsqrt(jnp.mean(xm * xm, axis=-1, keepdims=True) + EPS) * n2g_ref[...]
    h2 = y * (1.0 + sc2_ref[...]) + sh2_ref[...]
    _store_token_tiles(h2_ref, h2)

    lt = lax.dot_general(rw_ref[...], h2, (((1,), (1,)), ((), ())),
                         precision=lax.Precision.HIGHEST, preferred_element_type=f32) + rb_ref[...]
    n_exp = lt.shape[0]
    eidx = lax.broadcasted_iota(i32, lt.shape, 0)
    work = lt
    idxs, vals = [], []
    for _ in range(TOP_K):
        m = jnp.max(work, axis=0, keepdims=True)
        idx = jnp.min(jnp.where(work == m, eidx, n_exp), axis=0, keepdims=True)
        idxs.append(idx)
        vals.append(m)
        work = jnp.where(eidx == idx, -jnp.inf, work)
    exps = [jnp.exp(v - vals[0]) for v in vals]
    den = exps[0] + exps[1] + exps[2] + exps[3]
    onehot = jnp.zeros(lt.shape, f32)
    for kk in range(TOP_K):
        ti_ref[kk:kk + 1, :] = idxs[kk]
        gt_ref[kk:kk + 1, :] = exps[kk] / den
        onehot = onehot + (eidx == idxs[kk]).astype(f32)
    tri = (lax.broadcasted_iota(i32, (bt, bt), 0) < lax.broadcasted_iota(i32, (bt, bt), 1)).astype(bf16)
    before = jnp.dot(onehot.astype(bf16), tri, preferred_element_type=f32)
    for kk in range(TOP_K):
        rk_ref[kk:kk + 1, :] = jnp.sum(jnp.where(eidx == idxs[kk], before, 0.0), axis=0, keepdims=True).astype(i32)
    cnt_ref[...] = jnp.sum(onehot, axis=1, keepdims=True)


def _mixer(attn, vn, mixf, x2, n_seq, g1, sh2, sc2, n2g, ws_bf, sgu_bias, pool_bd, pool_scale, conv_w,
           wo_bf, rw_t, rb):
    t, d = x2.shape
    bt = min(MIXER_TOKENS, n_seq)
    tps = n_seq // bt
    hb = bt // HALO
    n_halo = t // HALO
    n_exp = rw_t.shape[0]
    row = lambda i: (i, 0)
    per_batch = pl.BlockSpec((None, 1, d), lambda i: (i // tps, 0, 0))
    col = lambda i: (0, i)
    return pl.pallas_call(
        functools.partial(_mixer_kernel, n_seq=n_seq, bt=bt),
        grid=(t // bt,),
        in_specs=[pl.BlockSpec((bt, ATT_Q_W), row), pl.BlockSpec((bt, SGU_W), row), pl.BlockSpec((bt, 1024), row),
                  pl.BlockSpec((HALO, 1024), lambda i: (jnp.maximum(i * hb - 1, 0), 0)),
                  pl.BlockSpec((HALO, 1024), lambda i: (jnp.minimum((i + 1) * hb, n_halo - 1), 0)),
                  pl.BlockSpec((bt, d), row), per_batch, per_batch, per_batch, _full(n2g),
                  _full(ws_bf), _full(sgu_bias), _full(pool_bd), _full(pool_scale), _full(conv_w),
                  _full(wo_bf), _full(rw_t), _full(rb)],
        out_specs=[pl.BlockSpec((bt, d), row), pl.BlockSpec((bt * SUBLANES, LANES), row),
                   pl.BlockSpec((TOP_K, bt), col), pl.BlockSpec((TOP_K, bt), col), pl.BlockSpec((TOP_K, bt), col),
                   pl.BlockSpec((None, n_exp, 1), lambda i: (i, 0, 0))],
        out_shape=[jax.ShapeDtypeStruct((t, d), f32), jax.ShapeDtypeStruct((t * SUBLANES, LANES), f32),
                   jax.ShapeDtypeStruct((TOP_K, t), i32), jax.ShapeDtypeStruct((TOP_K, t), f32),
                   jax.ShapeDtypeStruct((TOP_K, t), i32), jax.ShapeDtypeStruct((t // bt, n_exp, 1), f32)],
        compiler_params=_params("arbitrary"),
        name="mixer_router",
    )(attn, vn, mixf, mixf, mixf, x2, g1, sh2, sc2, n2g, ws_bf, sgu_bias, pool_bd, pool_scale, conv_w,
      wo_bf, rw_t, rb)


def _stage_index_kernel(ti_ref, rk_ref, off_ref, o_ref, *, w0, n_exp):
    base = (w0 + pl.program_id(0)) * n_exp
    ti = ti_ref[...]
    acc = rk_ref[...]
    for e in range(n_exp):
        acc = acc + jnp.where(ti == e, off_ref[base + e], 0)
    o_ref[...] = acc * SUBLANES


def _stage_index(top_i, rank, run_off, bt, w0, n_exp):
    k, t = top_i.shape
    col = lambda i: (0, i)
    return pl.pallas_call(
        functools.partial(_stage_index_kernel, w0=w0, n_exp=n_exp),
        grid=(t // bt,),
        in_specs=[pl.BlockSpec((k, bt), col), pl.BlockSpec((k, bt), col), pl.BlockSpec(memory_space=pltpu.SMEM)],
        out_specs=pl.BlockSpec((k, bt), col),
        out_shape=jax.ShapeDtypeStruct((k, t), i32),
        compiler_params=_params("arbitrary"),
        name="stage_index",
    )(top_i, rank, run_off)


def _tile_rows(t, n=1):
    return pl.ds(pl.multiple_of(t * SUBLANES, SUBLANES), n * SUBLANES)


def _tile_copy(src, s, dst, d, sem):
    return pltpu.make_async_copy(src.at[_tile_rows(s)], dst.at[_tile_rows(d)], sem)


def _for_each_run_piece(runs_ref, w, n_exp, max_len, fn):
    bits = [1 << b for b in range(max_len.bit_length() - 1, -1, -1)]

    def per_expert(e, carry):
        base = (w * n_exp + e) * 3
        stage0, slot0, length = runs_ref[base], runs_ref[base + 1], runs_ref[base + 2]
        done = 0
        for bit in bits:
            take = length & bit

            @pl.when(take != 0)
            def _():
                fn(stage0 + done, slot0 + done, bit)

            done = done + take
        return carry

    lax.fori_loop(0, n_exp, per_expert, 0)


def _wait_window(stage, buf, sem):
    pltpu.make_async_copy(stage.at[buf], stage.at[buf], sem.at[buf]).wait()


def _token_major(table):
    return table.T.reshape(-1)


def _per_buffer(slot, fn):
    for b in range(2):
        @pl.when(slot == b)
        def _():
            fn(b)


def _split_slabs(w_ref, g_ref, l_ref, scr):
    f = g_ref.shape[1]
    for s in range(w_ref.shape[0] // LANES):
        rows = slice(s * LANES, (s + 1) * LANES)
        scr[...] = w_ref[rows, :].T
        g_ref[rows, :] = scr[pl.ds(0, f, stride=2), :].T.astype(bf16)
        l_ref[rows, :] = scr[pl.ds(1, f, stride=2), :].T.astype(bf16)


def _split_w1(w1, layer):
    _, n_exp, d, f2 = w1.shape
    f = f2 // 2
    rows = 2 * LANES
    blk = pl.BlockSpec((None, rows, f), lambda e, j: (e, j, 0))
    return pl.pallas_call(
        _split_slabs,
        grid=(n_exp, d // rows),
        in_specs=[pl.BlockSpec((None, None, rows, f2), lambda e, j: (layer, e, j, 0))],
        out_specs=[blk, blk],
        out_shape=[jax.ShapeDtypeStruct((n_exp, d, f), bf16)] * 2,
        scratch_shapes=[pltpu.VMEM((f2, LANES), f32)],
        compiler_params=_params("arbitrary", "arbitrary"),
        name="split_w1",
    )(w1)


class _SplitSide:
    def __init__(self, w1, layer, n_steps):
        _, n_exp, d, f2 = w1.shape
        fits = [r for r in (LANES, 2 * LANES, 4 * LANES, 8 * LANES) if d % r == 0 and n_exp * (d // r) <= n_steps]
        self.ok = bool(fits)
        if not self.ok:
            return
        rows = fits[0]
        per = d // rows
        self.w1 = w1
        self.n_side = n_exp * per
        last = self.n_side - 1
        self.in_spec = pl.BlockSpec((None, None, rows, f2),
                                    lambda i: (layer, jnp.minimum(i, last) // per, jnp.minimum(i, last) % per, 0))
        self.out_spec = pl.BlockSpec((None, rows, f2 // 2),
                                     lambda i: (jnp.minimum(i, last) // per, jnp.minimum(i, last) % per, 0))
        self.out_shape = jax.ShapeDtypeStruct((n_exp, d, f2 // 2), bf16)
        self.scratch = pltpu.VMEM((f2, LANES), f32)

    def run(self, w_ref, g_ref, l_ref, scr):
        @pl.when(pl.program_id(0) < self.n_side)
        def _():
            _split_slabs(w_ref, g_ref, l_ref, scr)


def _dispatch_kernel(sidx_ref, runs_ref, pad_ref, h_ref, *rest, bt, w0, n_exp, fill_pad, side):
    rest = list(rest)
    if not fill_pad:
        rest.pop(0)
    w_ref = rest.pop(0) if side else None
    xs_ref = rest.pop(0)
    g_ref, l_ref = (rest.pop(0), rest.pop(0)) if side else (None, None)
    zero_ref = rest.pop(0) if fill_pad else None
    stage, sem = rest.pop(0), rest.pop(0)
    i = pl.program_id(0)
    slot = i % 2

    if fill_pad:
        @pl.when(i == 0)
        def _():
            zero_ref[...] = jnp.zeros_like(zero_ref)

            def per_expert(e, carry):
                first, count = pad_ref[e], pad_ref[n_exp + e]

                def issue(r, c):
                    _tile_copy(zero_ref, 0, xs_ref, first + r, sem.at[0]).start()
                    return c

                def drain(r, c):
                    _tile_copy(zero_ref, 0, xs_ref, 0, sem.at[0]).wait()
                    return c

                lax.fori_loop(0, count, issue, 0)
                lax.fori_loop(0, count, drain, 0)
                return carry

            lax.fori_loop(0, n_exp, per_expert, 0)

    def run_copy(buf):
        def make(stage_tile, slot_tile, n):
            return pltpu.make_async_copy(stage.at[buf, _tile_rows(stage_tile, n)], xs_ref.at[_tile_rows(slot_tile, n)],
                                         sem.at[buf])
        return make

    def place_all(b):
        def place(t, carry):
            tile = h_ref[_tile_rows(t), :]
            for kk in range(TOP_K):
                row = pl.multiple_of(sidx_ref[t * TOP_K + kk], SUBLANES)
                stage[b, pl.ds(row, SUBLANES), :] = tile
            return carry

        lax.fori_loop(0, bt, place, 0, unroll=ROW_UNROLL)

    _per_buffer(slot, place_all)
    _for_each_run_piece(runs_ref, w0 + i, n_exp, bt, lambda a, b, n: run_copy(slot)(a, b, n).start())
    if side:
        side.run(w_ref, g_ref, l_ref, rest.pop(0))

    @pl.when(i > 0)
    def _():
        _wait_window(stage, 1 - slot, sem)

    @pl.when(i == pl.num_programs(0) - 1)
    def _():
        _wait_window(stage, slot, sem)


def _dispatch(sidx, runs, pad, h2, xs, n_slots, bt, w0, n_exp, w1_side=None):
    t = h2.shape[0] // SUBLANES
    n_steps = t // bt
    fill_pad = xs is None
    side = _SplitSide(*w1_side, n_steps) if w1_side else None
    side = side if side is not None and side.ok else None
    smem = pl.BlockSpec(memory_space=pltpu.SMEM)
    in_specs = [pl.BlockSpec((TOP_K * bt,), lambda i: (i,), memory_space=pltpu.SMEM), smem, smem,
                pl.BlockSpec((bt * SUBLANES, LANES), lambda i: (i, 0))]
    args = [_token_major(sidx), runs, pad, h2]
    out_specs = [pl.BlockSpec(memory_space=pl.ANY)]
    out_shape = [jax.ShapeDtypeStruct((n_slots * SUBLANES, LANES), f32)]
    scratch = [pltpu.VMEM((2, TOP_K * bt * SUBLANES, LANES), f32), pltpu.SemaphoreType.DMA((2,))]
    if fill_pad:
        scratch = [pltpu.VMEM((SUBLANES, LANES), f32)] + scratch
    else:
        in_specs.append(pl.BlockSpec(memory_space=pl.ANY))
        args.append(xs)
    if side:
        in_specs.append(side.in_spec)
        args.append(side.w1)
        out_specs += [side.out_spec] * 2
        out_shape += [side.out_shape] * 2
        scratch.append(side.scratch)
    outs = pl.pallas_call(
        functools.partial(_dispatch_kernel, bt=bt, w0=w0, n_exp=n_exp, fill_pad=fill_pad, side=side),
        grid=(n_steps,),
        in_specs=in_specs,
        out_specs=out_specs,
        out_shape=out_shape,
        scratch_shapes=scratch,
        input_output_aliases={} if fill_pad else {4: 0},
        compiler_params=_params("arbitrary"),
        name="dispatch",
    )(*args)
    return outs[0], (tuple(outs[1:]) if side else None)


def _expert_kernel(be_ref, nu_ref, x_ref, w1g_ref, w1l_ref, b1g_ref, b1l_ref, w2f_ref, b2_ref, y_ref, w2_ref):
    i = pl.program_id(0)
    active = i < nu_ref[0]
    new_expert = jnp.logical_or(i == 0, be_ref[i] != be_ref[jnp.maximum(i - 1, 0)])

    @pl.when(jnp.logical_and(active, new_expert))
    def _():
        w2_ref[...] = w2f_ref[...].astype(bf16)

    @pl.when(active)
    def _():
        bm = x_ref.shape[0] // SUBLANES
        x = _load_token_tiles(x_ref, bm).astype(bf16)
        zg = jnp.dot(x, w1g_ref[...], preferred_element_type=f32) + b1g_ref[...]
        zl = jnp.dot(x, w1l_ref[...], preferred_element_type=f32) + b1l_ref[...]
        g = jnp.minimum(zg, SWIGLU_LIMIT)
        lin = jnp.clip(zl, -SWIGLU_LIMIT, SWIGLU_LIMIT)
        a = g * jax.nn.sigmoid(SWIGLU_ALPHA * g) * (lin + 1.0)
        y = jnp.dot(a.astype(bf16), w2_ref[...], preferred_element_type=f32) + b2_ref[...]
        _store_token_tiles(y_ref, y)

    @pl.when(i >= nu_ref[0])
    def _():
        y_ref[...] = jnp.zeros_like(y_ref)


def _experts(block_e, n_used, xs, w1g, w1l, b1g, b1l, w2_all, layer, b2):
    n_exp, d, f = w1g.shape
    bm = EXPERT_ROWS
    blk = (bm * SUBLANES, LANES)
    xrow = lambda i, be, nu: (jnp.minimum(i, nu[0] - 1), 0)
    wsel = lambda i, be, nu: (be[i], 0, 0)
    grid_spec = pltpu.PrefetchScalarGridSpec(
        num_scalar_prefetch=2,
        grid=(xs.shape[0] // blk[0],),
        in_specs=[pl.BlockSpec(blk, xrow),
                  pl.BlockSpec((None, d, f), wsel), pl.BlockSpec((None, d, f), wsel),
                  pl.BlockSpec((None, 1, f), wsel), pl.BlockSpec((None, 1, f), wsel),
                  pl.BlockSpec((None, None, f, d), lambda i, be, nu: (layer, be[i], 0, 0)),
                  pl.BlockSpec((None, 1, d), wsel)],
        out_specs=pl.BlockSpec(blk, lambda i, be, nu: (i, 0)),
        scratch_shapes=[pltpu.VMEM((f, d), bf16)],
    )
    return pl.pallas_call(
        _expert_kernel,
        grid_spec=grid_spec,
        out_shape=jax.ShapeDtypeStruct(xs.shape, f32),
        compiler_params=_params("arbitrary"),
        name="experts",
    )(block_e, n_used, xs, w1g, w1l, b1g, b1l, w2_all, b2)


def _combine_kernel(sidx_ref, gate_ref, runs_ref, ys_ref, x_ref, g2_ref, fg_ref, *rest, bt, w0, n_exp, final, side):
    rest = list(rest)
    w_ref = rest.pop(0) if side else None
    o_ref = rest.pop(0)
    g_ref, l_ref = (rest.pop(0), rest.pop(0)) if side else (None, None)
    stage, acc_ref, sem = rest.pop(0), rest.pop(0), rest.pop(0)
    i = pl.program_id(0)
    slot = i % 2

    def run_copy(buf):
        def make(stage_tile, slot_tile, n):
            return pltpu.make_async_copy(ys_ref.at[_tile_rows(slot_tile, n)], stage.at[buf, _tile_rows(stage_tile, n)],
                                         sem.at[buf])
        return make

    def fetch(w, buf):
        _for_each_run_piece(runs_ref, w, n_exp, bt, lambda a, b, n: run_copy(buf)(a, b, n).start())

    @pl.when(i == 0)
    def _():
        fetch(w0, 0)

    @pl.when(i + 1 < pl.num_programs(0))
    def _():
        fetch(w0 + i + 1, 1 - slot)

    if side:
        side.run(w_ref, g_ref, l_ref, rest.pop(0))

    _wait_window(stage, slot, sem)

    def sum_all(b):
        def token(t, carry):
            acc = None
            for kk in range(TOP_K):
                row = pl.multiple_of(sidx_ref[t * TOP_K + kk], SUBLANES)
                term = gate_ref[t * TOP_K + kk] * stage[b, pl.ds(row, SUBLANES), :]
                acc = term if acc is None else acc + term
            acc_ref[_tile_rows(t), :] = acc
            return carry

        lax.fori_loop(0, bt, token, 0, unroll=ROW_UNROLL)

    _per_buffer(slot, sum_all)
    xo = x_ref[...] + g2_ref[...] * _load_token_tiles(acc_ref, bt)
    if final:
        xo = xo * lax.rsqrt(jnp.mean(xo * xo, axis=-1, keepdims=True) + EPS) * fg_ref[...]
    o_ref[...] = xo


def _combine(sidx, gates, runs, ys, x_mid, n_seq, g2, final_g, final, bt, w0, n_exp, w1_side=None):
    t, d = x_mid.shape
    tps = n_seq // bt
    row = lambda i: (i, 0)
    n_steps = t // bt
    side = _SplitSide(*w1_side, n_steps) if w1_side else None
    side = side if side is not None and side.ok else None
    per_step = pl.BlockSpec((TOP_K * bt,), lambda i: (i,), memory_space=pltpu.SMEM)
    in_specs = [per_step, per_step, pl.BlockSpec(memory_space=pltpu.SMEM),
                pl.BlockSpec(memory_space=pl.ANY),
                pl.BlockSpec((bt, d), row),
                pl.BlockSpec((None, 1, d), lambda i: (i // tps, 0, 0)),
                _full(final_g)]
    args = [_token_major(sidx), _token_major(gates), runs, ys, x_mid, g2, final_g]
    out_specs = [pl.BlockSpec((bt, d), row)]
    out_shape = [jax.ShapeDtypeStruct((t, d), f32)]
    scratch = [pltpu.VMEM((2, TOP_K * bt * SUBLANES, LANES), f32), pltpu.VMEM((bt * SUBLANES, LANES), f32),
               pltpu.SemaphoreType.DMA((2,))]
    if side:
        in_specs.append(side.in_spec)
        args.append(side.w1)
        out_specs += [side.out_spec] * 2
        out_shape += [side.out_shape] * 2
        scratch.append(side.scratch)
    outs = pl.pallas_call(
        functools.partial(_combine_kernel, bt=bt, w0=w0, n_exp=n_exp, final=final, side=side),
        grid=(n_steps,),
        in_specs=in_specs,
        out_specs=out_specs,
        out_shape=out_shape,
        scratch_shapes=scratch,
        compiler_params=_params("arbitrary"),
        name="combine",
    )(*args)
    return outs[0], (tuple(outs[1:]) if side else None)


def _rope_tables(n_tokens):
    rows = n_tokens // GRID_W
    row = jnp.repeat(jnp.arange(rows), GRID_W).astype(f32)
    col = jnp.tile(jnp.arange(GRID_W), rows).astype(f32)
    n_freq = HEAD_DIM // 4
    inv = ROPE_BASE ** (-jnp.arange(n_freq, dtype=f32) / n_freq)
    ang = jnp.concatenate([row[:, None] * inv, col[:, None] * inv], axis=-1)
    cos, sin = jnp.cos(ang), jnp.sin(ang)
    return jnp.tile(cos, (1, 4)), jnp.concatenate([-sin, sin, -sin, sin], axis=-1)


def kernel(x, c, ctx, c_ctx, norm1_g, norm2_g, ada_w, ada_b, w_in, attn_sink, sgu_ws, sgu_b, sgu_ln_g, sgu_ln_b,
           pool_w, pool_scale, conv_w, w_out, router_w, router_b, exp_w1, exp_b1, exp_w2, exp_b2, final_g):
    n_batch, n_seq, d = x.shape
    n_ctx = ctx.shape[1]
    depth = ada_w.shape[0]
    n_exp = router_w.shape[2]
    assert d == SUBLANES * LANES, "token rows are moved as single (8, 128) tiles"
    t_lat, t_ctx = n_batch * n_seq, n_batch * n_ctx
    bm = EXPERT_ROWS

    cvec = jnp.concatenate([c, c_ctx[None, :], jnp.zeros((SUBLANES - n_batch - 1, d), f32)], axis=0)
    mods = _ada(cvec, ada_w, ada_b)

    cos_l, sin_l = _rope_tables(n_seq)
    cos_c, sin_c = jnp.ones((n_ctx, LANES), f32), jnp.zeros((n_ctx, LANES), f32)

    xl = x.reshape(t_lat, d)
    xc = ctx.reshape(t_ctx, d)
    row2 = lambda a: a.reshape(1, -1)
    w1_split = {}

    for l in range(depth):
        last = l == depth - 1
        ml = mods[l, :n_batch].reshape(n_batch, 6, 1, d)
        mc = jnp.broadcast_to(mods[l, n_batch].reshape(1, 6, 1, d), (n_batch, 6, 1, d))
        sh1l, sc1l, g1l, sh2l, sc2l, g2l = (ml[:, i] for i in range(6))
        sh1c, sc1c, g1c, sh2c, sc2c, g2c = (mc[:, i] for i in range(6))

        w_in_bf = w_in[l].astype(bf16)
        wo_bf = w_out[l].astype(bf16)
        ws_bf = sgu_ws[l].astype(bf16)
        sgu_bias = jnp.repeat(sgu_b[l].T, SGU_W // SGU_HEADS, axis=1)
        pool_bd = jax.scipy.linalg.block_diag(*[pool_w[l, g] for g in range(pool_w.shape[1])]).astype(bf16)
        n1g, n2g = row2(norm1_g[l]), row2(norm2_g[l])
        lng, lnb, psc = row2(sgu_ln_g[l]), row2(sgu_ln_b[l]), row2(pool_scale[l])
        rw_t = router_w[l].T
        rb = router_b[l].reshape(n_exp, 1)
        sink = attn_sink[l]
        mix_w = (ws_bf, sgu_bias, pool_bd, psc, conv_w[l], wo_bf, rw_t, rb)

        qc, kvc, vnc, mfc = _inproj(xc, n_ctx, sh1c, sc1c, n1g, w_in_bf, cos_c, sin_c, lng, lnb)
        ql, kvl, vnl, mfl = _inproj(xl, n_seq, sh1l, sc1l, n1g, w_in_bf, cos_l, sin_l, lng, lnb)
        attn_l = _window_attn(ql, kvl, kvc, sink, n_batch, n_seq, n_ctx)
        bt_l, bt_c = min(MIXER_TOKENS, n_seq), min(MIXER_TOKENS, n_ctx)
        xmid_l, h2_l, ti_l, gt_l, rk_l, cnt = _mixer(attn_l, vnl, mfl, xl, n_seq, g1l, sh2l, sc2l, n2g, *mix_w)
        n_win_l = cnt.shape[0]
        t_all = t_lat
        if not last:
            attn_c = _ctx_attn(qc, kvc, sink, n_batch, n_ctx)
            xmid_c, h2_c, ti_c, gt_c, rk_c, cnt_c = _mixer(attn_c, vnc, mfc, xc, n_ctx, g1c, sh2c, sc2c, n2g, *mix_w)
            cnt = jnp.concatenate([cnt, cnt_c], axis=0)
            t_all = t_lat + t_ctx

        win_cnt = cnt[:, :, 0].astype(i32)
        counts = jnp.sum(win_cnt, axis=0)
        padded = (counts + bm - 1) // bm * bm
        pends = jnp.cumsum(padded)
        pstarts = pends - padded
        n_blocks = -(-(t_all * TOP_K) // bm) + n_exp
        starts = jnp.arange(n_blocks, dtype=i32) * bm
        block_e = jnp.minimum(jnp.sum((pends[None, :] <= starts[:, None]).astype(i32), axis=1), n_exp - 1)
        n_used = (pends[-1:] // bm).astype(i32)
        pad = jnp.concatenate([pstarts + counts, padded - counts]).astype(i32)
        run_stage = jnp.cumsum(win_cnt, axis=1) - win_cnt
        run_slot = pstarts[None, :] + jnp.cumsum(win_cnt, axis=0) - win_cnt
        runs = jnp.stack([run_stage, run_slot, win_cnt], axis=-1).reshape(-1)
        run_off = run_stage.reshape(-1)
        sidx_l = _stage_index(ti_l, rk_l, run_off, bt_l, 0, n_exp)
        if not last:
            sidx_c = _stage_index(ti_c, rk_c, run_off, bt_c, n_win_l, n_exp)

        xs, split = _dispatch(sidx_l, runs, pad, h2_l, None, n_blocks * bm, bt_l, 0, n_exp,
                              None if l in w1_split else (exp_w1, l))
        if l not in w1_split:
            w1_split[l] = split or _split_w1(exp_w1, l)
        if not last:
            xs, _ = _dispatch(sidx_c, runs, pad, h2_c, xs, n_blocks * bm, bt_c, n_win_l, n_exp)

        w1g, w1l = w1_split[l]
        b1 = exp_b1[l]
        b1g, b1l = b1[:, None, 0::2], b1[:, None, 1::2]
        ys = _experts(block_e, n_used, xs, w1g, w1l, b1g, b1l, exp_w2, l, exp_b2[l][:, None, :])

        fg = row2(final_g)
        xl, split = _combine(sidx_l, gt_l, runs, ys, xmid_l, n_seq, g2l, fg, last, bt_l, 0, n_exp,
                             None if last else (exp_w1, l + 1))
        if split:
            w1_split[l + 1] = split
        if not last:
            xc, _ = _combine(sidx_c, gt_c, runs, ys, xmid_c, n_ctx, g2c, fg, False, bt_c, n_win_l, n_exp)

    return xl.reshape(n_batch, n_seq, d)
```

```python
import functools

import jax
import jax.numpy as jnp
from jax import lax
from jax.experimental import pallas as pl
from jax.experimental.pallas import tpu as pltpu

f32 = jnp.float32
bf16 = jnp.bfloat16
i32 = jnp.int32
u32 = jnp.uint32

GRID_W = 64
EPS = 1e-6
N_Q_HEADS = 8
HEAD_DIM = 64
WINDOW = 128
ROPE_BASE = 10000.0
ATT_Q_W = 512
ATT_KV_W = 128
SGU_HEADS = 4
SGU_W = 256
SGU_CHUNK = 128
POOL_CH = 256
CONV_CH = 256
MIX_WIDTH = 1280
TOP_K = 4
SWIGLU_LIMIT = 7.0
SWIGLU_ALPHA = 1.702
SQRT_HALF = 0.7071067811865476

LANES = 128
SUBLANES = 8
PACKED_ROWS = SUBLANES // 2
VMEM_LIMIT_BYTES = 56 * 1024 * 1024

INPROJ_TOKENS = 512
ATTN_TOKENS = 128
MIXER_TOKENS = 512
EXPERT_ROWS = 512
HALO = 8
ROW_UNROLL = 4


def _params(*sem):
    return pltpu.CompilerParams(dimension_semantics=sem, vmem_limit_bytes=VMEM_LIMIT_BYTES)


def _full(a):
    nd = a.ndim
    return pl.BlockSpec(a.shape, lambda *_: (0,) * nd)


def _store_token_tiles(ref, val):
    n = val.shape[0]
    for s in range(SUBLANES):
        ref[pl.ds(s, n, stride=SUBLANES), :] = val[:, s * LANES:(s + 1) * LANES]


def _load_token_tiles(ref, n):
    return jnp.concatenate([ref[pl.ds(s, n, stride=SUBLANES), :] for s in range(SUBLANES)], axis=1)


HIGH_HALF = 0xFFFF0000


def _store_packed_tiles(ref, val):
    n = val.shape[0]
    half = PACKED_ROWS * LANES
    for s in range(PACKED_ROWS):
        lo = val[:, s * LANES:(s + 1) * LANES].astype(bf16).astype(f32)
        hi = val[:, half + s * LANES:half + (s + 1) * LANES].astype(bf16).astype(f32)
        lo_bits = lax.shift_right_logical(lax.bitcast_convert_type(lo, u32), jnp.uint32(16))
        hi_bits = lax.bitcast_convert_type(hi, u32) & jnp.uint32(HIGH_HALF)
        ref[pl.ds(s, n, stride=PACKED_ROWS), :] = lo_bits | hi_bits


def _load_packed_tiles(ref, n):
    los, his = [], []
    for s in range(PACKED_ROWS):
        w = ref[pl.ds(s, n, stride=PACKED_ROWS), :]
        los.append(lax.bitcast_convert_type(lax.shift_left(w, jnp.uint32(16)), f32))
        his.append(lax.bitcast_convert_type(w & jnp.uint32(HIGH_HALF), f32))
    return jnp.concatenate(los + his, axis=1).astype(bf16)


def _gelu(x):
    return 0.5 * x * (1.0 + lax.erf(x * SQRT_HALF))


def _ada_kernel(c_ref, w_ref, b_ref, o_ref):
    c = c_ref[...]
    s = c * jax.nn.sigmoid(c)
    o_ref[...] = jnp.dot(s, w_ref[...], precision=lax.Precision.HIGHEST, preferred_element_type=f32) + b_ref[...]


def _ada(cvec, ada_w, ada_b):
    depth, d, n = ada_w.shape
    tn = 1536
    return pl.pallas_call(
        _ada_kernel,
        grid=(depth, n // tn),
        in_specs=[pl.BlockSpec(cvec.shape, lambda l, j: (0, 0)),
                  pl.BlockSpec((None, d, tn), lambda l, j: (l, 0, j)),
                  pl.BlockSpec((None, 1, tn), lambda l, j: (l, 0, j))],
        out_specs=pl.BlockSpec((None, cvec.shape[0], tn), lambda l, j: (l, 0, j)),
        out_shape=jax.ShapeDtypeStruct((depth, cvec.shape[0], n), f32),
        compiler_params=_params("arbitrary", "arbitrary"),
        name="ada_mod",
    )(cvec, ada_w, ada_b.reshape(depth, 1, n))


def _inproj_kernel(x_ref, sh_ref, sc_ref, g_ref, w_ref, cos_ref, sin_ref, lng_ref, lnb_ref,
                   q_ref, kv_ref, vn_ref, mixf_ref):
    x = x_ref[...]
    y = x * lax.rsqrt(jnp.mean(x * x, axis=-1, keepdims=True) + EPS) * g_ref[...]
    h = y * (1.0 + sc_ref[...]) + sh_ref[...]
    p = jnp.dot(h.astype(bf16), w_ref[...], preferred_element_type=f32)

    cos = cos_ref[...]
    sin = sin_ref[...]
    lane = lax.broadcasted_iota(i32, cos.shape, 1)
    first_half = (lane & (HEAD_DIM - 1)) < HEAD_DIM // 2

    def rope(t):
        partner = jnp.where(first_half, pltpu.roll(t, LANES - HEAD_DIM // 2, 1), pltpu.roll(t, HEAD_DIM // 2, 1))
        return t * cos + partner * sin

    scale = HEAD_DIM ** -0.5
    for m in range(ATT_Q_W // LANES):
        q_ref[:, m * LANES:(m + 1) * LANES] = (rope(p[:, m * LANES:(m + 1) * LANES]) * scale).astype(bf16)
    k = rope(p[:, 512:640])
    v = p[:, 640:768]
    kv_ref[:, 0:128] = k.astype(bf16)
    kv_ref[:, 128:256] = pltpu.roll(k, HEAD_DIM, 1).astype(bf16)
    kv_ref[:, 256:384] = v.astype(bf16)
    kv_ref[:, 384:512] = pltpu.roll(v, HEAD_DIM, 1).astype(bf16)

    u = _gelu(p[:, 768:1024])
    gv = _gelu(p[:, 1024:1280])
    mu = jnp.mean(gv, axis=-1, keepdims=True)
    var = jnp.mean(jnp.square(gv - mu), axis=-1, keepdims=True)
    vn_ref[...] = ((gv - mu) * lax.rsqrt(var + EPS) * lng_ref[...] + lnb_ref[...]).astype(bf16)

    mixf_ref[:, 0:256] = u
    mixf_ref[:, 256:512] = p[:, 1280:1536]
    mixf_ref[:, 512:768] = p[:, 1536:1792]
    mixf_ref[:, 768:1024] = p[:, 1792:2048] * p[:, 2048:2304]


def _inproj(x2, n_seq, shift, scale, g, w_bf, cos_t, sin_t, ln_g, ln_b):
    t, d = x2.shape
    bt = min(INPROJ_TOKENS, n_seq)
    tps = n_seq // bt
    ncol = w_bf.shape[1]
    row = lambda i: (i, 0)
    return pl.pallas_call(
        _inproj_kernel,
        grid=(t // bt,),
        in_specs=[pl.BlockSpec((bt, d), row),
                  pl.BlockSpec((None, 1, d), lambda i: (i // tps, 0, 0)),
                  pl.BlockSpec((None, 1, d), lambda i: (i // tps, 0, 0)),
                  _full(g),
                  pl.BlockSpec((d, ncol), lambda i: (0, 0)),
                  pl.BlockSpec((bt, LANES), lambda i: (i % tps, 0)),
                  pl.BlockSpec((bt, LANES), lambda i: (i % tps, 0)),
                  _full(ln_g), _full(ln_b)],
        out_specs=[pl.BlockSpec((bt, ATT_Q_W), row), pl.BlockSpec((bt, 512), row),
                   pl.BlockSpec((bt, SGU_W), row), pl.BlockSpec((bt, 1024), row)],
        out_shape=[jax.ShapeDtypeStruct((t, ATT_Q_W), bf16), jax.ShapeDtypeStruct((t, 512), bf16),
                   jax.ShapeDtypeStruct((t, SGU_W), bf16), jax.ShapeDtypeStruct((t, 1024), f32)],
        compiler_params=_params("arbitrary"),
        name="inproj",
    )(x2, shift, scale, g, w_bf, cos_t, sin_t, ln_g, ln_b)


def _attn_block(sink_ref, q, kv, bias, o_ref, row0):
    k_nat, k_swp, v_nat, v_swp = (kv[:, i * LANES:(i + 1) * LANES] for i in range(4))
    nq = q.shape[0]
    low = lax.broadcasted_iota(i32, (nq, LANES), 1) < HEAD_DIM
    top = lax.broadcasted_iota(i32, (2 * nq, 1), 0) < nq
    zero = jnp.zeros((nq, LANES), q.dtype)
    for kvh in range(2):
        chunks = [q[:, (2 * kvh + i) * LANES:(2 * kvh + i + 1) * LANES] for i in range(2)]
        outs = []
        for half in range(2):
            keep = low if half == 0 else jnp.logical_not(low)
            qz = jnp.concatenate([jnp.where(keep, c, zero) for c in chunks], axis=0)
            kh = k_nat if kvh == half else k_swp
            vh = v_nat if kvh == half else v_swp
            s = lax.dot_general(qz, kh, (((1,), (1,)), ((), ())), preferred_element_type=f32)
            if bias is not None:
                s = s + bias
            h0 = 4 * kvh + half
            sk = jnp.where(top, sink_ref[h0], sink_ref[h0 + 2])
            mx = jnp.maximum(jnp.max(s, axis=1, keepdims=True), sk)
            e = jnp.exp(s - mx)
            den = jnp.sum(e, axis=1, keepdims=True) + jnp.exp(sk - mx)
            outs.append(jnp.dot(e.astype(bf16), vh, preferred_element_type=f32) / den)
        for i in range(2):
            rows = slice(i * nq, (i + 1) * nq)
            m = 2 * kvh + i
            o_ref[row0:row0 + nq, m * LANES:(m + 1) * LANES] = jnp.where(low, outs[0][rows], outs[1][rows]).astype(bf16)


def _window_attn_kernel(sink_ref, q_ref, kvp_ref, kvm_ref, kvn_ref, kvx_ref, o_ref, *, nb):
    jj = pl.program_id(1)
    nq = ATTN_TOKENS
    nband = 3 * nq
    kvm = kvm_ref[...]
    kvx = kvx_ref[...]
    r = lax.broadcasted_iota(i32, (nq, nband), 0)
    c = lax.broadcasted_iota(i32, (nq, nband), 1)
    dlt = c - r
    in_window = (dlt >= 0) & (dlt <= 2 * WINDOW)
    ctx_zeros = jnp.zeros((nq, kvx.shape[0]), f32)
    for i, kv_band in enumerate((jnp.concatenate([kvp_ref[...], kvm], axis=0),
                                 jnp.concatenate([kvm, kvn_ref[...]], axis=0))):
        j = 2 * jj + i
        lo = jnp.where(j == 0, nq, 0)
        hi = jnp.where(j == nb - 1, 2 * nq, nband)
        valid = in_window & (c >= lo) & (c < hi)
        bias = jnp.concatenate([jnp.where(valid, 0.0, -jnp.inf).astype(f32), ctx_zeros], axis=1)
        _attn_block(sink_ref, q_ref[i * nq:(i + 1) * nq, :], jnp.concatenate([kv_band, kvx], axis=0),
                    jnp.concatenate([bias, bias], axis=0), o_ref, i * nq)


def _ctx_attn_kernel(sink_ref, q_ref, kvx_ref, o_ref):
    _attn_block(sink_ref, q_ref[...], kvx_ref[...], None, o_ref, 0)


def _window_attn(q, kv, kv_ctx, sink, n_batch, n_seq, n_ctx):
    t = q.shape[0]
    nb = n_seq // ATTN_TOKENS
    assert nb % 2 == 0
    nb2 = nb // 2
    one, two = (ATTN_TOKENS, 512), (2 * ATTN_TOKENS, 512)
    return pl.pallas_call(
        functools.partial(_window_attn_kernel, nb=nb),
        grid=(n_batch, nb2),
        in_specs=[pl.BlockSpec(memory_space=pltpu.SMEM),
                  pl.BlockSpec(two, lambda b, j: (b * nb2 + j, 0)),
                  pl.BlockSpec(one, lambda b, j: (b * nb + jnp.maximum(2 * j - 1, 0), 0)),
                  pl.BlockSpec(two, lambda b, j: (b * nb2 + j, 0)),
                  pl.BlockSpec(one, lambda b, j: (b * nb + jnp.minimum(2 * j + 2, nb - 1), 0)),
                  pl.BlockSpec((n_ctx, 512), lambda b, j: (b, 0))],
        out_specs=pl.BlockSpec(two, lambda b, j: (b * nb2 + j, 0)),
        out_shape=jax.ShapeDtypeStruct((t, ATT_Q_W), bf16),
        compiler_params=_params("arbitrary", "arbitrary"),
        name="window_attn",
    )(sink, q, kv, kv, kv, kv_ctx)


def _ctx_attn(q, kv_ctx, sink, n_batch, n_ctx):
    nb = n_ctx // ATTN_TOKENS
    blk = (ATTN_TOKENS, 512)
    return pl.pallas_call(
        _ctx_attn_kernel,
        grid=(n_batch, nb),
        in_specs=[pl.BlockSpec(memory_space=pltpu.SMEM),
                  pl.BlockSpec(blk, lambda b, j: (b * nb + j, 0)),
                  pl.BlockSpec((n_ctx, 512), lambda b, j: (b, 0))],
        out_specs=pl.BlockSpec(blk, lambda b, j: (b * nb + j, 0)),
        out_shape=jax.ShapeDtypeStruct(q.shape, bf16),
        compiler_params=_params("arbitrary", "arbitrary"),
        name="ctx_attn",
    )(sink, q, kv_ctx)


def _mixer_kernel(attn_ref, vn_ref, mf_ref, mfp_ref, mfn_ref, x_ref, g1_ref, sh2_ref, sc2_ref, n2g_ref,
                  ws_ref, sb_ref, pw_ref, ps_ref, cw_ref, wo_ref, rw_ref, rb_ref,
                  xmid_ref, h2_ref, ti_ref, gt_ref, rk_ref, cnt_ref, *, n_seq, bt):
    i = pl.program_id(0)
    tps = n_seq // bt
    si = i % tps
    first = si == 0
    last = si == tps - 1
    n_ext = bt + 2 * HALO

    mf = mf_ref[...]
    u = mf[:, 0:256]

    def extended(lo, hi):
        prev = jnp.where(first, 0.0, mfp_ref[:, lo:hi])
        nxt = jnp.where(last, 0.0, mfn_ref[:, lo:hi])
        return jnp.concatenate([prev, mf[:, lo:hi], nxt], axis=0)

    def shifted(a, s):
        return pltpu.roll(a, s % n_ext, 0)

    xe = extended(256, 512)
    a1 = shifted(xe, 1) + xe
    a2 = shifted(a1, 1) + shifted(a1, -1)
    a3 = shifted(a2, 2) + shifted(a2, -2)
    a4 = shifted(a3, 4) + shifted(a3, -4)
    lane = lax.broadcasted_iota(i32, (bt, POOL_CH), 1)
    grp = lane >> 6
    sl = slice(HALO, HALO + bt)
    wsum = jnp.where(grp == 0, a1[sl], jnp.where(grp == 1, a2[sl], jnp.where(grp == 2, a3[sl], a4[sl])))
    pos = lax.broadcasted_iota(i32, (bt, POOL_CH), 0) + si * bt
    halfw = jnp.left_shift(1, grp)
    cnt = jnp.minimum(pos + halfw, n_seq) - jnp.maximum(pos - halfw, 0)
    dpool = wsum / cnt.astype(f32) - mf[:, 256:512]
    yc = jnp.dot(dpool.astype(bf16), pw_ref[...], preferred_element_type=f32) * ps_ref[...]

    ye = extended(768, 1024)
    cw = cw_ref[...]
    z = shifted(ye, 1) * cw[0:1, :] + ye * cw[1:2, :] + shifted(ye, -1) * cw[2:3, :]
    yd = mf[:, 512:768] * z[sl]

    hgrp = lax.broadcasted_iota(i32, (SGU_CHUNK, SGU_W), 1) >> 6
    ybs = []
    for cidx in range(bt // SGU_CHUNK):
        rows = slice(cidx * SGU_CHUNK, (cidx + 1) * SGU_CHUNK)
        vn_c = vn_ref[rows, :]
        s = jnp.zeros((SGU_CHUNK, SGU_W), f32)
        for hh in range(SGU_HEADS):
            sh = jnp.dot(ws_ref[hh], vn_c, preferred_element_type=f32)
            s = jnp.where(hgrp == hh, sh, s)
        ybs.append(u[rows, :] * (s + sb_ref[...]))
    yb = jnp.concatenate(ybs, axis=0)

    mix = jnp.concatenate([attn_ref[...], yb.astype(bf16), yc.astype(bf16), yd.astype(bf16)], axis=1)
    mo = jnp.dot(mix, wo_ref[...], preferred_element_type=f32)
    xm = x_ref[...] + g1_ref[...] * mo
    xmid_ref[...] = xm

    y = xm * lax.rsqrt(jnp.mean(xm * xm, axis=-1, keepdims=True) + EPS) * n2g_ref[...]
    h2 = y * (1.0 + sc2_ref[...]) + sh2_ref[...]
    _store_packed_tiles(h2_ref, h2)

    lt = lax.dot_general(rw_ref[...], h2, (((1,), (1,)), ((), ())),
                         precision=lax.Precision.HIGHEST, preferred_element_type=f32) + rb_ref[...]
    n_exp = lt.shape[0]
    eidx = lax.broadcasted_iota(i32, lt.shape, 0)
    work = lt
    idxs, vals = [], []
    for _ in range(TOP_K):
        m = jnp.max(work, axis=0, keepdims=True)
        idx = jnp.min(jnp.where(work == m, eidx, n_exp), axis=0, keepdims=True)
        idxs.append(idx)
        vals.append(m)
        work = jnp.where(eidx == idx, -jnp.inf, work)
    exps = [jnp.exp(v - vals[0]) for v in vals]
    den = exps[0] + exps[1] + exps[2] + exps[3]
    onehot = jnp.zeros(lt.shape, f32)
    for kk in range(TOP_K):
        ti_ref[kk:kk + 1, :] = idxs[kk]
        gt_ref[kk:kk + 1, :] = exps[kk] / den
        onehot = onehot + (eidx == idxs[kk]).astype(f32)
    tri = (lax.broadcasted_iota(i32, (bt, bt), 0) < lax.broadcasted_iota(i32, (bt, bt), 1)).astype(bf16)
    before = jnp.dot(onehot.astype(bf16), tri, preferred_element_type=f32)
    for kk in range(TOP_K):
        rk_ref[kk:kk + 1, :] = jnp.sum(jnp.where(eidx == idxs[kk], before, 0.0), axis=0, keepdims=True).astype(i32)
    cnt_ref[...] = jnp.sum(onehot, axis=1, keepdims=True)


def _mixer(attn, vn, mixf, x2, n_seq, g1, sh2, sc2, n2g, ws_bf, sgu_bias, pool_bd, pool_scale, conv_w,
           wo_bf, rw_t, rb):
    t, d = x2.shape
    bt = min(MIXER_TOKENS, n_seq)
    tps = n_seq // bt
    hb = bt // HALO
    n_halo = t // HALO
    n_exp = rw_t.shape[0]
    row = lambda i: (i, 0)
    per_batch = pl.BlockSpec((None, 1, d), lambda i: (i // tps, 0, 0))
    col = lambda i: (0, i)
    return pl.pallas_call(
        functools.partial(_mixer_kernel, n_seq=n_seq, bt=bt),
        grid=(t // bt,),
        in_specs=[pl.BlockSpec((bt, ATT_Q_W), row), pl.BlockSpec((bt, SGU_W), row), pl.BlockSpec((bt, 1024), row),
                  pl.BlockSpec((HALO, 1024), lambda i: (jnp.maximum(i * hb - 1, 0), 0)),
                  pl.BlockSpec((HALO, 1024), lambda i: (jnp.minimum((i + 1) * hb, n_halo - 1), 0)),
                  pl.BlockSpec((bt, d), row), per_batch, per_batch, per_batch, _full(n2g),
                  _full(ws_bf), _full(sgu_bias), _full(pool_bd), _full(pool_scale), _full(conv_w),
                  _full(wo_bf), _full(rw_t), _full(rb)],
        out_specs=[pl.BlockSpec((bt, d), row), pl.BlockSpec((bt * PACKED_ROWS, LANES), row),
                   pl.BlockSpec((TOP_K, bt), col), pl.BlockSpec((TOP_K, bt), col), pl.BlockSpec((TOP_K, bt), col),
                   pl.BlockSpec((None, n_exp, 1), lambda i: (i, 0, 0))],
        out_shape=[jax.ShapeDtypeStruct((t, d), f32), jax.ShapeDtypeStruct((t * PACKED_ROWS, LANES), u32),
                   jax.ShapeDtypeStruct((TOP_K, t), i32), jax.ShapeDtypeStruct((TOP_K, t), f32),
                   jax.ShapeDtypeStruct((TOP_K, t), i32), jax.ShapeDtypeStruct((t // bt, n_exp, 1), f32)],
        compiler_params=_params("arbitrary"),
        name="mixer_router",
    )(attn, vn, mixf, mixf, mixf, x2, g1, sh2, sc2, n2g, ws_bf, sgu_bias, pool_bd, pool_scale, conv_w,
      wo_bf, rw_t, rb)


def _stage_index_kernel(ti_ref, rk_ref, off_ref, o_ref, *, w0, n_exp):
    base = (w0 + pl.program_id(0)) * n_exp
    ti = ti_ref[...]
    acc = rk_ref[...]
    for e in range(n_exp):
        acc = acc + jnp.where(ti == e, off_ref[base + e], 0)
    o_ref[...] = acc


def _stage_index(top_i, rank, run_off, bt, w0, n_exp):
    k, t = top_i.shape
    col = lambda i: (0, i)
    return pl.pallas_call(
        functools.partial(_stage_index_kernel, w0=w0, n_exp=n_exp),
        grid=(t // bt,),
        in_specs=[pl.BlockSpec((k, bt), col), pl.BlockSpec((k, bt), col), pl.BlockSpec(memory_space=pltpu.SMEM)],
        out_specs=pl.BlockSpec((k, bt), col),
        out_shape=jax.ShapeDtypeStruct((k, t), i32),
        compiler_params=_params("arbitrary"),
        name="stage_index",
    )(top_i, rank, run_off)


def _tile_rows(t, n=1, rows=SUBLANES):
    return pl.ds(pl.multiple_of(t * rows, rows), n * rows)


def _tile_copy(src, s, dst, d, sem, rows=SUBLANES):
    return pltpu.make_async_copy(src.at[_tile_rows(s, 1, rows)], dst.at[_tile_rows(d, 1, rows)], sem)


def _for_each_run_piece(runs_ref, w, n_exp, max_len, fn):
    bits = [1 << b for b in range(max_len.bit_length() - 1, -1, -1)]

    def per_expert(e, carry):
        base = (w * n_exp + e) * 3
        stage0, slot0, length = runs_ref[base], runs_ref[base + 1], runs_ref[base + 2]
        done = 0
        for bit in bits:
            take = length & bit

            @pl.when(take != 0)
            def _():
                fn(stage0 + done, slot0 + done, bit)

            done = done + take
        return carry

    lax.fori_loop(0, n_exp, per_expert, 0)


def _wait_window(stage, buf, sem):
    pltpu.make_async_copy(stage.at[buf], stage.at[buf], sem.at[buf]).wait()


def _token_major(table):
    return table.T.reshape(-1)


def _per_buffer(slot, fn):
    for b in range(2):
        @pl.when(slot == b)
        def _():
            fn(b)


def _split_slabs(w_ref, g_ref, l_ref, scr):
    f = g_ref.shape[1]
    for s in range(w_ref.shape[0] // LANES):
        rows = slice(s * LANES, (s + 1) * LANES)
        scr[...] = w_ref[rows, :].T
        g_ref[rows, :] = scr[pl.ds(0, f, stride=2), :].T.astype(bf16)
        l_ref[rows, :] = scr[pl.ds(1, f, stride=2), :].T.astype(bf16)


def _split_w1(w1, layer):
    _, n_exp, d, f2 = w1.shape
    f = f2 // 2
    rows = 2 * LANES
    blk = pl.BlockSpec((None, rows, f), lambda e, j: (e, j, 0))
    return pl.pallas_call(
        _split_slabs,
        grid=(n_exp, d // rows),
        in_specs=[pl.BlockSpec((None, None, rows, f2), lambda e, j: (layer, e, j, 0))],
        out_specs=[blk, blk],
        out_shape=[jax.ShapeDtypeStruct((n_exp, d, f), bf16)] * 2,
        scratch_shapes=[pltpu.VMEM((f2, LANES), f32)],
        compiler_params=_params("arbitrary", "arbitrary"),
        name="split_w1",
    )(w1)


class _SplitSide:
    def __init__(self, w1, layer, n_steps):
        _, n_exp, d, f2 = w1.shape
        fits = [r for r in (LANES, 2 * LANES, 4 * LANES, 8 * LANES) if d % r == 0 and n_exp * (d // r) <= n_steps]
        self.ok = bool(fits)
        if not self.ok:
            return
        rows = fits[0]
        per = d // rows
        self.w1 = w1
        self.n_side = n_exp * per
        last = self.n_side - 1
        self.in_spec = pl.BlockSpec((None, None, rows, f2),
                                    lambda i: (layer, jnp.minimum(i, last) // per, jnp.minimum(i, last) % per, 0))
        self.out_spec = pl.BlockSpec((None, rows, f2 // 2),
                                     lambda i: (jnp.minimum(i, last) // per, jnp.minimum(i, last) % per, 0))
        self.out_shape = jax.ShapeDtypeStruct((n_exp, d, f2 // 2), bf16)
        self.scratch = pltpu.VMEM((f2, LANES), f32)

    def run(self, w_ref, g_ref, l_ref, scr):
        @pl.when(pl.program_id(0) < self.n_side)
        def _():
            _split_slabs(w_ref, g_ref, l_ref, scr)


def _dispatch_kernel(sidx_ref, runs_ref, pad_ref, h_ref, *rest, bt, w0, n_exp, fill_pad, side):
    rest = list(rest)
    if not fill_pad:
        rest.pop(0)
    w_ref = rest.pop(0) if side else None
    xs_ref = rest.pop(0)
    g_ref, l_ref = (rest.pop(0), rest.pop(0)) if side else (None, None)
    zero_ref = rest.pop(0) if fill_pad else None
    stage, sem = rest.pop(0), rest.pop(0)
    i = pl.program_id(0)
    slot = i % 2

    if fill_pad:
        @pl.when(i == 0)
        def _():
            zero_ref[...] = jnp.zeros_like(zero_ref)

            def per_expert(e, carry):
                first, count = pad_ref[e], pad_ref[n_exp + e]

                def issue(r, c):
                    _tile_copy(zero_ref, 0, xs_ref, first + r, sem.at[0], PACKED_ROWS).start()
                    return c

                def drain(r, c):
                    _tile_copy(zero_ref, 0, xs_ref, 0, sem.at[0], PACKED_ROWS).wait()
                    return c

                lax.fori_loop(0, count, issue, 0)
                lax.fori_loop(0, count, drain, 0)
                return carry

            lax.fori_loop(0, n_exp, per_expert, 0)

    def run_copy(buf):
        def make(stage_tile, slot_tile, n):
            return pltpu.make_async_copy(stage.at[buf, _tile_rows(stage_tile, n, PACKED_ROWS)],
                                         xs_ref.at[_tile_rows(slot_tile, n, PACKED_ROWS)], sem.at[buf])
        return make

    def place_all(b):
        def place(t, carry):
            tile = h_ref[_tile_rows(t, 1, PACKED_ROWS), :]
            for kk in range(TOP_K):
                row = pl.multiple_of(sidx_ref[t * TOP_K + kk], PACKED_ROWS)
                stage[b, pl.ds(row, PACKED_ROWS), :] = tile
            return carry

        lax.fori_loop(0, bt, place, 0, unroll=ROW_UNROLL)

    _per_buffer(slot, place_all)
    _for_each_run_piece(runs_ref, w0 + i, n_exp, bt, lambda a, b, n: run_copy(slot)(a, b, n).start())
    if side:
        side.run(w_ref, g_ref, l_ref, rest.pop(0))

    @pl.when(i > 0)
    def _():
        _wait_window(stage, 1 - slot, sem)

    @pl.when(i == pl.num_programs(0) - 1)
    def _():
        _wait_window(stage, slot, sem)


def _dispatch(sidx, runs, pad, h2, xs, n_slots, bt, w0, n_exp, w1_side=None):
    t = h2.shape[0] // PACKED_ROWS
    n_steps = t // bt
    fill_pad = xs is None
    side = _SplitSide(*w1_side, n_steps) if w1_side else None
    side = side if side is not None and side.ok else None
    smem = pl.BlockSpec(memory_space=pltpu.SMEM)
    in_specs = [pl.BlockSpec((TOP_K * bt,), lambda i: (i,), memory_space=pltpu.SMEM), smem, smem,
                pl.BlockSpec((bt * PACKED_ROWS, LANES), lambda i: (i, 0))]
    args = [_token_major(sidx * PACKED_ROWS), runs, pad, h2]
    out_specs = [pl.BlockSpec(memory_space=pl.ANY)]
    out_shape = [jax.ShapeDtypeStruct((n_slots * PACKED_ROWS, LANES), u32)]
    scratch = [pltpu.VMEM((2, TOP_K * bt * PACKED_ROWS, LANES), u32), pltpu.SemaphoreType.DMA((2,))]
    if fill_pad:
        scratch = [pltpu.VMEM((PACKED_ROWS, LANES), u32)] + scratch
    else:
        in_specs.append(pl.BlockSpec(memory_space=pl.ANY))
        args.append(xs)
    if side:
        in_specs.append(side.in_spec)
        args.append(side.w1)
        out_specs += [side.out_spec] * 2
        out_shape += [side.out_shape] * 2
        scratch.append(side.scratch)
    outs = pl.pallas_call(
        functools.partial(_dispatch_kernel, bt=bt, w0=w0, n_exp=n_exp, fill_pad=fill_pad, side=side),
        grid=(n_steps,),
        in_specs=in_specs,
        out_specs=out_specs,
        out_shape=out_shape,
        scratch_shapes=scratch,
        input_output_aliases={} if fill_pad else {4: 0},
        compiler_params=_params("arbitrary"),
        name="dispatch",
    )(*args)
    return outs[0], (tuple(outs[1:]) if side else None)


def _expert_kernel(be_ref, nu_ref, x_ref, w1g_ref, w1l_ref, b1g_ref, b1l_ref, w2f_ref, b2_ref, y_ref, w2_ref):
    i = pl.program_id(0)
    active = i < nu_ref[0]
    new_expert = jnp.logical_or(i == 0, be_ref[i] != be_ref[jnp.maximum(i - 1, 0)])

    @pl.when(jnp.logical_and(active, new_expert))
    def _():
        w2_ref[...] = w2f_ref[...].astype(bf16)

    @pl.when(active)
    def _():
        x = _load_packed_tiles(x_ref, x_ref.shape[0] // PACKED_ROWS)
        zg = jnp.dot(x, w1g_ref[...], preferred_element_type=f32) + b1g_ref[...]
        zl = jnp.dot(x, w1l_ref[...], preferred_element_type=f32) + b1l_ref[...]
        g = jnp.minimum(zg, SWIGLU_LIMIT)
        lin = jnp.clip(zl, -SWIGLU_LIMIT, SWIGLU_LIMIT)
        a = g * jax.nn.sigmoid(SWIGLU_ALPHA * g) * (lin + 1.0)
        y = jnp.dot(a.astype(bf16), w2_ref[...], preferred_element_type=f32) + b2_ref[...]
        _store_token_tiles(y_ref, y)

    @pl.when(i >= nu_ref[0])
    def _():
        y_ref[...] = jnp.zeros_like(y_ref)


def _experts(block_e, n_used, xs, w1g, w1l, b1g, b1l, w2_all, layer, b2):
    n_exp, d, f = w1g.shape
    bm = EXPERT_ROWS
    n_slots = xs.shape[0] // PACKED_ROWS
    blk = (bm * SUBLANES, LANES)
    xrow = lambda i, be, nu: (jnp.minimum(i, nu[0] - 1), 0)
    wsel = lambda i, be, nu: (be[i], 0, 0)
    grid_spec = pltpu.PrefetchScalarGridSpec(
        num_scalar_prefetch=2,
        grid=(n_slots // bm,),
        in_specs=[pl.BlockSpec((bm * PACKED_ROWS, LANES), xrow),
                  pl.BlockSpec((None, d, f), wsel), pl.BlockSpec((None, d, f), wsel),
                  pl.BlockSpec((None, 1, f), wsel), pl.BlockSpec((None, 1, f), wsel),
                  pl.BlockSpec((None, None, f, d), lambda i, be, nu: (layer, be[i], 0, 0)),
                  pl.BlockSpec((None, 1, d), wsel)],
        out_specs=pl.BlockSpec(blk, lambda i, be, nu: (i, 0)),
        scratch_shapes=[pltpu.VMEM((f, d), bf16)],
    )
    return pl.pallas_call(
        _expert_kernel,
        grid_spec=grid_spec,
        out_shape=jax.ShapeDtypeStruct((n_slots * SUBLANES, LANES), f32),
        compiler_params=_params("arbitrary"),
        name="experts",
    )(block_e, n_used, xs, w1g, w1l, b1g, b1l, w2_all, b2)


def _combine_kernel(sidx_ref, gate_ref, runs_ref, ys_ref, x_ref, g2_ref, fg_ref, *rest, bt, w0, n_exp, final, side):
    rest = list(rest)
    w_ref = rest.pop(0) if side else None
    o_ref = rest.pop(0)
    g_ref, l_ref = (rest.pop(0), rest.pop(0)) if side else (None, None)
    stage, acc_ref, sem = rest.pop(0), rest.pop(0), rest.pop(0)
    i = pl.program_id(0)
    slot = i % 2

    def run_copy(buf):
        def make(stage_tile, slot_tile, n):
            return pltpu.make_async_copy(ys_ref.at[_tile_rows(slot_tile, n)], stage.at[buf, _tile_rows(stage_tile, n)],
                                         sem.at[buf])
        return make

    def fetch(w, buf):
        _for_each_run_piece(runs_ref, w, n_exp, bt, lambda a, b, n: run_copy(buf)(a, b, n).start())

    @pl.when(i == 0)
    def _():
        fetch(w0, 0)

    @pl.when(i + 1 < pl.num_programs(0))
    def _():
        fetch(w0 + i + 1, 1 - slot)

    if side:
        side.run(w_ref, g_ref, l_ref, rest.pop(0))

    _wait_window(stage, slot, sem)

    def sum_all(b):
        def token(t, carry):
            acc = None
            for kk in range(TOP_K):
                row = pl.multiple_of(sidx_ref[t * TOP_K + kk], SUBLANES)
                term = gate_ref[t * TOP_K + kk] * stage[b, pl.ds(row, SUBLANES), :]
                acc = term if acc is None else acc + term
            acc_ref[_tile_rows(t), :] = acc
            return carry

        lax.fori_loop(0, bt, token, 0, unroll=ROW_UNROLL)

    _per_buffer(slot, sum_all)
    xo = x_ref[...] + g2_ref[...] * _load_token_tiles(acc_ref, bt)
    if final:
        xo = xo * lax.rsqrt(jnp.mean(xo * xo, axis=-1, keepdims=True) + EPS) * fg_ref[...]
    o_ref[...] = xo


def _combine(sidx, gates, runs, ys, x_mid, n_seq, g2, final_g, final, bt, w0, n_exp, w1_side=None):
    t, d = x_mid.shape
    tps = n_seq // bt
    row = lambda i: (i, 0)
    n_steps = t // bt
    side = _SplitSide(*w1_side, n_steps) if w1_side else None
    side = side if side is not None and side.ok else None
    per_step = pl.BlockSpec((TOP_K * bt,), lambda i: (i,), memory_space=pltpu.SMEM)
    in_specs = [per_step, per_step, pl.BlockSpec(memory_space=pltpu.SMEM),
                pl.BlockSpec(memory_space=pl.ANY),
                pl.BlockSpec((bt, d), row),
                pl.BlockSpec((None, 1, d), lambda i: (i // tps, 0, 0)),
                _full(final_g)]
    args = [_token_major(sidx * SUBLANES), _token_major(gates), runs, ys, x_mid, g2, final_g]
    out_specs = [pl.BlockSpec((bt, d), row)]
    out_shape = [jax.ShapeDtypeStruct((t, d), f32)]
    scratch = [pltpu.VMEM((2, TOP_K * bt * SUBLANES, LANES), f32), pltpu.VMEM((bt * SUBLANES, LANES), f32),
               pltpu.SemaphoreType.DMA((2,))]
    if side:
        in_specs.append(side.in_spec)
        args.append(side.w1)
        out_specs += [side.out_spec] * 2
        out_shape += [side.out_shape] * 2
        scratch.append(side.scratch)
    outs = pl.pallas_call(
        functools.partial(_combine_kernel, bt=bt, w0=w0, n_exp=n_exp, final=final, side=side),
        grid=(n_steps,),
        in_specs=in_specs,
        out_specs=out_specs,
        out_shape=out_shape,
        scratch_shapes=scratch,
        compiler_params=_params("arbitrary"),
        name="combine",
    )(*args)
    return outs[0], (tuple(outs[1:]) if side else None)


def _rope_tables(n_tokens):
    rows = n_tokens // GRID_W
    row = jnp.repeat(jnp.arange(rows), GRID_W).astype(f32)
    col = jnp.tile(jnp.arange(GRID_W), rows).astype(f32)
    n_freq = HEAD_DIM // 4
    inv = ROPE_BASE ** (-jnp.arange(n_freq, dtype=f32) / n_freq)
    ang = jnp.concatenate([row[:, None] * inv, col[:, None] * inv], axis=-1)
    cos, sin = jnp.cos(ang), jnp.sin(ang)
    return jnp.tile(cos, (1, 4)), jnp.concatenate([-sin, sin, -sin, sin], axis=-1)


def kernel(x, c, ctx, c_ctx, norm1_g, norm2_g, ada_w, ada_b, w_in, attn_sink, sgu_ws, sgu_b, sgu_ln_g, sgu_ln_b,
           pool_w, pool_scale, conv_w, w_out, router_w, router_b, exp_w1, exp_b1, exp_w2, exp_b2, final_g):
    n_batch, n_seq, d = x.shape
    n_ctx = ctx.shape[1]
    depth = ada_w.shape[0]
    n_exp = router_w.shape[2]
    assert d == SUBLANES * LANES, "token rows are moved as single (8, 128) tiles"
    t_lat, t_ctx = n_batch * n_seq, n_batch * n_ctx
    bm = EXPERT_ROWS

    cvec = jnp.concatenate([c, c_ctx[None, :], jnp.zeros((SUBLANES - n_batch - 1, d), f32)], axis=0)
    mods = _ada(cvec, ada_w, ada_b)

    cos_l, sin_l = _rope_tables(n_seq)
    cos_c, sin_c = jnp.ones((n_ctx, LANES), f32), jnp.zeros((n_ctx, LANES), f32)

    xl = x.reshape(t_lat, d)
    xc = ctx.reshape(t_ctx, d)
    row2 = lambda a: a.reshape(1, -1)
    w1_split = {}

    for l in range(depth):
        last = l == depth - 1
        ml = mods[l, :n_batch].reshape(n_batch, 6, 1, d)
        mc = jnp.broadcast_to(mods[l, n_batch].reshape(1, 6, 1, d), (n_batch, 6, 1, d))
        sh1l, sc1l, g1l, sh2l, sc2l, g2l = (ml[:, i] for i in range(6))
        sh1c, sc1c, g1c, sh2c, sc2c, g2c = (mc[:, i] for i in range(6))

        w_in_bf = w_in[l].astype(bf16)
        wo_bf = w_out[l].astype(bf16)
        ws_bf = sgu_ws[l].astype(bf16)
        sgu_bias = jnp.repeat(sgu_b[l].T, SGU_W // SGU_HEADS, axis=1)
        pool_bd = jax.scipy.linalg.block_diag(*[pool_w[l, g] for g in range(pool_w.shape[1])]).astype(bf16)
        n1g, n2g = row2(norm1_g[l]), row2(norm2_g[l])
        lng, lnb, psc = row2(sgu_ln_g[l]), row2(sgu_ln_b[l]), row2(pool_scale[l])
        rw_t = router_w[l].T
        rb = router_b[l].reshape(n_exp, 1)
        sink = attn_sink[l]
        mix_w = (ws_bf, sgu_bias, pool_bd, psc, conv_w[l], wo_bf, rw_t, rb)

        qc, kvc, vnc, mfc = _inproj(xc, n_ctx, sh1c, sc1c, n1g, w_in_bf, cos_c, sin_c, lng, lnb)
        ql, kvl, vnl, mfl = _inproj(xl, n_seq, sh1l, sc1l, n1g, w_in_bf, cos_l, sin_l, lng, lnb)
        attn_l = _window_attn(ql, kvl, kvc, sink, n_batch, n_seq, n_ctx)
        bt_l, bt_c = min(MIXER_TOKENS, n_seq), min(MIXER_TOKENS, n_ctx)
        xmid_l, h2_l, ti_l, gt_l, rk_l, cnt = _mixer(attn_l, vnl, mfl, xl, n_seq, g1l, sh2l, sc2l, n2g, *mix_w)
        n_win_l = cnt.shape[0]
        t_all = t_lat
        if not last:
            attn_c = _ctx_attn(qc, kvc, sink, n_batch, n_ctx)
            xmid_c, h2_c, ti_c, gt_c, rk_c, cnt_c = _mixer(attn_c, vnc, mfc, xc, n_ctx, g1c, sh2c, sc2c, n2g, *mix_w)
            cnt = jnp.concatenate([cnt, cnt_c], axis=0)
            t_all = t_lat + t_ctx

        win_cnt = cnt[:, :, 0].astype(i32)
        counts = jnp.sum(win_cnt, axis=0)
        padded = (counts + bm - 1) // bm * bm
        pends = jnp.cumsum(padded)
        pstarts = pends - padded
        n_blocks = -(-(t_all * TOP_K) // bm) + n_exp
        starts = jnp.arange(n_blocks, dtype=i32) * bm
        block_e = jnp.minimum(jnp.sum((pends[None, :] <= starts[:, None]).astype(i32), axis=1), n_exp - 1)
        n_used = (pends[-1:] // bm).astype(i32)
        pad = jnp.concatenate([pstarts + counts, padded - counts]).astype(i32)
        run_stage = jnp.cumsum(win_cnt, axis=1) - win_cnt
        run_slot = pstarts[None, :] + jnp.cumsum(win_cnt, axis=0) - win_cnt
        runs = jnp.stack([run_stage, run_slot, win_cnt], axis=-1).reshape(-1)
        run_off = run_stage.reshape(-1)
        sidx_l = _stage_index(ti_l, rk_l, run_off, bt_l, 0, n_exp)
        if not last:
            sidx_c = _stage_index(ti_c, rk_c, run_off, bt_c, n_win_l, n_exp)

        xs, split = _dispatch(sidx_l, runs, pad, h2_l, None, n_blocks * bm, bt_l, 0, n_exp,
                              None if l in w1_split else (exp_w1, l))
        if l not in w1_split:
            w1_split[l] = split or _split_w1(exp_w1, l)
        if not last:
            xs, _ = _dispatch(sidx_c, runs, pad, h2_c, xs, n_blocks * bm, bt_c, n_win_l, n_exp)

        w1g, w1l = w1_split[l]
        b1 = exp_b1[l]
        b1g, b1l = b1[:, None, 0::2], b1[:, None, 1::2]
        ys = _experts(block_e, n_used, xs, w1g, w1l, b1g, b1l, exp_w2, l, exp_b2[l][:, None, :])

        fg = row2(final_g)
        xl, split = _combine(sidx_l, gt_l, runs, ys, xmid_l, n_seq, g2l, fg, last, bt_l, 0, n_exp,
                             None if last else (exp_w1, l + 1))
        if split:
            w1_split[l + 1] = split
        if not last:
            xc, _ = _combine(sidx_c, gt_c, runs, ys, xmid_c, n_ctx, g2c, fg, False, bt_c, n_win_l, n_exp)

    return xl.reshape(n_batch, n_seq, d)
```

```python
import functools

import jax
import jax.numpy as jnp
from jax import lax
from jax.experimental import pallas as pl
from jax.experimental.pallas import tpu as pltpu

f32 = jnp.float32
bf16 = jnp.bfloat16
i32 = jnp.int32
u32 = jnp.uint32

GRID_W = 64
EPS = 1e-6
N_Q_HEADS = 8
HEAD_DIM = 64
WINDOW = 128
ROPE_BASE = 10000.0
ATT_Q_W = 512
ATT_KV_W = 128
SGU_HEADS = 4
SGU_W = 256
SGU_CHUNK = 128
POOL_CH = 256
CONV_CH = 256
MIX_WIDTH = 1280
TOP_K = 4
SWIGLU_LIMIT = 7.0
SWIGLU_ALPHA = 1.702
SQRT_HALF = 0.7071067811865476

LANES = 128
SUBLANES = 8
PACKED_ROWS = SUBLANES // 2
VMEM_LIMIT_BYTES = 56 * 1024 * 1024

INPROJ_TOKENS = 512
ATTN_TOKENS = 128
MIXER_TOKENS = 512
EXPERT_ROWS = 512
HALO = 8
ROW_UNROLL = 8


def _params(*sem):
    return pltpu.CompilerParams(dimension_semantics=sem, vmem_limit_bytes=VMEM_LIMIT_BYTES)


def _full(a):
    nd = a.ndim
    return pl.BlockSpec(a.shape, lambda *_: (0,) * nd)


def _store_token_tiles(ref, val):
    n = val.shape[0]
    for s in range(SUBLANES):
        ref[pl.ds(s, n, stride=SUBLANES), :] = val[:, s * LANES:(s + 1) * LANES]


def _load_token_tiles(ref, n):
    return jnp.concatenate([ref[pl.ds(s, n, stride=SUBLANES), :] for s in range(SUBLANES)], axis=1)


HIGH_HALF = 0xFFFF0000


def _store_packed_tiles(ref, val):
    n = val.shape[0]
    half = PACKED_ROWS * LANES
    for s in range(PACKED_ROWS):
        lo = val[:, s * LANES:(s + 1) * LANES].astype(bf16).astype(f32)
        hi = val[:, half + s * LANES:half + (s + 1) * LANES].astype(bf16).astype(f32)
        lo_bits = lax.shift_right_logical(lax.bitcast_convert_type(lo, u32), jnp.uint32(16))
        hi_bits = lax.bitcast_convert_type(hi, u32) & jnp.uint32(HIGH_HALF)
        ref[pl.ds(s, n, stride=PACKED_ROWS), :] = lo_bits | hi_bits


def _load_packed_tiles(ref, n):
    los, his = [], []
    for s in range(PACKED_ROWS):
        w = ref[pl.ds(s, n, stride=PACKED_ROWS), :]
        los.append(lax.bitcast_convert_type(lax.shift_left(w, jnp.uint32(16)), f32))
        his.append(lax.bitcast_convert_type(w & jnp.uint32(HIGH_HALF), f32))
    return jnp.concatenate(los + his, axis=1).astype(bf16)


def _gelu(x):
    return 0.5 * x * (1.0 + lax.erf(x * SQRT_HALF))


def _ada_kernel(c_ref, w_ref, b_ref, o_ref):
    c = c_ref[...]
    s = c * jax.nn.sigmoid(c)
    o_ref[...] = jnp.dot(s, w_ref[...], precision=lax.Precision.HIGHEST, preferred_element_type=f32) + b_ref[...]


def _ada(cvec, ada_w, ada_b):
    depth, d, n = ada_w.shape
    tn = 1536
    return pl.pallas_call(
        _ada_kernel,
        grid=(depth, n // tn),
        in_specs=[pl.BlockSpec(cvec.shape, lambda l, j: (0, 0)),
                  pl.BlockSpec((None, d, tn), lambda l, j: (l, 0, j)),
                  pl.BlockSpec((None, 1, tn), lambda l, j: (l, 0, j))],
        out_specs=pl.BlockSpec((None, cvec.shape[0], tn), lambda l, j: (l, 0, j)),
        out_shape=jax.ShapeDtypeStruct((depth, cvec.shape[0], n), f32),
        compiler_params=_params("arbitrary", "arbitrary"),
        name="ada_mod",
    )(cvec, ada_w, ada_b.reshape(depth, 1, n))


def _inproj_kernel(x_ref, sh_ref, sc_ref, g_ref, w_ref, cos_ref, sin_ref, lng_ref, lnb_ref,
                   q_ref, kv_ref, vn_ref, mixf_ref):
    x = x_ref[...]
    y = x * lax.rsqrt(jnp.mean(x * x, axis=-1, keepdims=True) + EPS) * g_ref[...]
    h = y * (1.0 + sc_ref[...]) + sh_ref[...]
    p = jnp.dot(h.astype(bf16), w_ref[...], preferred_element_type=f32)

    cos = cos_ref[...]
    sin = sin_ref[...]
    lane = lax.broadcasted_iota(i32, cos.shape, 1)
    first_half = (lane & (HEAD_DIM - 1)) < HEAD_DIM // 2

    def rope(t):
        partner = jnp.where(first_half, pltpu.roll(t, LANES - HEAD_DIM // 2, 1), pltpu.roll(t, HEAD_DIM // 2, 1))
        return t * cos + partner * sin

    scale = HEAD_DIM ** -0.5
    for m in range(ATT_Q_W // LANES):
        q_ref[:, m * LANES:(m + 1) * LANES] = (rope(p[:, m * LANES:(m + 1) * LANES]) * scale).astype(bf16)
    k = rope(p[:, 512:640])
    v = p[:, 640:768]
    kv_ref[:, 0:128] = k.astype(bf16)
    kv_ref[:, 128:256] = pltpu.roll(k, HEAD_DIM, 1).astype(bf16)
    kv_ref[:, 256:384] = v.astype(bf16)
    kv_ref[:, 384:512] = pltpu.roll(v, HEAD_DIM, 1).astype(bf16)

    u = _gelu(p[:, 768:1024])
    gv = _gelu(p[:, 1024:1280])
    mu = jnp.mean(gv, axis=-1, keepdims=True)
    var = jnp.mean(jnp.square(gv - mu), axis=-1, keepdims=True)
    vn_ref[...] = ((gv - mu) * lax.rsqrt(var + EPS) * lng_ref[...] + lnb_ref[...]).astype(bf16)

    mixf_ref[:, 0:256] = u
    mixf_ref[:, 256:512] = p[:, 1280:1536]
    mixf_ref[:, 512:768] = p[:, 1536:1792]
    mixf_ref[:, 768:1024] = p[:, 1792:2048] * p[:, 2048:2304]


def _inproj(x2, n_seq, shift, scale, g, w_bf, cos_t, sin_t, ln_g, ln_b):
    t, d = x2.shape
    bt = min(INPROJ_TOKENS, n_seq)
    tps = n_seq // bt
    ncol = w_bf.shape[1]
    row = lambda i: (i, 0)
    return pl.pallas_call(
        _inproj_kernel,
        grid=(t // bt,),
        in_specs=[pl.BlockSpec((bt, d), row),
                  pl.BlockSpec((None, 1, d), lambda i: (i // tps, 0, 0)),
                  pl.BlockSpec((None, 1, d), lambda i: (i // tps, 0, 0)),
                  _full(g),
                  pl.BlockSpec((d, ncol), lambda i: (0, 0)),
                  pl.BlockSpec((bt, LANES), lambda i: (i % tps, 0)),
                  pl.BlockSpec((bt, LANES), lambda i: (i % tps, 0)),
                  _full(ln_g), _full(ln_b)],
        out_specs=[pl.BlockSpec((bt, ATT_Q_W), row), pl.BlockSpec((bt, 512), row),
                   pl.BlockSpec((bt, SGU_W), row), pl.BlockSpec((bt, 1024), row)],
        out_shape=[jax.ShapeDtypeStruct((t, ATT_Q_W), bf16), jax.ShapeDtypeStruct((t, 512), bf16),
                   jax.ShapeDtypeStruct((t, SGU_W), bf16), jax.ShapeDtypeStruct((t, 1024), f32)],
        compiler_params=_params("arbitrary"),
        name="inproj",
    )(x2, shift, scale, g, w_bf, cos_t, sin_t, ln_g, ln_b)


def _attn_block(sink_ref, q, kv, bias, o_ref, row0):
    k_nat, k_swp, v_nat, v_swp = (kv[:, i * LANES:(i + 1) * LANES] for i in range(4))
    nq = q.shape[0]
    low = lax.broadcasted_iota(i32, (nq, LANES), 1) < HEAD_DIM
    top = lax.broadcasted_iota(i32, (2 * nq, 1), 0) < nq
    zero = jnp.zeros((nq, LANES), q.dtype)
    for kvh in range(2):
        chunks = [q[:, (2 * kvh + i) * LANES:(2 * kvh + i + 1) * LANES] for i in range(2)]
        outs = []
        for half in range(2):
            keep = low if half == 0 else jnp.logical_not(low)
            qz = jnp.concatenate([jnp.where(keep, c, zero) for c in chunks], axis=0)
            kh = k_nat if kvh == half else k_swp
            vh = v_nat if kvh == half else v_swp
            s = lax.dot_general(qz, kh, (((1,), (1,)), ((), ())), preferred_element_type=f32)
            if bias is not None:
                s = s + bias
            h0 = 4 * kvh + half
            sk = jnp.where(top, sink_ref[h0], sink_ref[h0 + 2])
            mx = jnp.maximum(jnp.max(s, axis=1, keepdims=True), sk)
            e = jnp.exp(s - mx)
            den = jnp.sum(e, axis=1, keepdims=True) + jnp.exp(sk - mx)
            outs.append(jnp.dot(e.astype(bf16), vh, preferred_element_type=f32) / den)
        for i in range(2):
            rows = slice(i * nq, (i + 1) * nq)
            m = 2 * kvh + i
            o_ref[row0:row0 + nq, m * LANES:(m + 1) * LANES] = jnp.where(low, outs[0][rows], outs[1][rows]).astype(bf16)


def _window_attn_kernel(sink_ref, q_ref, kvp_ref, kvm_ref, kvn_ref, kvx_ref, o_ref, *, nb):
    jj = pl.program_id(1)
    nq = ATTN_TOKENS
    nband = 3 * nq
    kvm = kvm_ref[...]
    kvx = kvx_ref[...]
    r = lax.broadcasted_iota(i32, (nq, nband), 0)
    c = lax.broadcasted_iota(i32, (nq, nband), 1)
    dlt = c - r
    in_window = (dlt >= 0) & (dlt <= 2 * WINDOW)
    ctx_zeros = jnp.zeros((nq, kvx.shape[0]), f32)
    for i, kv_band in enumerate((jnp.concatenate([kvp_ref[...], kvm], axis=0),
                                 jnp.concatenate([kvm, kvn_ref[...]], axis=0))):
        j = 2 * jj + i
        lo = jnp.where(j == 0, nq, 0)
        hi = jnp.where(j == nb - 1, 2 * nq, nband)
        valid = in_window & (c >= lo) & (c < hi)
        bias = jnp.concatenate([jnp.where(valid, 0.0, -jnp.inf).astype(f32), ctx_zeros], axis=1)
        _attn_block(sink_ref, q_ref[i * nq:(i + 1) * nq, :], jnp.concatenate([kv_band, kvx], axis=0),
                    jnp.concatenate([bias, bias], axis=0), o_ref, i * nq)


def _ctx_attn_kernel(sink_ref, q_ref, kvx_ref, o_ref):
    _attn_block(sink_ref, q_ref[...], kvx_ref[...], None, o_ref, 0)


def _window_attn(q, kv, kv_ctx, sink, n_batch, n_seq, n_ctx):
    t = q.shape[0]
    nb = n_seq // ATTN_TOKENS
    assert nb % 2 == 0
    nb2 = nb // 2
    one, two = (ATTN_TOKENS, 512), (2 * ATTN_TOKENS, 512)
    return pl.pallas_call(
        functools.partial(_window_attn_kernel, nb=nb),
        grid=(n_batch, nb2),
        in_specs=[pl.BlockSpec(memory_space=pltpu.SMEM),
                  pl.BlockSpec(two, lambda b, j: (b * nb2 + j, 0)),
                  pl.BlockSpec(one, lambda b, j: (b * nb + jnp.maximum(2 * j - 1, 0), 0)),
                  pl.BlockSpec(two, lambda b, j: (b * nb2 + j, 0)),
                  pl.BlockSpec(one, lambda b, j: (b * nb + jnp.minimum(2 * j + 2, nb - 1), 0)),
                  pl.BlockSpec((n_ctx, 512), lambda b, j: (b, 0))],
        out_specs=pl.BlockSpec(two, lambda b, j: (b * nb2 + j, 0)),
        out_shape=jax.ShapeDtypeStruct((t, ATT_Q_W), bf16),
        compiler_params=_params("arbitrary", "arbitrary"),
        name="window_attn",
    )(sink, q, kv, kv, kv, kv_ctx)


def _ctx_attn(q, kv_ctx, sink, n_batch, n_ctx):
    nb = n_ctx // ATTN_TOKENS
    blk = (ATTN_TOKENS, 512)
    return pl.pallas_call(
        _ctx_attn_kernel,
        grid=(n_batch, nb),
        in_specs=[pl.BlockSpec(memory_space=pltpu.SMEM),
                  pl.BlockSpec(blk, lambda b, j: (b * nb + j, 0)),
                  pl.BlockSpec((n_ctx, 512), lambda b, j: (b, 0))],
        out_specs=pl.BlockSpec(blk, lambda b, j: (b * nb + j, 0)),
        out_shape=jax.ShapeDtypeStruct(q.shape, bf16),
        compiler_params=_params("arbitrary", "arbitrary"),
        name="ctx_attn",
    )(sink, q, kv_ctx)


def _mixer_kernel(attn_ref, vn_ref, mf_ref, mfp_ref, mfn_ref, x_ref, g1_ref, sh2_ref, sc2_ref, n2g_ref,
                  ws_ref, sb_ref, pw_ref, ps_ref, cw_ref, wo_ref, rw_ref, rb_ref,
                  xmid_ref, h2_ref, *table_refs, n_seq, bt):
    s4_refs, s8_refs, gt_refs = (table_refs[j * TOP_K:(j + 1) * TOP_K] for j in range(3))
    cnt_ref = table_refs[3 * TOP_K]
    i = pl.program_id(0)
    tps = n_seq // bt
    si = i % tps
    first = si == 0
    last = si == tps - 1
    n_ext = bt + 2 * HALO

    mf = mf_ref[...]
    u = mf[:, 0:256]

    def extended(lo, hi):
        prev = jnp.where(first, 0.0, mfp_ref[:, lo:hi])
        nxt = jnp.where(last, 0.0, mfn_ref[:, lo:hi])
        return jnp.concatenate([prev, mf[:, lo:hi], nxt], axis=0)

    def shifted(a, s):
        return pltpu.roll(a, s % n_ext, 0)

    xe = extended(256, 512)
    a1 = shifted(xe, 1) + xe
    a2 = shifted(a1, 1) + shifted(a1, -1)
    a3 = shifted(a2, 2) + shifted(a2, -2)
    a4 = shifted(a3, 4) + shifted(a3, -4)
    lane = lax.broadcasted_iota(i32, (bt, POOL_CH), 1)
    grp = lane >> 6
    sl = slice(HALO, HALO + bt)
    wsum = jnp.where(grp == 0, a1[sl], jnp.where(grp == 1, a2[sl], jnp.where(grp == 2, a3[sl], a4[sl])))
    pos = lax.broadcasted_iota(i32, (bt, POOL_CH), 0) + si * bt
    halfw = jnp.left_shift(1, grp)
    cnt = jnp.minimum(pos + halfw, n_seq) - jnp.maximum(pos - halfw, 0)
    dpool = wsum / cnt.astype(f32) - mf[:, 256:512]
    yc = jnp.dot(dpool.astype(bf16), pw_ref[...], preferred_element_type=f32) * ps_ref[...]

    ye = extended(768, 1024)
    cw = cw_ref[...]
    z = shifted(ye, 1) * cw[0:1, :] + ye * cw[1:2, :] + shifted(ye, -1) * cw[2:3, :]
    yd = mf[:, 512:768] * z[sl]

    hgrp = lax.broadcasted_iota(i32, (SGU_CHUNK, SGU_W), 1) >> 6
    ybs = []
    for cidx in range(bt // SGU_CHUNK):
        rows = slice(cidx * SGU_CHUNK, (cidx + 1) * SGU_CHUNK)
        vn_c = vn_ref[rows, :]
        s = jnp.zeros((SGU_CHUNK, SGU_W), f32)
        for hh in range(SGU_HEADS):
            sh = jnp.dot(ws_ref[hh], vn_c, preferred_element_type=f32)
            s = jnp.where(hgrp == hh, sh, s)
        ybs.append(u[rows, :] * (s + sb_ref[...]))
    yb = jnp.concatenate(ybs, axis=0)

    mix = jnp.concatenate([attn_ref[...], yb.astype(bf16), yc.astype(bf16), yd.astype(bf16)], axis=1)
    mo = jnp.dot(mix, wo_ref[...], preferred_element_type=f32)
    xm = x_ref[...] + g1_ref[...] * mo
    xmid_ref[...] = xm

    y = xm * lax.rsqrt(jnp.mean(xm * xm, axis=-1, keepdims=True) + EPS) * n2g_ref[...]
    h2 = y * (1.0 + sc2_ref[...]) + sh2_ref[...]
    _store_packed_tiles(h2_ref, h2)

    lt = lax.dot_general(rw_ref[...], h2, (((1,), (1,)), ((), ())),
                         precision=lax.Precision.HIGHEST, preferred_element_type=f32) + rb_ref[...]
    n_exp = lt.shape[0]
    eidx = lax.broadcasted_iota(i32, lt.shape, 0)
    work = lt
    idxs, vals = [], []
    for _ in range(TOP_K):
        m = jnp.max(work, axis=0, keepdims=True)
        idx = jnp.min(jnp.where(work == m, eidx, n_exp), axis=0, keepdims=True)
        idxs.append(idx)
        vals.append(m)
        work = jnp.where(eidx == idx, -jnp.inf, work)
    exps = [jnp.exp(v - vals[0]) for v in vals]
    den = exps[0] + exps[1] + exps[2] + exps[3]
    onehot = jnp.zeros(lt.shape, f32)
    for kk in range(TOP_K):
        gt_refs[kk][...] = exps[kk] / den
        onehot = onehot + (eidx == idxs[kk]).astype(f32)
    tri = (lax.broadcasted_iota(i32, (bt, bt), 0) < lax.broadcasted_iota(i32, (bt, bt), 1)).astype(bf16)
    before = jnp.dot(onehot.astype(bf16), tri, preferred_element_type=f32)
    cnt = jnp.sum(onehot, axis=1, keepdims=True)
    row = lax.broadcasted_iota(i32, (n_exp, LANES), 0)
    incl = jnp.broadcast_to(cnt, (n_exp, LANES))
    shift = 1
    while shift < n_exp:
        incl = incl + jnp.where(row >= shift, pltpu.roll(incl, shift, 0), 0.0)
        shift *= 2
    place = before + (incl[:, 0:1] - cnt)
    for kk in range(TOP_K):
        tile = jnp.sum(jnp.where(eidx == idxs[kk], place, 0.0), axis=0, keepdims=True).astype(i32)
        s4_refs[kk][...] = tile * PACKED_ROWS
        s8_refs[kk][...] = tile * SUBLANES
    cnt_ref[...] = cnt


def _mixer(attn, vn, mixf, x2, n_seq, g1, sh2, sc2, n2g, ws_bf, sgu_bias, pool_bd, pool_scale, conv_w,
           wo_bf, rw_t, rb):
    t, d = x2.shape
    bt = min(MIXER_TOKENS, n_seq)
    tps = n_seq // bt
    hb = bt // HALO
    n_halo = t // HALO
    n_exp = rw_t.shape[0]
    row = lambda i: (i, 0)
    per_batch = pl.BlockSpec((None, 1, d), lambda i: (i // tps, 0, 0))
    col = lambda i: (0, i)
    outs = pl.pallas_call(
        functools.partial(_mixer_kernel, n_seq=n_seq, bt=bt),
        grid=(t // bt,),
        in_specs=[pl.BlockSpec((bt, ATT_Q_W), row), pl.BlockSpec((bt, SGU_W), row), pl.BlockSpec((bt, 1024), row),
                  pl.BlockSpec((HALO, 1024), lambda i: (jnp.maximum(i * hb - 1, 0), 0)),
                  pl.BlockSpec((HALO, 1024), lambda i: (jnp.minimum((i + 1) * hb, n_halo - 1), 0)),
                  pl.BlockSpec((bt, d), row), per_batch, per_batch, per_batch, _full(n2g),
                  _full(ws_bf), _full(sgu_bias), _full(pool_bd), _full(pool_scale), _full(conv_w),
                  _full(wo_bf), _full(rw_t), _full(rb)],
        out_specs=[pl.BlockSpec((bt, d), row), pl.BlockSpec((bt * PACKED_ROWS, LANES), row)]
        + [pl.BlockSpec((1, bt), col)] * (3 * TOP_K) + [pl.BlockSpec((None, n_exp, 1), lambda i: (i, 0, 0))],
        out_shape=[jax.ShapeDtypeStruct((t, d), f32), jax.ShapeDtypeStruct((t * PACKED_ROWS, LANES), u32)]
        + [jax.ShapeDtypeStruct((1, t), i32)] * (2 * TOP_K) + [jax.ShapeDtypeStruct((1, t), f32)] * TOP_K
        + [jax.ShapeDtypeStruct((t // bt, n_exp, 1), f32)],
        compiler_params=_params("arbitrary"),
        name="mixer_router",
    )(attn, vn, mixf, mixf, mixf, x2, g1, sh2, sc2, n2g, ws_bf, sgu_bias, pool_bd, pool_scale, conv_w,
      wo_bf, rw_t, rb)
    tables = [o.reshape(t) for o in outs[2:2 + 3 * TOP_K]]
    return outs[0], outs[1], tables[:TOP_K], tables[TOP_K:2 * TOP_K], tables[2 * TOP_K:], outs[-1]


def _per_assignment_specs(bt):
    return [pl.BlockSpec((bt,), lambda i: (i,), memory_space=pltpu.SMEM)] * TOP_K


def _tile_rows(t, n=1, rows=SUBLANES):
    return pl.ds(pl.multiple_of(t * rows, rows), n * rows)


def _tile_copy(src, s, dst, d, sem, rows=SUBLANES):
    return pltpu.make_async_copy(src.at[_tile_rows(s, 1, rows)], dst.at[_tile_rows(d, 1, rows)], sem)


def _for_each_run_piece(runs_ref, w, n_exp, max_len, fn):
    bits = [1 << b for b in range(max_len.bit_length() - 1, -1, -1)]

    def per_expert(e, carry):
        base = (w * n_exp + e) * 3
        stage0, slot0, length = runs_ref[base], runs_ref[base + 1], runs_ref[base + 2]
        done = 0
        for bit in bits:
            take = length & bit

            @pl.when(take != 0)
            def _():
                fn(stage0 + done, slot0 + done, bit)

            done = done + take
        return carry

    lax.fori_loop(0, n_exp, per_expert, 0)


def _wait_window(stage, buf, sem):
    pltpu.make_async_copy(stage.at[buf], stage.at[buf], sem.at[buf]).wait()


def _per_buffer(slot, fn):
    for b in range(2):
        @pl.when(slot == b)
        def _():
            fn(b)


def _split_slabs(w_ref, g_ref, l_ref, scr):
    f = g_ref.shape[1]
    for s in range(w_ref.shape[0] // LANES):
        rows = slice(s * LANES, (s + 1) * LANES)
        scr[...] = w_ref[rows, :].T
        g_ref[rows, :] = scr[pl.ds(0, f, stride=2), :].T.astype(bf16)
        l_ref[rows, :] = scr[pl.ds(1, f, stride=2), :].T.astype(bf16)


def _split_w1(w1, layer):
    _, n_exp, d, f2 = w1.shape
    f = f2 // 2
    rows = 2 * LANES
    blk = pl.BlockSpec((None, rows, f), lambda e, j: (e, j, 0))
    return pl.pallas_call(
        _split_slabs,
        grid=(n_exp, d // rows),
        in_specs=[pl.BlockSpec((None, None, rows, f2), lambda e, j: (layer, e, j, 0))],
        out_specs=[blk, blk],
        out_shape=[jax.ShapeDtypeStruct((n_exp, d, f), bf16)] * 2,
        scratch_shapes=[pltpu.VMEM((f2, LANES), f32)],
        compiler_params=_params("arbitrary", "arbitrary"),
        name="split_w1",
    )(w1)


class _SplitSide:
    def __init__(self, w1, layer, n_steps):
        _, n_exp, d, f2 = w1.shape
        fits = [r for r in (LANES, 2 * LANES, 4 * LANES, 8 * LANES) if d % r == 0 and n_exp * (d // r) <= n_steps]
        self.ok = bool(fits)
        if not self.ok:
            return
        rows = fits[0]
        per = d // rows
        self.w1 = w1
        self.n_side = n_exp * per
        last = self.n_side - 1
        self.in_spec = pl.BlockSpec((None, None, rows, f2),
                                    lambda i: (layer, jnp.minimum(i, last) // per, jnp.minimum(i, last) % per, 0))
        self.out_spec = pl.BlockSpec((None, rows, f2 // 2),
                                     lambda i: (jnp.minimum(i, last) // per, jnp.minimum(i, last) % per, 0))
        self.out_shape = jax.ShapeDtypeStruct((n_exp, d, f2 // 2), bf16)
        self.scratch = pltpu.VMEM((f2, LANES), f32)

    def run(self, w_ref, g_ref, l_ref, scr):
        @pl.when(pl.program_id(0) < self.n_side)
        def _():
            _split_slabs(w_ref, g_ref, l_ref, scr)


def _dispatch_kernel(*refs, bt, w0, n_exp, fill_pad, side):
    row_refs, (runs_ref, pad_ref, h_ref), rest = refs[:TOP_K], refs[TOP_K:TOP_K + 3], list(refs[TOP_K + 3:])
    if not fill_pad:
        rest.pop(0)
    w_ref = rest.pop(0) if side else None
    xs_ref = rest.pop(0)
    g_ref, l_ref = (rest.pop(0), rest.pop(0)) if side else (None, None)
    zero_ref = rest.pop(0) if fill_pad else None
    stage, sem = rest.pop(0), rest.pop(0)
    i = pl.program_id(0)
    slot = i % 2

    if fill_pad:
        @pl.when(i == 0)
        def _():
            zero_ref[...] = jnp.zeros_like(zero_ref)

            def per_expert(e, carry):
                first, count = pad_ref[e], pad_ref[n_exp + e]

                def issue(r, c):
                    _tile_copy(zero_ref, 0, xs_ref, first + r, sem.at[0], PACKED_ROWS).start()
                    return c

                def drain(r, c):
                    _tile_copy(zero_ref, 0, xs_ref, 0, sem.at[0], PACKED_ROWS).wait()
                    return c

                lax.fori_loop(0, count, issue, 0)
                lax.fori_loop(0, count, drain, 0)
                return carry

            lax.fori_loop(0, n_exp, per_expert, 0)

    def run_copy(buf):
        def make(stage_tile, slot_tile, n):
            return pltpu.make_async_copy(stage.at[buf, _tile_rows(stage_tile, n, PACKED_ROWS)],
                                         xs_ref.at[_tile_rows(slot_tile, n, PACKED_ROWS)], sem.at[buf])
        return make

    def place_all(b):
        def place(t, carry):
            tile = h_ref[_tile_rows(t, 1, PACKED_ROWS), :]
            for kk in range(TOP_K):
                row = pl.multiple_of(row_refs[kk][t], PACKED_ROWS)
                stage[b, pl.ds(row, PACKED_ROWS), :] = tile
            return carry

        lax.fori_loop(0, bt, place, 0, unroll=ROW_UNROLL)

    _per_buffer(slot, place_all)
    _for_each_run_piece(runs_ref, w0 + i, n_exp, bt, lambda a, b, n: run_copy(slot)(a, b, n).start())
    if side:
        side.run(w_ref, g_ref, l_ref, rest.pop(0))

    @pl.when(i > 0)
    def _():
        _wait_window(stage, 1 - slot, sem)

    @pl.when(i == pl.num_programs(0) - 1)
    def _():
        _wait_window(stage, slot, sem)


def _dispatch(rows, runs, pad, h2, xs, n_slots, bt, w0, n_exp, w1_side=None):
    t = h2.shape[0] // PACKED_ROWS
    n_steps = t // bt
    fill_pad = xs is None
    side = _SplitSide(*w1_side, n_steps) if w1_side else None
    side = side if side is not None and side.ok else None
    smem = pl.BlockSpec(memory_space=pltpu.SMEM)
    in_specs = _per_assignment_specs(bt) + [smem, smem, pl.BlockSpec((bt * PACKED_ROWS, LANES), lambda i: (i, 0))]
    args = list(rows) + [runs, pad, h2]
    out_specs = [pl.BlockSpec(memory_space=pl.ANY)]
    out_shape = [jax.ShapeDtypeStruct((n_slots * PACKED_ROWS, LANES), u32)]
    scratch = [pltpu.VMEM((2, TOP_K * bt * PACKED_ROWS, LANES), u32), pltpu.SemaphoreType.DMA((2,))]
    if fill_pad:
        scratch = [pltpu.VMEM((PACKED_ROWS, LANES), u32)] + scratch
    else:
        in_specs.append(pl.BlockSpec(memory_space=pl.ANY))
        args.append(xs)
    if side:
        in_specs.append(side.in_spec)
        args.append(side.w1)
        out_specs += [side.out_spec] * 2
        out_shape += [side.out_shape] * 2
        scratch.append(side.scratch)
    outs = pl.pallas_call(
        functools.partial(_dispatch_kernel, bt=bt, w0=w0, n_exp=n_exp, fill_pad=fill_pad, side=side),
        grid=(n_steps,),
        in_specs=in_specs,
        out_specs=out_specs,
        out_shape=out_shape,
        scratch_shapes=scratch,
        input_output_aliases={} if fill_pad else {TOP_K + 3: 0},
        compiler_params=_params("arbitrary"),
        name="dispatch",
    )(*args)
    return outs[0], (tuple(outs[1:]) if side else None)


def _expert_kernel(be_ref, nu_ref, x_ref, w1g_ref, w1l_ref, b1g_ref, b1l_ref, w2f_ref, b2_ref, y_ref, w2_ref):
    i = pl.program_id(0)
    active = i < nu_ref[0]
    new_expert = jnp.logical_or(i == 0, be_ref[i] != be_ref[jnp.maximum(i - 1, 0)])

    @pl.when(jnp.logical_and(active, new_expert))
    def _():
        w2_ref[...] = w2f_ref[...].astype(bf16)

    @pl.when(active)
    def _():
        x = _load_packed_tiles(x_ref, x_ref.shape[0] // PACKED_ROWS)
        zg = jnp.dot(x, w1g_ref[...], preferred_element_type=f32) + b1g_ref[...]
        zl = jnp.dot(x, w1l_ref[...], preferred_element_type=f32) + b1l_ref[...]
        g = jnp.minimum(zg, SWIGLU_LIMIT)
        lin = jnp.clip(zl, -SWIGLU_LIMIT, SWIGLU_LIMIT)
        a = g * jax.nn.sigmoid(SWIGLU_ALPHA * g) * (lin + 1.0)
        y = jnp.dot(a.astype(bf16), w2_ref[...], preferred_element_type=f32) + b2_ref[...]
        _store_token_tiles(y_ref, y)

    @pl.when(i >= nu_ref[0])
    def _():
        y_ref[...] = jnp.zeros_like(y_ref)


def _experts(block_e, n_used, xs, w1g, w1l, b1g, b1l, w2_all, layer, b2):
    n_exp, d, f = w1g.shape
    bm = EXPERT_ROWS
    n_slots = xs.shape[0] // PACKED_ROWS
    blk = (bm * SUBLANES, LANES)
    xrow = lambda i, be, nu: (jnp.minimum(i, nu[0] - 1), 0)
    wsel = lambda i, be, nu: (be[i], 0, 0)
    grid_spec = pltpu.PrefetchScalarGridSpec(
        num_scalar_prefetch=2,
        grid=(n_slots // bm,),
        in_specs=[pl.BlockSpec((bm * PACKED_ROWS, LANES), xrow),
                  pl.BlockSpec((None, d, f), wsel), pl.BlockSpec((None, d, f), wsel),
                  pl.BlockSpec((None, 1, f), wsel), pl.BlockSpec((None, 1, f), wsel),
                  pl.BlockSpec((None, None, f, d), lambda i, be, nu: (layer, be[i], 0, 0)),
                  pl.BlockSpec((None, 1, d), wsel)],
        out_specs=pl.BlockSpec(blk, lambda i, be, nu: (i, 0)),
        scratch_shapes=[pltpu.VMEM((f, d), bf16)],
    )
    return pl.pallas_call(
        _expert_kernel,
        grid_spec=grid_spec,
        out_shape=jax.ShapeDtypeStruct((n_slots * SUBLANES, LANES), f32),
        compiler_params=_params("arbitrary"),
        name="experts",
    )(block_e, n_used, xs, w1g, w1l, b1g, b1l, w2_all, b2)


def _combine_kernel(*refs, bt, w0, n_exp, final, side):
    row_refs, gate_refs = refs[:TOP_K], refs[TOP_K:2 * TOP_K]
    (runs_ref, ys_ref, x_ref, g2_ref, fg_ref), rest = refs[2 * TOP_K:2 * TOP_K + 5], list(refs[2 * TOP_K + 5:])
    w_ref = rest.pop(0) if side else None
    o_ref = rest.pop(0)
    g_ref, l_ref = (rest.pop(0), rest.pop(0)) if side else (None, None)
    stage, acc_ref, sem = rest.pop(0), rest.pop(0), rest.pop(0)
    i = pl.program_id(0)
    slot = i % 2

    def run_copy(buf):
        def make(stage_tile, slot_tile, n):
            return pltpu.make_async_copy(ys_ref.at[_tile_rows(slot_tile, n)], stage.at[buf, _tile_rows(stage_tile, n)],
                                         sem.at[buf])
        return make

    def fetch(w, buf):
        _for_each_run_piece(runs_ref, w, n_exp, bt, lambda a, b, n: run_copy(buf)(a, b, n).start())

    @pl.when(i == 0)
    def _():
        fetch(w0, 0)

    @pl.when(i + 1 < pl.num_programs(0))
    def _():
        fetch(w0 + i + 1, 1 - slot)

    if side:
        side.run(w_ref, g_ref, l_ref, rest.pop(0))

    _wait_window(stage, slot, sem)

    def sum_all(b):
        def token(t, carry):
            acc = None
            for kk in range(TOP_K):
                row = pl.multiple_of(row_refs[kk][t], SUBLANES)
                term = gate_refs[kk][t] * stage[b, pl.ds(row, SUBLANES), :]
                acc = term if acc is None else acc + term
            acc_ref[_tile_rows(t), :] = acc
            return carry

        lax.fori_loop(0, bt, token, 0, unroll=ROW_UNROLL)

    _per_buffer(slot, sum_all)
    xo = x_ref[...] + g2_ref[...] * _load_token_tiles(acc_ref, bt)
    if final:
        xo = xo * lax.rsqrt(jnp.mean(xo * xo, axis=-1, keepdims=True) + EPS) * fg_ref[...]
    o_ref[...] = xo


def _combine(rows, gates, runs, ys, x_mid, n_seq, g2, final_g, final, bt, w0, n_exp, w1_side=None):
    t, d = x_mid.shape
    tps = n_seq // bt
    row = lambda i: (i, 0)
    n_steps = t // bt
    side = _SplitSide(*w1_side, n_steps) if w1_side else None
    side = side if side is not None and side.ok else None
    in_specs = _per_assignment_specs(bt) + _per_assignment_specs(bt) + [
        pl.BlockSpec(memory_space=pltpu.SMEM),
        pl.BlockSpec(memory_space=pl.ANY),
        pl.BlockSpec((bt, d), row),
        pl.BlockSpec((None, 1, d), lambda i: (i // tps, 0, 0)),
        _full(final_g)]
    args = list(rows) + list(gates) + [runs, ys, x_mid, g2, final_g]
    out_specs = [pl.BlockSpec((bt, d), row)]
    out_shape = [jax.ShapeDtypeStruct((t, d), f32)]
    scratch = [pltpu.VMEM((2, TOP_K * bt * SUBLANES, LANES), f32), pltpu.VMEM((bt * SUBLANES, LANES), f32),
               pltpu.SemaphoreType.DMA((2,))]
    if side:
        in_specs.append(side.in_spec)
        args.append(side.w1)
        out_specs += [side.out_spec] * 2
        out_shape += [side.out_shape] * 2
        scratch.append(side.scratch)
    outs = pl.pallas_call(
        functools.partial(_combine_kernel, bt=bt, w0=w0, n_exp=n_exp, final=final, side=side),
        grid=(n_steps,),
        in_specs=in_specs,
        out_specs=out_specs,
        out_shape=out_shape,
        scratch_shapes=scratch,
        compiler_params=_params("arbitrary"),
        name="combine",
    )(*args)
    return outs[0], (tuple(outs[1:]) if side else None)


def _rope_tables(n_tokens):
    rows = n_tokens // GRID_W
    row = jnp.repeat(jnp.arange(rows), GRID_W).astype(f32)
    col = jnp.tile(jnp.arange(GRID_W), rows).astype(f32)
    n_freq = HEAD_DIM // 4
    inv = ROPE_BASE ** (-jnp.arange(n_freq, dtype=f32) / n_freq)
    ang = jnp.concatenate([row[:, None] * inv, col[:, None] * inv], axis=-1)
    cos, sin = jnp.cos(ang), jnp.sin(ang)
    return jnp.tile(cos, (1, 4)), jnp.concatenate([-sin, sin, -sin, sin], axis=-1)


def kernel(x, c, ctx, c_ctx, norm1_g, norm2_g, ada_w, ada_b, w_in, attn_sink, sgu_ws, sgu_b, sgu_ln_g, sgu_ln_b,
           pool_w, pool_scale, conv_w, w_out, router_w, router_b, exp_w1, exp_b1, exp_w2, exp_b2, final_g):
    n_batch, n_seq, d = x.shape
    n_ctx = ctx.shape[1]
    depth = ada_w.shape[0]
    n_exp = router_w.shape[2]
    assert d == SUBLANES * LANES, "token rows are moved as single (8, 128) tiles"
    t_lat, t_ctx = n_batch * n_seq, n_batch * n_ctx
    bm = EXPERT_ROWS

    cvec = jnp.concatenate([c, c_ctx[None, :], jnp.zeros((SUBLANES - n_batch - 1, d), f32)], axis=0)
    mods = _ada(cvec, ada_w, ada_b)

    cos_l, sin_l = _rope_tables(n_seq)
    cos_c, sin_c = jnp.ones((n_ctx, LANES), f32), jnp.zeros((n_ctx, LANES), f32)

    xl = x.reshape(t_lat, d)
    xc = ctx.reshape(t_ctx, d)
    row2 = lambda a: a.reshape(1, -1)
    w1_split = {}

    for l in range(depth):
        last = l == depth - 1
        ml = mods[l, :n_batch].reshape(n_batch, 6, 1, d)
        mc = jnp.broadcast_to(mods[l, n_batch].reshape(1, 6, 1, d), (n_batch, 6, 1, d))
        sh1l, sc1l, g1l, sh2l, sc2l, g2l = (ml[:, i] for i in range(6))
        sh1c, sc1c, g1c, sh2c, sc2c, g2c = (mc[:, i] for i in range(6))

        w_in_bf = w_in[l].astype(bf16)
        wo_bf = w_out[l].astype(bf16)
        ws_bf = sgu_ws[l].astype(bf16)
        sgu_bias = jnp.repeat(sgu_b[l].T, SGU_W // SGU_HEADS, axis=1)
        pool_bd = jax.scipy.linalg.block_diag(*[pool_w[l, g] for g in range(pool_w.shape[1])]).astype(bf16)
        n1g, n2g = row2(norm1_g[l]), row2(norm2_g[l])
        lng, lnb, psc = row2(sgu_ln_g[l]), row2(sgu_ln_b[l]), row2(pool_scale[l])
        rw_t = router_w[l].T
        rb = router_b[l].reshape(n_exp, 1)
        sink = attn_sink[l]
        mix_w = (ws_bf, sgu_bias, pool_bd, psc, conv_w[l], wo_bf, rw_t, rb)

        qc, kvc, vnc, mfc = _inproj(xc, n_ctx, sh1c, sc1c, n1g, w_in_bf, cos_c, sin_c, lng, lnb)
        ql, kvl, vnl, mfl = _inproj(xl, n_seq, sh1l, sc1l, n1g, w_in_bf, cos_l, sin_l, lng, lnb)
        attn_l = _window_attn(ql, kvl, kvc, sink, n_batch, n_seq, n_ctx)
        bt_l, bt_c = min(MIXER_TOKENS, n_seq), min(MIXER_TOKENS, n_ctx)
        xmid_l, h2_l, s4_l, s8_l, gt_l, cnt = _mixer(attn_l, vnl, mfl, xl, n_seq, g1l, sh2l, sc2l, n2g, *mix_w)
        n_win_l = cnt.shape[0]
        t_all = t_lat
        if not last:
            attn_c = _ctx_attn(qc, kvc, sink, n_batch, n_ctx)
            xmid_c, h2_c, s4_c, s8_c, gt_c, cnt_c = _mixer(attn_c, vnc, mfc, xc, n_ctx, g1c, sh2c, sc2c, n2g, *mix_w)
            cnt = jnp.concatenate([cnt, cnt_c], axis=0)
            t_all = t_lat + t_ctx

        win_cnt = cnt[:, :, 0].astype(i32)
        counts = jnp.sum(win_cnt, axis=0)
        padded = (counts + bm - 1) // bm * bm
        pends = jnp.cumsum(padded)
        pstarts = pends - padded
        n_blocks = -(-(t_all * TOP_K) // bm) + n_exp
        starts = jnp.arange(n_blocks, dtype=i32) * bm
        block_e = jnp.minimum(jnp.sum((pends[None, :] <= starts[:, None]).astype(i32), axis=1), n_exp - 1)
        n_used = (pends[-1:] // bm).astype(i32)
        pad = jnp.concatenate([pstarts + counts, padded - counts]).astype(i32)
        run_stage = jnp.cumsum(win_cnt, axis=1) - win_cnt
        run_slot = pstarts[None, :] + jnp.cumsum(win_cnt, axis=0) - win_cnt
        runs = jnp.stack([run_stage, run_slot, win_cnt], axis=-1).reshape(-1)

        xs, split = _dispatch(s4_l, runs, pad, h2_l, None, n_blocks * bm, bt_l, 0, n_exp,
                              None if l in w1_split else (exp_w1, l))
        if l not in w1_split:
            w1_split[l] = split or _split_w1(exp_w1, l)
        if not last:
            xs, _ = _dispatch(s4_c, runs, pad, h2_c, xs, n_blocks * bm, bt_c, n_win_l, n_exp)

        w1g, w1l = w1_split[l]
        b1 = exp_b1[l]
        b1g, b1l = b1[:, None, 0::2], b1[:, None, 1::2]
        ys = _experts(block_e, n_used, xs, w1g, w1l, b1g, b1l, exp_w2, l, exp_b2[l][:, None, :])

        fg = row2(final_g)
        xl, split = _combine(s8_l, gt_l, runs, ys, xmid_l, n_seq, g2l, fg, last, bt_l, 0, n_exp,
                             None if last else (exp_w1, l + 1))
        if split:
            w1_split[l + 1] = split
        if not last:
            xc, _ = _combine(s8_c, gt_c, runs, ys, xmid_c, n_ctx, g2c, fg, False, bt_c, n_win_l, n_exp)

    return xl.reshape(n_batch, n_seq, d)
```

```python
import functools

import jax
import jax.numpy as jnp
from jax import lax
from jax.experimental import pallas as pl
from jax.experimental.pallas import tpu as pltpu

f32 = jnp.float32
bf16 = jnp.bfloat16
i32 = jnp.int32
u32 = jnp.uint32

GRID_W = 64
EPS = 1e-6
N_Q_HEADS = 8
HEAD_DIM = 64
WINDOW = 128
ROPE_BASE = 10000.0
ATT_Q_W = 512
ATT_KV_W = 128
SGU_HEADS = 4
SGU_W = 256
SGU_CHUNK = 128
POOL_CH = 256
CONV_CH = 256
MIX_WIDTH = 1280
TOP_K = 4
SWIGLU_LIMIT = 7.0
SWIGLU_ALPHA = 1.702
SQRT_HALF = 0.7071067811865476

LANES = 128
SUBLANES = 8
PACKED_ROWS = SUBLANES // 2
VMEM_LIMIT_BYTES = 56 * 1024 * 1024

INPROJ_TOKENS = 512
ATTN_TOKENS = 128
ATTN_BLOCKS_PER_STEP = 4
MIXER_TOKENS = 512
EXPERT_ROWS = 512
HALO = 8
ROW_UNROLL = 16


def _params(*sem):
    return pltpu.CompilerParams(dimension_semantics=sem, vmem_limit_bytes=VMEM_LIMIT_BYTES)


def _full(a):
    nd = a.ndim
    return pl.BlockSpec(a.shape, lambda *_: (0,) * nd)


def _store_token_tiles(ref, val):
    n = val.shape[0]
    for s in range(SUBLANES):
        ref[pl.ds(s, n, stride=SUBLANES), :] = val[:, s * LANES:(s + 1) * LANES]


def _load_token_tiles(ref, n):
    return jnp.concatenate([ref[pl.ds(s, n, stride=SUBLANES), :] for s in range(SUBLANES)], axis=1)


HIGH_HALF = 0xFFFF0000


def _store_packed_tiles(ref, val):
    n = val.shape[0]
    half = PACKED_ROWS * LANES
    for s in range(PACKED_ROWS):
        lo = val[:, s * LANES:(s + 1) * LANES].astype(bf16).astype(f32)
        hi = val[:, half + s * LANES:half + (s + 1) * LANES].astype(bf16).astype(f32)
        lo_bits = lax.shift_right_logical(lax.bitcast_convert_type(lo, u32), jnp.uint32(16))
        hi_bits = lax.bitcast_convert_type(hi, u32) & jnp.uint32(HIGH_HALF)
        ref[pl.ds(s, n, stride=PACKED_ROWS), :] = lo_bits | hi_bits


def _load_packed_tiles(ref, n):
    los, his = [], []
    for s in range(PACKED_ROWS):
        w = ref[pl.ds(s, n, stride=PACKED_ROWS), :]
        los.append(lax.bitcast_convert_type(lax.shift_left(w, jnp.uint32(16)), f32))
        his.append(lax.bitcast_convert_type(w & jnp.uint32(HIGH_HALF), f32))
    return jnp.concatenate(los + his, axis=1).astype(bf16)


def _gelu(x):
    return 0.5 * x * (1.0 + lax.erf(x * SQRT_HALF))


def _ada_kernel(c_ref, w_ref, b_ref, o_ref):
    c = c_ref[...]
    s = c * jax.nn.sigmoid(c)
    o_ref[...] = jnp.dot(s, w_ref[...], precision=lax.Precision.HIGHEST, preferred_element_type=f32) + b_ref[...]


def _ada(cvec, ada_w, ada_b):
    depth, d, n = ada_w.shape
    tn = 1536
    return pl.pallas_call(
        _ada_kernel,
        grid=(depth, n // tn),
        in_specs=[pl.BlockSpec(cvec.shape, lambda l, j: (0, 0)),
                  pl.BlockSpec((None, d, tn), lambda l, j: (l, 0, j)),
                  pl.BlockSpec((None, 1, tn), lambda l, j: (l, 0, j))],
        out_specs=pl.BlockSpec((None, cvec.shape[0], tn), lambda l, j: (l, 0, j)),
        out_shape=jax.ShapeDtypeStruct((depth, cvec.shape[0], n), f32),
        compiler_params=_params("arbitrary", "arbitrary"),
        name="ada_mod",
    )(cvec, ada_w, ada_b.reshape(depth, 1, n))


def _inproj_kernel(x_ref, sh_ref, sc_ref, g_ref, w_ref, cos_ref, sin_ref, lng_ref, lnb_ref,
                   q_ref, kv_ref, vn_ref, mixf_ref):
    x = x_ref[...]
    y = x * lax.rsqrt(jnp.mean(x * x, axis=-1, keepdims=True) + EPS) * g_ref[...]
    h = y * (1.0 + sc_ref[...]) + sh_ref[...]
    p = jnp.dot(h.astype(bf16), w_ref[...], preferred_element_type=f32)

    cos = cos_ref[...]
    sin = sin_ref[...]
    lane = lax.broadcasted_iota(i32, cos.shape, 1)
    first_half = (lane & (HEAD_DIM - 1)) < HEAD_DIM // 2

    def rope(t):
        partner = jnp.where(first_half, pltpu.roll(t, LANES - HEAD_DIM // 2, 1), pltpu.roll(t, HEAD_DIM // 2, 1))
        return t * cos + partner * sin

    scale = HEAD_DIM ** -0.5
    for m in range(ATT_Q_W // LANES):
        q_ref[:, m * LANES:(m + 1) * LANES] = (rope(p[:, m * LANES:(m + 1) * LANES]) * scale).astype(bf16)
    k = rope(p[:, 512:640])
    v = p[:, 640:768]
    kv_ref[:, 0:128] = k.astype(bf16)
    kv_ref[:, 128:256] = pltpu.roll(k, HEAD_DIM, 1).astype(bf16)
    kv_ref[:, 256:384] = v.astype(bf16)
    kv_ref[:, 384:512] = pltpu.roll(v, HEAD_DIM, 1).astype(bf16)

    u = _gelu(p[:, 768:1024])
    gv = _gelu(p[:, 1024:1280])
    mu = jnp.mean(gv, axis=-1, keepdims=True)
    var = jnp.mean(jnp.square(gv - mu), axis=-1, keepdims=True)
    vn_ref[...] = ((gv - mu) * lax.rsqrt(var + EPS) * lng_ref[...] + lnb_ref[...]).astype(bf16)

    mixf_ref[:, 0:256] = u
    mixf_ref[:, 256:512] = p[:, 1280:1536]
    mixf_ref[:, 512:768] = p[:, 1536:1792]
    mixf_ref[:, 768:1024] = p[:, 1792:2048] * p[:, 2048:2304]


def _inproj(x2, n_seq, shift, scale, g, w_bf, cos_t, sin_t, ln_g, ln_b):
    t, d = x2.shape
    bt = min(INPROJ_TOKENS, n_seq)
    tps = n_seq // bt
    ncol = w_bf.shape[1]
    row = lambda i: (i, 0)
    return pl.pallas_call(
        _inproj_kernel,
        grid=(t // bt,),
        in_specs=[pl.BlockSpec((bt, d), row),
                  pl.BlockSpec((None, 1, d), lambda i: (i // tps, 0, 0)),
                  pl.BlockSpec((None, 1, d), lambda i: (i // tps, 0, 0)),
                  _full(g),
                  pl.BlockSpec((d, ncol), lambda i: (0, 0)),
                  pl.BlockSpec((bt, LANES), lambda i: (i % tps, 0)),
                  pl.BlockSpec((bt, LANES), lambda i: (i % tps, 0)),
                  _full(ln_g), _full(ln_b)],
        out_specs=[pl.BlockSpec((bt, ATT_Q_W), row), pl.BlockSpec((bt, 512), row),
                   pl.BlockSpec((bt, SGU_W), row), pl.BlockSpec((bt, 1024), row)],
        out_shape=[jax.ShapeDtypeStruct((t, ATT_Q_W), bf16), jax.ShapeDtypeStruct((t, 512), bf16),
                   jax.ShapeDtypeStruct((t, SGU_W), bf16), jax.ShapeDtypeStruct((t, 1024), f32)],
        compiler_params=_params("arbitrary"),
        name="inproj",
    )(x2, shift, scale, g, w_bf, cos_t, sin_t, ln_g, ln_b)


def _attn_block(sink_ref, q, kv, bias, o_ref, row0):
    k_nat, k_swp, v_nat, v_swp = (kv[:, i * LANES:(i + 1) * LANES] for i in range(4))
    nq = q.shape[0]
    low = lax.broadcasted_iota(i32, (nq, LANES), 1) < HEAD_DIM
    top = lax.broadcasted_iota(i32, (2 * nq, 1), 0) < nq
    zero = jnp.zeros((nq, LANES), q.dtype)
    for kvh in range(2):
        chunks = [q[:, (2 * kvh + i) * LANES:(2 * kvh + i + 1) * LANES] for i in range(2)]
        outs = []
        for half in range(2):
            keep = low if half == 0 else jnp.logical_not(low)
            qz = jnp.concatenate([jnp.where(keep, c, zero) for c in chunks], axis=0)
            kh = k_nat if kvh == half else k_swp
            vh = v_nat if kvh == half else v_swp
            s = lax.dot_general(qz, kh, (((1,), (1,)), ((), ())), preferred_element_type=f32)
            if bias is not None:
                s = s + bias
            h0 = 4 * kvh + half
            sk = jnp.where(top, sink_ref[h0], sink_ref[h0 + 2])
            mx = jnp.maximum(jnp.max(s, axis=1, keepdims=True), sk)
            e = jnp.exp(s - mx)
            den = jnp.sum(e, axis=1, keepdims=True) + jnp.exp(sk - mx)
            outs.append(jnp.dot(e.astype(bf16), vh, preferred_element_type=f32) / den)
        for i in range(2):
            rows = slice(i * nq, (i + 1) * nq)
            m = 2 * kvh + i
            o_ref[row0:row0 + nq, m * LANES:(m + 1) * LANES] = jnp.where(low, outs[0][rows], outs[1][rows]).astype(bf16)


def _window_attn_kernel(sink_ref, q_ref, kvp_ref, kvm_ref, kvn_ref, kvx_ref, o_ref, *, nb, per_step):
    jj = pl.program_id(1)
    nq = ATTN_TOKENS
    nband = 3 * nq
    kv_all = jnp.concatenate([kvp_ref[...], kvm_ref[...], kvn_ref[...]], axis=0)
    kvx = kvx_ref[...]
    r = lax.broadcasted_iota(i32, (nq, nband), 0)
    c = lax.broadcasted_iota(i32, (nq, nband), 1)
    dlt = c - r
    in_window = (dlt >= 0) & (dlt <= 2 * WINDOW)
    ctx_zeros = jnp.zeros((nq, kvx.shape[0]), f32)
    for i in range(per_step):
        j = per_step * jj + i
        lo = jnp.where(j == 0, nq, 0)
        hi = jnp.where(j == nb - 1, 2 * nq, nband)
        valid = in_window & (c >= lo) & (c < hi)
        bias = jnp.concatenate([jnp.where(valid, 0.0, -jnp.inf).astype(f32), ctx_zeros], axis=1)
        kv = jnp.concatenate([kv_all[i * nq:i * nq + nband], kvx], axis=0)
        _attn_block(sink_ref, q_ref[i * nq:(i + 1) * nq, :], kv, jnp.concatenate([bias, bias], axis=0), o_ref, i * nq)


def _ctx_attn_kernel(sink_ref, q_ref, kvx_ref, o_ref):
    _attn_block(sink_ref, q_ref[...], kvx_ref[...], None, o_ref, 0)


def _window_attn(q, kv, kv_ctx, sink, n_batch, n_seq, n_ctx):
    t = q.shape[0]
    nb = n_seq // ATTN_TOKENS
    p = next(c for c in (ATTN_BLOCKS_PER_STEP, 2, 1) if nb % c == 0)
    steps = nb // p
    one, many = (ATTN_TOKENS, 512), (p * ATTN_TOKENS, 512)
    return pl.pallas_call(
        functools.partial(_window_attn_kernel, nb=nb, per_step=p),
        grid=(n_batch, steps),
        in_specs=[pl.BlockSpec(memory_space=pltpu.SMEM),
                  pl.BlockSpec(many, lambda b, j: (b * steps + j, 0)),
                  pl.BlockSpec(one, lambda b, j: (b * nb + jnp.maximum(p * j - 1, 0), 0)),
                  pl.BlockSpec(many, lambda b, j: (b * steps + j, 0)),
                  pl.BlockSpec(one, lambda b, j: (b * nb + jnp.minimum(p * j + p, nb - 1), 0)),
                  pl.BlockSpec((n_ctx, 512), lambda b, j: (b, 0))],
        out_specs=pl.BlockSpec(many, lambda b, j: (b * steps + j, 0)),
        out_shape=jax.ShapeDtypeStruct((t, ATT_Q_W), bf16),
        compiler_params=_params("arbitrary", "arbitrary"),
        name="window_attn",
    )(sink, q, kv, kv, kv, kv_ctx)


def _ctx_attn(q, kv_ctx, sink, n_batch, n_ctx):
    nb = n_ctx // ATTN_TOKENS
    blk = (ATTN_TOKENS, 512)
    return pl.pallas_call(
        _ctx_attn_kernel,
        grid=(n_batch, nb),
        in_specs=[pl.BlockSpec(memory_space=pltpu.SMEM),
                  pl.BlockSpec(blk, lambda b, j: (b * nb + j, 0)),
                  pl.BlockSpec((n_ctx, 512), lambda b, j: (b, 0))],
        out_specs=pl.BlockSpec(blk, lambda b, j: (b * nb + j, 0)),
        out_shape=jax.ShapeDtypeStruct(q.shape, bf16),
        compiler_params=_params("arbitrary", "arbitrary"),
        name="ctx_attn",
    )(sink, q, kv_ctx)


def _mixer_kernel(attn_ref, vn_ref, mf_ref, mfp_ref, mfn_ref, x_ref, g1_ref, sh2_ref, sc2_ref, n2g_ref,
                  ws_ref, sb_ref, pw_ref, ps_ref, cw_ref, wo_ref, rw_ref, rb_ref,
                  xmid_ref, h2_ref, *table_refs, n_seq, bt):
    s4_refs, s8_refs, gt_refs = (table_refs[j * TOP_K:(j + 1) * TOP_K] for j in range(3))
    cnt_ref = table_refs[3 * TOP_K]
    i = pl.program_id(0)
    tps = n_seq // bt
    si = i % tps
    first = si == 0
    last = si == tps - 1
    n_ext = bt + 2 * HALO

    mf = mf_ref[...]
    u = mf[:, 0:256]

    def extended(lo, hi):
        prev = jnp.where(first, 0.0, mfp_ref[:, lo:hi])
        nxt = jnp.where(last, 0.0, mfn_ref[:, lo:hi])
        return jnp.concatenate([prev, mf[:, lo:hi], nxt], axis=0)

    def shifted(a, s):
        return pltpu.roll(a, s % n_ext, 0)

    xe = extended(256, 512)
    a1 = shifted(xe, 1) + xe
    a2 = shifted(a1, 1) + shifted(a1, -1)
    a3 = shifted(a2, 2) + shifted(a2, -2)
    a4 = shifted(a3, 4) + shifted(a3, -4)
    lane = lax.broadcasted_iota(i32, (bt, POOL_CH), 1)
    grp = lane >> 6
    sl = slice(HALO, HALO + bt)
    wsum = jnp.where(grp == 0, a1[sl], jnp.where(grp == 1, a2[sl], jnp.where(grp == 2, a3[sl], a4[sl])))
    pos = lax.broadcasted_iota(i32, (bt, POOL_CH), 0) + si * bt
    halfw = jnp.left_shift(1, grp)
    cnt = jnp.minimum(pos + halfw, n_seq) - jnp.maximum(pos - halfw, 0)
    dpool = wsum / cnt.astype(f32) - mf[:, 256:512]
    yc = jnp.dot(dpool.astype(bf16), pw_ref[...], preferred_element_type=f32) * ps_ref[...]

    ye = extended(768, 1024)
    cw = cw_ref[...]
    z = shifted(ye, 1) * cw[0:1, :] + ye * cw[1:2, :] + shifted(ye, -1) * cw[2:3, :]
    yd = mf[:, 512:768] * z[sl]

    hgrp = lax.broadcasted_iota(i32, (SGU_CHUNK, SGU_W), 1) >> 6
    ybs = []
    for cidx in range(bt // SGU_CHUNK):
        rows = slice(cidx * SGU_CHUNK, (cidx + 1) * SGU_CHUNK)
        vn_c = vn_ref[rows, :]
        s = jnp.zeros((SGU_CHUNK, SGU_W), f32)
        for hh in range(SGU_HEADS):
            sh = jnp.dot(ws_ref[hh], vn_c, preferred_element_type=f32)
            s = jnp.where(hgrp == hh, sh, s)
        ybs.append(u[rows, :] * (s + sb_ref[...]))
    yb = jnp.concatenate(ybs, axis=0)

    mix = jnp.concatenate([attn_ref[...], yb.astype(bf16), yc.astype(bf16), yd.astype(bf16)], axis=1)
    mo = jnp.dot(mix, wo_ref[...], preferred_element_type=f32)
    xm = x_ref[...] + g1_ref[...] * mo
    xmid_ref[...] = xm

    y = xm * lax.rsqrt(jnp.mean(xm * xm, axis=-1, keepdims=True) + EPS) * n2g_ref[...]
    h2 = y * (1.0 + sc2_ref[...]) + sh2_ref[...]
    _store_packed_tiles(h2_ref, h2)

    lt = lax.dot_general(rw_ref[...], h2, (((1,), (1,)), ((), ())),
                         precision=lax.Precision.HIGHEST, preferred_element_type=f32) + rb_ref[...]
    n_exp = lt.shape[0]
    eidx = lax.broadcasted_iota(i32, lt.shape, 0)
    work = lt
    idxs, vals = [], []
    for _ in range(TOP_K):
        m = jnp.max(work, axis=0, keepdims=True)
        idx = jnp.min(jnp.where(work == m, eidx, n_exp), axis=0, keepdims=True)
        idxs.append(idx)
        vals.append(m)
        work = jnp.where(eidx == idx, -jnp.inf, work)
    exps = [jnp.exp(v - vals[0]) for v in vals]
    den = exps[0] + exps[1] + exps[2] + exps[3]
    onehot = jnp.zeros(lt.shape, f32)
    for kk in range(TOP_K):
        gt_refs[kk][...] = exps[kk] / den
        onehot = onehot + (eidx == idxs[kk]).astype(f32)
    tri = (lax.broadcasted_iota(i32, (bt, bt), 0) < lax.broadcasted_iota(i32, (bt, bt), 1)).astype(bf16)
    before = jnp.dot(onehot.astype(bf16), tri, preferred_element_type=f32)
    cnt = jnp.sum(onehot, axis=1, keepdims=True)
    row = lax.broadcasted_iota(i32, (n_exp, LANES), 0)
    incl = jnp.broadcast_to(cnt, (n_exp, LANES))
    shift = 1
    while shift < n_exp:
        incl = incl + jnp.where(row >= shift, pltpu.roll(incl, shift, 0), 0.0)
        shift *= 2
    place = before + (incl[:, 0:1] - cnt)
    for kk in range(TOP_K):
        tile = jnp.sum(jnp.where(eidx == idxs[kk], place, 0.0), axis=0, keepdims=True).astype(i32)
        s4_refs[kk][...] = tile * PACKED_ROWS
        s8_refs[kk][...] = tile * SUBLANES
    cnt_ref[...] = cnt


def _mixer(attn, vn, mixf, x2, n_seq, g1, sh2, sc2, n2g, ws_bf, sgu_bias, pool_bd, pool_scale, conv_w,
           wo_bf, rw_t, rb):
    t, d = x2.shape
    bt = min(MIXER_TOKENS, n_seq)
    tps = n_seq // bt
    hb = bt // HALO
    n_halo = t // HALO
    n_exp = rw_t.shape[0]
    row = lambda i: (i, 0)
    per_batch = pl.BlockSpec((None, 1, d), lambda i: (i // tps, 0, 0))
    col = lambda i: (0, i)
    outs = pl.pallas_call(
        functools.partial(_mixer_kernel, n_seq=n_seq, bt=bt),
        grid=(t // bt,),
        in_specs=[pl.BlockSpec((bt, ATT_Q_W), row), pl.BlockSpec((bt, SGU_W), row), pl.BlockSpec((bt, 1024), row),
                  pl.BlockSpec((HALO, 1024), lambda i: (jnp.maximum(i * hb - 1, 0), 0)),
                  pl.BlockSpec((HALO, 1024), lambda i: (jnp.minimum((i + 1) * hb, n_halo - 1), 0)),
                  pl.BlockSpec((bt, d), row), per_batch, per_batch, per_batch, _full(n2g),
                  _full(ws_bf), _full(sgu_bias), _full(pool_bd), _full(pool_scale), _full(conv_w),
                  _full(wo_bf), _full(rw_t), _full(rb)],
        out_specs=[pl.BlockSpec((bt, d), row), pl.BlockSpec((bt * PACKED_ROWS, LANES), row)]
        + [pl.BlockSpec((1, bt), col)] * (3 * TOP_K) + [pl.BlockSpec((None, n_exp, 1), lambda i: (i, 0, 0))],
        out_shape=[jax.ShapeDtypeStruct((t, d), f32), jax.ShapeDtypeStruct((t * PACKED_ROWS, LANES), u32)]
        + [jax.ShapeDtypeStruct((1, t), i32)] * (2 * TOP_K) + [jax.ShapeDtypeStruct((1, t), f32)] * TOP_K
        + [jax.ShapeDtypeStruct((t // bt, n_exp, 1), f32)],
        compiler_params=_params("arbitrary"),
        name="mixer_router",
    )(attn, vn, mixf, mixf, mixf, x2, g1, sh2, sc2, n2g, ws_bf, sgu_bias, pool_bd, pool_scale, conv_w,
      wo_bf, rw_t, rb)
    tables = [o.reshape(t) for o in outs[2:2 + 3 * TOP_K]]
    return outs[0], outs[1], tables[:TOP_K], tables[TOP_K:2 * TOP_K], tables[2 * TOP_K:], outs[-1]


def _per_assignment_specs(bt):
    return [pl.BlockSpec((bt,), lambda i: (i,), memory_space=pltpu.SMEM)] * TOP_K


def _tile_rows(t, n=1, rows=SUBLANES):
    return pl.ds(pl.multiple_of(t * rows, rows), n * rows)


def _tile_copy(src, s, dst, d, sem, rows=SUBLANES):
    return pltpu.make_async_copy(src.at[_tile_rows(s, 1, rows)], dst.at[_tile_rows(d, 1, rows)], sem)


def _for_each_run_piece(runs_ref, w, n_exp, max_len, fn):
    bits = [1 << b for b in range(max_len.bit_length() - 1, -1, -1)]

    def per_expert(e, carry):
        base = (w * n_exp + e) * 3
        stage0, slot0, length = runs_ref[base], runs_ref[base + 1], runs_ref[base + 2]
        done = 0
        for bit in bits:
            take = length & bit

            @pl.when(take != 0)
            def _():
                fn(stage0 + done, slot0 + done, bit)

            done = done + take
        return carry

    lax.fori_loop(0, n_exp, per_expert, 0)


def _wait_window(stage, buf, sem):
    pltpu.make_async_copy(stage.at[buf], stage.at[buf], sem.at[buf]).wait()


def _per_buffer(slot, fn):
    for b in range(2):
        @pl.when(slot == b)
        def _():
            fn(b)


def _split_slabs(w_ref, g_ref, l_ref, scr):
    f = g_ref.shape[1]
    for s in range(w_ref.shape[0] // LANES):
        rows = slice(s * LANES, (s + 1) * LANES)
        scr[...] = w_ref[rows, :].T
        g_ref[rows, :] = scr[pl.ds(0, f, stride=2), :].T.astype(bf16)
        l_ref[rows, :] = scr[pl.ds(1, f, stride=2), :].T.astype(bf16)


def _split_w1(w1, layer):
    _, n_exp, d, f2 = w1.shape
    f = f2 // 2
    rows = 2 * LANES
    blk = pl.BlockSpec((None, rows, f), lambda e, j: (e, j, 0))
    return pl.pallas_call(
        _split_slabs,
        grid=(n_exp, d // rows),
        in_specs=[pl.BlockSpec((None, None, rows, f2), lambda e, j: (layer, e, j, 0))],
        out_specs=[blk, blk],
        out_shape=[jax.ShapeDtypeStruct((n_exp, d, f), bf16)] * 2,
        scratch_shapes=[pltpu.VMEM((f2, LANES), f32)],
        compiler_params=_params("arbitrary", "arbitrary"),
        name="split_w1",
    )(w1)


class _SplitSide:
    def __init__(self, w1, layer, n_steps):
        _, n_exp, d, f2 = w1.shape
        fits = [r for r in (LANES, 2 * LANES, 4 * LANES, 8 * LANES) if d % r == 0 and n_exp * (d // r) <= n_steps]
        self.ok = bool(fits)
        if not self.ok:
            return
        rows = fits[0]
        per = d // rows
        self.w1 = w1
        self.n_side = n_exp * per
        last = self.n_side - 1
        self.in_spec = pl.BlockSpec((None, None, rows, f2),
                                    lambda i: (layer, jnp.minimum(i, last) // per, jnp.minimum(i, last) % per, 0))
        self.out_spec = pl.BlockSpec((None, rows, f2 // 2),
                                     lambda i: (jnp.minimum(i, last) // per, jnp.minimum(i, last) % per, 0))
        self.out_shape = jax.ShapeDtypeStruct((n_exp, d, f2 // 2), bf16)
        self.scratch = pltpu.VMEM((f2, LANES), f32)

    def run(self, w_ref, g_ref, l_ref, scr):
        @pl.when(pl.program_id(0) < self.n_side)
        def _():
            _split_slabs(w_ref, g_ref, l_ref, scr)


def _dispatch_kernel(*refs, bt, w0, n_exp, fill_pad, side):
    row_refs, (runs_ref, pad_ref, h_ref), rest = refs[:TOP_K], refs[TOP_K:TOP_K + 3], list(refs[TOP_K + 3:])
    if not fill_pad:
        rest.pop(0)
    w_ref = rest.pop(0) if side else None
    xs_ref = rest.pop(0)
    g_ref, l_ref = (rest.pop(0), rest.pop(0)) if side else (None, None)
    zero_ref = rest.pop(0) if fill_pad else None
    stage, sem = rest.pop(0), rest.pop(0)
    i = pl.program_id(0)
    slot = i % 2

    if fill_pad:
        @pl.when(i == 0)
        def _():
            zero_ref[...] = jnp.zeros_like(zero_ref)

            def per_expert(e, carry):
                first, count = pad_ref[e], pad_ref[n_exp + e]

                def issue(r, c):
                    _tile_copy(zero_ref, 0, xs_ref, first + r, sem.at[0], PACKED_ROWS).start()
                    return c

                def drain(r, c):
                    _tile_copy(zero_ref, 0, xs_ref, 0, sem.at[0], PACKED_ROWS).wait()
                    return c

                lax.fori_loop(0, count, issue, 0)
                lax.fori_loop(0, count, drain, 0)
                return carry

            lax.fori_loop(0, n_exp, per_expert, 0)

    def run_copy(buf):
        def make(stage_tile, slot_tile, n):
            return pltpu.make_async_copy(stage.at[buf, _tile_rows(stage_tile, n, PACKED_ROWS)],
                                         xs_ref.at[_tile_rows(slot_tile, n, PACKED_ROWS)], sem.at[buf])
        return make

    def place_all(b):
        def place(t, carry):
            tile = h_ref[_tile_rows(t, 1, PACKED_ROWS), :]
            for kk in range(TOP_K):
                row = pl.multiple_of(row_refs[kk][t], PACKED_ROWS)
                stage[b, pl.ds(row, PACKED_ROWS), :] = tile
            return carry

        lax.fori_loop(0, bt, place, 0, unroll=ROW_UNROLL)

    _per_buffer(slot, place_all)
    _for_each_run_piece(runs_ref, w0 + i, n_exp, bt, lambda a, b, n: run_copy(slot)(a, b, n).start())
    if side:
        side.run(w_ref, g_ref, l_ref, rest.pop(0))

    @pl.when(i > 0)
    def _():
        _wait_window(stage, 1 - slot, sem)

    @pl.when(i == pl.num_programs(0) - 1)
    def _():
        _wait_window(stage, slot, sem)


def _dispatch(rows, runs, pad, h2, xs, n_slots, bt, w0, n_exp, w1_side=None):
    t = h2.shape[0] // PACKED_ROWS
    n_steps = t // bt
    fill_pad = xs is None
    side = _SplitSide(*w1_side, n_steps) if w1_side else None
    side = side if side is not None and side.ok else None
    smem = pl.BlockSpec(memory_space=pltpu.SMEM)
    in_specs = _per_assignment_specs(bt) + [smem, smem, pl.BlockSpec((bt * PACKED_ROWS, LANES), lambda i: (i, 0))]
    args = list(rows) + [runs, pad, h2]
    out_specs = [pl.BlockSpec(memory_space=pl.ANY)]
    out_shape = [jax.ShapeDtypeStruct((n_slots * PACKED_ROWS, LANES), u32)]
    scratch = [pltpu.VMEM((2, TOP_K * bt * PACKED_ROWS, LANES), u32), pltpu.SemaphoreType.DMA((2,))]
    if fill_pad:
        scratch = [pltpu.VMEM((PACKED_ROWS, LANES), u32)] + scratch
    else:
        in_specs.append(pl.BlockSpec(memory_space=pl.ANY))
        args.append(xs)
    if side:
        in_specs.append(side.in_spec)
        args.append(side.w1)
        out_specs += [side.out_spec] * 2
        out_shape += [side.out_shape] * 2
        scratch.append(side.scratch)
    outs = pl.pallas_call(
        functools.partial(_dispatch_kernel, bt=bt, w0=w0, n_exp=n_exp, fill_pad=fill_pad, side=side),
        grid=(n_steps,),
        in_specs=in_specs,
        out_specs=out_specs,
        out_shape=out_shape,
        scratch_shapes=scratch,
        input_output_aliases={} if fill_pad else {TOP_K + 3: 0},
        compiler_params=_params("arbitrary"),
        name="dispatch",
    )(*args)
    return outs[0], (tuple(outs[1:]) if side else None)


def _expert_kernel(be_ref, nu_ref, x_ref, w1g_ref, w1l_ref, b1g_ref, b1l_ref, w2f_ref, b2_ref, y_ref, w2_ref):
    i = pl.program_id(0)
    active = i < nu_ref[0]
    new_expert = jnp.logical_or(i == 0, be_ref[i] != be_ref[jnp.maximum(i - 1, 0)])

    @pl.when(jnp.logical_and(active, new_expert))
    def _():
        w2_ref[...] = w2f_ref[...].astype(bf16)

    @pl.when(active)
    def _():
        x = _load_packed_tiles(x_ref, x_ref.shape[0] // PACKED_ROWS)
        zg = jnp.dot(x, w1g_ref[...], preferred_element_type=f32) + b1g_ref[...]
        zl = jnp.dot(x, w1l_ref[...], preferred_element_type=f32) + b1l_ref[...]
        g = jnp.minimum(zg, SWIGLU_LIMIT)
        lin = jnp.clip(zl, -SWIGLU_LIMIT, SWIGLU_LIMIT)
        a = g * jax.nn.sigmoid(SWIGLU_ALPHA * g) * (lin + 1.0)
        y = jnp.dot(a.astype(bf16), w2_ref[...], preferred_element_type=f32) + b2_ref[...]
        _store_token_tiles(y_ref, y)

    @pl.when(i >= nu_ref[0])
    def _():
        y_ref[...] = jnp.zeros_like(y_ref)


def _experts(block_e, n_used, xs, w1g, w1l, b1g, b1l, w2_all, layer, b2):
    n_exp, d, f = w1g.shape
    bm = EXPERT_ROWS
    n_slots = xs.shape[0] // PACKED_ROWS
    blk = (bm * SUBLANES, LANES)
    xrow = lambda i, be, nu: (jnp.minimum(i, nu[0] - 1), 0)
    wsel = lambda i, be, nu: (be[i], 0, 0)
    grid_spec = pltpu.PrefetchScalarGridSpec(
        num_scalar_prefetch=2,
        grid=(n_slots // bm,),
        in_specs=[pl.BlockSpec((bm * PACKED_ROWS, LANES), xrow),
                  pl.BlockSpec((None, d, f), wsel), pl.BlockSpec((None, d, f), wsel),
                  pl.BlockSpec((None, 1, f), wsel), pl.BlockSpec((None, 1, f), wsel),
                  pl.BlockSpec((None, None, f, d), lambda i, be, nu: (layer, be[i], 0, 0)),
                  pl.BlockSpec((None, 1, d), wsel)],
        out_specs=pl.BlockSpec(blk, lambda i, be, nu: (i, 0)),
        scratch_shapes=[pltpu.VMEM((f, d), bf16)],
    )
    return pl.pallas_call(
        _expert_kernel,
        grid_spec=grid_spec,
        out_shape=jax.ShapeDtypeStruct((n_slots * SUBLANES, LANES), f32),
        compiler_params=_params("arbitrary"),
        name="experts",
    )(block_e, n_used, xs, w1g, w1l, b1g, b1l, w2_all, b2)


def _combine_kernel(*refs, bt, w0, n_exp, final, side):
    row_refs, gate_refs = refs[:TOP_K], refs[TOP_K:2 * TOP_K]
    (runs_ref, ys_ref, x_ref, g2_ref, fg_ref), rest = refs[2 * TOP_K:2 * TOP_K + 5], list(refs[2 * TOP_K + 5:])
    w_ref = rest.pop(0) if side else None
    o_ref = rest.pop(0)
    g_ref, l_ref = (rest.pop(0), rest.pop(0)) if side else (None, None)
    stage, acc_ref, sem = rest.pop(0), rest.pop(0), rest.pop(0)
    i = pl.program_id(0)
    slot = i % 2

    def run_copy(buf):
        def make(stage_tile, slot_tile, n):
            return pltpu.make_async_copy(ys_ref.at[_tile_rows(slot_tile, n)], stage.at[buf, _tile_rows(stage_tile, n)],
                                         sem.at[buf])
        return make

    def fetch(w, buf):
        _for_each_run_piece(runs_ref, w, n_exp, bt, lambda a, b, n: run_copy(buf)(a, b, n).start())

    @pl.when(i == 0)
    def _():
        fetch(w0, 0)

    @pl.when(i + 1 < pl.num_programs(0))
    def _():
        fetch(w0 + i + 1, 1 - slot)

    if side:
        side.run(w_ref, g_ref, l_ref, rest.pop(0))

    _wait_window(stage, slot, sem)

    def sum_all(b):
        def token(t, carry):
            acc = None
            for kk in range(TOP_K):
                row = pl.multiple_of(row_refs[kk][t], SUBLANES)
                term = gate_refs[kk][t] * stage[b, pl.ds(row, SUBLANES), :]
                acc = term if acc is None else acc + term
            acc_ref[_tile_rows(t), :] = acc
            return carry

        lax.fori_loop(0, bt, token, 0, unroll=ROW_UNROLL)

    _per_buffer(slot, sum_all)
    xo = x_ref[...] + g2_ref[...] * _load_token_tiles(acc_ref, bt)
    if final:
        xo = xo * lax.rsqrt(jnp.mean(xo * xo, axis=-1, keepdims=True) + EPS) * fg_ref[...]
    o_ref[...] = xo


def _combine(rows, gates, runs, ys, x_mid, n_seq, g2, final_g, final, bt, w0, n_exp, w1_side=None):
    t, d = x_mid.shape
    tps = n_seq // bt
    row = lambda i: (i, 0)
    n_steps = t // bt
    side = _SplitSide(*w1_side, n_steps) if w1_side else None
    side = side if side is not None and side.ok else None
    in_specs = _per_assignment_specs(bt) + _per_assignment_specs(bt) + [
        pl.BlockSpec(memory_space=pltpu.SMEM),
        pl.BlockSpec(memory_space=pl.ANY),
        pl.BlockSpec((bt, d), row),
        pl.BlockSpec((None, 1, d), lambda i: (i // tps, 0, 0)),
        _full(final_g)]
    args = list(rows) + list(gates) + [runs, ys, x_mid, g2, final_g]
    out_specs = [pl.BlockSpec((bt, d), row)]
    out_shape = [jax.ShapeDtypeStruct((t, d), f32)]
    scratch = [pltpu.VMEM((2, TOP_K * bt * SUBLANES, LANES), f32), pltpu.VMEM((bt * SUBLANES, LANES), f32),
               pltpu.SemaphoreType.DMA((2,))]
    if side:
        in_specs.append(side.in_spec)
        args.append(side.w1)
        out_specs += [side.out_spec] * 2
        out_shape += [side.out_shape] * 2
        scratch.append(side.scratch)
    outs = pl.pallas_call(
        functools.partial(_combine_kernel, bt=bt, w0=w0, n_exp=n_exp, final=final, side=side),
        grid=(n_steps,),
        in_specs=in_specs,
        out_specs=out_specs,
        out_shape=out_shape,
        scratch_shapes=scratch,
        compiler_params=_params("arbitrary"),
        name="combine",
    )(*args)
    return outs[0], (tuple(outs[1:]) if side else None)


def _rope_tables(n_tokens):
    rows = n_tokens // GRID_W
    row = jnp.repeat(jnp.arange(rows), GRID_W).astype(f32)
    col = jnp.tile(jnp.arange(GRID_W), rows).astype(f32)
    n_freq = HEAD_DIM // 4
    inv = ROPE_BASE ** (-jnp.arange(n_freq, dtype=f32) / n_freq)
    ang = jnp.concatenate([row[:, None] * inv, col[:, None] * inv], axis=-1)
    cos, sin = jnp.cos(ang), jnp.sin(ang)
    return jnp.tile(cos, (1, 4)), jnp.concatenate([-sin, sin, -sin, sin], axis=-1)


def kernel(x, c, ctx, c_ctx, norm1_g, norm2_g, ada_w, ada_b, w_in, attn_sink, sgu_ws, sgu_b, sgu_ln_g, sgu_ln_b,
           pool_w, pool_scale, conv_w, w_out, router_w, router_b, exp_w1, exp_b1, exp_w2, exp_b2, final_g):
    n_batch, n_seq, d = x.shape
    n_ctx = ctx.shape[1]
    depth = ada_w.shape[0]
    n_exp = router_w.shape[2]
    assert d == SUBLANES * LANES, "token rows are moved as single (8, 128) tiles"
    t_lat, t_ctx = n_batch * n_seq, n_batch * n_ctx
    bm = EXPERT_ROWS

    cvec = jnp.concatenate([c, c_ctx[None, :], jnp.zeros((SUBLANES - n_batch - 1, d), f32)], axis=0)
    mods = _ada(cvec, ada_w, ada_b)

    cos_l, sin_l = _rope_tables(n_seq)
    cos_c, sin_c = jnp.ones((n_ctx, LANES), f32), jnp.zeros((n_ctx, LANES), f32)

    xl = x.reshape(t_lat, d)
    xc = ctx.reshape(t_ctx, d)
    row2 = lambda a: a.reshape(1, -1)
    w1_split = {}

    for l in range(depth):
        last = l == depth - 1
        ml = mods[l, :n_batch].reshape(n_batch, 6, 1, d)
        mc = jnp.broadcast_to(mods[l, n_batch].reshape(1, 6, 1, d), (n_batch, 6, 1, d))
        sh1l, sc1l, g1l, sh2l, sc2l, g2l = (ml[:, i] for i in range(6))
        sh1c, sc1c, g1c, sh2c, sc2c, g2c = (mc[:, i] for i in range(6))

        w_in_bf = w_in[l].astype(bf16)
        wo_bf = w_out[l].astype(bf16)
        ws_bf = sgu_ws[l].astype(bf16)
        sgu_bias = jnp.repeat(sgu_b[l].T, SGU_W // SGU_HEADS, axis=1)
        pool_bd = jax.scipy.linalg.block_diag(*[pool_w[l, g] for g in range(pool_w.shape[1])]).astype(bf16)
        n1g, n2g = row2(norm1_g[l]), row2(norm2_g[l])
        lng, lnb, psc = row2(sgu_ln_g[l]), row2(sgu_ln_b[l]), row2(pool_scale[l])
        rw_t = router_w[l].T
        rb = router_b[l].reshape(n_exp, 1)
        sink = attn_sink[l]
        mix_w = (ws_bf, sgu_bias, pool_bd, psc, conv_w[l], wo_bf, rw_t, rb)

        qc, kvc, vnc, mfc = _inproj(xc, n_ctx, sh1c, sc1c, n1g, w_in_bf, cos_c, sin_c, lng, lnb)
        ql, kvl, vnl, mfl = _inproj(xl, n_seq, sh1l, sc1l, n1g, w_in_bf, cos_l, sin_l, lng, lnb)
        attn_l = _window_attn(ql, kvl, kvc, sink, n_batch, n_seq, n_ctx)
        bt_l, bt_c = min(MIXER_TOKENS, n_seq), min(MIXER_TOKENS, n_ctx)
        xmid_l, h2_l, s4_l, s8_l, gt_l, cnt = _mixer(attn_l, vnl, mfl, xl, n_seq, g1l, sh2l, sc2l, n2g, *mix_w)
        n_win_l = cnt.shape[0]
        t_all = t_lat
        if not last:
            attn_c = _ctx_attn(qc, kvc, sink, n_batch, n_ctx)
            xmid_c, h2_c, s4_c, s8_c, gt_c, cnt_c = _mixer(attn_c, vnc, mfc, xc, n_ctx, g1c, sh2c, sc2c, n2g, *mix_w)
            cnt = jnp.concatenate([cnt, cnt_c], axis=0)
            t_all = t_lat + t_ctx

        win_cnt = cnt[:, :, 0].astype(i32)
        counts = jnp.sum(win_cnt, axis=0)
        padded = (counts + bm - 1) // bm * bm
        pends = jnp.cumsum(padded)
        pstarts = pends - padded
        n_blocks = -(-(t_all * TOP_K) // bm) + n_exp
        starts = jnp.arange(n_blocks, dtype=i32) * bm
        block_e = jnp.minimum(jnp.sum((pends[None, :] <= starts[:, None]).astype(i32), axis=1), n_exp - 1)
        n_used = (pends[-1:] // bm).astype(i32)
        pad = jnp.concatenate([pstarts + counts, padded - counts]).astype(i32)
        run_stage = jnp.cumsum(win_cnt, axis=1) - win_cnt
        run_slot = pstarts[None, :] + jnp.cumsum(win_cnt, axis=0) - win_cnt
        runs = jnp.stack([run_stage, run_slot, win_cnt], axis=-1).reshape(-1)

        xs, split = _dispatch(s4_l, runs, pad, h2_l, None, n_blocks * bm, bt_l, 0, n_exp,
                              None if l in w1_split else (exp_w1, l))
        if l not in w1_split:
            w1_split[l] = split or _split_w1(exp_w1, l)
        if not last:
            xs, _ = _dispatch(s4_c, runs, pad, h2_c, xs, n_blocks * bm, bt_c, n_win_l, n_exp)

        w1g, w1l = w1_split[l]
        b1 = exp_b1[l]
        b1g, b1l = b1[:, None, 0::2], b1[:, None, 1::2]
        ys = _experts(block_e, n_used, xs, w1g, w1l, b1g, b1l, exp_w2, l, exp_b2[l][:, None, :])

        fg = row2(final_g)
        xl, split = _combine(s8_l, gt_l, runs, ys, xmid_l, n_seq, g2l, fg, last, bt_l, 0, n_exp,
                             None if last else (exp_w1, l + 1))
        if split:
            w1_split[l + 1] = split
        if not last:
            xc, _ = _combine(s8_c, gt_c, runs, ys, xmid_c, n_ctx, g2c, fg, False, bt_c, n_win_l, n_exp)

    return xl.reshape(n_batch, n_seq, d)
```

```python
import functools

import jax
import jax.numpy as jnp
from jax import lax
from jax.experimental import pallas as pl
from jax.experimental.pallas import tpu as pltpu

f32 = jnp.float32
bf16 = jnp.bfloat16
i32 = jnp.int32
u32 = jnp.uint32

GRID_W = 64
EPS = 1e-6
N_Q_HEADS = 8
HEAD_DIM = 64
WINDOW = 128
ROPE_BASE = 10000.0
ATT_Q_W = 512
ATT_KV_W = 128
SGU_HEADS = 4
SGU_W = 256
SGU_CHUNK = 128
POOL_CH = 256
CONV_CH = 256
MIX_WIDTH = 1280
TOP_K = 4
SWIGLU_LIMIT = 7.0
SWIGLU_ALPHA = 1.702
SQRT_HALF = 0.7071067811865476

LANES = 128
SUBLANES = 8
PACKED_ROWS = SUBLANES // 2
VMEM_LIMIT_BYTES = 56 * 1024 * 1024

INPROJ_TOKENS = 512
ATTN_TOKENS = 128
ATTN_BLOCKS_PER_STEP = 4
MIXER_TOKENS = 512
EXPERT_ROWS = 512
HALO = 8
ROW_UNROLL = 16


def _params(*sem):
    return pltpu.CompilerParams(dimension_semantics=sem, vmem_limit_bytes=VMEM_LIMIT_BYTES)


def _full(a):
    nd = a.ndim
    return pl.BlockSpec(a.shape, lambda *_: (0,) * nd)


def _store_token_tiles(ref, val):
    n = val.shape[0]
    for s in range(SUBLANES):
        ref[pl.ds(s, n, stride=SUBLANES), :] = val[:, s * LANES:(s + 1) * LANES]


def _load_token_tiles(ref, n):
    return jnp.concatenate([ref[pl.ds(s, n, stride=SUBLANES), :] for s in range(SUBLANES)], axis=1)


HIGH_HALF = 0xFFFF0000


def _store_packed_tiles(ref, val):
    n = val.shape[0]
    half = PACKED_ROWS * LANES
    for s in range(PACKED_ROWS):
        lo = val[:, s * LANES:(s + 1) * LANES].astype(bf16).astype(f32)
        hi = val[:, half + s * LANES:half + (s + 1) * LANES].astype(bf16).astype(f32)
        lo_bits = lax.shift_right_logical(lax.bitcast_convert_type(lo, u32), jnp.uint32(16))
        hi_bits = lax.bitcast_convert_type(hi, u32) & jnp.uint32(HIGH_HALF)
        ref[pl.ds(s, n, stride=PACKED_ROWS), :] = lo_bits | hi_bits


def _load_packed_tiles(ref, n):
    los, his = [], []
    for s in range(PACKED_ROWS):
        w = ref[pl.ds(s, n, stride=PACKED_ROWS), :]
        los.append(lax.bitcast_convert_type(lax.shift_left(w, jnp.uint32(16)), f32))
        his.append(lax.bitcast_convert_type(w & jnp.uint32(HIGH_HALF), f32))
    return jnp.concatenate(los + his, axis=1).astype(bf16)


def _gelu(x):
    return 0.5 * x * (1.0 + lax.erf(x * SQRT_HALF))


def _ada_kernel(c_ref, w_ref, b_ref, o_ref):
    c = c_ref[...]
    s = c * jax.nn.sigmoid(c)
    o_ref[...] = jnp.dot(s, w_ref[...], precision=lax.Precision.HIGHEST, preferred_element_type=f32) + b_ref[...]


def _ada(cvec, ada_w, ada_b):
    depth, d, n = ada_w.shape
    tn = 1536
    return pl.pallas_call(
        _ada_kernel,
        grid=(depth, n // tn),
        in_specs=[pl.BlockSpec(cvec.shape, lambda l, j: (0, 0)),
                  pl.BlockSpec((None, d, tn), lambda l, j: (l, 0, j)),
                  pl.BlockSpec((None, 1, tn), lambda l, j: (l, 0, j))],
        out_specs=pl.BlockSpec((None, cvec.shape[0], tn), lambda l, j: (l, 0, j)),
        out_shape=jax.ShapeDtypeStruct((depth, cvec.shape[0], n), f32),
        compiler_params=_params("arbitrary", "arbitrary"),
        name="ada_mod",
    )(cvec, ada_w, ada_b.reshape(depth, 1, n))


def _inproj_kernel(x_ref, sh_ref, sc_ref, g_ref, w_ref, cos_ref, sin_ref, lng_ref, lnb_ref,
                   q_ref, kv_ref, vn_ref, mixf_ref):
    x = x_ref[...]
    y = x * lax.rsqrt(jnp.mean(x * x, axis=-1, keepdims=True) + EPS) * g_ref[...]
    h = y * (1.0 + sc_ref[...]) + sh_ref[...]
    p = jnp.dot(h.astype(bf16), w_ref[...], preferred_element_type=f32)

    cos = cos_ref[...]
    sin = sin_ref[...]
    lane = lax.broadcasted_iota(i32, cos.shape, 1)
    first_half = (lane & (HEAD_DIM - 1)) < HEAD_DIM // 2

    def rope(t):
        partner = jnp.where(first_half, pltpu.roll(t, LANES - HEAD_DIM // 2, 1), pltpu.roll(t, HEAD_DIM // 2, 1))
        return t * cos + partner * sin

    scale = HEAD_DIM ** -0.5
    for m in range(ATT_Q_W // LANES):
        q_ref[:, m * LANES:(m + 1) * LANES] = (rope(p[:, m * LANES:(m + 1) * LANES]) * scale).astype(bf16)
    k = rope(p[:, 512:640])
    v = p[:, 640:768]
    kv_ref[:, 0:128] = k.astype(bf16)
    kv_ref[:, 128:256] = pltpu.roll(k, HEAD_DIM, 1).astype(bf16)
    kv_ref[:, 256:384] = v.astype(bf16)
    kv_ref[:, 384:512] = pltpu.roll(v, HEAD_DIM, 1).astype(bf16)

    u = _gelu(p[:, 768:1024])
    gv = _gelu(p[:, 1024:1280])
    mu = jnp.mean(gv, axis=-1, keepdims=True)
    var = jnp.mean(jnp.square(gv - mu), axis=-1, keepdims=True)
    vn_ref[...] = ((gv - mu) * lax.rsqrt(var + EPS) * lng_ref[...] + lnb_ref[...]).astype(bf16)

    mixf_ref[:, 0:256] = u
    mixf_ref[:, 256:512] = p[:, 1280:1536]
    mixf_ref[:, 512:768] = p[:, 1536:1792]
    mixf_ref[:, 768:1024] = p[:, 1792:2048] * p[:, 2048:2304]


def _inproj(x2, n_seq, shift, scale, g, w_bf, cos_t, sin_t, ln_g, ln_b):
    t, d = x2.shape
    bt = min(INPROJ_TOKENS, n_seq)
    tps = n_seq // bt
    ncol = w_bf.shape[1]
    row = lambda i: (i, 0)
    return pl.pallas_call(
        _inproj_kernel,
        grid=(t // bt,),
        in_specs=[pl.BlockSpec((bt, d), row),
                  pl.BlockSpec((None, 1, d), lambda i: (i // tps, 0, 0)),
                  pl.BlockSpec((None, 1, d), lambda i: (i // tps, 0, 0)),
                  _full(g),
                  pl.BlockSpec((d, ncol), lambda i: (0, 0)),
                  pl.BlockSpec((bt, LANES), lambda i: (i % tps, 0)),
                  pl.BlockSpec((bt, LANES), lambda i: (i % tps, 0)),
                  _full(ln_g), _full(ln_b)],
        out_specs=[pl.BlockSpec((bt, ATT_Q_W), row), pl.BlockSpec((bt, 512), row),
                   pl.BlockSpec((bt, SGU_W), row), pl.BlockSpec((bt, 1024), row)],
        out_shape=[jax.ShapeDtypeStruct((t, ATT_Q_W), bf16), jax.ShapeDtypeStruct((t, 512), bf16),
                   jax.ShapeDtypeStruct((t, SGU_W), bf16), jax.ShapeDtypeStruct((t, 1024), f32)],
        compiler_params=_params("arbitrary"),
        name="inproj",
    )(x2, shift, scale, g, w_bf, cos_t, sin_t, ln_g, ln_b)


def _attn_block(sink_ref, q, kv, bias, o_ref, row0):
    k_nat, k_swp, v_nat, v_swp = (kv[:, i * LANES:(i + 1) * LANES] for i in range(4))
    nq = q.shape[0]
    low = lax.broadcasted_iota(i32, (nq, LANES), 1) < HEAD_DIM
    top = lax.broadcasted_iota(i32, (2 * nq, 1), 0) < nq
    zero = jnp.zeros((nq, LANES), q.dtype)
    for kvh in range(2):
        chunks = [q[:, (2 * kvh + i) * LANES:(2 * kvh + i + 1) * LANES] for i in range(2)]
        outs = []
        for half in range(2):
            keep = low if half == 0 else jnp.logical_not(low)
            qz = jnp.concatenate([jnp.where(keep, c, zero) for c in chunks], axis=0)
            kh = k_nat if kvh == half else k_swp
            vh = v_nat if kvh == half else v_swp
            s = lax.dot_general(qz, kh, (((1,), (1,)), ((), ())), preferred_element_type=f32)
            if bias is not None:
                s = s + bias
            h0 = 4 * kvh + half
            sk = jnp.where(top, sink_ref[h0], sink_ref[h0 + 2])
            mx = jnp.maximum(jnp.max(s, axis=1, keepdims=True), sk)
            e = jnp.exp(s - mx)
            den = jnp.sum(e, axis=1, keepdims=True) + jnp.exp(sk - mx)
            outs.append(jnp.dot(e.astype(bf16), vh, preferred_element_type=f32) / den)
        for i in range(2):
            rows = slice(i * nq, (i + 1) * nq)
            m = 2 * kvh + i
            o_ref[row0:row0 + nq, m * LANES:(m + 1) * LANES] = jnp.where(low, outs[0][rows], outs[1][rows]).astype(bf16)


def _window_attn_kernel(sink_ref, q_ref, kvp_ref, kvm_ref, kvn_ref, kvx_ref, o_ref, *, nb, per_step):
    jj = pl.program_id(1)
    nq = ATTN_TOKENS
    nband = 3 * nq
    kv_all = jnp.concatenate([kvp_ref[...], kvm_ref[...], kvn_ref[...]], axis=0)
    kvx = kvx_ref[...]
    r = lax.broadcasted_iota(i32, (nq, nband), 0)
    c = lax.broadcasted_iota(i32, (nq, nband), 1)
    dlt = c - r
    in_window = (dlt >= 0) & (dlt <= 2 * WINDOW)
    ctx_zeros = jnp.zeros((nq, kvx.shape[0]), f32)
    for i in range(per_step):
        j = per_step * jj + i
        lo = jnp.where(j == 0, nq, 0)
        hi = jnp.where(j == nb - 1, 2 * nq, nband)
        valid = in_window & (c >= lo) & (c < hi)
        bias = jnp.concatenate([jnp.where(valid, 0.0, -jnp.inf).astype(f32), ctx_zeros], axis=1)
        kv = jnp.concatenate([kv_all[i * nq:i * nq + nband], kvx], axis=0)
        _attn_block(sink_ref, q_ref[i * nq:(i + 1) * nq, :], kv, jnp.concatenate([bias, bias], axis=0), o_ref, i * nq)


def _ctx_attn_kernel(sink_ref, q_ref, kvx_ref, o_ref):
    _attn_block(sink_ref, q_ref[...], kvx_ref[...], None, o_ref, 0)


def _window_attn(q, kv, kv_ctx, sink, n_batch, n_seq, n_ctx):
    t = q.shape[0]
    nb = n_seq // ATTN_TOKENS
    p = next(c for c in (ATTN_BLOCKS_PER_STEP, 2, 1) if nb % c == 0)
    steps = nb // p
    one, many = (ATTN_TOKENS, 512), (p * ATTN_TOKENS, 512)
    return pl.pallas_call(
        functools.partial(_window_attn_kernel, nb=nb, per_step=p),
        grid=(n_batch, steps),
        in_specs=[pl.BlockSpec(memory_space=pltpu.SMEM),
                  pl.BlockSpec(many, lambda b, j: (b * steps + j, 0)),
                  pl.BlockSpec(one, lambda b, j: (b * nb + jnp.maximum(p * j - 1, 0), 0)),
                  pl.BlockSpec(many, lambda b, j: (b * steps + j, 0)),
                  pl.BlockSpec(one, lambda b, j: (b * nb + jnp.minimum(p * j + p, nb - 1), 0)),
                  pl.BlockSpec((n_ctx, 512), lambda b, j: (b, 0))],
        out_specs=pl.BlockSpec(many, lambda b, j: (b * steps + j, 0)),
        out_shape=jax.ShapeDtypeStruct((t, ATT_Q_W), bf16),
        compiler_params=_params("arbitrary", "arbitrary"),
        name="window_attn",
    )(sink, q, kv, kv, kv, kv_ctx)


def _ctx_attn(q, kv_ctx, sink, n_batch, n_ctx):
    nb = n_ctx // ATTN_TOKENS
    blk = (ATTN_TOKENS, 512)
    return pl.pallas_call(
        _ctx_attn_kernel,
        grid=(n_batch, nb),
        in_specs=[pl.BlockSpec(memory_space=pltpu.SMEM),
                  pl.BlockSpec(blk, lambda b, j: (b * nb + j, 0)),
                  pl.BlockSpec((n_ctx, 512), lambda b, j: (b, 0))],
        out_specs=pl.BlockSpec(blk, lambda b, j: (b * nb + j, 0)),
        out_shape=jax.ShapeDtypeStruct(q.shape, bf16),
        compiler_params=_params("arbitrary", "arbitrary"),
        name="ctx_attn",
    )(sink, q, kv_ctx)


def _mixer_kernel(attn_ref, vn_ref, mf_ref, mfp_ref, mfn_ref, x_ref, g1_ref, sh2_ref, sc2_ref, n2g_ref,
                  ws_ref, sb_ref, pw_ref, ps_ref, cw_ref, wo_ref, rw_ref, rb_ref,
                  xmid_ref, h2_ref, *table_refs, n_seq, bt):
    s4_refs, s8_refs, gt_refs = (table_refs[j * TOP_K:(j + 1) * TOP_K] for j in range(3))
    cnt_ref = table_refs[3 * TOP_K]
    i = pl.program_id(0)
    tps = n_seq // bt
    si = i % tps
    first = si == 0
    last = si == tps - 1
    n_ext = bt + 2 * HALO

    mf = mf_ref[...]
    u = mf[:, 0:256]

    def extended(lo, hi):
        prev = jnp.where(first, 0.0, mfp_ref[:, lo:hi])
        nxt = jnp.where(last, 0.0, mfn_ref[:, lo:hi])
        return jnp.concatenate([prev, mf[:, lo:hi], nxt], axis=0)

    def shifted(a, s):
        return pltpu.roll(a, s % n_ext, 0)

    xe = extended(256, 512)
    a1 = shifted(xe, 1) + xe
    a2 = shifted(a1, 1) + shifted(a1, -1)
    a3 = shifted(a2, 2) + shifted(a2, -2)
    a4 = shifted(a3, 4) + shifted(a3, -4)
    lane = lax.broadcasted_iota(i32, (bt, POOL_CH), 1)
    grp = lane >> 6
    sl = slice(HALO, HALO + bt)
    wsum = jnp.where(grp == 0, a1[sl], jnp.where(grp == 1, a2[sl], jnp.where(grp == 2, a3[sl], a4[sl])))
    pos = lax.broadcasted_iota(i32, (bt, POOL_CH), 0) + si * bt
    halfw = jnp.left_shift(1, grp)
    cnt = jnp.minimum(pos + halfw, n_seq) - jnp.maximum(pos - halfw, 0)
    dpool = wsum / cnt.astype(f32) - mf[:, 256:512]
    yc = jnp.dot(dpool.astype(bf16), pw_ref[...], preferred_element_type=f32) * ps_ref[...]

    ye = extended(768, 1024)
    cw = cw_ref[...]
    z = shifted(ye, 1) * cw[0:1, :] + ye * cw[1:2, :] + shifted(ye, -1) * cw[2:3, :]
    yd = mf[:, 512:768] * z[sl]

    hgrp = lax.broadcasted_iota(i32, (SGU_CHUNK, SGU_W), 1) >> 6
    ybs = []
    for cidx in range(bt // SGU_CHUNK):
        rows = slice(cidx * SGU_CHUNK, (cidx + 1) * SGU_CHUNK)
        vn_c = vn_ref[rows, :]
        s = jnp.zeros((SGU_CHUNK, SGU_W), f32)
        for hh in range(SGU_HEADS):
            sh = jnp.dot(ws_ref[hh], vn_c, preferred_element_type=f32)
            s = jnp.where(hgrp == hh, sh, s)
        ybs.append(u[rows, :] * (s + sb_ref[...]))
    yb = jnp.concatenate(ybs, axis=0)

    mix = jnp.concatenate([attn_ref[...], yb.astype(bf16), yc.astype(bf16), yd.astype(bf16)], axis=1)
    mo = jnp.dot(mix, wo_ref[...], preferred_element_type=f32)
    xm = x_ref[...] + g1_ref[...] * mo
    xmid_ref[...] = xm

    y = xm * lax.rsqrt(jnp.mean(xm * xm, axis=-1, keepdims=True) + EPS) * n2g_ref[...]
    h2 = y * (1.0 + sc2_ref[...]) + sh2_ref[...]
    _store_packed_tiles(h2_ref, h2)

    def split(a):
        hi = a.astype(bf16)
        return hi, (a - hi.astype(f32)).astype(bf16)

    def dot_nt(a, b):
        return lax.dot_general(a, b, (((1,), (1,)), ((), ())), preferred_element_type=f32)

    (w_hi, w_lo), (h_hi, h_lo) = split(rw_ref[...]), split(h2)
    lt = dot_nt(w_hi, h_hi) + (dot_nt(w_hi, h_lo) + dot_nt(w_lo, h_hi)) + rb_ref[...]
    n_exp = lt.shape[0]
    eidx = lax.broadcasted_iota(i32, lt.shape, 0)
    work = lt
    idxs, vals = [], []
    for _ in range(TOP_K):
        m = jnp.max(work, axis=0, keepdims=True)
        idx = jnp.min(jnp.where(work == m, eidx, n_exp), axis=0, keepdims=True)
        idxs.append(idx)
        vals.append(m)
        work = jnp.where(eidx == idx, -jnp.inf, work)
    exps = [jnp.exp(v - vals[0]) for v in vals]
    den = exps[0] + exps[1] + exps[2] + exps[3]
    onehot = jnp.zeros(lt.shape, f32)
    for kk in range(TOP_K):
        gt_refs[kk][...] = exps[kk] / den
        onehot = onehot + (eidx == idxs[kk]).astype(f32)
    tri = (lax.broadcasted_iota(i32, (bt, bt), 0) < lax.broadcasted_iota(i32, (bt, bt), 1)).astype(bf16)
    before = jnp.dot(onehot.astype(bf16), tri, preferred_element_type=f32)
    cnt = jnp.sum(onehot, axis=1, keepdims=True)
    row = lax.broadcasted_iota(i32, (n_exp, LANES), 0)
    incl = jnp.broadcast_to(cnt, (n_exp, LANES))
    shift = 1
    while shift < n_exp:
        incl = incl + jnp.where(row >= shift, pltpu.roll(incl, shift, 0), 0.0)
        shift *= 2
    place = before + (incl[:, 0:1] - cnt)
    for kk in range(TOP_K):
        tile = jnp.sum(jnp.where(eidx == idxs[kk], place, 0.0), axis=0, keepdims=True).astype(i32)
        s4_refs[kk][...] = tile * PACKED_ROWS
        s8_refs[kk][...] = tile * SUBLANES
    cnt_ref[...] = cnt


def _mixer(attn, vn, mixf, x2, n_seq, g1, sh2, sc2, n2g, ws_bf, sgu_bias, pool_bd, pool_scale, conv_w,
           wo_bf, rw_t, rb):
    t, d = x2.shape
    bt = min(MIXER_TOKENS, n_seq)
    tps = n_seq // bt
    hb = bt // HALO
    n_halo = t // HALO
    n_exp = rw_t.shape[0]
    row = lambda i: (i, 0)
    per_batch = pl.BlockSpec((None, 1, d), lambda i: (i // tps, 0, 0))
    col = lambda i: (0, i)
    outs = pl.pallas_call(
        functools.partial(_mixer_kernel, n_seq=n_seq, bt=bt),
        grid=(t // bt,),
        in_specs=[pl.BlockSpec((bt, ATT_Q_W), row), pl.BlockSpec((bt, SGU_W), row), pl.BlockSpec((bt, 1024), row),
                  pl.BlockSpec((HALO, 1024), lambda i: (jnp.maximum(i * hb - 1, 0), 0)),
                  pl.BlockSpec((HALO, 1024), lambda i: (jnp.minimum((i + 1) * hb, n_halo - 1), 0)),
                  pl.BlockSpec((bt, d), row), per_batch, per_batch, per_batch, _full(n2g),
                  _full(ws_bf), _full(sgu_bias), _full(pool_bd), _full(pool_scale), _full(conv_w),
                  _full(wo_bf), _full(rw_t), _full(rb)],
        out_specs=[pl.BlockSpec((bt, d), row), pl.BlockSpec((bt * PACKED_ROWS, LANES), row)]
        + [pl.BlockSpec((1, bt), col)] * (3 * TOP_K) + [pl.BlockSpec((None, n_exp, 1), lambda i: (i, 0, 0))],
        out_shape=[jax.ShapeDtypeStruct((t, d), f32), jax.ShapeDtypeStruct((t * PACKED_ROWS, LANES), u32)]
        + [jax.ShapeDtypeStruct((1, t), i32)] * (2 * TOP_K) + [jax.ShapeDtypeStruct((1, t), f32)] * TOP_K
        + [jax.ShapeDtypeStruct((t // bt, n_exp, 1), f32)],
        compiler_params=_params("arbitrary"),
        name="mixer_router",
    )(attn, vn, mixf, mixf, mixf, x2, g1, sh2, sc2, n2g, ws_bf, sgu_bias, pool_bd, pool_scale, conv_w,
      wo_bf, rw_t, rb)
    tables = [o.reshape(t) for o in outs[2:2 + 3 * TOP_K]]
    return outs[0], outs[1], tables[:TOP_K], tables[TOP_K:2 * TOP_K], tables[2 * TOP_K:], outs[-1]


def _per_assignment_specs(bt):
    return [pl.BlockSpec((bt,), lambda i: (i,), memory_space=pltpu.SMEM)] * TOP_K


def _tile_rows(t, n=1, rows=SUBLANES):
    return pl.ds(pl.multiple_of(t * rows, rows), n * rows)


def _tile_copy(src, s, dst, d, sem, rows=SUBLANES):
    return pltpu.make_async_copy(src.at[_tile_rows(s, 1, rows)], dst.at[_tile_rows(d, 1, rows)], sem)


def _for_each_run_piece(runs_ref, w, n_exp, max_len, fn):
    bits = [1 << b for b in range(max_len.bit_length() - 1, -1, -1)]

    def per_expert(e, carry):
        base = (w * n_exp + e) * 3
        stage0, slot0, length = runs_ref[base], runs_ref[base + 1], runs_ref[base + 2]
        done = 0
        for bit in bits:
            take = length & bit

            @pl.when(take != 0)
            def _():
                fn(stage0 + done, slot0 + done, bit)

            done = done + take
        return carry

    lax.fori_loop(0, n_exp, per_expert, 0)


def _wait_window(stage, buf, sem):
    pltpu.make_async_copy(stage.at[buf], stage.at[buf], sem.at[buf]).wait()


def _per_buffer(slot, fn):
    for b in range(2):
        @pl.when(slot == b)
        def _():
            fn(b)


def _split_slabs(w_ref, g_ref, l_ref, scr):
    f = g_ref.shape[1]
    for s in range(w_ref.shape[0] // LANES):
        rows = slice(s * LANES, (s + 1) * LANES)
        scr[...] = w_ref[rows, :].T
        g_ref[rows, :] = scr[pl.ds(0, f, stride=2), :].T.astype(bf16)
        l_ref[rows, :] = scr[pl.ds(1, f, stride=2), :].T.astype(bf16)


def _split_w1(w1, layer):
    _, n_exp, d, f2 = w1.shape
    f = f2 // 2
    rows = 2 * LANES
    blk = pl.BlockSpec((None, rows, f), lambda e, j: (e, j, 0))
    return pl.pallas_call(
        _split_slabs,
        grid=(n_exp, d // rows),
        in_specs=[pl.BlockSpec((None, None, rows, f2), lambda e, j: (layer, e, j, 0))],
        out_specs=[blk, blk],
        out_shape=[jax.ShapeDtypeStruct((n_exp, d, f), bf16)] * 2,
        scratch_shapes=[pltpu.VMEM((f2, LANES), f32)],
        compiler_params=_params("arbitrary", "arbitrary"),
        name="split_w1",
    )(w1)


class _SplitSide:
    def __init__(self, w1, layer, n_steps):
        _, n_exp, d, f2 = w1.shape
        fits = [r for r in (LANES, 2 * LANES, 4 * LANES, 8 * LANES) if d % r == 0 and n_exp * (d // r) <= n_steps]
        self.ok = bool(fits)
        if not self.ok:
            return
        rows = fits[0]
        per = d // rows
        self.w1 = w1
        self.n_side = n_exp * per
        last = self.n_side - 1
        self.in_spec = pl.BlockSpec((None, None, rows, f2),
                                    lambda i: (layer, jnp.minimum(i, last) // per, jnp.minimum(i, last) % per, 0))
        self.out_spec = pl.BlockSpec((None, rows, f2 // 2),
                                     lambda i: (jnp.minimum(i, last) // per, jnp.minimum(i, last) % per, 0))
        self.out_shape = jax.ShapeDtypeStruct((n_exp, d, f2 // 2), bf16)
        self.scratch = pltpu.VMEM((f2, LANES), f32)

    def run(self, w_ref, g_ref, l_ref, scr):
        @pl.when(pl.program_id(0) < self.n_side)
        def _():
            _split_slabs(w_ref, g_ref, l_ref, scr)


def _dispatch_kernel(*refs, bt, w0, n_exp, fill_pad, side):
    row_refs, (runs_ref, pad_ref, h_ref), rest = refs[:TOP_K], refs[TOP_K:TOP_K + 3], list(refs[TOP_K + 3:])
    if not fill_pad:
        rest.pop(0)
    w_ref = rest.pop(0) if side else None
    xs_ref = rest.pop(0)
    g_ref, l_ref = (rest.pop(0), rest.pop(0)) if side else (None, None)
    zero_ref = rest.pop(0) if fill_pad else None
    stage, sem = rest.pop(0), rest.pop(0)
    i = pl.program_id(0)
    slot = i % 2

    if fill_pad:
        @pl.when(i == 0)
        def _():
            zero_ref[...] = jnp.zeros_like(zero_ref)

            def per_expert(e, carry):
                first, count = pad_ref[e], pad_ref[n_exp + e]

                def issue(r, c):
                    _tile_copy(zero_ref, 0, xs_ref, first + r, sem.at[0], PACKED_ROWS).start()
                    return c

                def drain(r, c):
                    _tile_copy(zero_ref, 0, xs_ref, 0, sem.at[0], PACKED_ROWS).wait()
                    return c

                lax.fori_loop(0, count, issue, 0)
                lax.fori_loop(0, count, drain, 0)
                return carry

            lax.fori_loop(0, n_exp, per_expert, 0)

    def run_copy(buf):
        def make(stage_tile, slot_tile, n):
            return pltpu.make_async_copy(stage.at[buf, _tile_rows(stage_tile, n, PACKED_ROWS)],
                                         xs_ref.at[_tile_rows(slot_tile, n, PACKED_ROWS)], sem.at[buf])
        return make

    def place_all(b):
        def place(t, carry):
            tile = h_ref[_tile_rows(t, 1, PACKED_ROWS), :]
            for kk in range(TOP_K):
                row = pl.multiple_of(row_refs[kk][t], PACKED_ROWS)
                stage[b, pl.ds(row, PACKED_ROWS), :] = tile
            return carry

        lax.fori_loop(0, bt, place, 0, unroll=ROW_UNROLL)

    _per_buffer(slot, place_all)
    _for_each_run_piece(runs_ref, w0 + i, n_exp, bt, lambda a, b, n: run_copy(slot)(a, b, n).start())
    if side:
        side.run(w_ref, g_ref, l_ref, rest.pop(0))

    @pl.when(i > 0)
    def _():
        _wait_window(stage, 1 - slot, sem)

    @pl.when(i == pl.num_programs(0) - 1)
    def _():
        _wait_window(stage, slot, sem)


def _dispatch(rows, runs, pad, h2, xs, n_slots, bt, w0, n_exp, w1_side=None):
    t = h2.shape[0] // PACKED_ROWS
    n_steps = t // bt
    fill_pad = xs is None
    side = _SplitSide(*w1_side, n_steps) if w1_side else None
    side = side if side is not None and side.ok else None
    smem = pl.BlockSpec(memory_space=pltpu.SMEM)
    in_specs = _per_assignment_specs(bt) + [smem, smem, pl.BlockSpec((bt * PACKED_ROWS, LANES), lambda i: (i, 0))]
    args = list(rows) + [runs, pad, h2]
    out_specs = [pl.BlockSpec(memory_space=pl.ANY)]
    out_shape = [jax.ShapeDtypeStruct((n_slots * PACKED_ROWS, LANES), u32)]
    scratch = [pltpu.VMEM((2, TOP_K * bt * PACKED_ROWS, LANES), u32), pltpu.SemaphoreType.DMA((2,))]
    if fill_pad:
        scratch = [pltpu.VMEM((PACKED_ROWS, LANES), u32)] + scratch
    else:
        in_specs.append(pl.BlockSpec(memory_space=pl.ANY))
        args.append(xs)
    if side:
        in_specs.append(side.in_spec)
        args.append(side.w1)
        out_specs += [side.out_spec] * 2
        out_shape += [side.out_shape] * 2
        scratch.append(side.scratch)
    outs = pl.pallas_call(
        functools.partial(_dispatch_kernel, bt=bt, w0=w0, n_exp=n_exp, fill_pad=fill_pad, side=side),
        grid=(n_steps,),
        in_specs=in_specs,
        out_specs=out_specs,
        out_shape=out_shape,
        scratch_shapes=scratch,
        input_output_aliases={} if fill_pad else {TOP_K + 3: 0},
        compiler_params=_params("arbitrary"),
        name="dispatch",
    )(*args)
    return outs[0], (tuple(outs[1:]) if side else None)


def _expert_kernel(be_ref, nu_ref, x_ref, w1g_ref, w1l_ref, b1g_ref, b1l_ref, w2f_ref, b2_ref, y_ref, w2_ref):
    i = pl.program_id(0)
    active = i < nu_ref[0]
    new_expert = jnp.logical_or(i == 0, be_ref[i] != be_ref[jnp.maximum(i - 1, 0)])

    @pl.when(jnp.logical_and(active, new_expert))
    def _():
        w2_ref[...] = w2f_ref[...].astype(bf16)

    @pl.when(active)
    def _():
        x = _load_packed_tiles(x_ref, x_ref.shape[0] // PACKED_ROWS)
        zg = jnp.dot(x, w1g_ref[...], preferred_element_type=f32) + b1g_ref[...]
        zl = jnp.dot(x, w1l_ref[...], preferred_element_type=f32) + b1l_ref[...]
        g = jnp.minimum(zg, SWIGLU_LIMIT)
        lin = jnp.clip(zl, -SWIGLU_LIMIT, SWIGLU_LIMIT)
        a = g * jax.nn.sigmoid(SWIGLU_ALPHA * g) * (lin + 1.0)
        y = jnp.dot(a.astype(bf16), w2_ref[...], preferred_element_type=f32) + b2_ref[...]
        _store_token_tiles(y_ref, y)

    @pl.when(i >= nu_ref[0])
    def _():
        y_ref[...] = jnp.zeros_like(y_ref)


def _experts(block_e, n_used, xs, w1g, w1l, b1g, b1l, w2_all, layer, b2):
    n_exp, d, f = w1g.shape
    bm = EXPERT_ROWS
    n_slots = xs.shape[0] // PACKED_ROWS
    blk = (bm * SUBLANES, LANES)
    xrow = lambda i, be, nu: (jnp.minimum(i, nu[0] - 1), 0)
    wsel = lambda i, be, nu: (be[i], 0, 0)
    grid_spec = pltpu.PrefetchScalarGridSpec(
        num_scalar_prefetch=2,
        grid=(n_slots // bm,),
        in_specs=[pl.BlockSpec((bm * PACKED_ROWS, LANES), xrow),
                  pl.BlockSpec((None, d, f), wsel), pl.BlockSpec((None, d, f), wsel),
                  pl.BlockSpec((None, 1, f), wsel), pl.BlockSpec((None, 1, f), wsel),
                  pl.BlockSpec((None, None, f, d), lambda i, be, nu: (layer, be[i], 0, 0)),
                  pl.BlockSpec((None, 1, d), wsel)],
        out_specs=pl.BlockSpec(blk, lambda i, be, nu: (i, 0)),
        scratch_shapes=[pltpu.VMEM((f, d), bf16)],
    )
    return pl.pallas_call(
        _expert_kernel,
        grid_spec=grid_spec,
        out_shape=jax.ShapeDtypeStruct((n_slots * SUBLANES, LANES), f32),
        compiler_params=_params("arbitrary"),
        name="experts",
    )(block_e, n_used, xs, w1g, w1l, b1g, b1l, w2_all, b2)


def _combine_kernel(*refs, bt, w0, n_exp, final, side):
    row_refs, gate_refs = refs[:TOP_K], refs[TOP_K:2 * TOP_K]
    (runs_ref, ys_ref, x_ref, g2_ref, fg_ref), rest = refs[2 * TOP_K:2 * TOP_K + 5], list(refs[2 * TOP_K + 5:])
    w_ref = rest.pop(0) if side else None
    o_ref = rest.pop(0)
    g_ref, l_ref = (rest.pop(0), rest.pop(0)) if side else (None, None)
    stage, acc_ref, sem = rest.pop(0), rest.pop(0), rest.pop(0)
    i = pl.program_id(0)
    slot = i % 2

    def run_copy(buf):
        def make(stage_tile, slot_tile, n):
            return pltpu.make_async_copy(ys_ref.at[_tile_rows(slot_tile, n)], stage.at[buf, _tile_rows(stage_tile, n)],
                                         sem.at[buf])
        return make

    def fetch(w, buf):
        _for_each_run_piece(runs_ref, w, n_exp, bt, lambda a, b, n: run_copy(buf)(a, b, n).start())

    @pl.when(i == 0)
    def _():
        fetch(w0, 0)

    @pl.when(i + 1 < pl.num_programs(0))
    def _():
        fetch(w0 + i + 1, 1 - slot)

    if side:
        side.run(w_ref, g_ref, l_ref, rest.pop(0))

    _wait_window(stage, slot, sem)

    def sum_all(b):
        def token(t, carry):
            acc = None
            for kk in range(TOP_K):
                row = pl.multiple_of(row_refs[kk][t], SUBLANES)
                term = gate_refs[kk][t] * stage[b, pl.ds(row, SUBLANES), :]
                acc = term if acc is None else acc + term
            acc_ref[_tile_rows(t), :] = acc
            return carry

        lax.fori_loop(0, bt, token, 0, unroll=ROW_UNROLL)

    _per_buffer(slot, sum_all)
    xo = x_ref[...] + g2_ref[...] * _load_token_tiles(acc_ref, bt)
    if final:
        xo = xo * lax.rsqrt(jnp.mean(xo * xo, axis=-1, keepdims=True) + EPS) * fg_ref[...]
    o_ref[...] = xo


def _combine(rows, gates, runs, ys, x_mid, n_seq, g2, final_g, final, bt, w0, n_exp, w1_side=None):
    t, d = x_mid.shape
    tps = n_seq // bt
    row = lambda i: (i, 0)
    n_steps = t // bt
    side = _SplitSide(*w1_side, n_steps) if w1_side else None
    side = side if side is not None and side.ok else None
    in_specs = _per_assignment_specs(bt) + _per_assignment_specs(bt) + [
        pl.BlockSpec(memory_space=pltpu.SMEM),
        pl.BlockSpec(memory_space=pl.ANY),
        pl.BlockSpec((bt, d), row),
        pl.BlockSpec((None, 1, d), lambda i: (i // tps, 0, 0)),
        _full(final_g)]
    args = list(rows) + list(gates) + [runs, ys, x_mid, g2, final_g]
    out_specs = [pl.BlockSpec((bt, d), row)]
    out_shape = [jax.ShapeDtypeStruct((t, d), f32)]
    scratch = [pltpu.VMEM((2, TOP_K * bt * SUBLANES, LANES), f32), pltpu.VMEM((bt * SUBLANES, LANES), f32),
               pltpu.SemaphoreType.DMA((2,))]
    if side:
        in_specs.append(side.in_spec)
        args.append(side.w1)
        out_specs += [side.out_spec] * 2
        out_shape += [side.out_shape] * 2
        scratch.append(side.scratch)
    outs = pl.pallas_call(
        functools.partial(_combine_kernel, bt=bt, w0=w0, n_exp=n_exp, final=final, side=side),
        grid=(n_steps,),
        in_specs=in_specs,
        out_specs=out_specs,
        out_shape=out_shape,
        scratch_shapes=scratch,
        compiler_params=_params("arbitrary"),
        name="combine",
    )(*args)
    return outs[0], (tuple(outs[1:]) if side else None)


def _rope_tables(n_tokens):
    rows = n_tokens // GRID_W
    row = jnp.repeat(jnp.arange(rows), GRID_W).astype(f32)
    col = jnp.tile(jnp.arange(GRID_W), rows).astype(f32)
    n_freq = HEAD_DIM // 4
    inv = ROPE_BASE ** (-jnp.arange(n_freq, dtype=f32) / n_freq)
    ang = jnp.concatenate([row[:, None] * inv, col[:, None] * inv], axis=-1)
    cos, sin = jnp.cos(ang), jnp.sin(ang)
    return jnp.tile(cos, (1, 4)), jnp.concatenate([-sin, sin, -sin, sin], axis=-1)


def kernel(x, c, ctx, c_ctx, norm1_g, norm2_g, ada_w, ada_b, w_in, attn_sink, sgu_ws, sgu_b, sgu_ln_g, sgu_ln_b,
           pool_w, pool_scale, conv_w, w_out, router_w, router_b, exp_w1, exp_b1, exp_w2, exp_b2, final_g):
    n_batch, n_seq, d = x.shape
    n_ctx = ctx.shape[1]
    depth = ada_w.shape[0]
    n_exp = router_w.shape[2]
    assert d == SUBLANES * LANES, "token rows are moved as single (8, 128) tiles"
    t_lat, t_ctx = n_batch * n_seq, n_batch * n_ctx
    bm = EXPERT_ROWS

    cvec = jnp.concatenate([c, c_ctx[None, :], jnp.zeros((SUBLANES - n_batch - 1, d), f32)], axis=0)
    mods = _ada(cvec, ada_w, ada_b)

    cos_l, sin_l = _rope_tables(n_seq)
    cos_c, sin_c = jnp.ones((n_ctx, LANES), f32), jnp.zeros((n_ctx, LANES), f32)

    xl = x.reshape(t_lat, d)
    xc = ctx.reshape(t_ctx, d)
    row2 = lambda a: a.reshape(1, -1)
    w1_split = {}

    for l in range(depth):
        last = l == depth - 1
        ml = mods[l, :n_batch].reshape(n_batch, 6, 1, d)
        mc = jnp.broadcast_to(mods[l, n_batch].reshape(1, 6, 1, d), (n_batch, 6, 1, d))
        sh1l, sc1l, g1l, sh2l, sc2l, g2l = (ml[:, i] for i in range(6))
        sh1c, sc1c, g1c, sh2c, sc2c, g2c = (mc[:, i] for i in range(6))

        w_in_bf = w_in[l].astype(bf16)
        wo_bf = w_out[l].astype(bf16)
        ws_bf = sgu_ws[l].astype(bf16)
        sgu_bias = jnp.repeat(sgu_b[l].T, SGU_W // SGU_HEADS, axis=1)
        pool_bd = jax.scipy.linalg.block_diag(*[pool_w[l, g] for g in range(pool_w.shape[1])]).astype(bf16)
        n1g, n2g = row2(norm1_g[l]), row2(norm2_g[l])
        lng, lnb, psc = row2(sgu_ln_g[l]), row2(sgu_ln_b[l]), row2(pool_scale[l])
        rw_t = router_w[l].T
        rb = router_b[l].reshape(n_exp, 1)
        sink = attn_sink[l]
        mix_w = (ws_bf, sgu_bias, pool_bd, psc, conv_w[l], wo_bf, rw_t, rb)

        qc, kvc, vnc, mfc = _inproj(xc, n_ctx, sh1c, sc1c, n1g, w_in_bf, cos_c, sin_c, lng, lnb)
        ql, kvl, vnl, mfl = _inproj(xl, n_seq, sh1l, sc1l, n1g, w_in_bf, cos_l, sin_l, lng, lnb)
        attn_l = _window_attn(ql, kvl, kvc, sink, n_batch, n_seq, n_ctx)
        bt_l, bt_c = min(MIXER_TOKENS, n_seq), min(MIXER_TOKENS, n_ctx)
        xmid_l, h2_l, s4_l, s8_l, gt_l, cnt = _mixer(attn_l, vnl, mfl, xl, n_seq, g1l, sh2l, sc2l, n2g, *mix_w)
        n_win_l = cnt.shape[0]
        t_all = t_lat
        if not last:
            attn_c = _ctx_attn(qc, kvc, sink, n_batch, n_ctx)
            xmid_c, h2_c, s4_c, s8_c, gt_c, cnt_c = _mixer(attn_c, vnc, mfc, xc, n_ctx, g1c, sh2c, sc2c, n2g, *mix_w)
            cnt = jnp.concatenate([cnt, cnt_c], axis=0)
            t_all = t_lat + t_ctx

        win_cnt = cnt[:, :, 0].astype(i32)
        counts = jnp.sum(win_cnt, axis=0)
        padded = (counts + bm - 1) // bm * bm
        pends = jnp.cumsum(padded)
        pstarts = pends - padded
        n_blocks = -(-(t_all * TOP_K) // bm) + n_exp
        starts = jnp.arange(n_blocks, dtype=i32) * bm
        block_e = jnp.minimum(jnp.sum((pends[None, :] <= starts[:, None]).astype(i32), axis=1), n_exp - 1)
        n_used = (pends[-1:] // bm).astype(i32)
        pad = jnp.concatenate([pstarts + counts, padded - counts]).astype(i32)
        run_stage = jnp.cumsum(win_cnt, axis=1) - win_cnt
        run_slot = pstarts[None, :] + jnp.cumsum(win_cnt, axis=0) - win_cnt
        runs = jnp.stack([run_stage, run_slot, win_cnt], axis=-1).reshape(-1)

        xs, split = _dispatch(s4_l, runs, pad, h2_l, None, n_blocks * bm, bt_l, 0, n_exp,
                              None if l in w1_split else (exp_w1, l))
        if l not in w1_split:
            w1_split[l] = split or _split_w1(exp_w1, l)
        if not last:
            xs, _ = _dispatch(s4_c, runs, pad, h2_c, xs, n_blocks * bm, bt_c, n_win_l, n_exp)

        w1g, w1l = w1_split[l]
        b1 = exp_b1[l]
        b1g, b1l = b1[:, None, 0::2], b1[:, None, 1::2]
        ys = _experts(block_e, n_used, xs, w1g, w1l, b1g, b1l, exp_w2, l, exp_b2[l][:, None, :])

        fg = row2(final_g)
        xl, split = _combine(s8_l, gt_l, runs, ys, xmid_l, n_seq, g2l, fg, last, bt_l, 0, n_exp,
                             None if last else (exp_w1, l + 1))
        if split:
            w1_split[l + 1] = split
        if not last:
            xc, _ = _combine(s8_c, gt_c, runs, ys, xmid_c, n_ctx, g2c, fg, False, bt_c, n_win_l, n_exp)

    return xl.reshape(n_batch, n_seq, d)
```

```python
import functools

import jax
import jax.numpy as jnp
from jax import lax
from jax.experimental import pallas as pl
from jax.experimental.pallas import tpu as pltpu

f32 = jnp.float32
bf16 = jnp.bfloat16
i32 = jnp.int32
u32 = jnp.uint32

GRID_W = 64
EPS = 1e-6
N_KV_HEADS = 2
GQA_GROUP = 4
HEAD_DIM = 64
WINDOW = 128
ROPE_BASE = 10000.0
ATT_Q_W = 512
SGU_HEADS = 4
SGU_W = 256
SGU_CHUNK = 128
POOL_CH = 256
TOP_K = 4
SWIGLU_LIMIT = 7.0
SWIGLU_ALPHA = 1.702
SQRT_HALF = 0.7071067811865476

LANES = 128
SUBLANES = 8
PACKED_ROWS = SUBLANES // 2
VMEM_LIMIT_BYTES = 56 * 1024 * 1024

INPROJ_TOKENS = 512
ATTN_TOKENS = 128
ATTN_BLOCKS_PER_STEP = 4
MIXER_TOKENS = 512
EXPERT_ROWS = 512
HALO = 8
ROW_UNROLL = 16


def _params(*sem):
    return pltpu.CompilerParams(dimension_semantics=sem, vmem_limit_bytes=VMEM_LIMIT_BYTES)


def _full(a):
    nd = a.ndim
    return pl.BlockSpec(a.shape, lambda *_: (0,) * nd)


def _store_token_tiles(ref, val):
    n = val.shape[0]
    for s in range(SUBLANES):
        ref[pl.ds(s, n, stride=SUBLANES), :] = val[:, s * LANES:(s + 1) * LANES]


def _load_token_tiles(ref, n):
    return jnp.concatenate([ref[pl.ds(s, n, stride=SUBLANES), :] for s in range(SUBLANES)], axis=1)


HIGH_HALF = 0xFFFF0000


def _store_packed_tiles(ref, val):
    n = val.shape[0]
    half = PACKED_ROWS * LANES
    for s in range(PACKED_ROWS):
        lo = val[:, s * LANES:(s + 1) * LANES].astype(bf16).astype(f32)
        hi = val[:, half + s * LANES:half + (s + 1) * LANES].astype(bf16).astype(f32)
        lo_bits = lax.shift_right_logical(lax.bitcast_convert_type(lo, u32), jnp.uint32(16))
        hi_bits = lax.bitcast_convert_type(hi, u32) & jnp.uint32(HIGH_HALF)
        ref[pl.ds(s, n, stride=PACKED_ROWS), :] = lo_bits | hi_bits


def _load_packed_tiles(ref, n):
    los, his = [], []
    for s in range(PACKED_ROWS):
        w = ref[pl.ds(s, n, stride=PACKED_ROWS), :]
        los.append(lax.bitcast_convert_type(lax.shift_left(w, jnp.uint32(16)), f32))
        his.append(lax.bitcast_convert_type(w & jnp.uint32(HIGH_HALF), f32))
    return jnp.concatenate(los + his, axis=1).astype(bf16)


def _gelu(x):
    return 0.5 * x * (1.0 + lax.erf(x * SQRT_HALF))


def _ada_kernel(c_ref, w_ref, b_ref, o_ref):
    c = c_ref[...]
    s = c * jax.nn.sigmoid(c)
    o_ref[...] = jnp.dot(s, w_ref[...], precision=lax.Precision.HIGHEST, preferred_element_type=f32) + b_ref[...]


def _ada(cvec, ada_w, ada_b):
    depth, d, n = ada_w.shape
    tn = 1536
    return pl.pallas_call(
        _ada_kernel,
        grid=(depth, n // tn),
        in_specs=[pl.BlockSpec(cvec.shape, lambda l, j: (0, 0)),
                  pl.BlockSpec((None, d, tn), lambda l, j: (l, 0, j)),
                  pl.BlockSpec((None, 1, tn), lambda l, j: (l, 0, j))],
        out_specs=pl.BlockSpec((None, cvec.shape[0], tn), lambda l, j: (l, 0, j)),
        out_shape=jax.ShapeDtypeStruct((depth, cvec.shape[0], n), f32),
        compiler_params=_params("arbitrary", "arbitrary"),
        name="ada_mod",
    )(cvec, ada_w, ada_b.reshape(depth, 1, n))


def _inproj_kernel(x_ref, sh_ref, sc_ref, g_ref, w_ref, cos_ref, sin_ref, lng_ref, lnb_ref,
                   q_ref, kv_ref, vn_ref, mixf_ref):
    x = x_ref[...]
    y = x * lax.rsqrt(jnp.mean(x * x, axis=-1, keepdims=True) + EPS) * g_ref[...]
    h = y * (1.0 + sc_ref[...]) + sh_ref[...]
    p = jnp.dot(h.astype(bf16), w_ref[...], preferred_element_type=f32)

    cos = cos_ref[...]
    sin = sin_ref[...]
    lane = lax.broadcasted_iota(i32, cos.shape, 1)
    first_half = (lane & (HEAD_DIM - 1)) < HEAD_DIM // 2

    def rope(t):
        partner = jnp.where(first_half, pltpu.roll(t, LANES - HEAD_DIM // 2, 1), pltpu.roll(t, HEAD_DIM // 2, 1))
        return t * cos + partner * sin

    scale = HEAD_DIM ** -0.5
    for m in range(ATT_Q_W // LANES):
        q_ref[:, m * LANES:(m + 1) * LANES] = (rope(p[:, m * LANES:(m + 1) * LANES]) * scale).astype(bf16)
    k = rope(p[:, 512:640])
    v = p[:, 640:768]
    kv_ref[:, 0:128] = k.astype(bf16)
    kv_ref[:, 128:256] = pltpu.roll(k, HEAD_DIM, 1).astype(bf16)
    kv_ref[:, 256:384] = v.astype(bf16)
    kv_ref[:, 384:512] = pltpu.roll(v, HEAD_DIM, 1).astype(bf16)

    u = _gelu(p[:, 768:1024])
    gv = _gelu(p[:, 1024:1280])
    mu = jnp.mean(gv, axis=-1, keepdims=True)
    var = jnp.mean(jnp.square(gv - mu), axis=-1, keepdims=True)
    vn_ref[...] = ((gv - mu) * lax.rsqrt(var + EPS) * lng_ref[...] + lnb_ref[...]).astype(bf16)

    mixf_ref[:, 0:256] = u
    mixf_ref[:, 256:512] = p[:, 1280:1536]
    mixf_ref[:, 512:768] = p[:, 1536:1792]
    mixf_ref[:, 768:1024] = p[:, 1792:2048] * p[:, 2048:2304]


def _inproj(x2, n_seq, shift, scale, g, w_bf, cos_t, sin_t, ln_g, ln_b):
    t, d = x2.shape
    bt = min(INPROJ_TOKENS, n_seq)
    tps = n_seq // bt
    ncol = w_bf.shape[1]
    row = lambda i: (i, 0)
    return pl.pallas_call(
        _inproj_kernel,
        grid=(t // bt,),
        in_specs=[pl.BlockSpec((bt, d), row),
                  pl.BlockSpec((None, 1, d), lambda i: (i // tps, 0, 0)),
                  pl.BlockSpec((None, 1, d), lambda i: (i // tps, 0, 0)),
                  _full(g),
                  pl.BlockSpec((d, ncol), lambda i: (0, 0)),
                  pl.BlockSpec((bt, LANES), lambda i: (i % tps, 0)),
                  pl.BlockSpec((bt, LANES), lambda i: (i % tps, 0)),
                  _full(ln_g), _full(ln_b)],
        out_specs=[pl.BlockSpec((bt, ATT_Q_W), row), pl.BlockSpec((bt, 512), row),
                   pl.BlockSpec((bt, SGU_W), row), pl.BlockSpec((bt, 1024), row)],
        out_shape=[jax.ShapeDtypeStruct((t, ATT_Q_W), bf16), jax.ShapeDtypeStruct((t, 512), bf16),
                   jax.ShapeDtypeStruct((t, SGU_W), bf16), jax.ShapeDtypeStruct((t, 1024), f32)],
        compiler_params=_params("arbitrary"),
        name="inproj",
    )(x2, shift, scale, g, w_bf, cos_t, sin_t, ln_g, ln_b)


def _attn_block(sink_ref, q, kv, bias, o_ref, row0):
    k_nat, k_swp, v_nat, v_swp = (kv[:, i * LANES:(i + 1) * LANES] for i in range(4))
    nq = q.shape[0]
    low = lax.broadcasted_iota(i32, (nq, LANES), 1) < HEAD_DIM
    top = lax.broadcasted_iota(i32, (2 * nq, 1), 0) < nq
    zero = jnp.zeros((nq, LANES), q.dtype)
    assert (N_KV_HEADS, GQA_GROUP, 2 * HEAD_DIM) == (2, 4, LANES), "two heads per 128-lane chunk, two chunks per kv head"
    for kvh in range(N_KV_HEADS):
        chunks = [q[:, (2 * kvh + i) * LANES:(2 * kvh + i + 1) * LANES] for i in range(2)]
        outs = []
        for half in range(2):
            keep = low if half == 0 else jnp.logical_not(low)
            qz = jnp.concatenate([jnp.where(keep, c, zero) for c in chunks], axis=0)
            kh = k_nat if kvh == half else k_swp
            vh = v_nat if kvh == half else v_swp
            s = lax.dot_general(qz, kh, (((1,), (1,)), ((), ())), preferred_element_type=f32)
            if bias is not None:
                s = s + bias
            h0 = GQA_GROUP * kvh + half
            sk = jnp.where(top, sink_ref[h0], sink_ref[h0 + 2])
            mx = jnp.maximum(jnp.max(s, axis=1, keepdims=True), sk)
            e = jnp.exp(s - mx)
            den = jnp.sum(e, axis=1, keepdims=True) + jnp.exp(sk - mx)
            outs.append(jnp.dot(e.astype(bf16), vh, preferred_element_type=f32) / den)
        for i in range(2):
            rows = slice(i * nq, (i + 1) * nq)
            m = 2 * kvh + i
            o_ref[row0:row0 + nq, m * LANES:(m + 1) * LANES] = jnp.where(low, outs[0][rows], outs[1][rows]).astype(bf16)


def _window_attn_kernel(sink_ref, q_ref, kvp_ref, kvm_ref, kvn_ref, kvx_ref, o_ref, *, nb, per_step):
    jj = pl.program_id(1)
    nq = ATTN_TOKENS
    nband = 3 * nq
    kv_all = jnp.concatenate([kvp_ref[...], kvm_ref[...], kvn_ref[...]], axis=0)
    kvx = kvx_ref[...]
    r = lax.broadcasted_iota(i32, (nq, nband), 0)
    c = lax.broadcasted_iota(i32, (nq, nband), 1)
    dlt = c - r
    in_window = (dlt >= 0) & (dlt <= 2 * WINDOW)
    ctx_zeros = jnp.zeros((nq, kvx.shape[0]), f32)
    for i in range(per_step):
        j = per_step * jj + i
        lo = jnp.where(j == 0, nq, 0)
        hi = jnp.where(j == nb - 1, 2 * nq, nband)
        valid = in_window & (c >= lo) & (c < hi)
        bias = jnp.concatenate([jnp.where(valid, 0.0, -jnp.inf).astype(f32), ctx_zeros], axis=1)
        kv = jnp.concatenate([kv_all[i * nq:i * nq + nband], kvx], axis=0)
        _attn_block(sink_ref, q_ref[i * nq:(i + 1) * nq, :], kv, jnp.concatenate([bias, bias], axis=0), o_ref, i * nq)


def _ctx_attn_kernel(sink_ref, q_ref, kvx_ref, o_ref):
    _attn_block(sink_ref, q_ref[...], kvx_ref[...], None, o_ref, 0)


def _window_attn(q, kv, kv_ctx, sink, n_batch, n_seq, n_ctx):
    t = q.shape[0]
    nb = n_seq // ATTN_TOKENS
    p = next(c for c in (ATTN_BLOCKS_PER_STEP, 2, 1) if nb % c == 0)
    steps = nb // p
    one, many = (ATTN_TOKENS, 512), (p * ATTN_TOKENS, 512)
    return pl.pallas_call(
        functools.partial(_window_attn_kernel, nb=nb, per_step=p),
        grid=(n_batch, steps),
        in_specs=[pl.BlockSpec(memory_space=pltpu.SMEM),
                  pl.BlockSpec(many, lambda b, j: (b * steps + j, 0)),
                  pl.BlockSpec(one, lambda b, j: (b * nb + jnp.maximum(p * j - 1, 0), 0)),
                  pl.BlockSpec(many, lambda b, j: (b * steps + j, 0)),
                  pl.BlockSpec(one, lambda b, j: (b * nb + jnp.minimum(p * j + p, nb - 1), 0)),
                  pl.BlockSpec((n_ctx, 512), lambda b, j: (b, 0))],
        out_specs=pl.BlockSpec(many, lambda b, j: (b * steps + j, 0)),
        out_shape=jax.ShapeDtypeStruct((t, ATT_Q_W), bf16),
        compiler_params=_params("arbitrary", "arbitrary"),
        name="window_attn",
    )(sink, q, kv, kv, kv, kv_ctx)


def _ctx_attn(q, kv_ctx, sink, n_batch, n_ctx):
    nb = n_ctx // ATTN_TOKENS
    blk = (ATTN_TOKENS, 512)
    return pl.pallas_call(
        _ctx_attn_kernel,
        grid=(n_batch, nb),
        in_specs=[pl.BlockSpec(memory_space=pltpu.SMEM),
                  pl.BlockSpec(blk, lambda b, j: (b * nb + j, 0)),
                  pl.BlockSpec((n_ctx, 512), lambda b, j: (b, 0))],
        out_specs=pl.BlockSpec(blk, lambda b, j: (b * nb + j, 0)),
        out_shape=jax.ShapeDtypeStruct(q.shape, bf16),
        compiler_params=_params("arbitrary", "arbitrary"),
        name="ctx_attn",
    )(sink, q, kv_ctx)


def _mixer_kernel(attn_ref, vn_ref, mf_ref, mfp_ref, mfn_ref, x_ref, g1_ref, sh2_ref, sc2_ref, n2g_ref,
                  ws_ref, sb_ref, pw_ref, ps_ref, cw_ref, wo_ref, rw_ref, rb_ref,
                  xmid_ref, h2_ref, *table_refs, n_seq, bt):
    s4_refs, s8_refs, gt_refs = (table_refs[j * TOP_K:(j + 1) * TOP_K] for j in range(3))
    cnt_ref = table_refs[3 * TOP_K]
    i = pl.program_id(0)
    tps = n_seq // bt
    si = i % tps
    first = si == 0
    last = si == tps - 1
    n_ext = bt + 2 * HALO

    mf = mf_ref[...]
    u = mf[:, 0:256]

    def extended(lo, hi):
        prev = jnp.where(first, 0.0, mfp_ref[:, lo:hi])
        nxt = jnp.where(last, 0.0, mfn_ref[:, lo:hi])
        return jnp.concatenate([prev, mf[:, lo:hi], nxt], axis=0)

    def shifted(a, s):
        return pltpu.roll(a, s % n_ext, 0)

    xe = extended(256, 512)
    a1 = shifted(xe, 1) + xe
    a2 = shifted(a1, 1) + shifted(a1, -1)
    a3 = shifted(a2, 2) + shifted(a2, -2)
    a4 = shifted(a3, 4) + shifted(a3, -4)
    lane = lax.broadcasted_iota(i32, (bt, POOL_CH), 1)
    grp = lane >> 6
    sl = slice(HALO, HALO + bt)
    wsum = jnp.where(grp == 0, a1[sl], jnp.where(grp == 1, a2[sl], jnp.where(grp == 2, a3[sl], a4[sl])))
    pos = lax.broadcasted_iota(i32, (bt, POOL_CH), 0) + si * bt
    halfw = jnp.left_shift(1, grp)
    cnt = jnp.minimum(pos + halfw, n_seq) - jnp.maximum(pos - halfw, 0)
    dpool = wsum / cnt.astype(f32) - mf[:, 256:512]
    yc = jnp.dot(dpool.astype(bf16), pw_ref[...], preferred_element_type=f32) * ps_ref[...]

    ye = extended(768, 1024)
    cw = cw_ref[...]
    z = shifted(ye, 1) * cw[0:1, :] + ye * cw[1:2, :] + shifted(ye, -1) * cw[2:3, :]
    yd = mf[:, 512:768] * z[sl]

    hgrp = lax.broadcasted_iota(i32, (SGU_CHUNK, SGU_W), 1) >> 6
    ybs = []
    for cidx in range(bt // SGU_CHUNK):
        rows = slice(cidx * SGU_CHUNK, (cidx + 1) * SGU_CHUNK)
        vn_c = vn_ref[rows, :]
        s = jnp.zeros((SGU_CHUNK, SGU_W), f32)
        for hh in range(SGU_HEADS):
            sh = jnp.dot(ws_ref[hh], vn_c, preferred_element_type=f32)
            s = jnp.where(hgrp == hh, sh, s)
        ybs.append(u[rows, :] * (s + sb_ref[...]))
    yb = jnp.concatenate(ybs, axis=0)

    mix = jnp.concatenate([attn_ref[...], yb.astype(bf16), yc.astype(bf16), yd.astype(bf16)], axis=1)
    mo = jnp.dot(mix, wo_ref[...], preferred_element_type=f32)
    xm = x_ref[...] + g1_ref[...] * mo
    xmid_ref[...] = xm

    y = xm * lax.rsqrt(jnp.mean(xm * xm, axis=-1, keepdims=True) + EPS) * n2g_ref[...]
    h2 = y * (1.0 + sc2_ref[...]) + sh2_ref[...]
    _store_packed_tiles(h2_ref, h2)

    def split(a):
        hi = a.astype(bf16)
        return hi, (a - hi.astype(f32)).astype(bf16)

    def dot_nt(a, b):
        return lax.dot_general(a, b, (((1,), (1,)), ((), ())), preferred_element_type=f32)

    (w_hi, w_lo), (h_hi, h_lo) = split(rw_ref[...]), split(h2)
    lt = dot_nt(w_hi, h_hi) + (dot_nt(w_hi, h_lo) + dot_nt(w_lo, h_hi)) + rb_ref[...]
    n_exp = lt.shape[0]
    eidx = lax.broadcasted_iota(i32, lt.shape, 0)
    work = lt
    idxs, vals = [], []
    for _ in range(TOP_K):
        m = jnp.max(work, axis=0, keepdims=True)
        idx = jnp.min(jnp.where(work == m, eidx, n_exp), axis=0, keepdims=True)
        idxs.append(idx)
        vals.append(m)
        work = jnp.where(eidx == idx, -jnp.inf, work)
    exps = [jnp.exp(v - vals[0]) for v in vals]
    den = exps[0] + exps[1] + exps[2] + exps[3]
    onehot = jnp.zeros(lt.shape, f32)
    for kk in range(TOP_K):
        gt_refs[kk][...] = exps[kk] / den
        onehot = onehot + (eidx == idxs[kk]).astype(f32)
    tri = (lax.broadcasted_iota(i32, (bt, bt), 0) < lax.broadcasted_iota(i32, (bt, bt), 1)).astype(bf16)
    before = jnp.dot(onehot.astype(bf16), tri, preferred_element_type=f32)
    cnt = jnp.sum(onehot, axis=1, keepdims=True)
    row = lax.broadcasted_iota(i32, (n_exp, LANES), 0)
    incl = jnp.broadcast_to(cnt, (n_exp, LANES))
    shift = 1
    while shift < n_exp:
        incl = incl + jnp.where(row >= shift, pltpu.roll(incl, shift, 0), 0.0)
        shift *= 2
    place = before + (incl[:, 0:1] - cnt)
    for kk in range(TOP_K):
        tile = jnp.sum(jnp.where(eidx == idxs[kk], place, 0.0), axis=0, keepdims=True).astype(i32)
        s4_refs[kk][...] = tile * PACKED_ROWS
        s8_refs[kk][...] = tile * SUBLANES
    cnt_ref[...] = cnt


def _mixer(attn, vn, mixf, x2, n_seq, g1, sh2, sc2, n2g, ws_bf, sgu_bias, pool_bd, pool_scale, conv_w,
           wo_bf, rw_t, rb):
    t, d = x2.shape
    bt = min(MIXER_TOKENS, n_seq)
    tps = n_seq // bt
    hb = bt // HALO
    n_halo = t // HALO
    n_exp = rw_t.shape[0]
    row = lambda i: (i, 0)
    per_batch = pl.BlockSpec((None, 1, d), lambda i: (i // tps, 0, 0))
    col = lambda i: (0, i)
    outs = pl.pallas_call(
        functools.partial(_mixer_kernel, n_seq=n_seq, bt=bt),
        grid=(t // bt,),
        in_specs=[pl.BlockSpec((bt, ATT_Q_W), row), pl.BlockSpec((bt, SGU_W), row), pl.BlockSpec((bt, 1024), row),
                  pl.BlockSpec((HALO, 1024), lambda i: (jnp.maximum(i * hb - 1, 0), 0)),
                  pl.BlockSpec((HALO, 1024), lambda i: (jnp.minimum((i + 1) * hb, n_halo - 1), 0)),
                  pl.BlockSpec((bt, d), row), per_batch, per_batch, per_batch, _full(n2g),
                  _full(ws_bf), _full(sgu_bias), _full(pool_bd), _full(pool_scale), _full(conv_w),
                  _full(wo_bf), _full(rw_t), _full(rb)],
        out_specs=[pl.BlockSpec((bt, d), row), pl.BlockSpec((bt * PACKED_ROWS, LANES), row)]
        + [pl.BlockSpec((1, bt), col)] * (3 * TOP_K) + [pl.BlockSpec((None, n_exp, 1), lambda i: (i, 0, 0))],
        out_shape=[jax.ShapeDtypeStruct((t, d), f32), jax.ShapeDtypeStruct((t * PACKED_ROWS, LANES), u32)]
        + [jax.ShapeDtypeStruct((1, t), i32)] * (2 * TOP_K) + [jax.ShapeDtypeStruct((1, t), f32)] * TOP_K
        + [jax.ShapeDtypeStruct((t // bt, n_exp, 1), f32)],
        compiler_params=_params("arbitrary"),
        name="mixer_router",
    )(attn, vn, mixf, mixf, mixf, x2, g1, sh2, sc2, n2g, ws_bf, sgu_bias, pool_bd, pool_scale, conv_w,
      wo_bf, rw_t, rb)
    tables = [o.reshape(t) for o in outs[2:2 + 3 * TOP_K]]
    return outs[0], outs[1], tables[:TOP_K], tables[TOP_K:2 * TOP_K], tables[2 * TOP_K:], outs[-1]


def _per_assignment_specs(bt):
    return [pl.BlockSpec((bt,), lambda i: (i,), memory_space=pltpu.SMEM)] * TOP_K


def _tile_rows(t, n=1, rows=SUBLANES):
    return pl.ds(pl.multiple_of(t * rows, rows), n * rows)


def _tile_copy(src, s, dst, d, sem, rows=SUBLANES):
    return pltpu.make_async_copy(src.at[_tile_rows(s, 1, rows)], dst.at[_tile_rows(d, 1, rows)], sem)


def _for_each_run_piece(runs_ref, w, n_exp, max_len, fn):
    bits = [1 << b for b in range(max_len.bit_length() - 1, -1, -1)]

    def per_expert(e, carry):
        base = (w * n_exp + e) * 3
        stage0, slot0, length = runs_ref[base], runs_ref[base + 1], runs_ref[base + 2]
        done = 0
        for bit in bits:
            take = length & bit

            @pl.when(take != 0)
            def _():
                fn(stage0 + done, slot0 + done, bit)

            done = done + take
        return carry

    lax.fori_loop(0, n_exp, per_expert, 0)


def _wait_window(stage, buf, sem):
    pltpu.make_async_copy(stage.at[buf], stage.at[buf], sem.at[buf]).wait()


def _per_buffer(slot, fn):
    for b in range(2):
        @pl.when(slot == b)
        def _():
            fn(b)


def _split_slabs(w_ref, g_ref, l_ref, scr):
    f = g_ref.shape[1]
    for s in range(w_ref.shape[0] // LANES):
        rows = slice(s * LANES, (s + 1) * LANES)
        scr[...] = w_ref[rows, :].T
        g_ref[rows, :] = scr[pl.ds(0, f, stride=2), :].T.astype(bf16)
        l_ref[rows, :] = scr[pl.ds(1, f, stride=2), :].T.astype(bf16)


def _split_w1(w1, layer):
    _, n_exp, d, f2 = w1.shape
    f = f2 // 2
    rows = 2 * LANES
    blk = pl.BlockSpec((None, rows, f), lambda e, j: (e, j, 0))
    return pl.pallas_call(
        _split_slabs,
        grid=(n_exp, d // rows),
        in_specs=[pl.BlockSpec((None, None, rows, f2), lambda e, j: (layer, e, j, 0))],
        out_specs=[blk, blk],
        out_shape=[jax.ShapeDtypeStruct((n_exp, d, f), bf16)] * 2,
        scratch_shapes=[pltpu.VMEM((f2, LANES), f32)],
        compiler_params=_params("arbitrary", "arbitrary"),
        name="split_w1",
    )(w1)


class _SplitSide:
    def __init__(self, w1, layer, n_steps):
        _, n_exp, d, f2 = w1.shape
        fits = [r for r in (LANES, 2 * LANES, 4 * LANES, 8 * LANES) if d % r == 0 and n_exp * (d // r) <= n_steps]
        self.ok = bool(fits)
        if not self.ok:
            return
        rows = fits[0]
        per = d // rows
        self.w1 = w1
        self.n_side = n_exp * per
        last = self.n_side - 1
        self.in_spec = pl.BlockSpec((None, None, rows, f2),
                                    lambda i: (layer, jnp.minimum(i, last) // per, jnp.minimum(i, last) % per, 0))
        self.out_spec = pl.BlockSpec((None, rows, f2 // 2),
                                     lambda i: (jnp.minimum(i, last) // per, jnp.minimum(i, last) % per, 0))
        self.out_shape = jax.ShapeDtypeStruct((n_exp, d, f2 // 2), bf16)
        self.scratch = pltpu.VMEM((f2, LANES), f32)

    def run(self, w_ref, g_ref, l_ref, scr):
        @pl.when(pl.program_id(0) < self.n_side)
        def _():
            _split_slabs(w_ref, g_ref, l_ref, scr)


def _dispatch_kernel(*refs, bt, w0, n_exp, fill_pad, side):
    row_refs, (runs_ref, pad_ref, h_ref), rest = refs[:TOP_K], refs[TOP_K:TOP_K + 3], list(refs[TOP_K + 3:])
    if not fill_pad:
        rest.pop(0)
    w_ref = rest.pop(0) if side else None
    xs_ref = rest.pop(0)
    g_ref, l_ref = (rest.pop(0), rest.pop(0)) if side else (None, None)
    zero_ref = rest.pop(0) if fill_pad else None
    stage, sem = rest.pop(0), rest.pop(0)
    i = pl.program_id(0)
    slot = i % 2

    if fill_pad:
        @pl.when(i == 0)
        def _():
            zero_ref[...] = jnp.zeros_like(zero_ref)

            def per_expert(e, carry):
                first, count = pad_ref[e], pad_ref[n_exp + e]

                def issue(r, c):
                    _tile_copy(zero_ref, 0, xs_ref, first + r, sem.at[0], PACKED_ROWS).start()
                    return c

                def drain(r, c):
                    _tile_copy(zero_ref, 0, xs_ref, 0, sem.at[0], PACKED_ROWS).wait()
                    return c

                lax.fori_loop(0, count, issue, 0)
                lax.fori_loop(0, count, drain, 0)
                return carry

            lax.fori_loop(0, n_exp, per_expert, 0)

    def run_copy(buf):
        def make(stage_tile, slot_tile, n):
            return pltpu.make_async_copy(stage.at[buf, _tile_rows(stage_tile, n, PACKED_ROWS)],
                                         xs_ref.at[_tile_rows(slot_tile, n, PACKED_ROWS)], sem.at[buf])
        return make

    def place_all(b):
        def place(t, carry):
            tile = h_ref[_tile_rows(t, 1, PACKED_ROWS), :]
            for kk in range(TOP_K):
                row = pl.multiple_of(row_refs[kk][t], PACKED_ROWS)
                stage[b, pl.ds(row, PACKED_ROWS), :] = tile
            return carry

        lax.fori_loop(0, bt, place, 0, unroll=ROW_UNROLL)

    _per_buffer(slot, place_all)
    _for_each_run_piece(runs_ref, w0 + i, n_exp, bt, lambda a, b, n: run_copy(slot)(a, b, n).start())
    if side:
        side.run(w_ref, g_ref, l_ref, rest.pop(0))

    @pl.when(i > 0)
    def _():
        _wait_window(stage, 1 - slot, sem)

    @pl.when(i == pl.num_programs(0) - 1)
    def _():
        _wait_window(stage, slot, sem)


def _dispatch(rows, runs, pad, h2, xs, n_slots, bt, w0, n_exp, w1_side=None):
    t = h2.shape[0] // PACKED_ROWS
    n_steps = t // bt
    fill_pad = xs is None
    side = _SplitSide(*w1_side, n_steps) if w1_side else None
    side = side if side is not None and side.ok else None
    smem = pl.BlockSpec(memory_space=pltpu.SMEM)
    in_specs = _per_assignment_specs(bt) + [smem, smem, pl.BlockSpec((bt * PACKED_ROWS, LANES), lambda i: (i, 0))]
    args = list(rows) + [runs, pad, h2]
    out_specs = [pl.BlockSpec(memory_space=pl.ANY)]
    out_shape = [jax.ShapeDtypeStruct((n_slots * PACKED_ROWS, LANES), u32)]
    scratch = [pltpu.VMEM((2, TOP_K * bt * PACKED_ROWS, LANES), u32), pltpu.SemaphoreType.DMA((2,))]
    if fill_pad:
        scratch = [pltpu.VMEM((PACKED_ROWS, LANES), u32)] + scratch
    else:
        in_specs.append(pl.BlockSpec(memory_space=pl.ANY))
        args.append(xs)
    if side:
        in_specs.append(side.in_spec)
        args.append(side.w1)
        out_specs += [side.out_spec] * 2
        out_shape += [side.out_shape] * 2
        scratch.append(side.scratch)
    outs = pl.pallas_call(
        functools.partial(_dispatch_kernel, bt=bt, w0=w0, n_exp=n_exp, fill_pad=fill_pad, side=side),
        grid=(n_steps,),
        in_specs=in_specs,
        out_specs=out_specs,
        out_shape=out_shape,
        scratch_shapes=scratch,
        input_output_aliases={} if fill_pad else {TOP_K + 3: 0},
        compiler_params=_params("arbitrary"),
        name="dispatch",
    )(*args)
    return outs[0], (tuple(outs[1:]) if side else None)


def _expert_kernel(be_ref, nu_ref, x_ref, w1g_ref, w1l_ref, b1g_ref, b1l_ref, w2f_ref, b2_ref, y_ref, w2_ref):
    i = pl.program_id(0)
    active = i < nu_ref[0]
    new_expert = jnp.logical_or(i == 0, be_ref[i] != be_ref[jnp.maximum(i - 1, 0)])

    @pl.when(jnp.logical_and(active, new_expert))
    def _():
        w2_ref[...] = w2f_ref[...].astype(bf16)

    @pl.when(active)
    def _():
        x = _load_packed_tiles(x_ref, x_ref.shape[0] // PACKED_ROWS)
        zg = jnp.dot(x, w1g_ref[...], preferred_element_type=f32) + b1g_ref[...]
        zl = jnp.dot(x, w1l_ref[...], preferred_element_type=f32) + b1l_ref[...]
        g = jnp.minimum(zg, SWIGLU_LIMIT)
        lin = jnp.clip(zl, -SWIGLU_LIMIT, SWIGLU_LIMIT)
        a = g * jax.nn.sigmoid(SWIGLU_ALPHA * g) * (lin + 1.0)
        y = jnp.dot(a.astype(bf16), w2_ref[...], preferred_element_type=f32) + b2_ref[...]
        _store_token_tiles(y_ref, y)

    @pl.when(i >= nu_ref[0])
    def _():
        y_ref[...] = jnp.zeros_like(y_ref)


def _experts(block_e, n_used, xs, w1g, w1l, b1g, b1l, w2_all, layer, b2):
    n_exp, d, f = w1g.shape
    bm = EXPERT_ROWS
    n_slots = xs.shape[0] // PACKED_ROWS
    blk = (bm * SUBLANES, LANES)
    xrow = lambda i, be, nu: (jnp.minimum(i, nu[0] - 1), 0)
    wsel = lambda i, be, nu: (be[i], 0, 0)
    grid_spec = pltpu.PrefetchScalarGridSpec(
        num_scalar_prefetch=2,
        grid=(n_slots // bm,),
        in_specs=[pl.BlockSpec((bm * PACKED_ROWS, LANES), xrow),
                  pl.BlockSpec((None, d, f), wsel), pl.BlockSpec((None, d, f), wsel),
                  pl.BlockSpec((None, 1, f), wsel), pl.BlockSpec((None, 1, f), wsel),
                  pl.BlockSpec((None, None, f, d), lambda i, be, nu: (layer, be[i], 0, 0)),
                  pl.BlockSpec((None, 1, d), wsel)],
        out_specs=pl.BlockSpec(blk, lambda i, be, nu: (i, 0)),
        scratch_shapes=[pltpu.VMEM((f, d), bf16)],
    )
    return pl.pallas_call(
        _expert_kernel,
        grid_spec=grid_spec,
        out_shape=jax.ShapeDtypeStruct((n_slots * SUBLANES, LANES), f32),
        compiler_params=_params("arbitrary"),
        name="experts",
    )(block_e, n_used, xs, w1g, w1l, b1g, b1l, w2_all, b2)


def _combine_kernel(*refs, bt, w0, n_exp, final, side):
    row_refs, gate_refs = refs[:TOP_K], refs[TOP_K:2 * TOP_K]
    (runs_ref, ys_ref, x_ref, g2_ref, fg_ref), rest = refs[2 * TOP_K:2 * TOP_K + 5], list(refs[2 * TOP_K + 5:])
    w_ref = rest.pop(0) if side else None
    o_ref = rest.pop(0)
    g_ref, l_ref = (rest.pop(0), rest.pop(0)) if side else (None, None)
    stage, acc_ref, sem = rest.pop(0), rest.pop(0), rest.pop(0)
    i = pl.program_id(0)
    slot = i % 2

    def run_copy(buf):
        def make(stage_tile, slot_tile, n):
            return pltpu.make_async_copy(ys_ref.at[_tile_rows(slot_tile, n)], stage.at[buf, _tile_rows(stage_tile, n)],
                                         sem.at[buf])
        return make

    def fetch(w, buf):
        _for_each_run_piece(runs_ref, w, n_exp, bt, lambda a, b, n: run_copy(buf)(a, b, n).start())

    @pl.when(i == 0)
    def _():
        fetch(w0, 0)

    @pl.when(i + 1 < pl.num_programs(0))
    def _():
        fetch(w0 + i + 1, 1 - slot)

    if side:
        side.run(w_ref, g_ref, l_ref, rest.pop(0))

    _wait_window(stage, slot, sem)

    def sum_all(b):
        def token(t, carry):
            acc = None
            for kk in range(TOP_K):
                row = pl.multiple_of(row_refs[kk][t], SUBLANES)
                term = gate_refs[kk][t] * stage[b, pl.ds(row, SUBLANES), :]
                acc = term if acc is None else acc + term
            acc_ref[_tile_rows(t), :] = acc
            return carry

        lax.fori_loop(0, bt, token, 0, unroll=ROW_UNROLL)

    _per_buffer(slot, sum_all)
    xo = x_ref[...] + g2_ref[...] * _load_token_tiles(acc_ref, bt)
    if final:
        xo = xo * lax.rsqrt(jnp.mean(xo * xo, axis=-1, keepdims=True) + EPS) * fg_ref[...]
    o_ref[...] = xo


def _combine(rows, gates, runs, ys, x_mid, n_seq, g2, final_g, final, bt, w0, n_exp, w1_side=None):
    t, d = x_mid.shape
    tps = n_seq // bt
    row = lambda i: (i, 0)
    n_steps = t // bt
    side = _SplitSide(*w1_side, n_steps) if w1_side else None
    side = side if side is not None and side.ok else None
    in_specs = _per_assignment_specs(bt) + _per_assignment_specs(bt) + [
        pl.BlockSpec(memory_space=pltpu.SMEM),
        pl.BlockSpec(memory_space=pl.ANY),
        pl.BlockSpec((bt, d), row),
        pl.BlockSpec((None, 1, d), lambda i: (i // tps, 0, 0)),
        _full(final_g)]
    args = list(rows) + list(gates) + [runs, ys, x_mid, g2, final_g]
    out_specs = [pl.BlockSpec((bt, d), row)]
    out_shape = [jax.ShapeDtypeStruct((t, d), f32)]
    scratch = [pltpu.VMEM((2, TOP_K * bt * SUBLANES, LANES), f32), pltpu.VMEM((bt * SUBLANES, LANES), f32),
               pltpu.SemaphoreType.DMA((2,))]
    if side:
        in_specs.append(side.in_spec)
        args.append(side.w1)
        out_specs += [side.out_spec] * 2
        out_shape += [side.out_shape] * 2
        scratch.append(side.scratch)
    outs = pl.pallas_call(
        functools.partial(_combine_kernel, bt=bt, w0=w0, n_exp=n_exp, final=final, side=side),
        grid=(n_steps,),
        in_specs=in_specs,
        out_specs=out_specs,
        out_shape=out_shape,
        scratch_shapes=scratch,
        compiler_params=_params("arbitrary"),
        name="combine",
    )(*args)
    return outs[0], (tuple(outs[1:]) if side else None)


def _rope_tables(n_tokens):
    rows = n_tokens // GRID_W
    row = jnp.repeat(jnp.arange(rows), GRID_W).astype(f32)
    col = jnp.tile(jnp.arange(GRID_W), rows).astype(f32)
    n_freq = HEAD_DIM // 4
    inv = ROPE_BASE ** (-jnp.arange(n_freq, dtype=f32) / n_freq)
    ang = jnp.concatenate([row[:, None] * inv, col[:, None] * inv], axis=-1)
    cos, sin = jnp.cos(ang), jnp.sin(ang)
    return jnp.tile(cos, (1, 4)), jnp.concatenate([-sin, sin, -sin, sin], axis=-1)


def kernel(x, c, ctx, c_ctx, norm1_g, norm2_g, ada_w, ada_b, w_in, attn_sink, sgu_ws, sgu_b, sgu_ln_g, sgu_ln_b,
           pool_w, pool_scale, conv_w, w_out, router_w, router_b, exp_w1, exp_b1, exp_w2, exp_b2, final_g):
    n_batch, n_seq, d = x.shape
    n_ctx = ctx.shape[1]
    depth = ada_w.shape[0]
    n_exp = router_w.shape[2]
    assert d == SUBLANES * LANES, "token rows are moved as single (8, 128) tiles"
    t_lat, t_ctx = n_batch * n_seq, n_batch * n_ctx
    bm = EXPERT_ROWS

    cvec = jnp.concatenate([c, c_ctx[None, :], jnp.zeros((SUBLANES - n_batch - 1, d), f32)], axis=0)
    mods = _ada(cvec, ada_w, ada_b)

    cos_l, sin_l = _rope_tables(n_seq)
    cos_c, sin_c = jnp.ones((n_ctx, LANES), f32), jnp.zeros((n_ctx, LANES), f32)

    xl = x.reshape(t_lat, d)
    xc = ctx.reshape(t_ctx, d)
    row2 = lambda a: a.reshape(1, -1)
    w1_split = {}

    for l in range(depth):
        last = l == depth - 1
        ml = mods[l, :n_batch].reshape(n_batch, 6, 1, d)
        mc = jnp.broadcast_to(mods[l, n_batch].reshape(1, 6, 1, d), (n_batch, 6, 1, d))
        sh1l, sc1l, g1l, sh2l, sc2l, g2l = (ml[:, i] for i in range(6))
        sh1c, sc1c, g1c, sh2c, sc2c, g2c = (mc[:, i] for i in range(6))

        w_in_bf = w_in[l].astype(bf16)
        wo_bf = w_out[l].astype(bf16)
        ws_bf = sgu_ws[l].astype(bf16)
        sgu_bias = jnp.repeat(sgu_b[l].T, SGU_W // SGU_HEADS, axis=1)
        pool_bd = jax.scipy.linalg.block_diag(*[pool_w[l, g] for g in range(pool_w.shape[1])]).astype(bf16)
        n1g, n2g = row2(norm1_g[l]), row2(norm2_g[l])
        lng, lnb, psc = row2(sgu_ln_g[l]), row2(sgu_ln_b[l]), row2(pool_scale[l])
        rw_t = router_w[l].T
        rb = router_b[l].reshape(n_exp, 1)
        sink = attn_sink[l]
        mix_w = (ws_bf, sgu_bias, pool_bd, psc, conv_w[l], wo_bf, rw_t, rb)

        qc, kvc, vnc, mfc = _inproj(xc, n_ctx, sh1c, sc1c, n1g, w_in_bf, cos_c, sin_c, lng, lnb)
        ql, kvl, vnl, mfl = _inproj(xl, n_seq, sh1l, sc1l, n1g, w_in_bf, cos_l, sin_l, lng, lnb)
        attn_l = _window_attn(ql, kvl, kvc, sink, n_batch, n_seq, n_ctx)
        bt_l, bt_c = min(MIXER_TOKENS, n_seq), min(MIXER_TOKENS, n_ctx)
        xmid_l, h2_l, s4_l, s8_l, gt_l, cnt = _mixer(attn_l, vnl, mfl, xl, n_seq, g1l, sh2l, sc2l, n2g, *mix_w)
        n_win_l = cnt.shape[0]
        t_all = t_lat
        if not last:
            attn_c = _ctx_attn(qc, kvc, sink, n_batch, n_ctx)
            xmid_c, h2_c, s4_c, s8_c, gt_c, cnt_c = _mixer(attn_c, vnc, mfc, xc, n_ctx, g1c, sh2c, sc2c, n2g, *mix_w)
            cnt = jnp.concatenate([cnt, cnt_c], axis=0)
            t_all = t_lat + t_ctx

        win_cnt = cnt[:, :, 0].astype(i32)
        counts = jnp.sum(win_cnt, axis=0)
        padded = (counts + bm - 1) // bm * bm
        pends = jnp.cumsum(padded)
        pstarts = pends - padded
        n_blocks = -(-(t_all * TOP_K) // bm) + n_exp
        starts = jnp.arange(n_blocks, dtype=i32) * bm
        block_e = jnp.minimum(jnp.sum((pends[None, :] <= starts[:, None]).astype(i32), axis=1), n_exp - 1)
        n_used = (pends[-1:] // bm).astype(i32)
        pad = jnp.concatenate([pstarts + counts, padded - counts]).astype(i32)
        run_stage = jnp.cumsum(win_cnt, axis=1) - win_cnt
        run_slot = pstarts[None, :] + jnp.cumsum(win_cnt, axis=0) - win_cnt
        runs = jnp.stack([run_stage, run_slot, win_cnt], axis=-1).reshape(-1)

        xs, split = _dispatch(s4_l, runs, pad, h2_l, None, n_blocks * bm, bt_l, 0, n_exp,
                              None if l in w1_split else (exp_w1, l))
        if l not in w1_split:
            w1_split[l] = split or _split_w1(exp_w1, l)
        if not last:
            xs, _ = _dispatch(s4_c, runs, pad, h2_c, xs, n_blocks * bm, bt_c, n_win_l, n_exp)

        w1g, w1l = w1_split[l]
        b1 = exp_b1[l]
        b1g, b1l = b1[:, None, 0::2], b1[:, None, 1::2]
        ys = _experts(block_e, n_used, xs, w1g, w1l, b1g, b1l, exp_w2, l, exp_b2[l][:, None, :])

        fg = row2(final_g)
        xl, split = _combine(s8_l, gt_l, runs, ys, xmid_l, n_seq, g2l, fg, last, bt_l, 0, n_exp,
                             None if last else (exp_w1, l + 1))
        if split:
            w1_split[l + 1] = split
        if not last:
            xc, _ = _combine(s8_c, gt_c, runs, ys, xmid_c, n_ctx, g2c, fg, False, bt_c, n_win_l, n_exp)

    return xl.reshape(n_batch, n_seq, d)
```

```python
import functools

import jax
import jax.numpy as jnp
from jax import lax
from jax.experimental import pallas as pl
from jax.experimental.pallas import tpu as pltpu

f32 = jnp.float32
bf16 = jnp.bfloat16
i32 = jnp.int32
u32 = jnp.uint32

GRID_W = 64
EPS = 1e-6
N_KV_HEADS = 2
GQA_GROUP = 4
HEAD_DIM = 64
WINDOW = 128
ROPE_BASE = 10000.0
ATT_Q_W = 512
SGU_HEADS = 4
SGU_W = 256
SGU_CHUNK = 128
POOL_CH = 256
TOP_K = 4
SWIGLU_LIMIT = 7.0
SWIGLU_ALPHA = 1.702
SQRT_HALF = 0.7071067811865476

LANES = 128
SUBLANES = 8
PACKED_ROWS = SUBLANES // 2
VMEM_LIMIT_BYTES = 56 * 1024 * 1024

INPROJ_TOKENS = 512
ATTN_TOKENS = 128
ATTN_BLOCKS_PER_STEP = 4
MIXER_TOKENS = 512
EXPERT_ROWS = 512
HALO = 8
ROW_UNROLL = 16


def _params(*sem):
    return pltpu.CompilerParams(dimension_semantics=sem, vmem_limit_bytes=VMEM_LIMIT_BYTES)


def _full(a):
    nd = a.ndim
    return pl.BlockSpec(a.shape, lambda *_: (0,) * nd)


def _store_token_tiles(ref, val):
    n = val.shape[0]
    for s in range(SUBLANES):
        ref[pl.ds(s, n, stride=SUBLANES), :] = val[:, s * LANES:(s + 1) * LANES]


def _load_token_tiles(ref, n):
    return jnp.concatenate([ref[pl.ds(s, n, stride=SUBLANES), :] for s in range(SUBLANES)], axis=1)


HIGH_HALF = 0xFFFF0000


def _store_packed_tiles(ref, val):
    n = val.shape[0]
    half = PACKED_ROWS * LANES
    for s in range(PACKED_ROWS):
        lo = val[:, s * LANES:(s + 1) * LANES].astype(bf16).astype(f32)
        hi = val[:, half + s * LANES:half + (s + 1) * LANES].astype(bf16).astype(f32)
        lo_bits = lax.shift_right_logical(lax.bitcast_convert_type(lo, u32), jnp.uint32(16))
        hi_bits = lax.bitcast_convert_type(hi, u32) & jnp.uint32(HIGH_HALF)
        ref[pl.ds(s, n, stride=PACKED_ROWS), :] = lo_bits | hi_bits


def _load_packed_tiles(ref, n):
    los, his = [], []
    for s in range(PACKED_ROWS):
        w = ref[pl.ds(s, n, stride=PACKED_ROWS), :]
        los.append(lax.bitcast_convert_type(lax.shift_left(w, jnp.uint32(16)), f32))
        his.append(lax.bitcast_convert_type(w & jnp.uint32(HIGH_HALF), f32))
    return jnp.concatenate(los + his, axis=1).astype(bf16)


def _gelu(x):
    return 0.5 * x * (1.0 + lax.erf(x * SQRT_HALF))


def _ada_kernel(c_ref, w_ref, b_ref, o_ref):
    c = c_ref[...]
    s = c * jax.nn.sigmoid(c)
    o_ref[...] = jnp.dot(s, w_ref[...], precision=lax.Precision.HIGHEST, preferred_element_type=f32) + b_ref[...]


def _ada(cvec, ada_w, ada_b):
    depth, d, n = ada_w.shape
    tn = 1536
    return pl.pallas_call(
        _ada_kernel,
        grid=(depth, n // tn),
        in_specs=[pl.BlockSpec(cvec.shape, lambda l, j: (0, 0)),
                  pl.BlockSpec((None, d, tn), lambda l, j: (l, 0, j)),
                  pl.BlockSpec((None, 1, tn), lambda l, j: (l, 0, j))],
        out_specs=pl.BlockSpec((None, cvec.shape[0], tn), lambda l, j: (l, 0, j)),
        out_shape=jax.ShapeDtypeStruct((depth, cvec.shape[0], n), f32),
        compiler_params=_params("arbitrary", "arbitrary"),
        name="ada_mod",
    )(cvec, ada_w, ada_b.reshape(depth, 1, n))


def _inproj_kernel(x_ref, sh_ref, sc_ref, g_ref, w_ref, cos_ref, sin_ref, lng_ref, lnb_ref,
                   q_ref, kv_ref, vn_ref, mixf_ref):
    x = x_ref[...]
    y = x * lax.rsqrt(jnp.mean(x * x, axis=-1, keepdims=True) + EPS) * g_ref[...]
    h = y * (1.0 + sc_ref[...]) + sh_ref[...]
    p = jnp.dot(h.astype(bf16), w_ref[...], preferred_element_type=f32)

    cos = cos_ref[...]
    sin = sin_ref[...]
    lane = lax.broadcasted_iota(i32, cos.shape, 1)
    first_half = (lane & (HEAD_DIM - 1)) < HEAD_DIM // 2

    def rope(t):
        partner = jnp.where(first_half, pltpu.roll(t, LANES - HEAD_DIM // 2, 1), pltpu.roll(t, HEAD_DIM // 2, 1))
        return t * cos + partner * sin

    scale = HEAD_DIM ** -0.5
    for m in range(ATT_Q_W // LANES):
        q_ref[:, m * LANES:(m + 1) * LANES] = (rope(p[:, m * LANES:(m + 1) * LANES]) * scale).astype(bf16)
    k = rope(p[:, 512:640])
    v = p[:, 640:768]
    kv_ref[:, 0:128] = k.astype(bf16)
    kv_ref[:, 128:256] = pltpu.roll(k, HEAD_DIM, 1).astype(bf16)
    kv_ref[:, 256:384] = v.astype(bf16)
    kv_ref[:, 384:512] = pltpu.roll(v, HEAD_DIM, 1).astype(bf16)

    u = _gelu(p[:, 768:1024])
    gv = _gelu(p[:, 1024:1280])
    mu = jnp.mean(gv, axis=-1, keepdims=True)
    var = jnp.mean(jnp.square(gv - mu), axis=-1, keepdims=True)
    vn_ref[...] = ((gv - mu) * lax.rsqrt(var + EPS) * lng_ref[...] + lnb_ref[...]).astype(bf16)

    mixf_ref[:, 0:256] = u
    mixf_ref[:, 256:512] = p[:, 1280:1536]
    mixf_ref[:, 512:768] = p[:, 1536:1792]
    mixf_ref[:, 768:1024] = p[:, 1792:2048] * p[:, 2048:2304]


def _inproj(x2, n_seq, shift, scale, g, w_bf, cos_t, sin_t, ln_g, ln_b):
    t, d = x2.shape
    bt = min(INPROJ_TOKENS, n_seq)
    tps = n_seq // bt
    ncol = w_bf.shape[1]
    row = lambda i: (i, 0)
    return pl.pallas_call(
        _inproj_kernel,
        grid=(t // bt,),
        in_specs=[pl.BlockSpec((bt, d), row),
                  pl.BlockSpec((None, 1, d), lambda i: (i // tps, 0, 0)),
                  pl.BlockSpec((None, 1, d), lambda i: (i // tps, 0, 0)),
                  _full(g),
                  pl.BlockSpec((d, ncol), lambda i: (0, 0)),
                  pl.BlockSpec((bt, LANES), lambda i: (i % tps, 0)),
                  pl.BlockSpec((bt, LANES), lambda i: (i % tps, 0)),
                  _full(ln_g), _full(ln_b)],
        out_specs=[pl.BlockSpec((bt, ATT_Q_W), row), pl.BlockSpec((bt, 512), row),
                   pl.BlockSpec((bt, SGU_W), row), pl.BlockSpec((bt, 1024), row)],
        out_shape=[jax.ShapeDtypeStruct((t, ATT_Q_W), bf16), jax.ShapeDtypeStruct((t, 512), bf16),
                   jax.ShapeDtypeStruct((t, SGU_W), bf16), jax.ShapeDtypeStruct((t, 1024), f32)],
        compiler_params=_params("arbitrary"),
        name="inproj",
    )(x2, shift, scale, g, w_bf, cos_t, sin_t, ln_g, ln_b)


def _attn_block(sink_ref, q, kv, bias, o_ref, row0):
    k_nat, k_swp, v_nat, v_swp = (kv[:, i * LANES:(i + 1) * LANES] for i in range(4))
    nq = q.shape[0]
    low = lax.broadcasted_iota(i32, (nq, LANES), 1) < HEAD_DIM
    top = lax.broadcasted_iota(i32, (2 * nq, 1), 0) < nq
    zero = jnp.zeros((nq, LANES), q.dtype)
    assert (N_KV_HEADS, GQA_GROUP, 2 * HEAD_DIM) == (2, 4, LANES), "two heads per 128-lane chunk, two chunks per kv head"
    for kvh in range(N_KV_HEADS):
        chunks = [q[:, (2 * kvh + i) * LANES:(2 * kvh + i + 1) * LANES] for i in range(2)]
        outs = []
        for half in range(2):
            keep = low if half == 0 else jnp.logical_not(low)
            qz = jnp.concatenate([jnp.where(keep, c, zero) for c in chunks], axis=0)
            kh = k_nat if kvh == half else k_swp
            vh = v_nat if kvh == half else v_swp
            s = lax.dot_general(qz, kh, (((1,), (1,)), ((), ())), preferred_element_type=f32)
            if bias is not None:
                s = s + bias
            h0 = GQA_GROUP * kvh + half
            sk = jnp.where(top, sink_ref[h0], sink_ref[h0 + 2])
            mx = jnp.maximum(jnp.max(s, axis=1, keepdims=True), sk)
            e = jnp.exp(s - mx)
            den = jnp.sum(e, axis=1, keepdims=True) + jnp.exp(sk - mx)
            outs.append(jnp.dot(e.astype(bf16), vh, preferred_element_type=f32) / den)
        for i in range(2):
            rows = slice(i * nq, (i + 1) * nq)
            m = 2 * kvh + i
            o_ref[row0:row0 + nq, m * LANES:(m + 1) * LANES] = jnp.where(low, outs[0][rows], outs[1][rows]).astype(bf16)


def _window_attn_kernel(sink_ref, q_ref, kvp_ref, kvm_ref, kvn_ref, kvx_ref, o_ref, *, nb, per_step):
    jj = pl.program_id(1)
    nq = ATTN_TOKENS
    nband = 3 * nq
    kv_all = jnp.concatenate([kvp_ref[...], kvm_ref[...], kvn_ref[...]], axis=0)
    kvx = kvx_ref[...]
    r = lax.broadcasted_iota(i32, (nq, nband), 0)
    c = lax.broadcasted_iota(i32, (nq, nband), 1)
    dlt = c - r
    in_window = (dlt >= 0) & (dlt <= 2 * WINDOW)
    ctx_zeros = jnp.zeros((nq, kvx.shape[0]), f32)
    for i in range(per_step):
        j = per_step * jj + i
        lo = jnp.where(j == 0, nq, 0)
        hi = jnp.where(j == nb - 1, 2 * nq, nband)
        valid = in_window & (c >= lo) & (c < hi)
        bias = jnp.concatenate([jnp.where(valid, 0.0, -jnp.inf).astype(f32), ctx_zeros], axis=1)
        kv = jnp.concatenate([kv_all[i * nq:i * nq + nband], kvx], axis=0)
        _attn_block(sink_ref, q_ref[i * nq:(i + 1) * nq, :], kv, jnp.concatenate([bias, bias], axis=0), o_ref, i * nq)


def _ctx_attn_kernel(sink_ref, q_ref, kvx_ref, o_ref):
    _attn_block(sink_ref, q_ref[...], kvx_ref[...], None, o_ref, 0)


def _window_attn(q, kv, kv_ctx, sink, n_batch, n_seq, n_ctx):
    t = q.shape[0]
    nb = n_seq // ATTN_TOKENS
    p = next(c for c in (ATTN_BLOCKS_PER_STEP, 2, 1) if nb % c == 0)
    steps = nb // p
    one, many = (ATTN_TOKENS, 512), (p * ATTN_TOKENS, 512)
    return pl.pallas_call(
        functools.partial(_window_attn_kernel, nb=nb, per_step=p),
        grid=(n_batch, steps),
        in_specs=[pl.BlockSpec(memory_space=pltpu.SMEM),
                  pl.BlockSpec(many, lambda b, j: (b * steps + j, 0)),
                  pl.BlockSpec(one, lambda b, j: (b * nb + jnp.maximum(p * j - 1, 0), 0)),
                  pl.BlockSpec(many, lambda b, j: (b * steps + j, 0)),
                  pl.BlockSpec(one, lambda b, j: (b * nb + jnp.minimum(p * j + p, nb - 1), 0)),
                  pl.BlockSpec((n_ctx, 512), lambda b, j: (b, 0))],
        out_specs=pl.BlockSpec(many, lambda b, j: (b * steps + j, 0)),
        out_shape=jax.ShapeDtypeStruct((t, ATT_Q_W), bf16),
        compiler_params=_params("arbitrary", "arbitrary"),
        name="window_attn",
    )(sink, q, kv, kv, kv, kv_ctx)


def _ctx_attn(q, kv_ctx, sink, n_batch, n_ctx):
    nb = n_ctx // ATTN_TOKENS
    blk = (ATTN_TOKENS, 512)
    return pl.pallas_call(
        _ctx_attn_kernel,
        grid=(n_batch, nb),
        in_specs=[pl.BlockSpec(memory_space=pltpu.SMEM),
                  pl.BlockSpec(blk, lambda b, j: (b * nb + j, 0)),
                  pl.BlockSpec((n_ctx, 512), lambda b, j: (b, 0))],
        out_specs=pl.BlockSpec(blk, lambda b, j: (b * nb + j, 0)),
        out_shape=jax.ShapeDtypeStruct(q.shape, bf16),
        compiler_params=_params("arbitrary", "arbitrary"),
        name="ctx_attn",
    )(sink, q, kv_ctx)


def _mixer_kernel(attn_ref, vn_ref, mf_ref, mfp_ref, mfn_ref, x_ref, g1_ref, sh2_ref, sc2_ref, n2g_ref,
                  ws_ref, sb_ref, pw_ref, ps_ref, cw_ref, wo_ref, rw_ref, rb_ref,
                  xmid_ref, h2_ref, *table_refs, n_seq, bt):
    s4_refs, s8_refs, gt_refs = (table_refs[j * TOP_K:(j + 1) * TOP_K] for j in range(3))
    cnt_ref = table_refs[3 * TOP_K]
    i = pl.program_id(0)
    tps = n_seq // bt
    si = i % tps
    first = si == 0
    last = si == tps - 1
    n_ext = bt + 2 * HALO

    mf = mf_ref[...]
    u = mf[:, 0:256]

    def extended(lo, hi):
        prev = jnp.where(first, 0.0, mfp_ref[:, lo:hi])
        nxt = jnp.where(last, 0.0, mfn_ref[:, lo:hi])
        return jnp.concatenate([prev, mf[:, lo:hi], nxt], axis=0)

    def shifted(a, s):
        return pltpu.roll(a, s % n_ext, 0)

    xe = extended(256, 512)
    a1 = shifted(xe, 1) + xe
    a2 = shifted(a1, 1) + shifted(a1, -1)
    a3 = shifted(a2, 2) + shifted(a2, -2)
    a4 = shifted(a3, 4) + shifted(a3, -4)
    lane = lax.broadcasted_iota(i32, (bt, POOL_CH), 1)
    grp = lane >> 6
    sl = slice(HALO, HALO + bt)
    wsum = jnp.where(grp == 0, a1[sl], jnp.where(grp == 1, a2[sl], jnp.where(grp == 2, a3[sl], a4[sl])))
    pos = lax.broadcasted_iota(i32, (bt, POOL_CH), 0) + si * bt
    halfw = jnp.left_shift(1, grp)
    cnt = jnp.minimum(pos + halfw, n_seq) - jnp.maximum(pos - halfw, 0)
    dpool = wsum / cnt.astype(f32) - mf[:, 256:512]
    yc = jnp.dot(dpool.astype(bf16), pw_ref[...], preferred_element_type=f32) * ps_ref[...]

    ye = extended(768, 1024)
    cw = cw_ref[...]
    z = shifted(ye, 1) * cw[0:1, :] + ye * cw[1:2, :] + shifted(ye, -1) * cw[2:3, :]
    yd = mf[:, 512:768] * z[sl]

    hgrp = lax.broadcasted_iota(i32, (SGU_CHUNK, SGU_W), 1) >> 6
    ybs = []
    for cidx in range(bt // SGU_CHUNK):
        rows = slice(cidx * SGU_CHUNK, (cidx + 1) * SGU_CHUNK)
        vn_c = vn_ref[rows, :]
        s = jnp.zeros((SGU_CHUNK, SGU_W), f32)
        for hh in range(SGU_HEADS):
            sh = jnp.dot(ws_ref[hh], vn_c, preferred_element_type=f32)
            s = jnp.where(hgrp == hh, sh, s)
        ybs.append(u[rows, :] * (s + sb_ref[...]))
    yb = jnp.concatenate(ybs, axis=0)

    mix = jnp.concatenate([attn_ref[...], yb.astype(bf16), yc.astype(bf16), yd.astype(bf16)], axis=1)
    mo = jnp.dot(mix, wo_ref[...], preferred_element_type=f32)
    xm = x_ref[...] + g1_ref[...] * mo
    xmid_ref[...] = xm

    y = xm * lax.rsqrt(jnp.mean(xm * xm, axis=-1, keepdims=True) + EPS) * n2g_ref[...]
    h2 = y * (1.0 + sc2_ref[...]) + sh2_ref[...]
    _store_packed_tiles(h2_ref, h2)

    def split(a):
        hi = a.astype(bf16)
        return hi, (a - hi.astype(f32)).astype(bf16)

    def dot_nt(a, b):
        return lax.dot_general(a, b, (((1,), (1,)), ((), ())), preferred_element_type=f32)

    (w_hi, w_lo), (h_hi, h_lo) = split(rw_ref[...]), split(h2)
    lt = dot_nt(w_hi, h_hi) + (dot_nt(w_hi, h_lo) + dot_nt(w_lo, h_hi)) + rb_ref[...]
    n_exp = lt.shape[0]
    eidx = lax.broadcasted_iota(i32, lt.shape, 0)
    work = lt
    idxs, vals = [], []
    for _ in range(TOP_K):
        m = jnp.max(work, axis=0, keepdims=True)
        idx = jnp.min(jnp.where(work == m, eidx, n_exp), axis=0, keepdims=True)
        idxs.append(idx)
        vals.append(m)
        work = jnp.where(eidx == idx, -jnp.inf, work)
    exps = [jnp.exp(v - vals[0]) for v in vals]
    den = exps[0] + exps[1] + exps[2] + exps[3]
    onehot = jnp.zeros(lt.shape, f32)
    for kk in range(TOP_K):
        gt_refs[kk][...] = exps[kk] / den
        onehot = onehot + (eidx == idxs[kk]).astype(f32)
    tri = (lax.broadcasted_iota(i32, (bt, bt), 0) < lax.broadcasted_iota(i32, (bt, bt), 1)).astype(bf16)
    before = jnp.dot(onehot.astype(bf16), tri, preferred_element_type=f32)
    cnt = jnp.sum(onehot, axis=1, keepdims=True)
    row = lax.broadcasted_iota(i32, (n_exp, LANES), 0)
    incl = jnp.broadcast_to(cnt, (n_exp, LANES))
    shift = 1
    while shift < n_exp:
        incl = incl + jnp.where(row >= shift, pltpu.roll(incl, shift, 0), 0.0)
        shift *= 2
    place = before + (incl[:, 0:1] - cnt)
    for kk in range(TOP_K):
        tile = jnp.sum(jnp.where(eidx == idxs[kk], place, 0.0), axis=0, keepdims=True).astype(i32)
        s4_refs[kk][...] = tile * PACKED_ROWS
        s8_refs[kk][...] = tile * SUBLANES
    cnt_ref[...] = cnt


def _mixer(attn, vn, mixf, x2, n_seq, g1, sh2, sc2, n2g, ws_bf, sgu_bias, pool_bd, pool_scale, conv_w,
           wo_bf, rw_t, rb):
    t, d = x2.shape
    bt = min(MIXER_TOKENS, n_seq)
    tps = n_seq // bt
    hb = bt // HALO
    n_halo = t // HALO
    n_exp = rw_t.shape[0]
    row = lambda i: (i, 0)
    per_batch = pl.BlockSpec((None, 1, d), lambda i: (i // tps, 0, 0))
    col = lambda i: (0, i)
    outs = pl.pallas_call(
        functools.partial(_mixer_kernel, n_seq=n_seq, bt=bt),
        grid=(t // bt,),
        in_specs=[pl.BlockSpec((bt, ATT_Q_W), row), pl.BlockSpec((bt, SGU_W), row), pl.BlockSpec((bt, 1024), row),
                  pl.BlockSpec((HALO, 1024), lambda i: (jnp.maximum(i * hb - 1, 0), 0)),
                  pl.BlockSpec((HALO, 1024), lambda i: (jnp.minimum((i + 1) * hb, n_halo - 1), 0)),
                  pl.BlockSpec((bt, d), row), per_batch, per_batch, per_batch, _full(n2g),
                  _full(ws_bf), _full(sgu_bias), _full(pool_bd), _full(pool_scale), _full(conv_w),
                  _full(wo_bf), _full(rw_t), _full(rb)],
        out_specs=[pl.BlockSpec((bt, d), row), pl.BlockSpec((bt * PACKED_ROWS, LANES), row)]
        + [pl.BlockSpec((1, bt), col)] * (3 * TOP_K) + [pl.BlockSpec((None, n_exp, 1), lambda i: (i, 0, 0))],
        out_shape=[jax.ShapeDtypeStruct((t, d), f32), jax.ShapeDtypeStruct((t * PACKED_ROWS, LANES), u32)]
        + [jax.ShapeDtypeStruct((1, t), i32)] * (2 * TOP_K) + [jax.ShapeDtypeStruct((1, t), f32)] * TOP_K
        + [jax.ShapeDtypeStruct((t // bt, n_exp, 1), f32)],
        compiler_params=_params("arbitrary"),
        name="mixer_router",
    )(attn, vn, mixf, mixf, mixf, x2, g1, sh2, sc2, n2g, ws_bf, sgu_bias, pool_bd, pool_scale, conv_w,
      wo_bf, rw_t, rb)
    tables = [o.reshape(t) for o in outs[2:2 + 3 * TOP_K]]
    return outs[0], outs[1], tables[:TOP_K], tables[TOP_K:2 * TOP_K], tables[2 * TOP_K:], outs[-1]


def _per_assignment_specs(bt):
    return [pl.BlockSpec((bt,), lambda i: (i,), memory_space=pltpu.SMEM)] * TOP_K


def _tile_rows(t, n=1, rows=SUBLANES):
    return pl.ds(pl.multiple_of(t * rows, rows), n * rows)


def _tile_copy(src, s, dst, d, sem, rows=SUBLANES):
    return pltpu.make_async_copy(src.at[_tile_rows(s, 1, rows)], dst.at[_tile_rows(d, 1, rows)], sem)


def _for_each_run_piece(runs_ref, w, n_exp, max_len, fn):
    bits = [1 << b for b in range(max_len.bit_length() - 1, -1, -1)]

    def per_expert(e, carry):
        base = (w * n_exp + e) * 3
        stage0, slot0, length = runs_ref[base], runs_ref[base + 1], runs_ref[base + 2]
        done = 0
        for bit in bits:
            take = length & bit

            @pl.when(take != 0)
            def _():
                fn(stage0 + done, slot0 + done, bit)

            done = done + take
        return carry

    lax.fori_loop(0, n_exp, per_expert, 0)


def _wait_window(stage, buf, sem):
    pltpu.make_async_copy(stage.at[buf], stage.at[buf], sem.at[buf]).wait()


def _per_buffer(slot, fn):
    for b in range(2):
        @pl.when(slot == b)
        def _():
            fn(b)


def _split_slabs(w_ref, g_ref, l_ref, scr):
    f = g_ref.shape[1]
    for s in range(w_ref.shape[0] // LANES):
        rows = slice(s * LANES, (s + 1) * LANES)
        scr[...] = w_ref[rows, :].T
        g_ref[rows, :] = scr[pl.ds(0, f, stride=2), :].T.astype(bf16)
        l_ref[rows, :] = scr[pl.ds(1, f, stride=2), :].T.astype(bf16)


def _split_w1(w1, layer):
    _, n_exp, d, f2 = w1.shape
    f = f2 // 2
    rows = 2 * LANES
    blk = pl.BlockSpec((None, rows, f), lambda e, j: (e, j, 0))
    return pl.pallas_call(
        _split_slabs,
        grid=(n_exp, d // rows),
        in_specs=[pl.BlockSpec((None, None, rows, f2), lambda e, j: (layer, e, j, 0))],
        out_specs=[blk, blk],
        out_shape=[jax.ShapeDtypeStruct((n_exp, d, f), bf16)] * 2,
        scratch_shapes=[pltpu.VMEM((f2, LANES), f32)],
        compiler_params=_params("arbitrary", "arbitrary"),
        name="split_w1",
    )(w1)


class _SplitSide:
    def __init__(self, w1, layer, n_steps):
        _, n_exp, d, f2 = w1.shape
        fits = [r for r in (LANES, 2 * LANES, 4 * LANES, 8 * LANES) if d % r == 0 and n_exp * (d // r) <= n_steps]
        self.ok = bool(fits)
        if not self.ok:
            return
        rows = fits[0]
        per = d // rows
        self.w1 = w1
        self.n_side = n_exp * per
        last = self.n_side - 1
        self.in_spec = pl.BlockSpec((None, None, rows, f2),
                                    lambda i: (layer, jnp.minimum(i, last) // per, jnp.minimum(i, last) % per, 0))
        self.out_spec = pl.BlockSpec((None, rows, f2 // 2),
                                     lambda i: (jnp.minimum(i, last) // per, jnp.minimum(i, last) % per, 0))
        self.out_shape = jax.ShapeDtypeStruct((n_exp, d, f2 // 2), bf16)
        self.scratch = pltpu.VMEM((f2, LANES), f32)

    def run(self, w_ref, g_ref, l_ref, scr):
        @pl.when(pl.program_id(0) < self.n_side)
        def _():
            _split_slabs(w_ref, g_ref, l_ref, scr)


def _dispatch_kernel(*refs, bt, w0, n_exp, fill_pad, side, later):
    row_refs, (runs_ref, pad_ref, h_ref), rest = refs[:TOP_K], refs[TOP_K:TOP_K + 3], list(refs[TOP_K + 3:])
    if not fill_pad:
        rest.pop(0)
    w_ref = rest.pop(0) if side else None
    xs_ref = rest.pop(0)
    g_ref, l_ref = (rest.pop(0), rest.pop(0)) if side else (None, None)
    zero_ref = rest.pop(0) if fill_pad else None
    stage, sem = rest.pop(0), rest.pop(0)
    i = pl.program_id(0)
    slot = i % 2

    if fill_pad:
        @pl.when(i == 0)
        def _():
            zero_ref[...] = jnp.zeros_like(zero_ref)

            def per_expert(e, carry):
                first, count = pad_ref[e], pad_ref[n_exp + e]

                def issue(r, c):
                    _tile_copy(zero_ref, 0, xs_ref, first + r, sem.at[0], PACKED_ROWS).start()
                    return c

                def drain(r, c):
                    _tile_copy(zero_ref, 0, xs_ref, 0, sem.at[0], PACKED_ROWS).wait()
                    return c

                lax.fori_loop(0, count, issue, 0)
                lax.fori_loop(0, count, drain, 0)
                return carry

            lax.fori_loop(0, n_exp, per_expert, 0)

            block = zero_ref.shape[0] // PACKED_ROWS
            tail_first, tail_blocks = pad_ref[2 * n_exp], pad_ref[2 * n_exp + 1]

            def zero_block(j):
                return pltpu.make_async_copy(zero_ref, xs_ref.at[_tile_rows(tail_first + j * block, block, PACKED_ROWS)],
                                             sem.at[0])

            def start_block(j, c):
                zero_block(j).start()
                return c

            def wait_block(j, c):
                zero_block(0).wait()
                return c

            lax.fori_loop(0, tail_blocks, start_block, 0)
            lax.fori_loop(0, tail_blocks, wait_block, 0)

            if later:
                w_first, n_windows, bt_later = later
                assert bt_later <= block and (TOP_K * bt_later) % block == 0

                def zero_run(stage_tile, slot_tile, n):
                    del stage_tile
                    pltpu.make_async_copy(zero_ref.at[_tile_rows(0, n, PACKED_ROWS)],
                                          xs_ref.at[_tile_rows(slot_tile, n, PACKED_ROWS)], sem.at[0]).start()

                for w in range(w_first, w_first + n_windows):
                    _for_each_run_piece(runs_ref, w, n_exp, bt_later, zero_run)
                for _ in range(n_windows * TOP_K * bt_later // block):
                    pltpu.make_async_copy(zero_ref, zero_ref, sem.at[0]).wait()

    def run_copy(buf):
        def make(stage_tile, slot_tile, n):
            return pltpu.make_async_copy(stage.at[buf, _tile_rows(stage_tile, n, PACKED_ROWS)],
                                         xs_ref.at[_tile_rows(slot_tile, n, PACKED_ROWS)], sem.at[buf])
        return make

    def place_all(b):
        def place(t, carry):
            tile = h_ref[_tile_rows(t, 1, PACKED_ROWS), :]
            for kk in range(TOP_K):
                row = pl.multiple_of(row_refs[kk][t], PACKED_ROWS)
                stage[b, pl.ds(row, PACKED_ROWS), :] = tile
            return carry

        lax.fori_loop(0, bt, place, 0, unroll=ROW_UNROLL)

    _per_buffer(slot, place_all)
    _for_each_run_piece(runs_ref, w0 + i, n_exp, bt, lambda a, b, n: run_copy(slot)(a, b, n).start())
    if side:
        side.run(w_ref, g_ref, l_ref, rest.pop(0))

    @pl.when(i > 0)
    def _():
        _wait_window(stage, 1 - slot, sem)

    @pl.when(i == pl.num_programs(0) - 1)
    def _():
        _wait_window(stage, slot, sem)


def _dispatch(rows, runs, pad, h2, xs, n_slots, bt, w0, n_exp, w1_side=None, later=None):
    t = h2.shape[0] // PACKED_ROWS
    n_steps = t // bt
    fill_pad = xs is None
    side = _SplitSide(*w1_side, n_steps) if w1_side else None
    side = side if side is not None and side.ok else None
    smem = pl.BlockSpec(memory_space=pltpu.SMEM)
    in_specs = _per_assignment_specs(bt) + [smem, smem, pl.BlockSpec((bt * PACKED_ROWS, LANES), lambda i: (i, 0))]
    args = list(rows) + [runs, pad, h2]
    out_specs = [pl.BlockSpec(memory_space=pl.ANY)]
    out_shape = [jax.ShapeDtypeStruct((n_slots * PACKED_ROWS, LANES), u32)]
    scratch = [pltpu.VMEM((2, TOP_K * bt * PACKED_ROWS, LANES), u32), pltpu.SemaphoreType.DMA((2,))]
    if fill_pad:
        scratch = [pltpu.VMEM((EXPERT_ROWS * PACKED_ROWS, LANES), u32)] + scratch
    else:
        in_specs.append(pl.BlockSpec(memory_space=pl.ANY))
        args.append(xs)
    if side:
        in_specs.append(side.in_spec)
        args.append(side.w1)
        out_specs += [side.out_spec] * 2
        out_shape += [side.out_shape] * 2
        scratch.append(side.scratch)
    outs = pl.pallas_call(
        functools.partial(_dispatch_kernel, bt=bt, w0=w0, n_exp=n_exp, fill_pad=fill_pad, side=side, later=later),
        grid=(n_steps,),
        in_specs=in_specs,
        out_specs=out_specs,
        out_shape=out_shape,
        scratch_shapes=scratch,
        input_output_aliases={} if fill_pad else {TOP_K + 3: 0},
        compiler_params=_params("arbitrary"),
        name="dispatch",
    )(*args)
    return outs[0], (tuple(outs[1:]) if side else None)


def _expert_kernel(be_ref, nu_ref, x_ref, w1g_ref, w1l_ref, b1g_ref, b1l_ref, w2f_ref, b2_ref, y_ref, w2_ref):
    i = pl.program_id(0)
    active = i < nu_ref[0]
    new_expert = jnp.logical_or(i == 0, be_ref[i] != be_ref[jnp.maximum(i - 1, 0)])

    @pl.when(jnp.logical_and(active, new_expert))
    def _():
        w2_ref[...] = w2f_ref[...].astype(bf16)

    @pl.when(active)
    def _():
        x = _load_packed_tiles(x_ref, x_ref.shape[0] // PACKED_ROWS)
        zg = jnp.dot(x, w1g_ref[...], preferred_element_type=f32) + b1g_ref[...]
        zl = jnp.dot(x, w1l_ref[...], preferred_element_type=f32) + b1l_ref[...]
        g = jnp.minimum(zg, SWIGLU_LIMIT)
        lin = jnp.clip(zl, -SWIGLU_LIMIT, SWIGLU_LIMIT)
        a = g * jax.nn.sigmoid(SWIGLU_ALPHA * g) * (lin + 1.0)
        y = jnp.dot(a.astype(bf16), w2_ref[...], preferred_element_type=f32) + b2_ref[...]
        _store_token_tiles(y_ref, y)

    @pl.when(i >= nu_ref[0])
    def _():
        y_ref[...] = jnp.zeros_like(y_ref)


def _experts(block_e, n_used, xs, w1g, w1l, b1g, b1l, w2_all, layer, b2):
    n_exp, d, f = w1g.shape
    bm = EXPERT_ROWS
    n_slots = xs.shape[0] // PACKED_ROWS
    blk = (bm * SUBLANES, LANES)
    xrow = lambda i, be, nu: (jnp.minimum(i, nu[0] - 1), 0)
    wsel = lambda i, be, nu: (be[i], 0, 0)
    grid_spec = pltpu.PrefetchScalarGridSpec(
        num_scalar_prefetch=2,
        grid=(n_slots // bm,),
        in_specs=[pl.BlockSpec((bm * PACKED_ROWS, LANES), xrow),
                  pl.BlockSpec((None, d, f), wsel), pl.BlockSpec((None, d, f), wsel),
                  pl.BlockSpec((None, 1, f), wsel), pl.BlockSpec((None, 1, f), wsel),
                  pl.BlockSpec((None, None, f, d), lambda i, be, nu: (layer, be[i], 0, 0)),
                  pl.BlockSpec((None, 1, d), wsel)],
        out_specs=pl.BlockSpec(blk, lambda i, be, nu: (i, 0)),
        scratch_shapes=[pltpu.VMEM((f, d), bf16)],
    )
    return pl.pallas_call(
        _expert_kernel,
        grid_spec=grid_spec,
        out_shape=jax.ShapeDtypeStruct((n_slots * SUBLANES, LANES), f32),
        compiler_params=_params("arbitrary"),
        name="experts",
    )(block_e, n_used, xs, w1g, w1l, b1g, b1l, w2_all, b2)


def _combine_kernel(*refs, bt, w0, n_exp, final, side):
    row_refs, gate_refs = refs[:TOP_K], refs[TOP_K:2 * TOP_K]
    (runs_ref, ys_ref, x_ref, g2_ref, fg_ref), rest = refs[2 * TOP_K:2 * TOP_K + 5], list(refs[2 * TOP_K + 5:])
    w_ref = rest.pop(0) if side else None
    o_ref = rest.pop(0)
    g_ref, l_ref = (rest.pop(0), rest.pop(0)) if side else (None, None)
    stage, acc_ref, sem = rest.pop(0), rest.pop(0), rest.pop(0)
    i = pl.program_id(0)
    slot = i % 2

    def run_copy(buf):
        def make(stage_tile, slot_tile, n):
            return pltpu.make_async_copy(ys_ref.at[_tile_rows(slot_tile, n)], stage.at[buf, _tile_rows(stage_tile, n)],
                                         sem.at[buf])
        return make

    def fetch(w, buf):
        _for_each_run_piece(runs_ref, w, n_exp, bt, lambda a, b, n: run_copy(buf)(a, b, n).start())

    @pl.when(i == 0)
    def _():
        fetch(w0, 0)

    @pl.when(i + 1 < pl.num_programs(0))
    def _():
        fetch(w0 + i + 1, 1 - slot)

    if side:
        side.run(w_ref, g_ref, l_ref, rest.pop(0))

    _wait_window(stage, slot, sem)

    def sum_all(b):
        def token(t, carry):
            acc = None
            for kk in range(TOP_K):
                row = pl.multiple_of(row_refs[kk][t], SUBLANES)
                term = gate_refs[kk][t] * stage[b, pl.ds(row, SUBLANES), :]
                acc = term if acc is None else acc + term
            acc_ref[_tile_rows(t), :] = acc
            return carry

        lax.fori_loop(0, bt, token, 0, unroll=ROW_UNROLL)

    _per_buffer(slot, sum_all)
    xo = x_ref[...] + g2_ref[...] * _load_token_tiles(acc_ref, bt)
    if final:
        xo = xo * lax.rsqrt(jnp.mean(xo * xo, axis=-1, keepdims=True) + EPS) * fg_ref[...]
    o_ref[...] = xo


def _combine(rows, gates, runs, ys, x_mid, n_seq, g2, final_g, final, bt, w0, n_exp, w1_side=None):
    t, d = x_mid.shape
    tps = n_seq // bt
    row = lambda i: (i, 0)
    n_steps = t // bt
    side = _SplitSide(*w1_side, n_steps) if w1_side else None
    side = side if side is not None and side.ok else None
    in_specs = _per_assignment_specs(bt) + _per_assignment_specs(bt) + [
        pl.BlockSpec(memory_space=pltpu.SMEM),
        pl.BlockSpec(memory_space=pl.ANY),
        pl.BlockSpec((bt, d), row),
        pl.BlockSpec((None, 1, d), lambda i: (i // tps, 0, 0)),
        _full(final_g)]
    args = list(rows) + list(gates) + [runs, ys, x_mid, g2, final_g]
    out_specs = [pl.BlockSpec((bt, d), row)]
    out_shape = [jax.ShapeDtypeStruct((t, d), f32)]
    scratch = [pltpu.VMEM((2, TOP_K * bt * SUBLANES, LANES), f32), pltpu.VMEM((bt * SUBLANES, LANES), f32),
               pltpu.SemaphoreType.DMA((2,))]
    if side:
        in_specs.append(side.in_spec)
        args.append(side.w1)
        out_specs += [side.out_spec] * 2
        out_shape += [side.out_shape] * 2
        scratch.append(side.scratch)
    outs = pl.pallas_call(
        functools.partial(_combine_kernel, bt=bt, w0=w0, n_exp=n_exp, final=final, side=side),
        grid=(n_steps,),
        in_specs=in_specs,
        out_specs=out_specs,
        out_shape=out_shape,
        scratch_shapes=scratch,
        compiler_params=_params("arbitrary"),
        name="combine",
    )(*args)
    return outs[0], (tuple(outs[1:]) if side else None)


def _rope_tables(n_tokens):
    rows = n_tokens // GRID_W
    row = jnp.repeat(jnp.arange(rows), GRID_W).astype(f32)
    col = jnp.tile(jnp.arange(GRID_W), rows).astype(f32)
    n_freq = HEAD_DIM // 4
    inv = ROPE_BASE ** (-jnp.arange(n_freq, dtype=f32) / n_freq)
    ang = jnp.concatenate([row[:, None] * inv, col[:, None] * inv], axis=-1)
    cos, sin = jnp.cos(ang), jnp.sin(ang)
    return jnp.tile(cos, (1, 4)), jnp.concatenate([-sin, sin, -sin, sin], axis=-1)


def kernel(x, c, ctx, c_ctx, norm1_g, norm2_g, ada_w, ada_b, w_in, attn_sink, sgu_ws, sgu_b, sgu_ln_g, sgu_ln_b,
           pool_w, pool_scale, conv_w, w_out, router_w, router_b, exp_w1, exp_b1, exp_w2, exp_b2, final_g):
    n_batch, n_seq, d = x.shape
    n_ctx = ctx.shape[1]
    depth = ada_w.shape[0]
    n_exp = router_w.shape[2]
    assert d == SUBLANES * LANES, "token rows are moved as single (8, 128) tiles"
    t_lat, t_ctx = n_batch * n_seq, n_batch * n_ctx
    bm = EXPERT_ROWS

    cvec = jnp.concatenate([c, c_ctx[None, :], jnp.zeros((SUBLANES - n_batch - 1, d), f32)], axis=0)
    mods = _ada(cvec, ada_w, ada_b)

    cos_l, sin_l = _rope_tables(n_seq)
    cos_c, sin_c = jnp.ones((n_ctx, LANES), f32), jnp.zeros((n_ctx, LANES), f32)

    xl = x.reshape(t_lat, d)
    xc = ctx.reshape(t_ctx, d)
    row2 = lambda a: a.reshape(1, -1)
    w1_split = {}

    for l in range(depth):
        last = l == depth - 1
        ml = mods[l, :n_batch].reshape(n_batch, 6, 1, d)
        mc = jnp.broadcast_to(mods[l, n_batch].reshape(1, 6, 1, d), (n_batch, 6, 1, d))
        sh1l, sc1l, g1l, sh2l, sc2l, g2l = (ml[:, i] for i in range(6))
        sh1c, sc1c, g1c, sh2c, sc2c, g2c = (mc[:, i] for i in range(6))

        w_in_bf = w_in[l].astype(bf16)
        wo_bf = w_out[l].astype(bf16)
        ws_bf = sgu_ws[l].astype(bf16)
        sgu_bias = jnp.repeat(sgu_b[l].T, SGU_W // SGU_HEADS, axis=1)
        pool_bd = jax.scipy.linalg.block_diag(*[pool_w[l, g] for g in range(pool_w.shape[1])]).astype(bf16)
        n1g, n2g = row2(norm1_g[l]), row2(norm2_g[l])
        lng, lnb, psc = row2(sgu_ln_g[l]), row2(sgu_ln_b[l]), row2(pool_scale[l])
        rw_t = router_w[l].T
        rb = router_b[l].reshape(n_exp, 1)
        sink = attn_sink[l]
        mix_w = (ws_bf, sgu_bias, pool_bd, psc, conv_w[l], wo_bf, rw_t, rb)

        qc, kvc, vnc, mfc = _inproj(xc, n_ctx, sh1c, sc1c, n1g, w_in_bf, cos_c, sin_c, lng, lnb)
        ql, kvl, vnl, mfl = _inproj(xl, n_seq, sh1l, sc1l, n1g, w_in_bf, cos_l, sin_l, lng, lnb)
        attn_l = _window_attn(ql, kvl, kvc, sink, n_batch, n_seq, n_ctx)
        bt_l, bt_c = min(MIXER_TOKENS, n_seq), min(MIXER_TOKENS, n_ctx)
        xmid_l, h2_l, s4_l, s8_l, gt_l, cnt = _mixer(attn_l, vnl, mfl, xl, n_seq, g1l, sh2l, sc2l, n2g, *mix_w)
        n_win_l = cnt.shape[0]
        t_all = t_lat
        if not last:
            attn_c = _ctx_attn(qc, kvc, sink, n_batch, n_ctx)
            xmid_c, h2_c, s4_c, s8_c, gt_c, cnt_c = _mixer(attn_c, vnc, mfc, xc, n_ctx, g1c, sh2c, sc2c, n2g, *mix_w)
            cnt = jnp.concatenate([cnt, cnt_c], axis=0)
            t_all = t_lat + t_ctx

        win_cnt = cnt[:, :, 0].astype(i32)
        counts = jnp.sum(win_cnt, axis=0)
        padded = (counts + bm - 1) // bm * bm
        pends = jnp.cumsum(padded)
        pstarts = pends - padded
        n_blocks = -(-(t_all * TOP_K) // bm) + n_exp
        starts = jnp.arange(n_blocks, dtype=i32) * bm
        block_e = jnp.minimum(jnp.sum((pends[None, :] <= starts[:, None]).astype(i32), axis=1), n_exp - 1)
        n_used = (pends[-1:] // bm).astype(i32)
        pad = jnp.concatenate([pstarts + counts, padded - counts, pends[-1:], n_blocks - pends[-1:] // bm]).astype(i32)
        run_stage = jnp.cumsum(win_cnt, axis=1) - win_cnt
        run_slot = pstarts[None, :] + jnp.cumsum(win_cnt, axis=0) - win_cnt
        runs = jnp.stack([run_stage, run_slot, win_cnt], axis=-1).reshape(-1)

        xs, split = _dispatch(s4_l, runs, pad, h2_l, None, n_blocks * bm, bt_l, 0, n_exp,
                              None if l in w1_split else (exp_w1, l),
                              None if last else (n_win_l, t_ctx // bt_c, bt_c))
        if l not in w1_split:
            w1_split[l] = split or _split_w1(exp_w1, l)
        if not last:
            xs, _ = _dispatch(s4_c, runs, pad, h2_c, xs, n_blocks * bm, bt_c, n_win_l, n_exp)

        w1g, w1l = w1_split[l]
        b1 = exp_b1[l]
        b1g, b1l = b1[:, None, 0::2], b1[:, None, 1::2]
        ys = _experts(block_e, n_used, xs, w1g, w1l, b1g, b1l, exp_w2, l, exp_b2[l][:, None, :])

        fg = row2(final_g)
        xl, split = _combine(s8_l, gt_l, runs, ys, xmid_l, n_seq, g2l, fg, last, bt_l, 0, n_exp,
                             None if last else (exp_w1, l + 1))
        if split:
            w1_split[l + 1] = split
        if not last:
            xc, _ = _combine(s8_c, gt_c, runs, ys, xmid_c, n_ctx, g2c, fg, False, bt_c, n_win_l, n_exp)

    return xl.reshape(n_batch, n_seq, d)
```

```python
import functools

import jax
import jax.numpy as jnp
from jax import lax
from jax.experimental import pallas as pl
from jax.experimental.pallas import tpu as pltpu

f32 = jnp.float32
bf16 = jnp.bfloat16
i32 = jnp.int32
u32 = jnp.uint32

GRID_W = 64
EPS = 1e-6
N_KV_HEADS = 2
GQA_GROUP = 4
HEAD_DIM = 64
WINDOW = 128
ROPE_BASE = 10000.0
ATT_Q_W = 512
SGU_HEADS = 4
SGU_W = 256
SGU_CHUNK = 128
POOL_CH = 256
TOP_K = 4
SWIGLU_LIMIT = 7.0
SWIGLU_ALPHA = 1.702
SQRT_HALF = 0.7071067811865476

LANES = 128
SUBLANES = 8
PACKED_ROWS = SUBLANES // 2
VMEM_LIMIT_BYTES = 56 * 1024 * 1024

INPROJ_TOKENS = 512
ATTN_TOKENS = 128
ATTN_BLOCKS_PER_STEP = 4
MIXER_TOKENS = 512
EXPERT_ROWS = 512
HALO = 8
ROW_UNROLL = 32


def _params(*sem):
    return pltpu.CompilerParams(dimension_semantics=sem, vmem_limit_bytes=VMEM_LIMIT_BYTES)


def _full(a):
    nd = a.ndim
    return pl.BlockSpec(a.shape, lambda *_: (0,) * nd)


def _store_token_tiles(ref, val):
    n = val.shape[0]
    for s in range(SUBLANES):
        ref[pl.ds(s, n, stride=SUBLANES), :] = val[:, s * LANES:(s + 1) * LANES]


def _load_token_tiles(ref, n):
    return jnp.concatenate([ref[pl.ds(s, n, stride=SUBLANES), :] for s in range(SUBLANES)], axis=1)


HIGH_HALF = 0xFFFF0000


def _store_packed_tiles(ref, val):
    n = val.shape[0]
    half = PACKED_ROWS * LANES
    for s in range(PACKED_ROWS):
        lo = val[:, s * LANES:(s + 1) * LANES].astype(bf16).astype(f32)
        hi = val[:, half + s * LANES:half + (s + 1) * LANES].astype(bf16).astype(f32)
        lo_bits = lax.shift_right_logical(lax.bitcast_convert_type(lo, u32), jnp.uint32(16))
        hi_bits = lax.bitcast_convert_type(hi, u32) & jnp.uint32(HIGH_HALF)
        ref[pl.ds(s, n, stride=PACKED_ROWS), :] = lo_bits | hi_bits


def _load_packed_tiles(ref, n):
    los, his = [], []
    for s in range(PACKED_ROWS):
        w = ref[pl.ds(s, n, stride=PACKED_ROWS), :]
        los.append(lax.bitcast_convert_type(lax.shift_left(w, jnp.uint32(16)), f32))
        his.append(lax.bitcast_convert_type(w & jnp.uint32(HIGH_HALF), f32))
    return jnp.concatenate(los + his, axis=1).astype(bf16)


def _gelu(x):
    return 0.5 * x * (1.0 + lax.erf(x * SQRT_HALF))


def _ada_kernel(c_ref, w_ref, b_ref, o_ref):
    c = c_ref[...]
    s = c * jax.nn.sigmoid(c)
    o_ref[...] = jnp.dot(s, w_ref[...], precision=lax.Precision.HIGHEST, preferred_element_type=f32) + b_ref[...]


def _ada(cvec, ada_w, ada_b):
    depth, d, n = ada_w.shape
    tn = 1536
    return pl.pallas_call(
        _ada_kernel,
        grid=(depth, n // tn),
        in_specs=[pl.BlockSpec(cvec.shape, lambda l, j: (0, 0)),
                  pl.BlockSpec((None, d, tn), lambda l, j: (l, 0, j)),
                  pl.BlockSpec((None, 1, tn), lambda l, j: (l, 0, j))],
        out_specs=pl.BlockSpec((None, cvec.shape[0], tn), lambda l, j: (l, 0, j)),
        out_shape=jax.ShapeDtypeStruct((depth, cvec.shape[0], n), f32),
        compiler_params=_params("arbitrary", "arbitrary"),
        name="ada_mod",
    )(cvec, ada_w, ada_b.reshape(depth, 1, n))


def _inproj_kernel(x_ref, sh_ref, sc_ref, g_ref, w_ref, cos_ref, sin_ref, lng_ref, lnb_ref,
                   q_ref, kv_ref, vn_ref, mixf_ref):
    x = x_ref[...]
    y = x * lax.rsqrt(jnp.mean(x * x, axis=-1, keepdims=True) + EPS) * g_ref[...]
    h = y * (1.0 + sc_ref[...]) + sh_ref[...]
    p = jnp.dot(h.astype(bf16), w_ref[...], preferred_element_type=f32)

    cos = cos_ref[...]
    sin = sin_ref[...]
    lane = lax.broadcasted_iota(i32, cos.shape, 1)
    first_half = (lane & (HEAD_DIM - 1)) < HEAD_DIM // 2

    def rope(t):
        partner = jnp.where(first_half, pltpu.roll(t, LANES - HEAD_DIM // 2, 1), pltpu.roll(t, HEAD_DIM // 2, 1))
        return t * cos + partner * sin

    scale = HEAD_DIM ** -0.5
    for m in range(ATT_Q_W // LANES):
        q_ref[:, m * LANES:(m + 1) * LANES] = (rope(p[:, m * LANES:(m + 1) * LANES]) * scale).astype(bf16)
    k = rope(p[:, 512:640])
    v = p[:, 640:768]
    kv_ref[:, 0:128] = k.astype(bf16)
    kv_ref[:, 128:256] = pltpu.roll(k, HEAD_DIM, 1).astype(bf16)
    kv_ref[:, 256:384] = v.astype(bf16)
    kv_ref[:, 384:512] = pltpu.roll(v, HEAD_DIM, 1).astype(bf16)

    u = _gelu(p[:, 768:1024])
    gv = _gelu(p[:, 1024:1280])
    mu = jnp.mean(gv, axis=-1, keepdims=True)
    var = jnp.mean(jnp.square(gv - mu), axis=-1, keepdims=True)
    vn_ref[...] = ((gv - mu) * lax.rsqrt(var + EPS) * lng_ref[...] + lnb_ref[...]).astype(bf16)

    mixf_ref[:, 0:256] = u
    mixf_ref[:, 256:512] = p[:, 1280:1536]
    mixf_ref[:, 512:768] = p[:, 1536:1792]
    mixf_ref[:, 768:1024] = p[:, 1792:2048] * p[:, 2048:2304]


def _inproj(x2, n_seq, shift, scale, g, w_bf, cos_t, sin_t, ln_g, ln_b):
    t, d = x2.shape
    bt = min(INPROJ_TOKENS, n_seq)
    tps = n_seq // bt
    ncol = w_bf.shape[1]
    row = lambda i: (i, 0)
    return pl.pallas_call(
        _inproj_kernel,
        grid=(t // bt,),
        in_specs=[pl.BlockSpec((bt, d), row),
                  pl.BlockSpec((None, 1, d), lambda i: (i // tps, 0, 0)),
                  pl.BlockSpec((None, 1, d), lambda i: (i // tps, 0, 0)),
                  _full(g),
                  pl.BlockSpec((d, ncol), lambda i: (0, 0)),
                  pl.BlockSpec((bt, LANES), lambda i: (i % tps, 0)),
                  pl.BlockSpec((bt, LANES), lambda i: (i % tps, 0)),
                  _full(ln_g), _full(ln_b)],
        out_specs=[pl.BlockSpec((bt, ATT_Q_W), row), pl.BlockSpec((bt, 512), row),
                   pl.BlockSpec((bt, SGU_W), row), pl.BlockSpec((bt, 1024), row)],
        out_shape=[jax.ShapeDtypeStruct((t, ATT_Q_W), bf16), jax.ShapeDtypeStruct((t, 512), bf16),
                   jax.ShapeDtypeStruct((t, SGU_W), bf16), jax.ShapeDtypeStruct((t, 1024), f32)],
        compiler_params=_params("arbitrary"),
        name="inproj",
    )(x2, shift, scale, g, w_bf, cos_t, sin_t, ln_g, ln_b)


def _attn_block(sink_ref, q, kv, bias, o_ref, row0):
    k_nat, k_swp, v_nat, v_swp = (kv[:, i * LANES:(i + 1) * LANES] for i in range(4))
    nq = q.shape[0]
    low = lax.broadcasted_iota(i32, (nq, LANES), 1) < HEAD_DIM
    top = lax.broadcasted_iota(i32, (2 * nq, 1), 0) < nq
    zero = jnp.zeros((nq, LANES), q.dtype)
    assert (N_KV_HEADS, GQA_GROUP, 2 * HEAD_DIM) == (2, 4, LANES), "two heads per 128-lane chunk, two chunks per kv head"
    for kvh in range(N_KV_HEADS):
        chunks = [q[:, (2 * kvh + i) * LANES:(2 * kvh + i + 1) * LANES] for i in range(2)]
        outs = []
        for half in range(2):
            keep = low if half == 0 else jnp.logical_not(low)
            qz = jnp.concatenate([jnp.where(keep, c, zero) for c in chunks], axis=0)
            kh = k_nat if kvh == half else k_swp
            vh = v_nat if kvh == half else v_swp
            s = lax.dot_general(qz, kh, (((1,), (1,)), ((), ())), preferred_element_type=f32)
            if bias is not None:
                s = s + bias
            h0 = GQA_GROUP * kvh + half
            sk = jnp.where(top, sink_ref[h0], sink_ref[h0 + 2])
            mx = jnp.maximum(jnp.max(s, axis=1, keepdims=True), sk)
            e = jnp.exp(s - mx)
            den = jnp.sum(e, axis=1, keepdims=True) + jnp.exp(sk - mx)
            outs.append(jnp.dot(e.astype(bf16), vh, preferred_element_type=f32) / den)
        for i in range(2):
            rows = slice(i * nq, (i + 1) * nq)
            m = 2 * kvh + i
            o_ref[row0:row0 + nq, m * LANES:(m + 1) * LANES] = jnp.where(low, outs[0][rows], outs[1][rows]).astype(bf16)


def _window_attn_kernel(sink_ref, q_ref, kvp_ref, kvm_ref, kvn_ref, kvx_ref, o_ref, *, nb, per_step):
    jj = pl.program_id(1)
    nq = ATTN_TOKENS
    nband = 3 * nq
    kv_all = jnp.concatenate([kvp_ref[...], kvm_ref[...], kvn_ref[...]], axis=0)
    kvx = kvx_ref[...]
    r = lax.broadcasted_iota(i32, (nq, nband), 0)
    c = lax.broadcasted_iota(i32, (nq, nband), 1)
    dlt = c - r
    in_window = (dlt >= 0) & (dlt <= 2 * WINDOW)
    ctx_zeros = jnp.zeros((nq, kvx.shape[0]), f32)
    for i in range(per_step):
        j = per_step * jj + i
        lo = jnp.where(j == 0, nq, 0)
        hi = jnp.where(j == nb - 1, 2 * nq, nband)
        valid = in_window & (c >= lo) & (c < hi)
        bias = jnp.concatenate([jnp.where(valid, 0.0, -jnp.inf).astype(f32), ctx_zeros], axis=1)
        kv = jnp.concatenate([kv_all[i * nq:i * nq + nband], kvx], axis=0)
        _attn_block(sink_ref, q_ref[i * nq:(i + 1) * nq, :], kv, jnp.concatenate([bias, bias], axis=0), o_ref, i * nq)


def _ctx_attn_kernel(sink_ref, q_ref, kvx_ref, o_ref):
    _attn_block(sink_ref, q_ref[...], kvx_ref[...], None, o_ref, 0)


def _window_attn(q, kv, kv_ctx, sink, n_batch, n_seq, n_ctx):
    t = q.shape[0]
    nb = n_seq // ATTN_TOKENS
    p = next(c for c in (ATTN_BLOCKS_PER_STEP, 2, 1) if nb % c == 0)
    steps = nb // p
    one, many = (ATTN_TOKENS, 512), (p * ATTN_TOKENS, 512)
    return pl.pallas_call(
        functools.partial(_window_attn_kernel, nb=nb, per_step=p),
        grid=(n_batch, steps),
        in_specs=[pl.BlockSpec(memory_space=pltpu.SMEM),
                  pl.BlockSpec(many, lambda b, j: (b * steps + j, 0)),
                  pl.BlockSpec(one, lambda b, j: (b * nb + jnp.maximum(p * j - 1, 0), 0)),
                  pl.BlockSpec(many, lambda b, j: (b * steps + j, 0)),
                  pl.BlockSpec(one, lambda b, j: (b * nb + jnp.minimum(p * j + p, nb - 1), 0)),
                  pl.BlockSpec((n_ctx, 512), lambda b, j: (b, 0))],
        out_specs=pl.BlockSpec(many, lambda b, j: (b * steps + j, 0)),
        out_shape=jax.ShapeDtypeStruct((t, ATT_Q_W), bf16),
        compiler_params=_params("arbitrary", "arbitrary"),
        name="window_attn",
    )(sink, q, kv, kv, kv, kv_ctx)


def _ctx_attn(q, kv_ctx, sink, n_batch, n_ctx):
    nb = n_ctx // ATTN_TOKENS
    blk = (ATTN_TOKENS, 512)
    return pl.pallas_call(
        _ctx_attn_kernel,
        grid=(n_batch, nb),
        in_specs=[pl.BlockSpec(memory_space=pltpu.SMEM),
                  pl.BlockSpec(blk, lambda b, j: (b * nb + j, 0)),
                  pl.BlockSpec((n_ctx, 512), lambda b, j: (b, 0))],
        out_specs=pl.BlockSpec(blk, lambda b, j: (b * nb + j, 0)),
        out_shape=jax.ShapeDtypeStruct(q.shape, bf16),
        compiler_params=_params("arbitrary", "arbitrary"),
        name="ctx_attn",
    )(sink, q, kv_ctx)


def _mixer_kernel(attn_ref, vn_ref, mf_ref, mfp_ref, mfn_ref, x_ref, g1_ref, sh2_ref, sc2_ref, n2g_ref,
                  ws_ref, sb_ref, pw_ref, ps_ref, cw_ref, wo_ref, rw_ref, rb_ref,
                  xmid_ref, h2_ref, *table_refs, n_seq, bt):
    s4_refs, s8_refs, gt_refs = (table_refs[j * TOP_K:(j + 1) * TOP_K] for j in range(3))
    cnt_ref = table_refs[3 * TOP_K]
    i = pl.program_id(0)
    tps = n_seq // bt
    si = i % tps
    first = si == 0
    last = si == tps - 1
    n_ext = bt + 2 * HALO

    mf = mf_ref[...]
    u = mf[:, 0:256]

    def extended(lo, hi):
        prev = jnp.where(first, 0.0, mfp_ref[:, lo:hi])
        nxt = jnp.where(last, 0.0, mfn_ref[:, lo:hi])
        return jnp.concatenate([prev, mf[:, lo:hi], nxt], axis=0)

    def shifted(a, s):
        return pltpu.roll(a, s % n_ext, 0)

    xe = extended(256, 512)
    a1 = shifted(xe, 1) + xe
    a2 = shifted(a1, 1) + shifted(a1, -1)
    a3 = shifted(a2, 2) + shifted(a2, -2)
    a4 = shifted(a3, 4) + shifted(a3, -4)
    lane = lax.broadcasted_iota(i32, (bt, POOL_CH), 1)
    grp = lane >> 6
    sl = slice(HALO, HALO + bt)
    wsum = jnp.where(grp == 0, a1[sl], jnp.where(grp == 1, a2[sl], jnp.where(grp == 2, a3[sl], a4[sl])))
    pos = lax.broadcasted_iota(i32, (bt, POOL_CH), 0) + si * bt
    halfw = jnp.left_shift(1, grp)
    cnt = jnp.minimum(pos + halfw, n_seq) - jnp.maximum(pos - halfw, 0)
    dpool = wsum / cnt.astype(f32) - mf[:, 256:512]
    yc = jnp.dot(dpool.astype(bf16), pw_ref[...], preferred_element_type=f32) * ps_ref[...]

    ye = extended(768, 1024)
    cw = cw_ref[...]
    z = shifted(ye, 1) * cw[0:1, :] + ye * cw[1:2, :] + shifted(ye, -1) * cw[2:3, :]
    yd = mf[:, 512:768] * z[sl]

    hgrp = lax.broadcasted_iota(i32, (SGU_CHUNK, SGU_W), 1) >> 6
    ybs = []
    for cidx in range(bt // SGU_CHUNK):
        rows = slice(cidx * SGU_CHUNK, (cidx + 1) * SGU_CHUNK)
        vn_c = vn_ref[rows, :]
        s = jnp.zeros((SGU_CHUNK, SGU_W), f32)
        for hh in range(SGU_HEADS):
            sh = jnp.dot(ws_ref[hh], vn_c, preferred_element_type=f32)
            s = jnp.where(hgrp == hh, sh, s)
        ybs.append(u[rows, :] * (s + sb_ref[...]))
    yb = jnp.concatenate(ybs, axis=0)

    mix = jnp.concatenate([attn_ref[...], yb.astype(bf16), yc.astype(bf16), yd.astype(bf16)], axis=1)
    mo = jnp.dot(mix, wo_ref[...], preferred_element_type=f32)
    xm = x_ref[...] + g1_ref[...] * mo
    xmid_ref[...] = xm

    y = xm * lax.rsqrt(jnp.mean(xm * xm, axis=-1, keepdims=True) + EPS) * n2g_ref[...]
    h2 = y * (1.0 + sc2_ref[...]) + sh2_ref[...]
    _store_packed_tiles(h2_ref, h2)

    def split(a):
        hi = a.astype(bf16)
        return hi, (a - hi.astype(f32)).astype(bf16)

    def dot_nt(a, b):
        return lax.dot_general(a, b, (((1,), (1,)), ((), ())), preferred_element_type=f32)

    (w_hi, w_lo), (h_hi, h_lo) = split(rw_ref[...]), split(h2)
    lt = dot_nt(w_hi, h_hi) + (dot_nt(w_hi, h_lo) + dot_nt(w_lo, h_hi)) + rb_ref[...]
    n_exp = lt.shape[0]
    eidx = lax.broadcasted_iota(i32, lt.shape, 0)
    work = lt
    idxs, vals = [], []
    for _ in range(TOP_K):
        m = jnp.max(work, axis=0, keepdims=True)
        idx = jnp.min(jnp.where(work == m, eidx, n_exp), axis=0, keepdims=True)
        idxs.append(idx)
        vals.append(m)
        work = jnp.where(eidx == idx, -jnp.inf, work)
    exps = [jnp.exp(v - vals[0]) for v in vals]
    den = exps[0] + exps[1] + exps[2] + exps[3]
    onehot = jnp.zeros(lt.shape, f32)
    for kk in range(TOP_K):
        gt_refs[kk][...] = exps[kk] / den
        onehot = onehot + (eidx == idxs[kk]).astype(f32)
    tri = (lax.broadcasted_iota(i32, (bt, bt), 0) < lax.broadcasted_iota(i32, (bt, bt), 1)).astype(bf16)
    before = jnp.dot(onehot.astype(bf16), tri, preferred_element_type=f32)
    cnt = jnp.sum(onehot, axis=1, keepdims=True)
    row = lax.broadcasted_iota(i32, (n_exp, LANES), 0)
    incl = jnp.broadcast_to(cnt, (n_exp, LANES))
    shift = 1
    while shift < n_exp:
        incl = incl + jnp.where(row >= shift, pltpu.roll(incl, shift, 0), 0.0)
        shift *= 2
    place = before + (incl[:, 0:1] - cnt)
    for kk in range(TOP_K):
        tile = jnp.sum(jnp.where(eidx == idxs[kk], place, 0.0), axis=0, keepdims=True).astype(i32)
        s4_refs[kk][...] = tile * PACKED_ROWS
        s8_refs[kk][...] = tile * SUBLANES
    cnt_ref[...] = cnt


def _mixer(attn, vn, mixf, x2, n_seq, g1, sh2, sc2, n2g, ws_bf, sgu_bias, pool_bd, pool_scale, conv_w,
           wo_bf, rw_t, rb):
    t, d = x2.shape
    bt = min(MIXER_TOKENS, n_seq)
    tps = n_seq // bt
    hb = bt // HALO
    n_halo = t // HALO
    n_exp = rw_t.shape[0]
    row = lambda i: (i, 0)
    per_batch = pl.BlockSpec((None, 1, d), lambda i: (i // tps, 0, 0))
    col = lambda i: (0, i)
    outs = pl.pallas_call(
        functools.partial(_mixer_kernel, n_seq=n_seq, bt=bt),
        grid=(t // bt,),
        in_specs=[pl.BlockSpec((bt, ATT_Q_W), row), pl.BlockSpec((bt, SGU_W), row), pl.BlockSpec((bt, 1024), row),
                  pl.BlockSpec((HALO, 1024), lambda i: (jnp.maximum(i * hb - 1, 0), 0)),
                  pl.BlockSpec((HALO, 1024), lambda i: (jnp.minimum((i + 1) * hb, n_halo - 1), 0)),
                  pl.BlockSpec((bt, d), row), per_batch, per_batch, per_batch, _full(n2g),
                  _full(ws_bf), _full(sgu_bias), _full(pool_bd), _full(pool_scale), _full(conv_w),
                  _full(wo_bf), _full(rw_t), _full(rb)],
        out_specs=[pl.BlockSpec((bt, d), row), pl.BlockSpec((bt * PACKED_ROWS, LANES), row)]
        + [pl.BlockSpec((1, bt), col)] * (3 * TOP_K) + [pl.BlockSpec((None, n_exp, 1), lambda i: (i, 0, 0))],
        out_shape=[jax.ShapeDtypeStruct((t, d), f32), jax.ShapeDtypeStruct((t * PACKED_ROWS, LANES), u32)]
        + [jax.ShapeDtypeStruct((1, t), i32)] * (2 * TOP_K) + [jax.ShapeDtypeStruct((1, t), f32)] * TOP_K
        + [jax.ShapeDtypeStruct((t // bt, n_exp, 1), f32)],
        compiler_params=_params("arbitrary"),
        name="mixer_router",
    )(attn, vn, mixf, mixf, mixf, x2, g1, sh2, sc2, n2g, ws_bf, sgu_bias, pool_bd, pool_scale, conv_w,
      wo_bf, rw_t, rb)
    tables = [o.reshape(t) for o in outs[2:2 + 3 * TOP_K]]
    return outs[0], outs[1], tables[:TOP_K], tables[TOP_K:2 * TOP_K], tables[2 * TOP_K:], outs[-1]


def _per_assignment_specs(bt):
    return [pl.BlockSpec((bt,), lambda i: (i,), memory_space=pltpu.SMEM)] * TOP_K


def _tile_rows(t, n=1, rows=SUBLANES):
    return pl.ds(pl.multiple_of(t * rows, rows), n * rows)


def _tile_copy(src, s, dst, d, sem, rows=SUBLANES):
    return pltpu.make_async_copy(src.at[_tile_rows(s, 1, rows)], dst.at[_tile_rows(d, 1, rows)], sem)


def _for_each_run_piece(runs_ref, w, n_exp, max_len, fn):
    bits = [1 << b for b in range(max_len.bit_length() - 1, -1, -1)]

    def per_expert(e, carry):
        base = (w * n_exp + e) * 3
        stage0, slot0, length = runs_ref[base], runs_ref[base + 1], runs_ref[base + 2]
        done = 0
        for bit in bits:
            take = length & bit

            @pl.when(take != 0)
            def _():
                fn(stage0 + done, slot0 + done, bit)

            done = done + take
        return carry

    lax.fori_loop(0, n_exp, per_expert, 0)


def _wait_window(stage, buf, sem):
    pltpu.make_async_copy(stage.at[buf], stage.at[buf], sem.at[buf]).wait()


def _per_buffer(slot, fn):
    for b in range(2):
        @pl.when(slot == b)
        def _():
            fn(b)


def _split_slabs(w_ref, g_ref, l_ref, scr):
    f = g_ref.shape[1]
    for s in range(w_ref.shape[0] // LANES):
        rows = slice(s * LANES, (s + 1) * LANES)
        scr[...] = w_ref[rows, :].T
        g_ref[rows, :] = scr[pl.ds(0, f, stride=2), :].T.astype(bf16)
        l_ref[rows, :] = scr[pl.ds(1, f, stride=2), :].T.astype(bf16)


def _split_w1(w1, layer):
    _, n_exp, d, f2 = w1.shape
    f = f2 // 2
    rows = 2 * LANES
    blk = pl.BlockSpec((None, rows, f), lambda e, j: (e, j, 0))
    return pl.pallas_call(
        _split_slabs,
        grid=(n_exp, d // rows),
        in_specs=[pl.BlockSpec((None, None, rows, f2), lambda e, j: (layer, e, j, 0))],
        out_specs=[blk, blk],
        out_shape=[jax.ShapeDtypeStruct((n_exp, d, f), bf16)] * 2,
        scratch_shapes=[pltpu.VMEM((f2, LANES), f32)],
        compiler_params=_params("arbitrary", "arbitrary"),
        name="split_w1",
    )(w1)


class _SplitSide:
    def __init__(self, w1, layer, n_steps):
        _, n_exp, d, f2 = w1.shape
        fits = [r for r in (LANES, 2 * LANES, 4 * LANES, 8 * LANES) if d % r == 0 and n_exp * (d // r) <= n_steps]
        self.ok = bool(fits)
        if not self.ok:
            return
        rows = fits[0]
        per = d // rows
        self.w1 = w1
        self.n_side = n_exp * per
        last = self.n_side - 1
        self.in_spec = pl.BlockSpec((None, None, rows, f2),
                                    lambda i: (layer, jnp.minimum(i, last) // per, jnp.minimum(i, last) % per, 0))
        self.out_spec = pl.BlockSpec((None, rows, f2 // 2),
                                     lambda i: (jnp.minimum(i, last) // per, jnp.minimum(i, last) % per, 0))
        self.out_shape = jax.ShapeDtypeStruct((n_exp, d, f2 // 2), bf16)
        self.scratch = pltpu.VMEM((f2, LANES), f32)

    def run(self, w_ref, g_ref, l_ref, scr):
        @pl.when(pl.program_id(0) < self.n_side)
        def _():
            _split_slabs(w_ref, g_ref, l_ref, scr)


def _dispatch_kernel(*refs, bt, w0, n_exp, fill_pad, side, later):
    row_refs, (runs_ref, pad_ref, h_ref), rest = refs[:TOP_K], refs[TOP_K:TOP_K + 3], list(refs[TOP_K + 3:])
    if not fill_pad:
        rest.pop(0)
    w_ref = rest.pop(0) if side else None
    xs_ref = rest.pop(0)
    g_ref, l_ref = (rest.pop(0), rest.pop(0)) if side else (None, None)
    zero_ref = rest.pop(0) if fill_pad else None
    stage, sem = rest.pop(0), rest.pop(0)
    i = pl.program_id(0)
    slot = i % 2

    if fill_pad:
        @pl.when(i == 0)
        def _():
            zero_ref[...] = jnp.zeros_like(zero_ref)

            def per_expert(e, carry):
                first, count = pad_ref[e], pad_ref[n_exp + e]

                def issue(r, c):
                    _tile_copy(zero_ref, 0, xs_ref, first + r, sem.at[0], PACKED_ROWS).start()
                    return c

                def drain(r, c):
                    _tile_copy(zero_ref, 0, xs_ref, 0, sem.at[0], PACKED_ROWS).wait()
                    return c

                lax.fori_loop(0, count, issue, 0)
                lax.fori_loop(0, count, drain, 0)
                return carry

            lax.fori_loop(0, n_exp, per_expert, 0)

            block = zero_ref.shape[0] // PACKED_ROWS
            tail_first, tail_blocks = pad_ref[2 * n_exp], pad_ref[2 * n_exp + 1]

            def zero_block(j):
                return pltpu.make_async_copy(zero_ref, xs_ref.at[_tile_rows(tail_first + j * block, block, PACKED_ROWS)],
                                             sem.at[0])

            def start_block(j, c):
                zero_block(j).start()
                return c

            def wait_block(j, c):
                zero_block(0).wait()
                return c

            lax.fori_loop(0, tail_blocks, start_block, 0)
            lax.fori_loop(0, tail_blocks, wait_block, 0)

            if later:
                w_first, n_windows, bt_later = later
                assert bt_later <= block and (TOP_K * bt_later) % block == 0

                def zero_run(stage_tile, slot_tile, n):
                    del stage_tile
                    pltpu.make_async_copy(zero_ref.at[_tile_rows(0, n, PACKED_ROWS)],
                                          xs_ref.at[_tile_rows(slot_tile, n, PACKED_ROWS)], sem.at[0]).start()

                for w in range(w_first, w_first + n_windows):
                    _for_each_run_piece(runs_ref, w, n_exp, bt_later, zero_run)
                for _ in range(n_windows * TOP_K * bt_later // block):
                    pltpu.make_async_copy(zero_ref, zero_ref, sem.at[0]).wait()

    def run_copy(buf):
        def make(stage_tile, slot_tile, n):
            return pltpu.make_async_copy(stage.at[buf, _tile_rows(stage_tile, n, PACKED_ROWS)],
                                         xs_ref.at[_tile_rows(slot_tile, n, PACKED_ROWS)], sem.at[buf])
        return make

    def place_all(b):
        def place(t, carry):
            tile = h_ref[_tile_rows(t, 1, PACKED_ROWS), :]
            for kk in range(TOP_K):
                row = pl.multiple_of(row_refs[kk][t], PACKED_ROWS)
                stage[b, pl.ds(row, PACKED_ROWS), :] = tile
            return carry

        lax.fori_loop(0, bt, place, 0, unroll=ROW_UNROLL)

    _per_buffer(slot, place_all)
    _for_each_run_piece(runs_ref, w0 + i, n_exp, bt, lambda a, b, n: run_copy(slot)(a, b, n).start())
    if side:
        side.run(w_ref, g_ref, l_ref, rest.pop(0))

    @pl.when(i > 0)
    def _():
        _wait_window(stage, 1 - slot, sem)

    @pl.when(i == pl.num_programs(0) - 1)
    def _():
        _wait_window(stage, slot, sem)


def _dispatch(rows, runs, pad, h2, xs, n_slots, bt, w0, n_exp, w1_side=None, later=None):
    t = h2.shape[0] // PACKED_ROWS
    n_steps = t // bt
    fill_pad = xs is None
    side = _SplitSide(*w1_side, n_steps) if w1_side else None
    side = side if side is not None and side.ok else None
    smem = pl.BlockSpec(memory_space=pltpu.SMEM)
    in_specs = _per_assignment_specs(bt) + [smem, smem, pl.BlockSpec((bt * PACKED_ROWS, LANES), lambda i: (i, 0))]
    args = list(rows) + [runs, pad, h2]
    out_specs = [pl.BlockSpec(memory_space=pl.ANY)]
    out_shape = [jax.ShapeDtypeStruct((n_slots * PACKED_ROWS, LANES), u32)]
    scratch = [pltpu.VMEM((2, TOP_K * bt * PACKED_ROWS, LANES), u32), pltpu.SemaphoreType.DMA((2,))]
    if fill_pad:
        scratch = [pltpu.VMEM((EXPERT_ROWS * PACKED_ROWS, LANES), u32)] + scratch
    else:
        in_specs.append(pl.BlockSpec(memory_space=pl.ANY))
        args.append(xs)
    if side:
        in_specs.append(side.in_spec)
        args.append(side.w1)
        out_specs += [side.out_spec] * 2
        out_shape += [side.out_shape] * 2
        scratch.append(side.scratch)
    outs = pl.pallas_call(
        functools.partial(_dispatch_kernel, bt=bt, w0=w0, n_exp=n_exp, fill_pad=fill_pad, side=side, later=later),
        grid=(n_steps,),
        in_specs=in_specs,
        out_specs=out_specs,
        out_shape=out_shape,
        scratch_shapes=scratch,
        input_output_aliases={} if fill_pad else {TOP_K + 3: 0},
        compiler_params=_params("arbitrary"),
        name="dispatch",
    )(*args)
    return outs[0], (tuple(outs[1:]) if side else None)


def _expert_kernel(be_ref, nu_ref, x_ref, w1g_ref, w1l_ref, b1g_ref, b1l_ref, w2f_ref, b2_ref, y_ref, w2_ref):
    i = pl.program_id(0)
    active = i < nu_ref[0]
    new_expert = jnp.logical_or(i == 0, be_ref[i] != be_ref[jnp.maximum(i - 1, 0)])

    @pl.when(jnp.logical_and(active, new_expert))
    def _():
        w2_ref[...] = w2f_ref[...].astype(bf16)

    @pl.when(active)
    def _():
        x = _load_packed_tiles(x_ref, x_ref.shape[0] // PACKED_ROWS)
        zg = jnp.dot(x, w1g_ref[...], preferred_element_type=f32) + b1g_ref[...]
        zl = jnp.dot(x, w1l_ref[...], preferred_element_type=f32) + b1l_ref[...]
        g = jnp.minimum(zg, SWIGLU_LIMIT)
        lin = jnp.clip(zl, -SWIGLU_LIMIT, SWIGLU_LIMIT)
        a = g * jax.nn.sigmoid(SWIGLU_ALPHA * g) * (lin + 1.0)
        y = jnp.dot(a.astype(bf16), w2_ref[...], preferred_element_type=f32) + b2_ref[...]
        _store_token_tiles(y_ref, y)

    @pl.when(i >= nu_ref[0])
    def _():
        y_ref[...] = jnp.zeros_like(y_ref)


def _experts(block_e, n_used, xs, w1g, w1l, b1g, b1l, w2_all, layer, b2):
    n_exp, d, f = w1g.shape
    bm = EXPERT_ROWS
    n_slots = xs.shape[0] // PACKED_ROWS
    blk = (bm * SUBLANES, LANES)
    xrow = lambda i, be, nu: (jnp.minimum(i, nu[0] - 1), 0)
    wsel = lambda i, be, nu: (be[i], 0, 0)
    grid_spec = pltpu.PrefetchScalarGridSpec(
        num_scalar_prefetch=2,
        grid=(n_slots // bm,),
        in_specs=[pl.BlockSpec((bm * PACKED_ROWS, LANES), xrow),
                  pl.BlockSpec((None, d, f), wsel), pl.BlockSpec((None, d, f), wsel),
                  pl.BlockSpec((None, 1, f), wsel), pl.BlockSpec((None, 1, f), wsel),
                  pl.BlockSpec((None, None, f, d), lambda i, be, nu: (layer, be[i], 0, 0)),
                  pl.BlockSpec((None, 1, d), wsel)],
        out_specs=pl.BlockSpec(blk, lambda i, be, nu: (i, 0)),
        scratch_shapes=[pltpu.VMEM((f, d), bf16)],
    )
    return pl.pallas_call(
        _expert_kernel,
        grid_spec=grid_spec,
        out_shape=jax.ShapeDtypeStruct((n_slots * SUBLANES, LANES), f32),
        compiler_params=_params("arbitrary"),
        name="experts",
    )(block_e, n_used, xs, w1g, w1l, b1g, b1l, w2_all, b2)


def _combine_kernel(*refs, bt, w0, n_exp, final, side):
    row_refs, gate_refs = refs[:TOP_K], refs[TOP_K:2 * TOP_K]
    (runs_ref, ys_ref, x_ref, g2_ref, fg_ref), rest = refs[2 * TOP_K:2 * TOP_K + 5], list(refs[2 * TOP_K + 5:])
    w_ref = rest.pop(0) if side else None
    o_ref = rest.pop(0)
    g_ref, l_ref = (rest.pop(0), rest.pop(0)) if side else (None, None)
    stage, acc_ref, sem = rest.pop(0), rest.pop(0), rest.pop(0)
    i = pl.program_id(0)
    slot = i % 2

    def run_copy(buf):
        def make(stage_tile, slot_tile, n):
            return pltpu.make_async_copy(ys_ref.at[_tile_rows(slot_tile, n)], stage.at[buf, _tile_rows(stage_tile, n)],
                                         sem.at[buf])
        return make

    def fetch(w, buf):
        _for_each_run_piece(runs_ref, w, n_exp, bt, lambda a, b, n: run_copy(buf)(a, b, n).start())

    @pl.when(i == 0)
    def _():
        fetch(w0, 0)

    @pl.when(i + 1 < pl.num_programs(0))
    def _():
        fetch(w0 + i + 1, 1 - slot)

    if side:
        side.run(w_ref, g_ref, l_ref, rest.pop(0))

    _wait_window(stage, slot, sem)

    def sum_all(b):
        def token(t, carry):
            acc = None
            for kk in range(TOP_K):
                row = pl.multiple_of(row_refs[kk][t], SUBLANES)
                term = gate_refs[kk][t] * stage[b, pl.ds(row, SUBLANES), :]
                acc = term if acc is None else acc + term
            acc_ref[_tile_rows(t), :] = acc
            return carry

        lax.fori_loop(0, bt, token, 0, unroll=ROW_UNROLL)

    _per_buffer(slot, sum_all)
    xo = x_ref[...] + g2_ref[...] * _load_token_tiles(acc_ref, bt)
    if final:
        xo = xo * lax.rsqrt(jnp.mean(xo * xo, axis=-1, keepdims=True) + EPS) * fg_ref[...]
    o_ref[...] = xo


def _combine(rows, gates, runs, ys, x_mid, n_seq, g2, final_g, final, bt, w0, n_exp, w1_side=None):
    t, d = x_mid.shape
    tps = n_seq // bt
    row = lambda i: (i, 0)
    n_steps = t // bt
    side = _SplitSide(*w1_side, n_steps) if w1_side else None
    side = side if side is not None and side.ok else None
    in_specs = _per_assignment_specs(bt) + _per_assignment_specs(bt) + [
        pl.BlockSpec(memory_space=pltpu.SMEM),
        pl.BlockSpec(memory_space=pl.ANY),
        pl.BlockSpec((bt, d), row),
        pl.BlockSpec((None, 1, d), lambda i: (i // tps, 0, 0)),
        _full(final_g)]
    args = list(rows) + list(gates) + [runs, ys, x_mid, g2, final_g]
    out_specs = [pl.BlockSpec((bt, d), row)]
    out_shape = [jax.ShapeDtypeStruct((t, d), f32)]
    scratch = [pltpu.VMEM((2, TOP_K * bt * SUBLANES, LANES), f32), pltpu.VMEM((bt * SUBLANES, LANES), f32),
               pltpu.SemaphoreType.DMA((2,))]
    if side:
        in_specs.append(side.in_spec)
        args.append(side.w1)
        out_specs += [side.out_spec] * 2
        out_shape += [side.out_shape] * 2
        scratch.append(side.scratch)
    outs = pl.pallas_call(
        functools.partial(_combine_kernel, bt=bt, w0=w0, n_exp=n_exp, final=final, side=side),
        grid=(n_steps,),
        in_specs=in_specs,
        out_specs=out_specs,
        out_shape=out_shape,
        scratch_shapes=scratch,
        compiler_params=_params("arbitrary"),
        name="combine",
    )(*args)
    return outs[0], (tuple(outs[1:]) if side else None)


def _rope_tables(n_tokens):
    rows = n_tokens // GRID_W
    row = jnp.repeat(jnp.arange(rows), GRID_W).astype(f32)
    col = jnp.tile(jnp.arange(GRID_W), rows).astype(f32)
    n_freq = HEAD_DIM // 4
    inv = ROPE_BASE ** (-jnp.arange(n_freq, dtype=f32) / n_freq)
    ang = jnp.concatenate([row[:, None] * inv, col[:, None] * inv], axis=-1)
    cos, sin = jnp.cos(ang), jnp.sin(ang)
    return jnp.tile(cos, (1, 4)), jnp.concatenate([-sin, sin, -sin, sin], axis=-1)


def kernel(x, c, ctx, c_ctx, norm1_g, norm2_g, ada_w, ada_b, w_in, attn_sink, sgu_ws, sgu_b, sgu_ln_g, sgu_ln_b,
           pool_w, pool_scale, conv_w, w_out, router_w, router_b, exp_w1, exp_b1, exp_w2, exp_b2, final_g):
    n_batch, n_seq, d = x.shape
    n_ctx = ctx.shape[1]
    depth = ada_w.shape[0]
    n_exp = router_w.shape[2]
    assert d == SUBLANES * LANES, "token rows are moved as single (8, 128) tiles"
    t_lat, t_ctx = n_batch * n_seq, n_batch * n_ctx
    bm = EXPERT_ROWS

    cvec = jnp.concatenate([c, c_ctx[None, :], jnp.zeros((SUBLANES - n_batch - 1, d), f32)], axis=0)
    mods = _ada(cvec, ada_w, ada_b)

    cos_l, sin_l = _rope_tables(n_seq)
    cos_c, sin_c = jnp.ones((n_ctx, LANES), f32), jnp.zeros((n_ctx, LANES), f32)

    xl = x.reshape(t_lat, d)
    xc = ctx.reshape(t_ctx, d)
    row2 = lambda a: a.reshape(1, -1)
    w1_split = {}

    for l in range(depth):
        last = l == depth - 1
        ml = mods[l, :n_batch].reshape(n_batch, 6, 1, d)
        mc = jnp.broadcast_to(mods[l, n_batch].reshape(1, 6, 1, d), (n_batch, 6, 1, d))
        sh1l, sc1l, g1l, sh2l, sc2l, g2l = (ml[:, i] for i in range(6))
        sh1c, sc1c, g1c, sh2c, sc2c, g2c = (mc[:, i] for i in range(6))

        w_in_bf = w_in[l].astype(bf16)
        wo_bf = w_out[l].astype(bf16)
        ws_bf = sgu_ws[l].astype(bf16)
        sgu_bias = jnp.repeat(sgu_b[l].T, SGU_W // SGU_HEADS, axis=1)
        pool_bd = jax.scipy.linalg.block_diag(*[pool_w[l, g] for g in range(pool_w.shape[1])]).astype(bf16)
        n1g, n2g = row2(norm1_g[l]), row2(norm2_g[l])
        lng, lnb, psc = row2(sgu_ln_g[l]), row2(sgu_ln_b[l]), row2(pool_scale[l])
        rw_t = router_w[l].T
        rb = router_b[l].reshape(n_exp, 1)
        sink = attn_sink[l]
        mix_w = (ws_bf, sgu_bias, pool_bd, psc, conv_w[l], wo_bf, rw_t, rb)

        qc, kvc, vnc, mfc = _inproj(xc, n_ctx, sh1c, sc1c, n1g, w_in_bf, cos_c, sin_c, lng, lnb)
        ql, kvl, vnl, mfl = _inproj(xl, n_seq, sh1l, sc1l, n1g, w_in_bf, cos_l, sin_l, lng, lnb)
        attn_l = _window_attn(ql, kvl, kvc, sink, n_batch, n_seq, n_ctx)
        bt_l, bt_c = min(MIXER_TOKENS, n_seq), min(MIXER_TOKENS, n_ctx)
        xmid_l, h2_l, s4_l, s8_l, gt_l, cnt = _mixer(attn_l, vnl, mfl, xl, n_seq, g1l, sh2l, sc2l, n2g, *mix_w)
        n_win_l = cnt.shape[0]
        t_all = t_lat
        if not last:
            attn_c = _ctx_attn(qc, kvc, sink, n_batch, n_ctx)
            xmid_c, h2_c, s4_c, s8_c, gt_c, cnt_c = _mixer(attn_c, vnc, mfc, xc, n_ctx, g1c, sh2c, sc2c, n2g, *mix_w)
            cnt = jnp.concatenate([cnt, cnt_c], axis=0)
            t_all = t_lat + t_ctx

        win_cnt = cnt[:, :, 0].astype(i32)
        counts = jnp.sum(win_cnt, axis=0)
        padded = (counts + bm - 1) // bm * bm
        pends = jnp.cumsum(padded)
        pstarts = pends - padded
        n_blocks = -(-(t_all * TOP_K) // bm) + n_exp
        starts = jnp.arange(n_blocks, dtype=i32) * bm
        block_e = jnp.minimum(jnp.sum((pends[None, :] <= starts[:, None]).astype(i32), axis=1), n_exp - 1)
        n_used = (pends[-1:] // bm).astype(i32)
        pad = jnp.concatenate([pstarts + counts, padded - counts, pends[-1:], n_blocks - pends[-1:] // bm]).astype(i32)
        run_stage = jnp.cumsum(win_cnt, axis=1) - win_cnt
        run_slot = pstarts[None, :] + jnp.cumsum(win_cnt, axis=0) - win_cnt
        runs = jnp.stack([run_stage, run_slot, win_cnt], axis=-1).reshape(-1)

        xs, split = _dispatch(s4_l, runs, pad, h2_l, None, n_blocks * bm, bt_l, 0, n_exp,
                              None if l in w1_split else (exp_w1, l),
                              None if last else (n_win_l, t_ctx // bt_c, bt_c))
        if l not in w1_split:
            w1_split[l] = split or _split_w1(exp_w1, l)
        if not last:
            xs, _ = _dispatch(s4_c, runs, pad, h2_c, xs, n_blocks * bm, bt_c, n_win_l, n_exp)

        w1g, w1l = w1_split[l]
        b1 = exp_b1[l]
        b1g, b1l = b1[:, None, 0::2], b1[:, None, 1::2]
        ys = _experts(block_e, n_used, xs, w1g, w1l, b1g, b1l, exp_w2, l, exp_b2[l][:, None, :])

        fg = row2(final_g)
        xl, split = _combine(s8_l, gt_l, runs, ys, xmid_l, n_seq, g2l, fg, last, bt_l, 0, n_exp,
                             None if last else (exp_w1, l + 1))
        if split:
            w1_split[l + 1] = split
        if not last:
            xc, _ = _combine(s8_c, gt_c, runs, ys, xmid_c, n_ctx, g2c, fg, False, bt_c, n_win_l, n_exp)

    return xl.reshape(n_batch, n_seq, d)
```

```python
import functools

import jax
import jax.numpy as jnp
from jax import lax
from jax.experimental import pallas as pl
from jax.experimental.pallas import tpu as pltpu

f32 = jnp.float32
bf16 = jnp.bfloat16
i32 = jnp.int32
u32 = jnp.uint32

GRID_W = 64
EPS = 1e-6
N_KV_HEADS = 2
GQA_GROUP = 4
HEAD_DIM = 64
WINDOW = 128
ROPE_BASE = 10000.0
ATT_Q_W = 512
SGU_HEADS = 4
SGU_W = 256
SGU_CHUNK = 128
POOL_CH = 256
TOP_K = 4
SWIGLU_LIMIT = 7.0
SWIGLU_ALPHA = 1.702
SQRT_HALF = 0.7071067811865476

LANES = 128
SUBLANES = 8
PACKED_ROWS = SUBLANES // 2
VMEM_LIMIT_BYTES = 56 * 1024 * 1024

INPROJ_TOKENS = 512
ATTN_TOKENS = 128
ATTN_BLOCKS_PER_STEP = 8
MIXER_TOKENS = 512
EXPERT_ROWS = 512
HALO = 8
ROW_UNROLL = 64


def _params(*sem):
    return pltpu.CompilerParams(dimension_semantics=sem, vmem_limit_bytes=VMEM_LIMIT_BYTES)


def _full(a):
    nd = a.ndim
    return pl.BlockSpec(a.shape, lambda *_: (0,) * nd)


def _store_token_tiles(ref, val):
    n = val.shape[0]
    for s in range(SUBLANES):
        ref[pl.ds(s, n, stride=SUBLANES), :] = val[:, s * LANES:(s + 1) * LANES]


def _load_token_tiles(ref, n):
    return jnp.concatenate([ref[pl.ds(s, n, stride=SUBLANES), :] for s in range(SUBLANES)], axis=1)


HIGH_HALF = 0xFFFF0000


def _store_packed_tiles(ref, val):
    n = val.shape[0]
    half = PACKED_ROWS * LANES
    for s in range(PACKED_ROWS):
        lo = val[:, s * LANES:(s + 1) * LANES].astype(bf16).astype(f32)
        hi = val[:, half + s * LANES:half + (s + 1) * LANES].astype(bf16).astype(f32)
        lo_bits = lax.shift_right_logical(lax.bitcast_convert_type(lo, u32), jnp.uint32(16))
        hi_bits = lax.bitcast_convert_type(hi, u32) & jnp.uint32(HIGH_HALF)
        ref[pl.ds(s, n, stride=PACKED_ROWS), :] = lo_bits | hi_bits


def _load_packed_tiles(ref, n):
    los, his = [], []
    for s in range(PACKED_ROWS):
        w = ref[pl.ds(s, n, stride=PACKED_ROWS), :]
        los.append(lax.bitcast_convert_type(lax.shift_left(w, jnp.uint32(16)), f32))
        his.append(lax.bitcast_convert_type(w & jnp.uint32(HIGH_HALF), f32))
    return jnp.concatenate(los + his, axis=1).astype(bf16)


def _gelu(x):
    return 0.5 * x * (1.0 + lax.erf(x * SQRT_HALF))


def _ada_kernel(c_ref, w_ref, b_ref, o_ref):
    c = c_ref[...]
    s = c * jax.nn.sigmoid(c)
    o_ref[...] = jnp.dot(s, w_ref[...], precision=lax.Precision.HIGHEST, preferred_element_type=f32) + b_ref[...]


def _ada(cvec, ada_w, ada_b):
    depth, d, n = ada_w.shape
    tn = 1536
    return pl.pallas_call(
        _ada_kernel,
        grid=(depth, n // tn),
        in_specs=[pl.BlockSpec(cvec.shape, lambda l, j: (0, 0)),
                  pl.BlockSpec((None, d, tn), lambda l, j: (l, 0, j)),
                  pl.BlockSpec((None, 1, tn), lambda l, j: (l, 0, j))],
        out_specs=pl.BlockSpec((None, cvec.shape[0], tn), lambda l, j: (l, 0, j)),
        out_shape=jax.ShapeDtypeStruct((depth, cvec.shape[0], n), f32),
        compiler_params=_params("arbitrary", "arbitrary"),
        name="ada_mod",
    )(cvec, ada_w, ada_b.reshape(depth, 1, n))


def _inproj_kernel(x_ref, sh_ref, sc_ref, g_ref, w_ref, cos_ref, sin_ref, lng_ref, lnb_ref,
                   q_ref, kv_ref, vn_ref, mixf_ref):
    x = x_ref[...]
    y = x * lax.rsqrt(jnp.mean(x * x, axis=-1, keepdims=True) + EPS) * g_ref[...]
    h = y * (1.0 + sc_ref[...]) + sh_ref[...]
    p = jnp.dot(h.astype(bf16), w_ref[...], preferred_element_type=f32)

    cos = cos_ref[...]
    sin = sin_ref[...]
    lane = lax.broadcasted_iota(i32, cos.shape, 1)
    first_half = (lane & (HEAD_DIM - 1)) < HEAD_DIM // 2

    def rope(t):
        partner = jnp.where(first_half, pltpu.roll(t, LANES - HEAD_DIM // 2, 1), pltpu.roll(t, HEAD_DIM // 2, 1))
        return t * cos + partner * sin

    scale = HEAD_DIM ** -0.5
    for m in range(ATT_Q_W // LANES):
        q_ref[:, m * LANES:(m + 1) * LANES] = (rope(p[:, m * LANES:(m + 1) * LANES]) * scale).astype(bf16)
    k = rope(p[:, 512:640])
    v = p[:, 640:768]
    kv_ref[:, 0:128] = k.astype(bf16)
    kv_ref[:, 128:256] = pltpu.roll(k, HEAD_DIM, 1).astype(bf16)
    kv_ref[:, 256:384] = v.astype(bf16)
    kv_ref[:, 384:512] = pltpu.roll(v, HEAD_DIM, 1).astype(bf16)

    u = _gelu(p[:, 768:1024])
    gv = _gelu(p[:, 1024:1280])
    mu = jnp.mean(gv, axis=-1, keepdims=True)
    var = jnp.mean(jnp.square(gv - mu), axis=-1, keepdims=True)
    vn_ref[...] = ((gv - mu) * lax.rsqrt(var + EPS) * lng_ref[...] + lnb_ref[...]).astype(bf16)

    mixf_ref[:, 0:256] = u
    mixf_ref[:, 256:512] = p[:, 1280:1536]
    mixf_ref[:, 512:768] = p[:, 1536:1792]
    mixf_ref[:, 768:1024] = p[:, 1792:2048] * p[:, 2048:2304]


def _inproj(x2, n_seq, shift, scale, g, w_bf, cos_t, sin_t, ln_g, ln_b):
    t, d = x2.shape
    bt = min(INPROJ_TOKENS, n_seq)
    tps = n_seq // bt
    ncol = w_bf.shape[1]
    row = lambda i: (i, 0)
    return pl.pallas_call(
        _inproj_kernel,
        grid=(t // bt,),
        in_specs=[pl.BlockSpec((bt, d), row),
                  pl.BlockSpec((None, 1, d), lambda i: (i // tps, 0, 0)),
                  pl.BlockSpec((None, 1, d), lambda i: (i // tps, 0, 0)),
                  _full(g),
                  pl.BlockSpec((d, ncol), lambda i: (0, 0)),
                  pl.BlockSpec((bt, LANES), lambda i: (i % tps, 0)),
                  pl.BlockSpec((bt, LANES), lambda i: (i % tps, 0)),
                  _full(ln_g), _full(ln_b)],
        out_specs=[pl.BlockSpec((bt, ATT_Q_W), row), pl.BlockSpec((bt, 512), row),
                   pl.BlockSpec((bt, SGU_W), row), pl.BlockSpec((bt, 1024), row)],
        out_shape=[jax.ShapeDtypeStruct((t, ATT_Q_W), bf16), jax.ShapeDtypeStruct((t, 512), bf16),
                   jax.ShapeDtypeStruct((t, SGU_W), bf16), jax.ShapeDtypeStruct((t, 1024), f32)],
        compiler_params=_params("arbitrary"),
        name="inproj",
    )(x2, shift, scale, g, w_bf, cos_t, sin_t, ln_g, ln_b)


def _attn_block(sink_ref, q, kv, bias, o_ref, row0):
    k_nat, k_swp, v_nat, v_swp = (kv[:, i * LANES:(i + 1) * LANES] for i in range(4))
    nq = q.shape[0]
    low = lax.broadcasted_iota(i32, (nq, LANES), 1) < HEAD_DIM
    top = lax.broadcasted_iota(i32, (2 * nq, 1), 0) < nq
    zero = jnp.zeros((nq, LANES), q.dtype)
    assert (N_KV_HEADS, GQA_GROUP, 2 * HEAD_DIM) == (2, 4, LANES), "two heads per 128-lane chunk, two chunks per kv head"
    for kvh in range(N_KV_HEADS):
        chunks = [q[:, (2 * kvh + i) * LANES:(2 * kvh + i + 1) * LANES] for i in range(2)]
        outs = []
        for half in range(2):
            keep = low if half == 0 else jnp.logical_not(low)
            qz = jnp.concatenate([jnp.where(keep, c, zero) for c in chunks], axis=0)
            kh = k_nat if kvh == half else k_swp
            vh = v_nat if kvh == half else v_swp
            s = lax.dot_general(qz, kh, (((1,), (1,)), ((), ())), preferred_element_type=f32)
            if bias is not None:
                s = s + bias
            h0 = GQA_GROUP * kvh + half
            sk = jnp.where(top, sink_ref[h0], sink_ref[h0 + 2])
            mx = jnp.maximum(jnp.max(s, axis=1, keepdims=True), sk)
            e = jnp.exp(s - mx)
            den = jnp.sum(e, axis=1, keepdims=True) + jnp.exp(sk - mx)
            outs.append(jnp.dot(e.astype(bf16), vh, preferred_element_type=f32) / den)
        for i in range(2):
            rows = slice(i * nq, (i + 1) * nq)
            m = 2 * kvh + i
            o_ref[row0:row0 + nq, m * LANES:(m + 1) * LANES] = jnp.where(low, outs[0][rows], outs[1][rows]).astype(bf16)


def _window_attn_kernel(sink_ref, q_ref, kvp_ref, kvm_ref, kvn_ref, kvx_ref, o_ref, *, nb, per_step):
    jj = pl.program_id(1)
    nq = ATTN_TOKENS
    nband = 3 * nq
    kv_all = jnp.concatenate([kvp_ref[...], kvm_ref[...], kvn_ref[...]], axis=0)
    kvx = kvx_ref[...]
    r = lax.broadcasted_iota(i32, (nq, nband), 0)
    c = lax.broadcasted_iota(i32, (nq, nband), 1)
    dlt = c - r
    in_window = (dlt >= 0) & (dlt <= 2 * WINDOW)
    ctx_zeros = jnp.zeros((nq, kvx.shape[0]), f32)
    for i in range(per_step):
        j = per_step * jj + i
        lo = jnp.where(j == 0, nq, 0)
        hi = jnp.where(j == nb - 1, 2 * nq, nband)
        valid = in_window & (c >= lo) & (c < hi)
        bias = jnp.concatenate([jnp.where(valid, 0.0, -jnp.inf).astype(f32), ctx_zeros], axis=1)
        kv = jnp.concatenate([kv_all[i * nq:i * nq + nband], kvx], axis=0)
        _attn_block(sink_ref, q_ref[i * nq:(i + 1) * nq, :], kv, jnp.concatenate([bias, bias], axis=0), o_ref, i * nq)


def _ctx_attn_kernel(sink_ref, q_ref, kvx_ref, o_ref):
    _attn_block(sink_ref, q_ref[...], kvx_ref[...], None, o_ref, 0)


def _window_attn(q, kv, kv_ctx, sink, n_batch, n_seq, n_ctx):
    t = q.shape[0]
    nb = n_seq // ATTN_TOKENS
    p = next(c for c in (ATTN_BLOCKS_PER_STEP, 2, 1) if nb % c == 0)
    steps = nb // p
    one, many = (ATTN_TOKENS, 512), (p * ATTN_TOKENS, 512)
    return pl.pallas_call(
        functools.partial(_window_attn_kernel, nb=nb, per_step=p),
        grid=(n_batch, steps),
        in_specs=[pl.BlockSpec(memory_space=pltpu.SMEM),
                  pl.BlockSpec(many, lambda b, j: (b * steps + j, 0)),
                  pl.BlockSpec(one, lambda b, j: (b * nb + jnp.maximum(p * j - 1, 0), 0)),
                  pl.BlockSpec(many, lambda b, j: (b * steps + j, 0)),
                  pl.BlockSpec(one, lambda b, j: (b * nb + jnp.minimum(p * j + p, nb - 1), 0)),
                  pl.BlockSpec((n_ctx, 512), lambda b, j: (b, 0))],
        out_specs=pl.BlockSpec(many, lambda b, j: (b * steps + j, 0)),
        out_shape=jax.ShapeDtypeStruct((t, ATT_Q_W), bf16),
        compiler_params=_params("arbitrary", "arbitrary"),
        name="window_attn",
    )(sink, q, kv, kv, kv, kv_ctx)


def _ctx_attn(q, kv_ctx, sink, n_batch, n_ctx):
    nb = n_ctx // ATTN_TOKENS
    blk = (ATTN_TOKENS, 512)
    return pl.pallas_call(
        _ctx_attn_kernel,
        grid=(n_batch, nb),
        in_specs=[pl.BlockSpec(memory_space=pltpu.SMEM),
                  pl.BlockSpec(blk, lambda b, j: (b * nb + j, 0)),
                  pl.BlockSpec((n_ctx, 512), lambda b, j: (b, 0))],
        out_specs=pl.BlockSpec(blk, lambda b, j: (b * nb + j, 0)),
        out_shape=jax.ShapeDtypeStruct(q.shape, bf16),
        compiler_params=_params("arbitrary", "arbitrary"),
        name="ctx_attn",
    )(sink, q, kv_ctx)


def _mixer_kernel(attn_ref, vn_ref, mf_ref, mfp_ref, mfn_ref, x_ref, g1_ref, sh2_ref, sc2_ref, n2g_ref,
                  ws_ref, sb_ref, pw_ref, ps_ref, cw_ref, wo_ref, rw_ref, rb_ref,
                  xmid_ref, h2_ref, *table_refs, n_seq, bt):
    s4_refs, s8_refs, gt_refs = (table_refs[j * TOP_K:(j + 1) * TOP_K] for j in range(3))
    cnt_ref = table_refs[3 * TOP_K]
    i = pl.program_id(0)
    tps = n_seq // bt
    si = i % tps
    first = si == 0
    last = si == tps - 1
    n_ext = bt + 2 * HALO

    mf = mf_ref[...]
    u = mf[:, 0:256]

    def extended(lo, hi):
        prev = jnp.where(first, 0.0, mfp_ref[:, lo:hi])
        nxt = jnp.where(last, 0.0, mfn_ref[:, lo:hi])
        return jnp.concatenate([prev, mf[:, lo:hi], nxt], axis=0)

    def shifted(a, s):
        return pltpu.roll(a, s % n_ext, 0)

    xe = extended(256, 512)
    a1 = shifted(xe, 1) + xe
    a2 = shifted(a1, 1) + shifted(a1, -1)
    a3 = shifted(a2, 2) + shifted(a2, -2)
    a4 = shifted(a3, 4) + shifted(a3, -4)
    lane = lax.broadcasted_iota(i32, (bt, POOL_CH), 1)
    grp = lane >> 6
    sl = slice(HALO, HALO + bt)
    wsum = jnp.where(grp == 0, a1[sl], jnp.where(grp == 1, a2[sl], jnp.where(grp == 2, a3[sl], a4[sl])))
    pos = lax.broadcasted_iota(i32, (bt, POOL_CH), 0) + si * bt
    halfw = jnp.left_shift(1, grp)
    cnt = jnp.minimum(pos + halfw, n_seq) - jnp.maximum(pos - halfw, 0)
    dpool = wsum / cnt.astype(f32) - mf[:, 256:512]
    yc = jnp.dot(dpool.astype(bf16), pw_ref[...], preferred_element_type=f32) * ps_ref[...]

    ye = extended(768, 1024)
    cw = cw_ref[...]
    z = shifted(ye, 1) * cw[0:1, :] + ye * cw[1:2, :] + shifted(ye, -1) * cw[2:3, :]
    yd = mf[:, 512:768] * z[sl]

    hgrp = lax.broadcasted_iota(i32, (SGU_CHUNK, SGU_W), 1) >> 6
    ybs = []
    for cidx in range(bt // SGU_CHUNK):
        rows = slice(cidx * SGU_CHUNK, (cidx + 1) * SGU_CHUNK)
        vn_c = vn_ref[rows, :]
        s = jnp.zeros((SGU_CHUNK, SGU_W), f32)
        for hh in range(SGU_HEADS):
            sh = jnp.dot(ws_ref[hh], vn_c, preferred_element_type=f32)
            s = jnp.where(hgrp == hh, sh, s)
        ybs.append(u[rows, :] * (s + sb_ref[...]))
    yb = jnp.concatenate(ybs, axis=0)

    mix = jnp.concatenate([attn_ref[...], yb.astype(bf16), yc.astype(bf16), yd.astype(bf16)], axis=1)
    mo = jnp.dot(mix, wo_ref[...], preferred_element_type=f32)
    xm = x_ref[...] + g1_ref[...] * mo
    xmid_ref[...] = xm

    y = xm * lax.rsqrt(jnp.mean(xm * xm, axis=-1, keepdims=True) + EPS) * n2g_ref[...]
    h2 = y * (1.0 + sc2_ref[...]) + sh2_ref[...]
    _store_packed_tiles(h2_ref, h2)

    def split(a):
        hi = a.astype(bf16)
        return hi, (a - hi.astype(f32)).astype(bf16)

    def dot_nt(a, b):
        return lax.dot_general(a, b, (((1,), (1,)), ((), ())), preferred_element_type=f32)

    (w_hi, w_lo), (h_hi, h_lo) = split(rw_ref[...]), split(h2)
    lt = dot_nt(w_hi, h_hi) + (dot_nt(w_hi, h_lo) + dot_nt(w_lo, h_hi)) + rb_ref[...]
    n_exp = lt.shape[0]
    eidx = lax.broadcasted_iota(i32, lt.shape, 0)
    work = lt
    idxs, vals = [], []
    for _ in range(TOP_K):
        m = jnp.max(work, axis=0, keepdims=True)
        idx = jnp.min(jnp.where(work == m, eidx, n_exp), axis=0, keepdims=True)
        idxs.append(idx)
        vals.append(m)
        work = jnp.where(eidx == idx, -jnp.inf, work)
    exps = [jnp.exp(v - vals[0]) for v in vals]
    den = exps[0] + exps[1] + exps[2] + exps[3]
    onehot = jnp.zeros(lt.shape, f32)
    for kk in range(TOP_K):
        gt_refs[kk][...] = exps[kk] / den
        onehot = onehot + (eidx == idxs[kk]).astype(f32)
    tri = (lax.broadcasted_iota(i32, (bt, bt), 0) < lax.broadcasted_iota(i32, (bt, bt), 1)).astype(bf16)
    before = jnp.dot(onehot.astype(bf16), tri, preferred_element_type=f32)
    cnt = jnp.sum(onehot, axis=1, keepdims=True)
    row = lax.broadcasted_iota(i32, (n_exp, LANES), 0)
    incl = jnp.broadcast_to(cnt, (n_exp, LANES))
    shift = 1
    while shift < n_exp:
        incl = incl + jnp.where(row >= shift, pltpu.roll(incl, shift, 0), 0.0)
        shift *= 2
    place = before + (incl[:, 0:1] - cnt)
    for kk in range(TOP_K):
        tile = jnp.sum(jnp.where(eidx == idxs[kk], place, 0.0), axis=0, keepdims=True).astype(i32)
        s4_refs[kk][...] = tile * PACKED_ROWS
        s8_refs[kk][...] = tile * SUBLANES
    cnt_ref[...] = cnt


def _mixer(attn, vn, mixf, x2, n_seq, g1, sh2, sc2, n2g, ws_bf, sgu_bias, pool_bd, pool_scale, conv_w,
           wo_bf, rw_t, rb):
    t, d = x2.shape
    bt = min(MIXER_TOKENS, n_seq)
    tps = n_seq // bt
    hb = bt // HALO
    n_halo = t // HALO
    n_exp = rw_t.shape[0]
    row = lambda i: (i, 0)
    per_batch = pl.BlockSpec((None, 1, d), lambda i: (i // tps, 0, 0))
    col = lambda i: (0, i)
    outs = pl.pallas_call(
        functools.partial(_mixer_kernel, n_seq=n_seq, bt=bt),
        grid=(t // bt,),
        in_specs=[pl.BlockSpec((bt, ATT_Q_W), row), pl.BlockSpec((bt, SGU_W), row), pl.BlockSpec((bt, 1024), row),
                  pl.BlockSpec((HALO, 1024), lambda i: (jnp.maximum(i * hb - 1, 0), 0)),
                  pl.BlockSpec((HALO, 1024), lambda i: (jnp.minimum((i + 1) * hb, n_halo - 1), 0)),
                  pl.BlockSpec((bt, d), row), per_batch, per_batch, per_batch, _full(n2g),
                  _full(ws_bf), _full(sgu_bias), _full(pool_bd), _full(pool_scale), _full(conv_w),
                  _full(wo_bf), _full(rw_t), _full(rb)],
        out_specs=[pl.BlockSpec((bt, d), row), pl.BlockSpec((bt * PACKED_ROWS, LANES), row)]
        + [pl.BlockSpec((1, bt), col)] * (3 * TOP_K) + [pl.BlockSpec((None, n_exp, 1), lambda i: (i, 0, 0))],
        out_shape=[jax.ShapeDtypeStruct((t, d), f32), jax.ShapeDtypeStruct((t * PACKED_ROWS, LANES), u32)]
        + [jax.ShapeDtypeStruct((1, t), i32)] * (2 * TOP_K) + [jax.ShapeDtypeStruct((1, t), f32)] * TOP_K
        + [jax.ShapeDtypeStruct((t // bt, n_exp, 1), f32)],
        compiler_params=_params("arbitrary"),
        name="mixer_router",
    )(attn, vn, mixf, mixf, mixf, x2, g1, sh2, sc2, n2g, ws_bf, sgu_bias, pool_bd, pool_scale, conv_w,
      wo_bf, rw_t, rb)
    tables = [o.reshape(t) for o in outs[2:2 + 3 * TOP_K]]
    return outs[0], outs[1], tables[:TOP_K], tables[TOP_K:2 * TOP_K], tables[2 * TOP_K:], outs[-1]


def _per_assignment_specs(bt):
    return [pl.BlockSpec((bt,), lambda i: (i,), memory_space=pltpu.SMEM)] * TOP_K


def _tile_rows(t, n=1, rows=SUBLANES):
    return pl.ds(pl.multiple_of(t * rows, rows), n * rows)


def _tile_copy(src, s, dst, d, sem, rows=SUBLANES):
    return pltpu.make_async_copy(src.at[_tile_rows(s, 1, rows)], dst.at[_tile_rows(d, 1, rows)], sem)


def _for_each_run_piece(runs_ref, w, n_exp, max_len, fn):
    bits = [1 << b for b in range(max_len.bit_length() - 1, -1, -1)]

    def per_expert(e, carry):
        base = (w * n_exp + e) * 3
        stage0, slot0, length = runs_ref[base], runs_ref[base + 1], runs_ref[base + 2]
        done = 0
        for bit in bits:
            take = length & bit

            @pl.when(take != 0)
            def _():
                fn(stage0 + done, slot0 + done, bit)

            done = done + take
        return carry

    lax.fori_loop(0, n_exp, per_expert, 0)


def _wait_window(stage, buf, sem):
    pltpu.make_async_copy(stage.at[buf], stage.at[buf], sem.at[buf]).wait()


def _per_buffer(slot, fn):
    for b in range(2):
        @pl.when(slot == b)
        def _():
            fn(b)


def _split_slabs(w_ref, g_ref, l_ref, scr):
    f = g_ref.shape[1]
    for s in range(w_ref.shape[0] // LANES):
        rows = slice(s * LANES, (s + 1) * LANES)
        scr[...] = w_ref[rows, :].T
        g_ref[rows, :] = scr[pl.ds(0, f, stride=2), :].T.astype(bf16)
        l_ref[rows, :] = scr[pl.ds(1, f, stride=2), :].T.astype(bf16)


def _split_w1(w1, layer):
    _, n_exp, d, f2 = w1.shape
    f = f2 // 2
    rows = 2 * LANES
    blk = pl.BlockSpec((None, rows, f), lambda e, j: (e, j, 0))
    return pl.pallas_call(
        _split_slabs,
        grid=(n_exp, d // rows),
        in_specs=[pl.BlockSpec((None, None, rows, f2), lambda e, j: (layer, e, j, 0))],
        out_specs=[blk, blk],
        out_shape=[jax.ShapeDtypeStruct((n_exp, d, f), bf16)] * 2,
        scratch_shapes=[pltpu.VMEM((f2, LANES), f32)],
        compiler_params=_params("arbitrary", "arbitrary"),
        name="split_w1",
    )(w1)


class _SplitSide:
    def __init__(self, w1, layer, n_steps):
        _, n_exp, d, f2 = w1.shape
        fits = [r for r in (LANES, 2 * LANES, 4 * LANES, 8 * LANES) if d % r == 0 and n_exp * (d // r) <= n_steps]
        self.ok = bool(fits)
        if not self.ok:
            return
        rows = fits[0]
        per = d // rows
        self.w1 = w1
        self.n_side = n_exp * per
        last = self.n_side - 1
        self.in_spec = pl.BlockSpec((None, None, rows, f2),
                                    lambda i: (layer, jnp.minimum(i, last) // per, jnp.minimum(i, last) % per, 0))
        self.out_spec = pl.BlockSpec((None, rows, f2 // 2),
                                     lambda i: (jnp.minimum(i, last) // per, jnp.minimum(i, last) % per, 0))
        self.out_shape = jax.ShapeDtypeStruct((n_exp, d, f2 // 2), bf16)
        self.scratch = pltpu.VMEM((f2, LANES), f32)

    def run(self, w_ref, g_ref, l_ref, scr):
        @pl.when(pl.program_id(0) < self.n_side)
        def _():
            _split_slabs(w_ref, g_ref, l_ref, scr)


def _dispatch_kernel(*refs, bt, w0, n_exp, fill_pad, side, later):
    row_refs, (runs_ref, pad_ref, h_ref), rest = refs[:TOP_K], refs[TOP_K:TOP_K + 3], list(refs[TOP_K + 3:])
    if not fill_pad:
        rest.pop(0)
    w_ref = rest.pop(0) if side else None
    xs_ref = rest.pop(0)
    g_ref, l_ref = (rest.pop(0), rest.pop(0)) if side else (None, None)
    zero_ref = rest.pop(0) if fill_pad else None
    stage, sem = rest.pop(0), rest.pop(0)
    i = pl.program_id(0)
    slot = i % 2

    if fill_pad:
        @pl.when(i == 0)
        def _():
            zero_ref[...] = jnp.zeros_like(zero_ref)

            def per_expert(e, carry):
                first, count = pad_ref[e], pad_ref[n_exp + e]

                def issue(r, c):
                    _tile_copy(zero_ref, 0, xs_ref, first + r, sem.at[0], PACKED_ROWS).start()
                    return c

                def drain(r, c):
                    _tile_copy(zero_ref, 0, xs_ref, 0, sem.at[0], PACKED_ROWS).wait()
                    return c

                lax.fori_loop(0, count, issue, 0)
                lax.fori_loop(0, count, drain, 0)
                return carry

            lax.fori_loop(0, n_exp, per_expert, 0)

            block = zero_ref.shape[0] // PACKED_ROWS
            tail_first, tail_blocks = pad_ref[2 * n_exp], pad_ref[2 * n_exp + 1]

            def zero_block(j):
                return pltpu.make_async_copy(zero_ref, xs_ref.at[_tile_rows(tail_first + j * block, block, PACKED_ROWS)],
                                             sem.at[0])

            def start_block(j, c):
                zero_block(j).start()
                return c

            def wait_block(j, c):
                zero_block(0).wait()
                return c

            lax.fori_loop(0, tail_blocks, start_block, 0)
            lax.fori_loop(0, tail_blocks, wait_block, 0)

            if later:
                w_first, n_windows, bt_later = later
                assert bt_later <= block and (TOP_K * bt_later) % block == 0

                def zero_run(stage_tile, slot_tile, n):
                    del stage_tile
                    pltpu.make_async_copy(zero_ref.at[_tile_rows(0, n, PACKED_ROWS)],
                                          xs_ref.at[_tile_rows(slot_tile, n, PACKED_ROWS)], sem.at[0]).start()

                for w in range(w_first, w_first + n_windows):
                    _for_each_run_piece(runs_ref, w, n_exp, bt_later, zero_run)
                for _ in range(n_windows * TOP_K * bt_later // block):
                    pltpu.make_async_copy(zero_ref, zero_ref, sem.at[0]).wait()

    def run_copy(buf):
        def make(stage_tile, slot_tile, n):
            return pltpu.make_async_copy(stage.at[buf, _tile_rows(stage_tile, n, PACKED_ROWS)],
                                         xs_ref.at[_tile_rows(slot_tile, n, PACKED_ROWS)], sem.at[buf])
        return make

    def place_all(b):
        def place(t, carry):
            tile = h_ref[_tile_rows(t, 1, PACKED_ROWS), :]
            for kk in range(TOP_K):
                row = pl.multiple_of(row_refs[kk][t], PACKED_ROWS)
                stage[b, pl.ds(row, PACKED_ROWS), :] = tile
            return carry

        lax.fori_loop(0, bt, place, 0, unroll=ROW_UNROLL)

    _per_buffer(slot, place_all)
    _for_each_run_piece(runs_ref, w0 + i, n_exp, bt, lambda a, b, n: run_copy(slot)(a, b, n).start())
    if side:
        side.run(w_ref, g_ref, l_ref, rest.pop(0))

    @pl.when(i > 0)
    def _():
        _wait_window(stage, 1 - slot, sem)

    @pl.when(i == pl.num_programs(0) - 1)
    def _():
        _wait_window(stage, slot, sem)


def _dispatch(rows, runs, pad, h2, xs, n_slots, bt, w0, n_exp, w1_side=None, later=None):
    t = h2.shape[0] // PACKED_ROWS
    n_steps = t // bt
    fill_pad = xs is None
    side = _SplitSide(*w1_side, n_steps) if w1_side else None
    side = side if side is not None and side.ok else None
    smem = pl.BlockSpec(memory_space=pltpu.SMEM)
    in_specs = _per_assignment_specs(bt) + [smem, smem, pl.BlockSpec((bt * PACKED_ROWS, LANES), lambda i: (i, 0))]
    args = list(rows) + [runs, pad, h2]
    out_specs = [pl.BlockSpec(memory_space=pl.ANY)]
    out_shape = [jax.ShapeDtypeStruct((n_slots * PACKED_ROWS, LANES), u32)]
    scratch = [pltpu.VMEM((2, TOP_K * bt * PACKED_ROWS, LANES), u32), pltpu.SemaphoreType.DMA((2,))]
    if fill_pad:
        scratch = [pltpu.VMEM((EXPERT_ROWS * PACKED_ROWS, LANES), u32)] + scratch
    else:
        in_specs.append(pl.BlockSpec(memory_space=pl.ANY))
        args.append(xs)
    if side:
        in_specs.append(side.in_spec)
        args.append(side.w1)
        out_specs += [side.out_spec] * 2
        out_shape += [side.out_shape] * 2
        scratch.append(side.scratch)
    outs = pl.pallas_call(
        functools.partial(_dispatch_kernel, bt=bt, w0=w0, n_exp=n_exp, fill_pad=fill_pad, side=side, later=later),
        grid=(n_steps,),
        in_specs=in_specs,
        out_specs=out_specs,
        out_shape=out_shape,
        scratch_shapes=scratch,
        input_output_aliases={} if fill_pad else {TOP_K + 3: 0},
        compiler_params=_params("arbitrary"),
        name="dispatch",
    )(*args)
    return outs[0], (tuple(outs[1:]) if side else None)


def _expert_kernel(be_ref, nu_ref, x_ref, w1g_ref, w1l_ref, b1g_ref, b1l_ref, w2f_ref, b2_ref, y_ref, w2_ref):
    i = pl.program_id(0)
    active = i < nu_ref[0]
    new_expert = jnp.logical_or(i == 0, be_ref[i] != be_ref[jnp.maximum(i - 1, 0)])

    @pl.when(jnp.logical_and(active, new_expert))
    def _():
        w2_ref[...] = w2f_ref[...].astype(bf16)

    @pl.when(active)
    def _():
        x = _load_packed_tiles(x_ref, x_ref.shape[0] // PACKED_ROWS)
        zg = jnp.dot(x, w1g_ref[...], preferred_element_type=f32) + b1g_ref[...]
        zl = jnp.dot(x, w1l_ref[...], preferred_element_type=f32) + b1l_ref[...]
        g = jnp.minimum(zg, SWIGLU_LIMIT)
        lin = jnp.clip(zl, -SWIGLU_LIMIT, SWIGLU_LIMIT)
        a = g * jax.nn.sigmoid(SWIGLU_ALPHA * g) * (lin + 1.0)
        y = jnp.dot(a.astype(bf16), w2_ref[...], preferred_element_type=f32) + b2_ref[...]
        _store_token_tiles(y_ref, y)

    @pl.when(i >= nu_ref[0])
    def _():
        y_ref[...] = jnp.zeros_like(y_ref)


def _experts(block_e, n_used, xs, w1g, w1l, b1g, b1l, w2_all, layer, b2):
    n_exp, d, f = w1g.shape
    bm = EXPERT_ROWS
    n_slots = xs.shape[0] // PACKED_ROWS
    blk = (bm * SUBLANES, LANES)
    xrow = lambda i, be, nu: (jnp.minimum(i, nu[0] - 1), 0)
    wsel = lambda i, be, nu: (be[i], 0, 0)
    grid_spec = pltpu.PrefetchScalarGridSpec(
        num_scalar_prefetch=2,
        grid=(n_slots // bm,),
        in_specs=[pl.BlockSpec((bm * PACKED_ROWS, LANES), xrow),
                  pl.BlockSpec((None, d, f), wsel), pl.BlockSpec((None, d, f), wsel),
                  pl.BlockSpec((None, 1, f), wsel), pl.BlockSpec((None, 1, f), wsel),
                  pl.BlockSpec((None, None, f, d), lambda i, be, nu: (layer, be[i], 0, 0)),
                  pl.BlockSpec((None, 1, d), wsel)],
        out_specs=pl.BlockSpec(blk, lambda i, be, nu: (i, 0)),
        scratch_shapes=[pltpu.VMEM((f, d), bf16)],
    )
    return pl.pallas_call(
        _expert_kernel,
        grid_spec=grid_spec,
        out_shape=jax.ShapeDtypeStruct((n_slots * SUBLANES, LANES), f32),
        compiler_params=_params("arbitrary"),
        name="experts",
    )(block_e, n_used, xs, w1g, w1l, b1g, b1l, w2_all, b2)


def _combine_kernel(*refs, bt, w0, n_exp, final, side):
    row_refs, gate_refs = refs[:TOP_K], refs[TOP_K:2 * TOP_K]
    (runs_ref, ys_ref, x_ref, g2_ref, fg_ref), rest = refs[2 * TOP_K:2 * TOP_K + 5], list(refs[2 * TOP_K + 5:])
    w_ref = rest.pop(0) if side else None
    o_ref = rest.pop(0)
    g_ref, l_ref = (rest.pop(0), rest.pop(0)) if side else (None, None)
    stage, acc_ref, sem = rest.pop(0), rest.pop(0), rest.pop(0)
    i = pl.program_id(0)
    slot = i % 2

    def run_copy(buf):
        def make(stage_tile, slot_tile, n):
            return pltpu.make_async_copy(ys_ref.at[_tile_rows(slot_tile, n)], stage.at[buf, _tile_rows(stage_tile, n)],
                                         sem.at[buf])
        return make

    def fetch(w, buf):
        _for_each_run_piece(runs_ref, w, n_exp, bt, lambda a, b, n: run_copy(buf)(a, b, n).start())

    @pl.when(i == 0)
    def _():
        fetch(w0, 0)

    @pl.when(i + 1 < pl.num_programs(0))
    def _():
        fetch(w0 + i + 1, 1 - slot)

    if side:
        side.run(w_ref, g_ref, l_ref, rest.pop(0))

    _wait_window(stage, slot, sem)

    def sum_all(b):
        def token(t, carry):
            acc = None
            for kk in range(TOP_K):
                row = pl.multiple_of(row_refs[kk][t], SUBLANES)
                term = gate_refs[kk][t] * stage[b, pl.ds(row, SUBLANES), :]
                acc = term if acc is None else acc + term
            acc_ref[_tile_rows(t), :] = acc
            return carry

        lax.fori_loop(0, bt, token, 0, unroll=ROW_UNROLL)

    _per_buffer(slot, sum_all)
    xo = x_ref[...] + g2_ref[...] * _load_token_tiles(acc_ref, bt)
    if final:
        xo = xo * lax.rsqrt(jnp.mean(xo * xo, axis=-1, keepdims=True) + EPS) * fg_ref[...]
    o_ref[...] = xo


def _combine(rows, gates, runs, ys, x_mid, n_seq, g2, final_g, final, bt, w0, n_exp, w1_side=None):
    t, d = x_mid.shape
    tps = n_seq // bt
    row = lambda i: (i, 0)
    n_steps = t // bt
    side = _SplitSide(*w1_side, n_steps) if w1_side else None
    side = side if side is not None and side.ok else None
    in_specs = _per_assignment_specs(bt) + _per_assignment_specs(bt) + [
        pl.BlockSpec(memory_space=pltpu.SMEM),
        pl.BlockSpec(memory_space=pl.ANY),
        pl.BlockSpec((bt, d), row),
        pl.BlockSpec((None, 1, d), lambda i: (i // tps, 0, 0)),
        _full(final_g)]
    args = list(rows) + list(gates) + [runs, ys, x_mid, g2, final_g]
    out_specs = [pl.BlockSpec((bt, d), row)]
    out_shape = [jax.ShapeDtypeStruct((t, d), f32)]
    scratch = [pltpu.VMEM((2, TOP_K * bt * SUBLANES, LANES), f32), pltpu.VMEM((bt * SUBLANES, LANES), f32),
               pltpu.SemaphoreType.DMA((2,))]
    if side:
        in_specs.append(side.in_spec)
        args.append(side.w1)
        out_specs += [side.out_spec] * 2
        out_shape += [side.out_shape] * 2
        scratch.append(side.scratch)
    outs = pl.pallas_call(
        functools.partial(_combine_kernel, bt=bt, w0=w0, n_exp=n_exp, final=final, side=side),
        grid=(n_steps,),
        in_specs=in_specs,
        out_specs=out_specs,
        out_shape=out_shape,
        scratch_shapes=scratch,
        compiler_params=_params("arbitrary"),
        name="combine",
    )(*args)
    return outs[0], (tuple(outs[1:]) if side else None)


def _rope_tables(n_tokens):
    rows = n_tokens // GRID_W
    row = jnp.repeat(jnp.arange(rows), GRID_W).astype(f32)
    col = jnp.tile(jnp.arange(GRID_W), rows).astype(f32)
    n_freq = HEAD_DIM // 4
    inv = ROPE_BASE ** (-jnp.arange(n_freq, dtype=f32) / n_freq)
    ang = jnp.concatenate([row[:, None] * inv, col[:, None] * inv], axis=-1)
    cos, sin = jnp.cos(ang), jnp.sin(ang)
    return jnp.tile(cos, (1, 4)), jnp.concatenate([-sin, sin, -sin, sin], axis=-1)


def kernel(x, c, ctx, c_ctx, norm1_g, norm2_g, ada_w, ada_b, w_in, attn_sink, sgu_ws, sgu_b, sgu_ln_g, sgu_ln_b,
           pool_w, pool_scale, conv_w, w_out, router_w, router_b, exp_w1, exp_b1, exp_w2, exp_b2, final_g):
    n_batch, n_seq, d = x.shape
    n_ctx = ctx.shape[1]
    depth = ada_w.shape[0]
    n_exp = router_w.shape[2]
    assert d == SUBLANES * LANES, "token rows are moved as single (8, 128) tiles"
    t_lat, t_ctx = n_batch * n_seq, n_batch * n_ctx
    bm = EXPERT_ROWS

    cvec = jnp.concatenate([c, c_ctx[None, :], jnp.zeros((SUBLANES - n_batch - 1, d), f32)], axis=0)
    mods = _ada(cvec, ada_w, ada_b)

    cos_l, sin_l = _rope_tables(n_seq)
    cos_c, sin_c = jnp.ones((n_ctx, LANES), f32), jnp.zeros((n_ctx, LANES), f32)

    xl = x.reshape(t_lat, d)
    xc = ctx.reshape(t_ctx, d)
    row2 = lambda a: a.reshape(1, -1)
    w1_split = {}

    for l in range(depth):
        last = l == depth - 1
        ml = mods[l, :n_batch].reshape(n_batch, 6, 1, d)
        mc = jnp.broadcast_to(mods[l, n_batch].reshape(1, 6, 1, d), (n_batch, 6, 1, d))
        sh1l, sc1l, g1l, sh2l, sc2l, g2l = (ml[:, i] for i in range(6))
        sh1c, sc1c, g1c, sh2c, sc2c, g2c = (mc[:, i] for i in range(6))

        w_in_bf = w_in[l].astype(bf16)
        wo_bf = w_out[l].astype(bf16)
        ws_bf = sgu_ws[l].astype(bf16)
        sgu_bias = jnp.repeat(sgu_b[l].T, SGU_W // SGU_HEADS, axis=1)
        pool_bd = jax.scipy.linalg.block_diag(*[pool_w[l, g] for g in range(pool_w.shape[1])]).astype(bf16)
        n1g, n2g = row2(norm1_g[l]), row2(norm2_g[l])
        lng, lnb, psc = row2(sgu_ln_g[l]), row2(sgu_ln_b[l]), row2(pool_scale[l])
        rw_t = router_w[l].T
        rb = router_b[l].reshape(n_exp, 1)
        sink = attn_sink[l]
        mix_w = (ws_bf, sgu_bias, pool_bd, psc, conv_w[l], wo_bf, rw_t, rb)

        qc, kvc, vnc, mfc = _inproj(xc, n_ctx, sh1c, sc1c, n1g, w_in_bf, cos_c, sin_c, lng, lnb)
        ql, kvl, vnl, mfl = _inproj(xl, n_seq, sh1l, sc1l, n1g, w_in_bf, cos_l, sin_l, lng, lnb)
        attn_l = _window_attn(ql, kvl, kvc, sink, n_batch, n_seq, n_ctx)
        bt_l, bt_c = min(MIXER_TOKENS, n_seq), min(MIXER_TOKENS, n_ctx)
        xmid_l, h2_l, s4_l, s8_l, gt_l, cnt = _mixer(attn_l, vnl, mfl, xl, n_seq, g1l, sh2l, sc2l, n2g, *mix_w)
        n_win_l = cnt.shape[0]
        t_all = t_lat
        if not last:
            attn_c = _ctx_attn(qc, kvc, sink, n_batch, n_ctx)
            xmid_c, h2_c, s4_c, s8_c, gt_c, cnt_c = _mixer(attn_c, vnc, mfc, xc, n_ctx, g1c, sh2c, sc2c, n2g, *mix_w)
            cnt = jnp.concatenate([cnt, cnt_c], axis=0)
            t_all = t_lat + t_ctx

        win_cnt = cnt[:, :, 0].astype(i32)
        counts = jnp.sum(win_cnt, axis=0)
        padded = (counts + bm - 1) // bm * bm
        pends = jnp.cumsum(padded)
        pstarts = pends - padded
        n_blocks = -(-(t_all * TOP_K) // bm) + n_exp
        starts = jnp.arange(n_blocks, dtype=i32) * bm
        block_e = jnp.minimum(jnp.sum((pends[None, :] <= starts[:, None]).astype(i32), axis=1), n_exp - 1)
        n_used = (pends[-1:] // bm).astype(i32)
        pad = jnp.concatenate([pstarts + counts, padded - counts, pends[-1:], n_blocks - pends[-1:] // bm]).astype(i32)
        run_stage = jnp.cumsum(win_cnt, axis=1) - win_cnt
        run_slot = pstarts[None, :] + jnp.cumsum(win_cnt, axis=0) - win_cnt
        runs = jnp.stack([run_stage, run_slot, win_cnt], axis=-1).reshape(-1)

        xs, split = _dispatch(s4_l, runs, pad, h2_l, None, n_blocks * bm, bt_l, 0, n_exp,
                              None if l in w1_split else (exp_w1, l),
                              None if last else (n_win_l, t_ctx // bt_c, bt_c))
        if l not in w1_split:
            w1_split[l] = split or _split_w1(exp_w1, l)
        if not last:
            xs, _ = _dispatch(s4_c, runs, pad, h2_c, xs, n_blocks * bm, bt_c, n_win_l, n_exp)

        w1g, w1l = w1_split[l]
        b1 = exp_b1[l]
        b1g, b1l = b1[:, None, 0::2], b1[:, None, 1::2]
        ys = _experts(block_e, n_used, xs, w1g, w1l, b1g, b1l, exp_w2, l, exp_b2[l][:, None, :])

        fg = row2(final_g)
        xl, split = _combine(s8_l, gt_l, runs, ys, xmid_l, n_seq, g2l, fg, last, bt_l, 0, n_exp,
                             None if last else (exp_w1, l + 1))
        if split:
            w1_split[l + 1] = split
        if not last:
            xc, _ = _combine(s8_c, gt_c, runs, ys, xmid_c, n_ctx, g2c, fg, False, bt_c, n_win_l, n_exp)

    return xl.reshape(n_batch, n_seq, d)
```

```python
import functools

import jax
import jax.numpy as jnp
from jax import lax
from jax.experimental import pallas as pl
from jax.experimental.pallas import tpu as pltpu

f32 = jnp.float32
bf16 = jnp.bfloat16
i32 = jnp.int32
u32 = jnp.uint32

GRID_W = 64
EPS = 1e-6
N_KV_HEADS = 2
GQA_GROUP = 4
HEAD_DIM = 64
WINDOW = 128
ROPE_BASE = 10000.0
ATT_Q_W = 512
SGU_HEADS = 4
SGU_W = 256
SGU_CHUNK = 128
POOL_CH = 256
TOP_K = 4
SWIGLU_LIMIT = 7.0
SWIGLU_ALPHA = 1.702
SQRT_HALF = 0.7071067811865476

LANES = 128
SUBLANES = 8
PACKED_ROWS = SUBLANES // 2
VMEM_LIMIT_BYTES = 56 * 1024 * 1024

INPROJ_TOKENS = 512
ATTN_TOKENS = 128
ATTN_BLOCKS_PER_STEP = 8
MIXER_TOKENS = 512
EXPERT_ROWS = 512
HALO = 8
ROW_UNROLL = 64


def _params(*sem):
    return pltpu.CompilerParams(dimension_semantics=sem, vmem_limit_bytes=VMEM_LIMIT_BYTES)


def _full(a):
    nd = a.ndim
    return pl.BlockSpec(a.shape, lambda *_: (0,) * nd)


def _store_token_tiles(ref, val):
    n = val.shape[0]
    for s in range(SUBLANES):
        ref[pl.ds(s, n, stride=SUBLANES), :] = val[:, s * LANES:(s + 1) * LANES]


def _load_token_tiles(ref, n):
    return jnp.concatenate([ref[pl.ds(s, n, stride=SUBLANES), :] for s in range(SUBLANES)], axis=1)


HIGH_HALF = 0xFFFF0000


def _store_packed_tiles(ref, val):
    n = val.shape[0]
    half = PACKED_ROWS * LANES
    for s in range(PACKED_ROWS):
        lo = val[:, s * LANES:(s + 1) * LANES].astype(bf16).astype(f32)
        hi = val[:, half + s * LANES:half + (s + 1) * LANES].astype(bf16).astype(f32)
        lo_bits = lax.shift_right_logical(lax.bitcast_convert_type(lo, u32), jnp.uint32(16))
        hi_bits = lax.bitcast_convert_type(hi, u32) & jnp.uint32(HIGH_HALF)
        ref[pl.ds(s, n, stride=PACKED_ROWS), :] = lo_bits | hi_bits


def _load_packed_tiles(ref, n):
    los, his = [], []
    for s in range(PACKED_ROWS):
        w = ref[pl.ds(s, n, stride=PACKED_ROWS), :]
        los.append(lax.bitcast_convert_type(lax.shift_left(w, jnp.uint32(16)), f32))
        his.append(lax.bitcast_convert_type(w & jnp.uint32(HIGH_HALF), f32))
    return jnp.concatenate(los + his, axis=1).astype(bf16)


def _gelu(x):
    return 0.5 * x * (1.0 + lax.erf(x * SQRT_HALF))


def _ada_kernel(c_ref, w_ref, b_ref, o_ref):
    c = c_ref[...]
    s = c * jax.nn.sigmoid(c)
    o_ref[...] = jnp.dot(s, w_ref[...], precision=lax.Precision.HIGHEST, preferred_element_type=f32) + b_ref[...]


def _ada(cvec, ada_w, ada_b):
    depth, d, n = ada_w.shape
    tn = 1536
    return pl.pallas_call(
        _ada_kernel,
        grid=(depth, n // tn),
        in_specs=[pl.BlockSpec(cvec.shape, lambda l, j: (0, 0)),
                  pl.BlockSpec((None, d, tn), lambda l, j: (l, 0, j)),
                  pl.BlockSpec((None, 1, tn), lambda l, j: (l, 0, j))],
        out_specs=pl.BlockSpec((None, cvec.shape[0], tn), lambda l, j: (l, 0, j)),
        out_shape=jax.ShapeDtypeStruct((depth, cvec.shape[0], n), f32),
        compiler_params=_params("arbitrary", "arbitrary"),
        name="ada_mod",
    )(cvec, ada_w, ada_b.reshape(depth, 1, n))


def _inproj_kernel(x_ref, sh_ref, sc_ref, g_ref, w_ref, cos_ref, sin_ref, lng_ref, lnb_ref,
                   q_ref, kv_ref, vn_ref, mixf_ref):
    x = x_ref[...]
    y = x * lax.rsqrt(jnp.mean(x * x, axis=-1, keepdims=True) + EPS) * g_ref[...]
    h = y * (1.0 + sc_ref[...]) + sh_ref[...]
    p = jnp.dot(h.astype(bf16), w_ref[...], preferred_element_type=f32)

    cos = cos_ref[...]
    sin = sin_ref[...]
    lane = lax.broadcasted_iota(i32, cos.shape, 1)
    first_half = (lane & (HEAD_DIM - 1)) < HEAD_DIM // 2

    def rope(t):
        partner = jnp.where(first_half, pltpu.roll(t, LANES - HEAD_DIM // 2, 1), pltpu.roll(t, HEAD_DIM // 2, 1))
        return t * cos + partner * sin

    scale = HEAD_DIM ** -0.5
    for m in range(ATT_Q_W // LANES):
        q_ref[:, m * LANES:(m + 1) * LANES] = (rope(p[:, m * LANES:(m + 1) * LANES]) * scale).astype(bf16)
    k = rope(p[:, 512:640])
    v = p[:, 640:768]
    kv_ref[:, 0:128] = k.astype(bf16)
    kv_ref[:, 128:256] = pltpu.roll(k, HEAD_DIM, 1).astype(bf16)
    kv_ref[:, 256:384] = v.astype(bf16)
    kv_ref[:, 384:512] = pltpu.roll(v, HEAD_DIM, 1).astype(bf16)

    u = _gelu(p[:, 768:1024])
    gv = _gelu(p[:, 1024:1280])
    mu = jnp.mean(gv, axis=-1, keepdims=True)
    var = jnp.mean(jnp.square(gv - mu), axis=-1, keepdims=True)
    vn_ref[...] = ((gv - mu) * lax.rsqrt(var + EPS) * lng_ref[...] + lnb_ref[...]).astype(bf16)

    mixf_ref[:, 0:256] = u
    mixf_ref[:, 256:512] = p[:, 1280:1536]
    mixf_ref[:, 512:768] = p[:, 1536:1792]
    mixf_ref[:, 768:1024] = p[:, 1792:2048] * p[:, 2048:2304]


def _inproj(x2, n_seq, shift, scale, g, w_bf, cos_t, sin_t, ln_g, ln_b):
    t, d = x2.shape
    bt = min(INPROJ_TOKENS, n_seq)
    tps = n_seq // bt
    ncol = w_bf.shape[1]
    row = lambda i: (i, 0)
    return pl.pallas_call(
        _inproj_kernel,
        grid=(t // bt,),
        in_specs=[pl.BlockSpec((bt, d), row),
                  pl.BlockSpec((None, 1, d), lambda i: (i // tps, 0, 0)),
                  pl.BlockSpec((None, 1, d), lambda i: (i // tps, 0, 0)),
                  _full(g),
                  pl.BlockSpec((d, ncol), lambda i: (0, 0)),
                  pl.BlockSpec((bt, LANES), lambda i: (i % tps, 0)),
                  pl.BlockSpec((bt, LANES), lambda i: (i % tps, 0)),
                  _full(ln_g), _full(ln_b)],
        out_specs=[pl.BlockSpec((bt, ATT_Q_W), row), pl.BlockSpec((bt, 512), row),
                   pl.BlockSpec((bt, SGU_W), row), pl.BlockSpec((bt, 1024), row)],
        out_shape=[jax.ShapeDtypeStruct((t, ATT_Q_W), bf16), jax.ShapeDtypeStruct((t, 512), bf16),
                   jax.ShapeDtypeStruct((t, SGU_W), bf16), jax.ShapeDtypeStruct((t, 1024), f32)],
        compiler_params=_params("arbitrary"),
        name="inproj",
    )(x2, shift, scale, g, w_bf, cos_t, sin_t, ln_g, ln_b)


def _attn_block(sink_ref, q, kv, bias, o_ref, row0):
    k_nat, k_swp, v_nat, v_swp = (kv[:, i * LANES:(i + 1) * LANES] for i in range(4))
    nq = q.shape[0]
    low = lax.broadcasted_iota(i32, (nq, LANES), 1) < HEAD_DIM
    top = lax.broadcasted_iota(i32, (2 * nq, 1), 0) < nq
    zero = jnp.zeros((nq, LANES), q.dtype)
    assert (N_KV_HEADS, GQA_GROUP, 2 * HEAD_DIM) == (2, 4, LANES), "two heads per 128-lane chunk, two chunks per kv head"
    for kvh in range(N_KV_HEADS):
        chunks = [q[:, (2 * kvh + i) * LANES:(2 * kvh + i + 1) * LANES] for i in range(2)]
        outs = []
        for half in range(2):
            keep = low if half == 0 else jnp.logical_not(low)
            qz = jnp.concatenate([jnp.where(keep, c, zero) for c in chunks], axis=0)
            kh = k_nat if kvh == half else k_swp
            vh = v_nat if kvh == half else v_swp
            s = lax.dot_general(qz, kh, (((1,), (1,)), ((), ())), preferred_element_type=f32)
            if bias is not None:
                s = s + bias
            h0 = GQA_GROUP * kvh + half
            sk = jnp.where(top, sink_ref[h0], sink_ref[h0 + 2])
            mx = jnp.maximum(jnp.max(s, axis=1, keepdims=True), sk)
            e = jnp.exp(s - mx)
            den = jnp.sum(e, axis=1, keepdims=True) + jnp.exp(sk - mx)
            outs.append(jnp.dot(e.astype(bf16), vh, preferred_element_type=f32) / den)
        for i in range(2):
            rows = slice(i * nq, (i + 1) * nq)
            m = 2 * kvh + i
            o_ref[row0:row0 + nq, m * LANES:(m + 1) * LANES] = jnp.where(low, outs[0][rows], outs[1][rows]).astype(bf16)


def _window_attn_kernel(sink_ref, q_ref, kvp_ref, kvm_ref, kvn_ref, kvx_ref, o_ref, *, nb, per_step):
    jj = pl.program_id(1)
    nq = ATTN_TOKENS
    nband = 3 * nq
    kv_all = jnp.concatenate([kvp_ref[...], kvm_ref[...], kvn_ref[...]], axis=0)
    kvx = kvx_ref[...]
    r = lax.broadcasted_iota(i32, (nq, nband), 0)
    c = lax.broadcasted_iota(i32, (nq, nband), 1)
    dlt = c - r
    in_window = (dlt >= 0) & (dlt <= 2 * WINDOW)
    ctx_zeros = jnp.zeros((nq, kvx.shape[0]), f32)
    for i in range(per_step):
        j = per_step * jj + i
        lo = jnp.where(j == 0, nq, 0)
        hi = jnp.where(j == nb - 1, 2 * nq, nband)
        valid = in_window & (c >= lo) & (c < hi)
        bias = jnp.concatenate([jnp.where(valid, 0.0, -jnp.inf).astype(f32), ctx_zeros], axis=1)
        kv = jnp.concatenate([kv_all[i * nq:i * nq + nband], kvx], axis=0)
        _attn_block(sink_ref, q_ref[i * nq:(i + 1) * nq, :], kv, jnp.concatenate([bias, bias], axis=0), o_ref, i * nq)


def _ctx_attn_kernel(sink_ref, q_ref, kvx_ref, o_ref):
    _attn_block(sink_ref, q_ref[...], kvx_ref[...], None, o_ref, 0)


def _window_attn(q, kv, kv_ctx, sink, n_batch, n_seq, n_ctx):
    t = q.shape[0]
    nb = n_seq // ATTN_TOKENS
    p = next(c for c in (ATTN_BLOCKS_PER_STEP, 2, 1) if nb % c == 0)
    steps = nb // p
    one, many = (ATTN_TOKENS, 512), (p * ATTN_TOKENS, 512)
    return pl.pallas_call(
        functools.partial(_window_attn_kernel, nb=nb, per_step=p),
        grid=(n_batch, steps),
        in_specs=[pl.BlockSpec(memory_space=pltpu.SMEM),
                  pl.BlockSpec(many, lambda b, j: (b * steps + j, 0)),
                  pl.BlockSpec(one, lambda b, j: (b * nb + jnp.maximum(p * j - 1, 0), 0)),
                  pl.BlockSpec(many, lambda b, j: (b * steps + j, 0)),
                  pl.BlockSpec(one, lambda b, j: (b * nb + jnp.minimum(p * j + p, nb - 1), 0)),
                  pl.BlockSpec((n_ctx, 512), lambda b, j: (b, 0))],
        out_specs=pl.BlockSpec(many, lambda b, j: (b * steps + j, 0)),
        out_shape=jax.ShapeDtypeStruct((t, ATT_Q_W), bf16),
        compiler_params=_params("arbitrary", "arbitrary"),
        name="window_attn",
    )(sink, q, kv, kv, kv, kv_ctx)


def _ctx_attn(q, kv_ctx, sink, n_batch, n_ctx):
    nb = n_ctx // ATTN_TOKENS
    blk = (ATTN_TOKENS, 512)
    return pl.pallas_call(
        _ctx_attn_kernel,
        grid=(n_batch, nb),
        in_specs=[pl.BlockSpec(memory_space=pltpu.SMEM),
                  pl.BlockSpec(blk, lambda b, j: (b * nb + j, 0)),
                  pl.BlockSpec((n_ctx, 512), lambda b, j: (b, 0))],
        out_specs=pl.BlockSpec(blk, lambda b, j: (b * nb + j, 0)),
        out_shape=jax.ShapeDtypeStruct(q.shape, bf16),
        compiler_params=_params("arbitrary", "arbitrary"),
        name="ctx_attn",
    )(sink, q, kv_ctx)


def _mixer_kernel(attn_ref, vn_ref, mf_ref, mfp_ref, mfn_ref, x_ref, g1_ref, sh2_ref, sc2_ref, n2g_ref,
                  ws_ref, sb_ref, pw_ref, ps_ref, cw_ref, wo_ref, rw_ref, rb_ref,
                  xmid_ref, h2_ref, *table_refs, n_seq, bt):
    s4_refs, s8_refs, gt_refs = (table_refs[j * TOP_K:(j + 1) * TOP_K] for j in range(3))
    cnt_ref = table_refs[3 * TOP_K]
    i = pl.program_id(0)
    tps = n_seq // bt
    si = i % tps
    first = si == 0
    last = si == tps - 1
    n_ext = bt + 2 * HALO

    mf = mf_ref[...]
    u = mf[:, 0:256]

    def extended(lo, hi):
        prev = jnp.where(first, 0.0, mfp_ref[:, lo:hi])
        nxt = jnp.where(last, 0.0, mfn_ref[:, lo:hi])
        return jnp.concatenate([prev, mf[:, lo:hi], nxt], axis=0)

    def shifted(a, s):
        return pltpu.roll(a, s % n_ext, 0)

    xe = extended(256, 512)
    a1 = shifted(xe, 1) + xe
    a2 = shifted(a1, 1) + shifted(a1, -1)
    a3 = shifted(a2, 2) + shifted(a2, -2)
    a4 = shifted(a3, 4) + shifted(a3, -4)
    lane = lax.broadcasted_iota(i32, (bt, POOL_CH), 1)
    grp = lane >> 6
    sl = slice(HALO, HALO + bt)
    wsum = jnp.where(grp == 0, a1[sl], jnp.where(grp == 1, a2[sl], jnp.where(grp == 2, a3[sl], a4[sl])))
    pos = lax.broadcasted_iota(i32, (bt, POOL_CH), 0) + si * bt
    halfw = jnp.left_shift(1, grp)
    cnt = jnp.minimum(pos + halfw, n_seq) - jnp.maximum(pos - halfw, 0)
    dpool = wsum / cnt.astype(f32) - mf[:, 256:512]
    yc = jnp.dot(dpool.astype(bf16), pw_ref[...], preferred_element_type=f32) * ps_ref[...]

    ye = extended(768, 1024)
    cw = cw_ref[...]
    z = shifted(ye, 1) * cw[0:1, :] + ye * cw[1:2, :] + shifted(ye, -1) * cw[2:3, :]
    yd = mf[:, 512:768] * z[sl]

    hgrp = lax.broadcasted_iota(i32, (SGU_CHUNK, SGU_W), 1) >> 6
    ybs = []
    for cidx in range(bt // SGU_CHUNK):
        rows = slice(cidx * SGU_CHUNK, (cidx + 1) * SGU_CHUNK)
        vn_c = vn_ref[rows, :]
        s = jnp.zeros((SGU_CHUNK, SGU_W), f32)
        for hh in range(SGU_HEADS):
            sh = jnp.dot(ws_ref[hh], vn_c, preferred_element_type=f32)
            s = jnp.where(hgrp == hh, sh, s)
        ybs.append(u[rows, :] * (s + sb_ref[...]))
    yb = jnp.concatenate(ybs, axis=0)

    mix = jnp.concatenate([attn_ref[...], yb.astype(bf16), yc.astype(bf16), yd.astype(bf16)], axis=1)
    mo = jnp.dot(mix, wo_ref[...], preferred_element_type=f32)
    xm = x_ref[...] + g1_ref[...] * mo
    xmid_ref[...] = xm

    y = xm * lax.rsqrt(jnp.mean(xm * xm, axis=-1, keepdims=True) + EPS) * n2g_ref[...]
    h2 = y * (1.0 + sc2_ref[...]) + sh2_ref[...]
    _store_packed_tiles(h2_ref, h2)

    def split(a):
        hi = a.astype(bf16)
        return hi, (a - hi.astype(f32)).astype(bf16)

    def dot_nt(a, b):
        return lax.dot_general(a, b, (((1,), (1,)), ((), ())), preferred_element_type=f32)

    (w_hi, w_lo), (h_hi, h_lo) = split(rw_ref[...]), split(h2)
    lt = dot_nt(w_hi, h_hi) + (dot_nt(w_hi, h_lo) + dot_nt(w_lo, h_hi)) + rb_ref[...]
    n_exp = lt.shape[0]
    eidx = lax.broadcasted_iota(i32, lt.shape, 0)
    work = lt
    idxs, vals = [], []
    for _ in range(TOP_K):
        m = jnp.max(work, axis=0, keepdims=True)
        idx = jnp.min(jnp.where(work == m, eidx, n_exp), axis=0, keepdims=True)
        idxs.append(idx)
        vals.append(m)
        work = jnp.where(eidx == idx, -jnp.inf, work)
    exps = [jnp.exp(v - vals[0]) for v in vals]
    den = exps[0] + exps[1] + exps[2] + exps[3]
    onehot = jnp.zeros(lt.shape, f32)
    for kk in range(TOP_K):
        gt_refs[kk][...] = exps[kk] / den
        onehot = onehot + (eidx == idxs[kk]).astype(f32)
    tri = (lax.broadcasted_iota(i32, (bt, bt), 0) < lax.broadcasted_iota(i32, (bt, bt), 1)).astype(bf16)
    before = jnp.dot(onehot.astype(bf16), tri, preferred_element_type=f32)
    cnt = jnp.sum(onehot, axis=1, keepdims=True)
    row = lax.broadcasted_iota(i32, (n_exp, LANES), 0)
    incl = jnp.broadcast_to(cnt, (n_exp, LANES))
    shift = 1
    while shift < n_exp:
        incl = incl + jnp.where(row >= shift, pltpu.roll(incl, shift, 0), 0.0)
        shift *= 2
    place = before + (incl[:, 0:1] - cnt)
    for kk in range(TOP_K):
        tile = jnp.sum(jnp.where(eidx == idxs[kk], place, 0.0), axis=0, keepdims=True).astype(i32)
        s4_refs[kk][...] = tile * PACKED_ROWS
        s8_refs[kk][...] = tile * SUBLANES
    cnt_ref[...] = cnt


def _mixer(attn, vn, mixf, x2, n_seq, g1, sh2, sc2, n2g, ws_bf, sgu_bias, pool_bd, pool_scale, conv_w,
           wo_bf, rw_t, rb):
    t, d = x2.shape
    bt = min(MIXER_TOKENS, n_seq)
    tps = n_seq // bt
    hb = bt // HALO
    n_halo = t // HALO
    n_exp = rw_t.shape[0]
    row = lambda i: (i, 0)
    per_batch = pl.BlockSpec((None, 1, d), lambda i: (i // tps, 0, 0))
    col = lambda i: (0, i)
    outs = pl.pallas_call(
        functools.partial(_mixer_kernel, n_seq=n_seq, bt=bt),
        grid=(t // bt,),
        in_specs=[pl.BlockSpec((bt, ATT_Q_W), row), pl.BlockSpec((bt, SGU_W), row), pl.BlockSpec((bt, 1024), row),
                  pl.BlockSpec((HALO, 1024), lambda i: (jnp.maximum(i * hb - 1, 0), 0)),
                  pl.BlockSpec((HALO, 1024), lambda i: (jnp.minimum((i + 1) * hb, n_halo - 1), 0)),
                  pl.BlockSpec((bt, d), row), per_batch, per_batch, per_batch, _full(n2g),
                  _full(ws_bf), _full(sgu_bias), _full(pool_bd), _full(pool_scale), _full(conv_w),
                  _full(wo_bf), _full(rw_t), _full(rb)],
        out_specs=[pl.BlockSpec((bt, d), row), pl.BlockSpec((bt * PACKED_ROWS, LANES), row)]
        + [pl.BlockSpec((1, bt), col)] * (3 * TOP_K) + [pl.BlockSpec((None, n_exp, 1), lambda i: (i, 0, 0))],
        out_shape=[jax.ShapeDtypeStruct((t, d), f32), jax.ShapeDtypeStruct((t * PACKED_ROWS, LANES), u32)]
        + [jax.ShapeDtypeStruct((1, t), i32)] * (2 * TOP_K) + [jax.ShapeDtypeStruct((1, t), f32)] * TOP_K
        + [jax.ShapeDtypeStruct((t // bt, n_exp, 1), f32)],
        compiler_params=_params("arbitrary"),
        name="mixer_router",
    )(attn, vn, mixf, mixf, mixf, x2, g1, sh2, sc2, n2g, ws_bf, sgu_bias, pool_bd, pool_scale, conv_w,
      wo_bf, rw_t, rb)
    tables = [o.reshape(t) for o in outs[2:2 + 3 * TOP_K]]
    return outs[0], outs[1], tables[:TOP_K], tables[TOP_K:2 * TOP_K], tables[2 * TOP_K:], outs[-1]


def _per_assignment_specs(bt):
    return [pl.BlockSpec((bt,), lambda i: (i,), memory_space=pltpu.SMEM)] * TOP_K


def _tile_rows(t, n=1, rows=SUBLANES):
    return pl.ds(pl.multiple_of(t * rows, rows), n * rows)


def _tile_copy(src, s, dst, d, sem, rows=SUBLANES):
    return pltpu.make_async_copy(src.at[_tile_rows(s, 1, rows)], dst.at[_tile_rows(d, 1, rows)], sem)


def _for_each_run_piece(runs_ref, w, n_exp, max_len, fn):
    bits = [1 << b for b in range(max_len.bit_length() - 1, -1, -1)]

    def per_expert(e, carry):
        base = (w * n_exp + e) * 3
        stage0, slot0, length = runs_ref[base], runs_ref[base + 1], runs_ref[base + 2]
        done = 0
        for bit in bits:
            take = length & bit

            @pl.when(take != 0)
            def _():
                fn(stage0 + done, slot0 + done, bit)

            done = done + take
        return carry

    lax.fori_loop(0, n_exp, per_expert, 0)


def _wait_window(stage, buf, sem):
    pltpu.make_async_copy(stage.at[buf], stage.at[buf], sem.at[buf]).wait()


def _per_buffer(slot, fn):
    for b in range(2):
        @pl.when(slot == b)
        def _():
            fn(b)


def _split_slabs(w_ref, g_ref, l_ref, scr):
    f = g_ref.shape[1]
    for s in range(w_ref.shape[0] // LANES):
        rows = slice(s * LANES, (s + 1) * LANES)
        scr[...] = w_ref[rows, :].T
        g_ref[rows, :] = scr[pl.ds(0, f, stride=2), :].T.astype(bf16)
        l_ref[rows, :] = scr[pl.ds(1, f, stride=2), :].T.astype(bf16)


def _split_w1(w1, layer):
    _, n_exp, d, f2 = w1.shape
    f = f2 // 2
    rows = 2 * LANES
    blk = pl.BlockSpec((None, rows, f), lambda e, j: (e, j, 0))
    return pl.pallas_call(
        _split_slabs,
        grid=(n_exp, d // rows),
        in_specs=[pl.BlockSpec((None, None, rows, f2), lambda e, j: (layer, e, j, 0))],
        out_specs=[blk, blk],
        out_shape=[jax.ShapeDtypeStruct((n_exp, d, f), bf16)] * 2,
        scratch_shapes=[pltpu.VMEM((f2, LANES), f32)],
        compiler_params=_params("arbitrary", "arbitrary"),
        name="split_w1",
    )(w1)


class _SplitSide:
    def __init__(self, w1, layer, n_steps):
        _, n_exp, d, f2 = w1.shape
        fits = [r for r in (LANES, 2 * LANES, 4 * LANES, 8 * LANES) if d % r == 0 and n_exp * (d // r) <= n_steps]
        self.ok = bool(fits)
        if not self.ok:
            return
        rows = fits[0]
        per = d // rows
        self.w1 = w1
        self.n_side = n_exp * per
        last = self.n_side - 1
        self.in_spec = pl.BlockSpec((None, None, rows, f2),
                                    lambda i: (layer, jnp.minimum(i, last) // per, jnp.minimum(i, last) % per, 0))
        self.out_spec = pl.BlockSpec((None, rows, f2 // 2),
                                     lambda i: (jnp.minimum(i, last) // per, jnp.minimum(i, last) % per, 0))
        self.out_shape = jax.ShapeDtypeStruct((n_exp, d, f2 // 2), bf16)
        self.scratch = pltpu.VMEM((f2, LANES), f32)

    def run(self, w_ref, g_ref, l_ref, scr):
        @pl.when(pl.program_id(0) < self.n_side)
        def _():
            _split_slabs(w_ref, g_ref, l_ref, scr)


def _dispatch_kernel(*refs, bt, w0, n_exp, fill_pad, side, later):
    row_refs, (runs_ref, pad_ref, h_ref), rest = refs[:TOP_K], refs[TOP_K:TOP_K + 3], list(refs[TOP_K + 3:])
    if not fill_pad:
        rest.pop(0)
    w_ref = rest.pop(0) if side else None
    xs_ref = rest.pop(0)
    g_ref, l_ref = (rest.pop(0), rest.pop(0)) if side else (None, None)
    zero_ref = rest.pop(0) if fill_pad else None
    stage, sem = rest.pop(0), rest.pop(0)
    i = pl.program_id(0)
    slot = i % 2

    if fill_pad:
        @pl.when(i == 0)
        def _():
            zero_ref[...] = jnp.zeros_like(zero_ref)

            def per_expert(e, carry):
                first, count = pad_ref[e], pad_ref[n_exp + e]

                def issue(r, c):
                    _tile_copy(zero_ref, 0, xs_ref, first + r, sem.at[0], PACKED_ROWS).start()
                    return c

                def drain(r, c):
                    _tile_copy(zero_ref, 0, xs_ref, 0, sem.at[0], PACKED_ROWS).wait()
                    return c

                lax.fori_loop(0, count, issue, 0)
                lax.fori_loop(0, count, drain, 0)
                return carry

            lax.fori_loop(0, n_exp, per_expert, 0)

            block = zero_ref.shape[0] // PACKED_ROWS
            tail_first, tail_blocks = pad_ref[2 * n_exp], pad_ref[2 * n_exp + 1]

            def zero_block(j):
                return pltpu.make_async_copy(zero_ref, xs_ref.at[_tile_rows(tail_first + j * block, block, PACKED_ROWS)],
                                             sem.at[0])

            def start_block(j, c):
                zero_block(j).start()
                return c

            def wait_block(j, c):
                zero_block(0).wait()
                return c

            lax.fori_loop(0, tail_blocks, start_block, 0)
            lax.fori_loop(0, tail_blocks, wait_block, 0)

            if later:
                w_first, n_windows, bt_later = later
                assert bt_later <= block and (TOP_K * bt_later) % block == 0

                def zero_run(stage_tile, slot_tile, n):
                    del stage_tile
                    pltpu.make_async_copy(zero_ref.at[_tile_rows(0, n, PACKED_ROWS)],
                                          xs_ref.at[_tile_rows(slot_tile, n, PACKED_ROWS)], sem.at[0]).start()

                for w in range(w_first, w_first + n_windows):
                    _for_each_run_piece(runs_ref, w, n_exp, bt_later, zero_run)
                for _ in range(n_windows * TOP_K * bt_later // block):
                    pltpu.make_async_copy(zero_ref, zero_ref, sem.at[0]).wait()

    def run_copy(buf):
        def make(stage_tile, slot_tile, n):
            return pltpu.make_async_copy(stage.at[buf, _tile_rows(stage_tile, n, PACKED_ROWS)],
                                         xs_ref.at[_tile_rows(slot_tile, n, PACKED_ROWS)], sem.at[buf])
        return make

    def place_all(b):
        def place(t, carry):
            tile = h_ref[_tile_rows(t, 1, PACKED_ROWS), :]
            for kk in range(TOP_K):
                row = pl.multiple_of(row_refs[kk][t], PACKED_ROWS)
                stage[b, pl.ds(row, PACKED_ROWS), :] = tile
            return carry

        lax.fori_loop(0, bt, place, 0, unroll=ROW_UNROLL)

    _per_buffer(slot, place_all)
    _for_each_run_piece(runs_ref, w0 + i, n_exp, bt,
                        lambda a, b, n: run_copy(slot)(a, b, n).start(priority=n.bit_length() % 2))
    if side:
        side.run(w_ref, g_ref, l_ref, rest.pop(0))

    @pl.when(i > 0)
    def _():
        _wait_window(stage, 1 - slot, sem)

    @pl.when(i == pl.num_programs(0) - 1)
    def _():
        _wait_window(stage, slot, sem)


def _dispatch(rows, runs, pad, h2, xs, n_slots, bt, w0, n_exp, w1_side=None, later=None):
    t = h2.shape[0] // PACKED_ROWS
    n_steps = t // bt
    fill_pad = xs is None
    side = _SplitSide(*w1_side, n_steps) if w1_side else None
    side = side if side is not None and side.ok else None
    smem = pl.BlockSpec(memory_space=pltpu.SMEM)
    in_specs = _per_assignment_specs(bt) + [smem, smem, pl.BlockSpec((bt * PACKED_ROWS, LANES), lambda i: (i, 0))]
    args = list(rows) + [runs, pad, h2]
    out_specs = [pl.BlockSpec(memory_space=pl.ANY)]
    out_shape = [jax.ShapeDtypeStruct((n_slots * PACKED_ROWS, LANES), u32)]
    scratch = [pltpu.VMEM((2, TOP_K * bt * PACKED_ROWS, LANES), u32), pltpu.SemaphoreType.DMA((2,))]
    if fill_pad:
        scratch = [pltpu.VMEM((EXPERT_ROWS * PACKED_ROWS, LANES), u32)] + scratch
    else:
        in_specs.append(pl.BlockSpec(memory_space=pl.ANY))
        args.append(xs)
    if side:
        in_specs.append(side.in_spec)
        args.append(side.w1)
        out_specs += [side.out_spec] * 2
        out_shape += [side.out_shape] * 2
        scratch.append(side.scratch)
    outs = pl.pallas_call(
        functools.partial(_dispatch_kernel, bt=bt, w0=w0, n_exp=n_exp, fill_pad=fill_pad, side=side, later=later),
        grid=(n_steps,),
        in_specs=in_specs,
        out_specs=out_specs,
        out_shape=out_shape,
        scratch_shapes=scratch,
        input_output_aliases={} if fill_pad else {TOP_K + 3: 0},
        compiler_params=_params("arbitrary"),
        name="dispatch",
    )(*args)
    return outs[0], (tuple(outs[1:]) if side else None)


def _expert_kernel(be_ref, nu_ref, x_ref, w1g_ref, w1l_ref, b1g_ref, b1l_ref, w2f_ref, b2_ref, y_ref, w2_ref):
    i = pl.program_id(0)
    active = i < nu_ref[0]
    new_expert = jnp.logical_or(i == 0, be_ref[i] != be_ref[jnp.maximum(i - 1, 0)])

    @pl.when(jnp.logical_and(active, new_expert))
    def _():
        w2_ref[...] = w2f_ref[...].astype(bf16)

    @pl.when(active)
    def _():
        x = _load_packed_tiles(x_ref, x_ref.shape[0] // PACKED_ROWS)
        zg = jnp.dot(x, w1g_ref[...], preferred_element_type=f32) + b1g_ref[...]
        zl = jnp.dot(x, w1l_ref[...], preferred_element_type=f32) + b1l_ref[...]
        g = jnp.minimum(zg, SWIGLU_LIMIT)
        lin = jnp.clip(zl, -SWIGLU_LIMIT, SWIGLU_LIMIT)
        a = g * jax.nn.sigmoid(SWIGLU_ALPHA * g) * (lin + 1.0)
        y = jnp.dot(a.astype(bf16), w2_ref[...], preferred_element_type=f32) + b2_ref[...]
        _store_token_tiles(y_ref, y)

    @pl.when(i >= nu_ref[0])
    def _():
        y_ref[...] = jnp.zeros_like(y_ref)


def _experts(block_e, n_used, xs, w1g, w1l, b1g, b1l, w2_all, layer, b2):
    n_exp, d, f = w1g.shape
    bm = EXPERT_ROWS
    n_slots = xs.shape[0] // PACKED_ROWS
    blk = (bm * SUBLANES, LANES)
    xrow = lambda i, be, nu: (jnp.minimum(i, nu[0] - 1), 0)
    wsel = lambda i, be, nu: (be[i], 0, 0)
    grid_spec = pltpu.PrefetchScalarGridSpec(
        num_scalar_prefetch=2,
        grid=(n_slots // bm,),
        in_specs=[pl.BlockSpec((bm * PACKED_ROWS, LANES), xrow),
                  pl.BlockSpec((None, d, f), wsel), pl.BlockSpec((None, d, f), wsel),
                  pl.BlockSpec((None, 1, f), wsel), pl.BlockSpec((None, 1, f), wsel),
                  pl.BlockSpec((None, None, f, d), lambda i, be, nu: (layer, be[i], 0, 0)),
                  pl.BlockSpec((None, 1, d), wsel)],
        out_specs=pl.BlockSpec(blk, lambda i, be, nu: (i, 0)),
        scratch_shapes=[pltpu.VMEM((f, d), bf16)],
    )
    return pl.pallas_call(
        _expert_kernel,
        grid_spec=grid_spec,
        out_shape=jax.ShapeDtypeStruct((n_slots * SUBLANES, LANES), f32),
        compiler_params=_params("arbitrary"),
        name="experts",
    )(block_e, n_used, xs, w1g, w1l, b1g, b1l, w2_all, b2)


def _combine_kernel(*refs, bt, w0, n_exp, final, side):
    row_refs, gate_refs = refs[:TOP_K], refs[TOP_K:2 * TOP_K]
    (runs_ref, ys_ref, x_ref, g2_ref, fg_ref), rest = refs[2 * TOP_K:2 * TOP_K + 5], list(refs[2 * TOP_K + 5:])
    w_ref = rest.pop(0) if side else None
    o_ref = rest.pop(0)
    g_ref, l_ref = (rest.pop(0), rest.pop(0)) if side else (None, None)
    stage, acc_ref, sem = rest.pop(0), rest.pop(0), rest.pop(0)
    i = pl.program_id(0)
    slot = i % 2

    def run_copy(buf):
        def make(stage_tile, slot_tile, n):
            return pltpu.make_async_copy(ys_ref.at[_tile_rows(slot_tile, n)], stage.at[buf, _tile_rows(stage_tile, n)],
                                         sem.at[buf])
        return make

    def fetch(w, buf):
        _for_each_run_piece(runs_ref, w, n_exp, bt,
                            lambda a, b, n: run_copy(buf)(a, b, n).start(priority=n.bit_length() % 2))

    @pl.when(i == 0)
    def _():
        fetch(w0, 0)

    @pl.when(i + 1 < pl.num_programs(0))
    def _():
        fetch(w0 + i + 1, 1 - slot)

    if side:
        side.run(w_ref, g_ref, l_ref, rest.pop(0))

    _wait_window(stage, slot, sem)

    def sum_all(b):
        def token(t, carry):
            acc = None
            for kk in range(TOP_K):
                row = pl.multiple_of(row_refs[kk][t], SUBLANES)
                term = gate_refs[kk][t] * stage[b, pl.ds(row, SUBLANES), :]
                acc = term if acc is None else acc + term
            acc_ref[_tile_rows(t), :] = acc
            return carry

        lax.fori_loop(0, bt, token, 0, unroll=ROW_UNROLL)

    _per_buffer(slot, sum_all)
    xo = x_ref[...] + g2_ref[...] * _load_token_tiles(acc_ref, bt)
    if final:
        xo = xo * lax.rsqrt(jnp.mean(xo * xo, axis=-1, keepdims=True) + EPS) * fg_ref[...]
    o_ref[...] = xo


def _combine(rows, gates, runs, ys, x_mid, n_seq, g2, final_g, final, bt, w0, n_exp, w1_side=None):
    t, d = x_mid.shape
    tps = n_seq // bt
    row = lambda i: (i, 0)
    n_steps = t // bt
    side = _SplitSide(*w1_side, n_steps) if w1_side else None
    side = side if side is not None and side.ok else None
    in_specs = _per_assignment_specs(bt) + _per_assignment_specs(bt) + [
        pl.BlockSpec(memory_space=pltpu.SMEM),
        pl.BlockSpec(memory_space=pl.ANY),
        pl.BlockSpec((bt, d), row),
        pl.BlockSpec((None, 1, d), lambda i: (i // tps, 0, 0)),
        _full(final_g)]
    args = list(rows) + list(gates) + [runs, ys, x_mid, g2, final_g]
    out_specs = [pl.BlockSpec((bt, d), row)]
    out_shape = [jax.ShapeDtypeStruct((t, d), f32)]
    scratch = [pltpu.VMEM((2, TOP_K * bt * SUBLANES, LANES), f32), pltpu.VMEM((bt * SUBLANES, LANES), f32),
               pltpu.SemaphoreType.DMA((2,))]
    if side:
        in_specs.append(side.in_spec)
        args.append(side.w1)
        out_specs += [side.out_spec] * 2
        out_shape += [side.out_shape] * 2
        scratch.append(side.scratch)
    outs = pl.pallas_call(
        functools.partial(_combine_kernel, bt=bt, w0=w0, n_exp=n_exp, final=final, side=side),
        grid=(n_steps,),
        in_specs=in_specs,
        out_specs=out_specs,
        out_shape=out_shape,
        scratch_shapes=scratch,
        compiler_params=_params("arbitrary"),
        name="combine",
    )(*args)
    return outs[0], (tuple(outs[1:]) if side else None)


def _rope_tables(n_tokens):
    rows = n_tokens // GRID_W
    row = jnp.repeat(jnp.arange(rows), GRID_W).astype(f32)
    col = jnp.tile(jnp.arange(GRID_W), rows).astype(f32)
    n_freq = HEAD_DIM // 4
    inv = ROPE_BASE ** (-jnp.arange(n_freq, dtype=f32) / n_freq)
    ang = jnp.concatenate([row[:, None] * inv, col[:, None] * inv], axis=-1)
    cos, sin = jnp.cos(ang), jnp.sin(ang)
    return jnp.tile(cos, (1, 4)), jnp.concatenate([-sin, sin, -sin, sin], axis=-1)


def kernel(x, c, ctx, c_ctx, norm1_g, norm2_g, ada_w, ada_b, w_in, attn_sink, sgu_ws, sgu_b, sgu_ln_g, sgu_ln_b,
           pool_w, pool_scale, conv_w, w_out, router_w, router_b, exp_w1, exp_b1, exp_w2, exp_b2, final_g):
    n_batch, n_seq, d = x.shape
    n_ctx = ctx.shape[1]
    depth = ada_w.shape[0]
    n_exp = router_w.shape[2]
    assert d == SUBLANES * LANES, "token rows are moved as single (8, 128) tiles"
    t_lat, t_ctx = n_batch * n_seq, n_batch * n_ctx
    bm = EXPERT_ROWS

    cvec = jnp.concatenate([c, c_ctx[None, :], jnp.zeros((SUBLANES - n_batch - 1, d), f32)], axis=0)
    mods = _ada(cvec, ada_w, ada_b)

    cos_l, sin_l = _rope_tables(n_seq)
    cos_c, sin_c = jnp.ones((n_ctx, LANES), f32), jnp.zeros((n_ctx, LANES), f32)

    xl = x.reshape(t_lat, d)
    xc = ctx.reshape(t_ctx, d)
    row2 = lambda a: a.reshape(1, -1)
    w1_split = {}

    for l in range(depth):
        last = l == depth - 1
        ml = mods[l, :n_batch].reshape(n_batch, 6, 1, d)
        mc = jnp.broadcast_to(mods[l, n_batch].reshape(1, 6, 1, d), (n_batch, 6, 1, d))
        sh1l, sc1l, g1l, sh2l, sc2l, g2l = (ml[:, i] for i in range(6))
        sh1c, sc1c, g1c, sh2c, sc2c, g2c = (mc[:, i] for i in range(6))

        w_in_bf = w_in[l].astype(bf16)
        wo_bf = w_out[l].astype(bf16)
        ws_bf = sgu_ws[l].astype(bf16)
        sgu_bias = jnp.repeat(sgu_b[l].T, SGU_W // SGU_HEADS, axis=1)
        pool_bd = jax.scipy.linalg.block_diag(*[pool_w[l, g] for g in range(pool_w.shape[1])]).astype(bf16)
        n1g, n2g = row2(norm1_g[l]), row2(norm2_g[l])
        lng, lnb, psc = row2(sgu_ln_g[l]), row2(sgu_ln_b[l]), row2(pool_scale[l])
        rw_t = router_w[l].T
        rb = router_b[l].reshape(n_exp, 1)
        sink = attn_sink[l]
        mix_w = (ws_bf, sgu_bias, pool_bd, psc, conv_w[l], wo_bf, rw_t, rb)

        qc, kvc, vnc, mfc = _inproj(xc, n_ctx, sh1c, sc1c, n1g, w_in_bf, cos_c, sin_c, lng, lnb)
        ql, kvl, vnl, mfl = _inproj(xl, n_seq, sh1l, sc1l, n1g, w_in_bf, cos_l, sin_l, lng, lnb)
        attn_l = _window_attn(ql, kvl, kvc, sink, n_batch, n_seq, n_ctx)
        bt_l, bt_c = min(MIXER_TOKENS, n_seq), min(MIXER_TOKENS, n_ctx)
        xmid_l, h2_l, s4_l, s8_l, gt_l, cnt = _mixer(attn_l, vnl, mfl, xl, n_seq, g1l, sh2l, sc2l, n2g, *mix_w)
        n_win_l = cnt.shape[0]
        t_all = t_lat
        if not last:
            attn_c = _ctx_attn(qc, kvc, sink, n_batch, n_ctx)
            xmid_c, h2_c, s4_c, s8_c, gt_c, cnt_c = _mixer(attn_c, vnc, mfc, xc, n_ctx, g1c, sh2c, sc2c, n2g, *mix_w)
            cnt = jnp.concatenate([cnt, cnt_c], axis=0)
            t_all = t_lat + t_ctx

        win_cnt = cnt[:, :, 0].astype(i32)
        counts = jnp.sum(win_cnt, axis=0)
        padded = (counts + bm - 1) // bm * bm
        pends = jnp.cumsum(padded)
        pstarts = pends - padded
        n_blocks = -(-(t_all * TOP_K) // bm) + n_exp
        starts = jnp.arange(n_blocks, dtype=i32) * bm
        block_e = jnp.minimum(jnp.sum((pends[None, :] <= starts[:, None]).astype(i32), axis=1), n_exp - 1)
        n_used = (pends[-1:] // bm).astype(i32)
        pad = jnp.concatenate([pstarts + counts, padded - counts, pends[-1:], n_blocks - pends[-1:] // bm]).astype(i32)
        run_stage = jnp.cumsum(win_cnt, axis=1) - win_cnt
        run_slot = pstarts[None, :] + jnp.cumsum(win_cnt, axis=0) - win_cnt
        runs = jnp.stack([run_stage, run_slot, win_cnt], axis=-1).reshape(-1)

        xs, split = _dispatch(s4_l, runs, pad, h2_l, None, n_blocks * bm, bt_l, 0, n_exp,
                              None if l in w1_split else (exp_w1, l),
                              None if last else (n_win_l, t_ctx // bt_c, bt_c))
        if l not in w1_split:
            w1_split[l] = split or _split_w1(exp_w1, l)
        if not last:
            xs, _ = _dispatch(s4_c, runs, pad, h2_c, xs, n_blocks * bm, bt_c, n_win_l, n_exp)

        w1g, w1l = w1_split[l]
        b1 = exp_b1[l]
        b1g, b1l = b1[:, None, 0::2], b1[:, None, 1::2]
        ys = _experts(block_e, n_used, xs, w1g, w1l, b1g, b1l, exp_w2, l, exp_b2[l][:, None, :])

        fg = row2(final_g)
        xl, split = _combine(s8_l, gt_l, runs, ys, xmid_l, n_seq, g2l, fg, last, bt_l, 0, n_exp,
                             None if last else (exp_w1, l + 1))
        if split:
            w1_split[l + 1] = split
        if not last:
            xc, _ = _combine(s8_c, gt_c, runs, ys, xmid_c, n_ctx, g2c, fg, False, bt_c, n_win_l, n_exp)

    return xl.reshape(n_batch, n_seq, d)
```
